```python
import jax, jax.numpy as jnp
from jax import lax
import numpy as np

D_MODEL = 1024
BATCH = 8
SEQ = 2048
DEPTH = 1
DEC_BATCH = 32
DEC_SEQ = 8
PAST_LEN = 8192
PAGE_SIZE = 128

D_HEAD = 64
MOBA_HEADS = 8
MOBA_BLOCK = 256
MOBA_TOPK = 3
NSA_HEADS = 8
NSA_KV_HEADS = 2
NSA_GROUP = NSA_HEADS // NSA_KV_HEADS
CMP_LEN = 32
CMP_STRIDE = 16
CMP_HIDDEN = 128
SLC_BLOCK = 64
SLC_TOPN = 16
WINDOW = 512
N_GROUPS = 4
EXPERTS_PER_GROUP = 8
N_EXPERTS = N_GROUPS * EXPERTS_PER_GROUP
D_EXPERT = 256
TOPK_IN_GROUP = 2
ROPE_THETA = 10000.0
EPS = 1e-6
Q_CHUNK = 128
TOKEN_BLOCK = 1024
NEG_INF = -1e30
BIG = 1e30
TINY = 1e-30
MOBA_WIDTH = MOBA_HEADS * D_HEAD
NSA_WIDTH = NSA_HEADS * D_HEAD
NSA_KV_WIDTH = NSA_KV_HEADS * D_HEAD
IN_SIZES = (MOBA_WIDTH,) * 3 + (NSA_WIDTH,) + (NSA_KV_WIDTH,) * 6 + (3 * NSA_HEADS, 2 * D_MODEL)
IN_COLS = sum(IN_SIZES)

kernel_name = 'hybrid_moba_nsa_hmoe_decode_step'


def rms_norm(x, g):
    xf = x.astype(jnp.float32)
    y = xf * lax.rsqrt(jnp.mean(xf * xf, axis=-1, keepdims=True) + EPS)
    return (y * g.astype(jnp.float32)).astype(x.dtype)


def modulate(h, shift, scale):
    return h * (1.0 + scale[:, None, :]) + shift[:, None, :]


def rope(x, pos):
    half = D_HEAD // 2
    inv = ROPE_THETA ** (-jnp.arange(half, dtype=jnp.float32) / half)
    ang = pos.astype(jnp.float32)[:, None] * inv[None, :]
    cos = jnp.cos(ang)[None, :, None, :]
    sin = jnp.sin(ang)[None, :, None, :]
    xf = x.astype(jnp.float32)
    x1, x2 = xf[..., :half], xf[..., half:]
    return jnp.concatenate([x1 * cos - x2 * sin, x2 * cos + x1 * sin], axis=-1).astype(x.dtype)


def masked_softmax(s, mask):
    s = jnp.where(mask, s, NEG_INF)
    m = jnp.max(s, axis=-1, keepdims=True)
    p = jnp.where(mask, jnp.exp(s - m), 0.0)
    return p / jnp.maximum(jnp.sum(p, axis=-1, keepdims=True), TINY)


def map_query_chunks(fn, q, pos):
    B, T = q.shape[0], q.shape[1]
    if T <= Q_CHUNK or T % Q_CHUNK != 0:
        return fn(q, pos)
    n = T // Q_CHUNK
    qs = jnp.swapaxes(q.reshape((B, n, Q_CHUNK) + q.shape[2:]), 0, 1)
    out = lax.map(lambda a: fn(a[0], a[1]), (qs, pos.reshape(n, Q_CHUNK)))
    return jnp.swapaxes(out, 0, 1).reshape((B, T) + out.shape[3:])


def pad_to_blocks(x, blk):
    L = x.shape[1]
    n = -(-L // blk)
    x = jnp.pad(x, ((0, 0), (0, n * blk - L)) + ((0, 0),) * (x.ndim - 2))
    return x.reshape((x.shape[0], n, blk) + x.shape[2:]), n


def gather_pages(cache, page_table):
    pages = cache[page_table]
    return pages.reshape((pages.shape[0], pages.shape[1] * pages.shape[2]) + pages.shape[3:])


def moba_attention(q, k, v, q_pos):
    B = k.shape[0]
    kb, nblk = pad_to_blocks(k, MOBA_BLOCK)
    vb, _ = pad_to_blocks(v, MOBA_BLOCK)
    kb = kb.transpose(0, 3, 1, 2, 4)
    vb = vb.transpose(0, 3, 1, 2, 4)
    k_mean = jnp.mean(kb.astype(jnp.float32), axis=3)
    n_sel = min(MOBA_TOPK, nblk)
    b_ix = jnp.arange(B)[:, None, None, None]
    h_ix = jnp.arange(MOBA_HEADS)[None, :, None, None]
    blk_ids = jnp.arange(nblk)
    offs = jnp.arange(MOBA_BLOCK)
    scale = D_HEAD ** -0.5

    def chunk(qc, pc):
        C = qc.shape[1]
        q_blk = pc // MOBA_BLOCK
        gate = jnp.einsum('bchd,bhnd->bhcn', qc.astype(jnp.float32), k_mean)
        past = blk_ids[None, :] < q_blk[:, None]
        _, top_i = lax.top_k(jnp.where(past, gate, NEG_INF), n_sel)
        own = jnp.broadcast_to(q_blk[:, None], (B, MOBA_HEADS, C, 1)).astype(top_i.dtype)
        idx = jnp.concatenate([top_i, own], axis=-1)
        valid = jnp.concatenate([top_i < q_blk[:, None], jnp.ones(own.shape, dtype=bool)], axis=-1)
        kg = kb[b_ix, h_ix, idx]
        vg = vb[b_ix, h_ix, idx]
        k_pos = idx[..., None] * MOBA_BLOCK + offs
        mask = valid[..., None] & (k_pos <= pc[:, None, None])
        s = jnp.einsum('bchd,bhcskd->bhcsk', qc, kg).astype(jnp.float32) * scale
        p = masked_softmax(s.reshape(B, MOBA_HEADS, C, -1), mask.reshape(B, MOBA_HEADS, C, -1))
        o = jnp.einsum('bhcsk,bhcskd->bchd', p.reshape(s.shape).astype(vg.dtype), vg)
        return o.astype(qc.dtype)

    return map_query_chunks(chunk, q, q_pos)


def compress(k, pe, w1, w2):
    B, L = k.shape[0], k.shape[1]
    nc = (L - CMP_LEN) // CMP_STRIDE + 1
    idx = jnp.arange(nc)[:, None] * CMP_STRIDE + jnp.arange(CMP_LEN)[None, :]
    blocks = k[:, idx] + pe[None, None, :, None, :]
    flat = blocks.transpose(0, 1, 3, 2, 4).reshape(B, nc, NSA_KV_HEADS, CMP_LEN * D_HEAD)
    return jax.nn.gelu(flat @ w1) @ w2


def nsa_attention(q, kc, vc, ks, vs, kw, vw, wp0, q_pos, pe_k, wk1, wk2, pe_v, wv1, wv2):
    B = kc.shape[0]
    k_cmp = compress(kc, pe_k, wk1, wk2)
    v_cmp = compress(vc, pe_v, wv1, wv2)
    nc = k_cmp.shape[1]
    cmp_start = jnp.arange(nc) * CMP_STRIDE
    cmp_end = cmp_start + CMP_LEN - 1
    ksb, nslc = pad_to_blocks(ks, SLC_BLOCK)
    vsb, _ = pad_to_blocks(vs, SLC_BLOCK)
    ksb = ksb.transpose(0, 3, 1, 2, 4)
    vsb = vsb.transpose(0, 3, 1, 2, 4)
    slc_start = jnp.arange(nslc) * SLC_BLOCK
    overlap = ((cmp_start[:, None] < slc_start[None, :] + SLC_BLOCK)
               & (cmp_start[:, None] + CMP_LEN > slc_start[None, :])).astype(jnp.float32)
    n_top = min(SLC_TOPN, nslc)
    kw_pad = jnp.pad(kw, ((0, 0), (WINDOW, 0), (0, 0), (0, 0)))
    vw_pad = jnp.pad(vw, ((0, 0), (WINDOW, 0), (0, 0), (0, 0)))
    b_ix = jnp.arange(B)[:, None, None, None]
    g_ix = jnp.arange(NSA_KV_HEADS)[None, :, None, None]
    offs = jnp.arange(SLC_BLOCK)
    scale = D_HEAD ** -0.5

    def chunk(qc, pc):
        C = qc.shape[1]
        qg = qc.reshape(B, C, NSA_KV_HEADS, NSA_GROUP, D_HEAD)
        s_c = jnp.einsum('bcgrd,bngd->bgrcn', qg, k_cmp).astype(jnp.float32) * scale
        p_c = masked_softmax(s_c, cmp_end[None, :] <= pc[:, None])
        o_c = jnp.einsum('bgrcn,bngd->bcgrd', p_c.astype(v_cmp.dtype), v_cmp)
        imp = jnp.einsum('bgrcn,nj->bgcj', p_c, overlap)
        q_blk = pc // SLC_BLOCK
        j = jnp.arange(nslc)[None, :]
        imp = jnp.where(j == q_blk[:, None], BIG, jnp.where(j < q_blk[:, None], imp, NEG_INF))
        _, top_i = lax.top_k(imp, n_top)
        kg = ksb[b_ix, g_ix, top_i]
        vg = vsb[b_ix, g_ix, top_i]
        k_pos = top_i[..., None] * SLC_BLOCK + offs
        mask_s = (top_i <= q_blk[:, None])[..., None] & (k_pos <= pc[:, None, None])
        s_s = jnp.einsum('bcgrd,bgcskd->bgrcsk', qg, kg).astype(jnp.float32) * scale
        p_s = masked_softmax(s_s.reshape(B, NSA_KV_HEADS, NSA_GROUP, C, -1),
                             mask_s.reshape(B, NSA_KV_HEADS, 1, C, -1)).reshape(s_s.shape)
        o_s = jnp.einsum('bgrcsk,bgcskd->bcgrd', p_s.astype(vg.dtype), vg)
        start = pc[0] - wp0
        kwc = lax.dynamic_slice_in_dim(kw_pad, start, WINDOW + C, axis=1)
        vwc = lax.dynamic_slice_in_dim(vw_pad, start, WINDOW + C, axis=1)
        w_pos = pc[0] - WINDOW + jnp.arange(WINDOW + C)
        dist = pc[:, None] - w_pos[None, :]
        mask_w = (dist >= 0) & (dist <= WINDOW) & (w_pos >= wp0)[None, :]
        s_w = jnp.einsum('bcgrd,bkgd->bgrck', qg, kwc).astype(jnp.float32) * scale
        p_w = masked_softmax(s_w, mask_w)
        o_w = jnp.einsum('bgrck,bkgd->bcgrd', p_w.astype(vwc.dtype), vwc)
        o = jnp.stack([o_c, o_s, o_w], axis=-2)
        return o.reshape(B, C, NSA_HEADS, 3, D_HEAD).astype(qc.dtype)

    return map_query_chunks(chunk, q, q_pos)


def hmoe(h, w_grp, w_exp, w_ein, w_eout):
    B, T, D = h.shape
    N = B * T

    def route_mix(t):
        p_grp = jax.nn.softmax((t @ w_grp).astype(jnp.float32), axis=-1)
        g_sel = jnp.argmax(p_grp, axis=-1)
        g_w = jnp.max(p_grp, axis=-1)
        le = (t @ w_exp).astype(jnp.float32).reshape(-1, N_GROUPS, EXPERTS_PER_GROUP)
        le_g = jnp.take_along_axis(le, g_sel[:, None, None], axis=1)[:, 0]
        top_v, top_i = lax.top_k(jax.nn.softmax(le_g, axis=-1), TOPK_IN_GROUP)
        top_v = top_v / jnp.sum(top_v, axis=-1, keepdims=True)
        within = jnp.sum(jax.nn.one_hot(top_i, EXPERTS_PER_GROUP, dtype=jnp.float32) * top_v[..., None], axis=1)
        combine = (g_w[:, None, None] * jax.nn.one_hot(g_sel, N_GROUPS, dtype=jnp.float32)[:, :, None]
                   * within[:, None, :]).reshape(-1, N_EXPERTS)
        hid = jnp.einsum('nd,edf->nef', t, w_ein)
        a, b = jnp.split(hid, 2, axis=-1)
        act = jax.nn.silu(a) * b * combine[..., None].astype(hid.dtype)
        return jnp.einsum('nef,efd->nd', act, w_eout)

    t = h.reshape(N, D)
    if N > TOKEN_BLOCK and N % TOKEN_BLOCK == 0:
        out = lax.map(route_mix, t.reshape(N // TOKEN_BLOCK, TOKEN_BLOCK, D)).reshape(N, D)
    else:
        out = route_mix(t)
    return out.reshape(B, T, D).astype(h.dtype)


def split_cols(proj):
    parts = []
    off = 0
    for n in IN_SIZES:
        parts.append(proj[..., off:off + n])
        off += n
    return parts


def layer(x, c, pos, past, lw):
    B, T = x.shape[0], x.shape[1]
    mod = jax.nn.silu(c) @ lw['w_ada'] + lw['b_ada']
    sh1, sc1, g1, sh2, sc2, g2 = jnp.split(mod, 6, axis=-1)
    h = modulate(rms_norm(x, lw['norm_mix_g']), sh1, sc1)
    a_q, a_k, a_v, b_q, b_kc, b_vc, b_ks, b_vs, b_kw, b_vw, b_gate, merge = split_cols(h @ lw['w_in'])
    a_q = rope(a_q.reshape(B, T, MOBA_HEADS, D_HEAD), pos)
    a_k = rope(a_k.reshape(B, T, MOBA_HEADS, D_HEAD), pos)
    a_v = a_v.reshape(B, T, MOBA_HEADS, D_HEAD)
    b_q = rope(b_q.reshape(B, T, NSA_HEADS, D_HEAD), pos)
    b_kc = rope(b_kc.reshape(B, T, NSA_KV_HEADS, D_HEAD), pos)
    b_vc = b_vc.reshape(B, T, NSA_KV_HEADS, D_HEAD)
    b_ks = rope(b_ks.reshape(B, T, NSA_KV_HEADS, D_HEAD), pos)
    b_vs = b_vs.reshape(B, T, NSA_KV_HEADS, D_HEAD)
    b_kw = rope(b_kw.reshape(B, T, NSA_KV_HEADS, D_HEAD), pos)
    b_vw = b_vw.reshape(B, T, NSA_KV_HEADS, D_HEAD)
    wb = min(WINDOW, PAST_LEN)
    if past is None:
        k_a, v_a, k_c, v_c, k_s, v_s = a_k, a_v, b_kc, b_vc, b_ks, b_vs
        k_w, v_w, wp0 = b_kw, b_vw, 0
    else:
        pk_a, pv_a, pk_c, pv_c, pk_s, pv_s, buf_k, buf_v = past
        k_a = jnp.concatenate([pk_a, a_k], axis=1)
        v_a = jnp.concatenate([pv_a, a_v], axis=1)
        k_c = jnp.concatenate([pk_c, b_kc], axis=1)
        v_c = jnp.concatenate([pv_c, b_vc], axis=1)
        k_s = jnp.concatenate([pk_s, b_ks], axis=1)
        v_s = jnp.concatenate([pv_s, b_vs], axis=1)
        k_w = jnp.concatenate([buf_k, b_kw], axis=1)
        v_w = jnp.concatenate([buf_v, b_vw], axis=1)
        wp0 = pk_a.shape[1] - buf_k.shape[1]
    pad_w = max(0, wb - k_w.shape[1])
    new_wk = jnp.pad(k_w, ((0, 0), (pad_w, 0), (0, 0), (0, 0)))[:, -wb:]
    new_wv = jnp.pad(v_w, ((0, 0), (pad_w, 0), (0, 0), (0, 0)))[:, -wb:]
    o_a = moba_attention(a_q, k_a, v_a, pos).reshape(B, T, MOBA_WIDTH) @ lw['w_br_a']
    br = nsa_attention(b_q, k_c, v_c, k_s, v_s, k_w, v_w, wp0, pos,
                       lw['pe_cmp_k'], lw['w_cmp_k1'], lw['w_cmp_k2'],
                       lw['pe_cmp_v'], lw['w_cmp_v1'], lw['w_cmp_v2'])
    gates = jax.nn.sigmoid(b_gate.reshape(B, T, NSA_HEADS, 3))
    o_b = jnp.sum(gates[..., None] * br, axis=3).reshape(B, T, NSA_WIDTH) @ lw['w_br_b']
    m_a, m_b = jnp.split(jax.nn.sigmoid(merge), 2, axis=-1)
    x = x + g1[:, None, :] * ((m_a * o_a + m_b * o_b) @ lw['w_out'])
    h2 = modulate(rms_norm(x, lw['norm_ffn_g']), sh2, sc2)
    x = x + g2[:, None, :] * hmoe(h2, lw['w_router_grp'], lw['w_router_exp'], lw['w_expert_in'], lw['w_expert_out'])
    return x, (a_k, a_v, b_kc, b_vc, b_ks, b_vs, new_wk, new_wv)


def setup_inputs(seed: int = 0) -> dict:
    key = jax.random.key(seed)
    ks = jax.random.split(key, 32)
    f32 = jnp.float32

    def nrm(k, shape, scale):
        return jax.random.normal(k, shape, f32) * scale

    n_pages = PAST_LEN // PAGE_SIZE
    n_phys = (DEC_BATCH * n_pages * 5) // 4
    wb = min(WINDOW, PAST_LEN)
    page_table = jax.random.permutation(ks[0], n_phys)[:DEC_BATCH * n_pages].reshape(DEC_BATCH, n_pages).astype(jnp.int32)
    return {
        'x_prompt': nrm(ks[1], (BATCH, SEQ, D_MODEL), 1.0),
        'x_sample': nrm(ks[2], (DEC_BATCH, DEC_SEQ, D_MODEL), 1.0),
        'c_prompt': nrm(ks[3], (BATCH, D_MODEL), 1.0),
        'c_sample': nrm(ks[4], (DEC_BATCH, D_MODEL), 1.0),
        'cache_moba_k': nrm(ks[5], (DEPTH, n_phys, PAGE_SIZE, MOBA_HEADS, D_HEAD), 1.0),
        'cache_moba_v': nrm(ks[6], (DEPTH, n_phys, PAGE_SIZE, MOBA_HEADS, D_HEAD), 1.0),
        'cache_nsa_cmp_k': nrm(ks[7], (DEPTH, n_phys, PAGE_SIZE, NSA_KV_HEADS, D_HEAD), 1.0),
        'cache_nsa_cmp_v': nrm(ks[8], (DEPTH, n_phys, PAGE_SIZE, NSA_KV_HEADS, D_HEAD), 1.0),
        'cache_nsa_slc_k': nrm(ks[9], (DEPTH, n_phys, PAGE_SIZE, NSA_KV_HEADS, D_HEAD), 1.0),
        'cache_nsa_slc_v': nrm(ks[10], (DEPTH, n_phys, PAGE_SIZE, NSA_KV_HEADS, D_HEAD), 1.0),
        'state_nsa_win_k': nrm(ks[11], (DEPTH, DEC_BATCH, wb, NSA_KV_HEADS, D_HEAD), 1.0),
        'state_nsa_win_v': nrm(ks[12], (DEPTH, DEC_BATCH, wb, NSA_KV_HEADS, D_HEAD), 1.0),
        'page_table': page_table,
        'w_ada': nrm(ks[13], (DEPTH, D_MODEL, 6 * D_MODEL), D_MODEL ** -0.5),
        'b_ada': nrm(ks[14], (DEPTH, 6 * D_MODEL), 0.01),
        'norm_mix_g': 1.0 + nrm(ks[15], (DEPTH, D_MODEL), 0.02),
        'w_in': nrm(ks[16], (DEPTH, D_MODEL, IN_COLS), D_MODEL ** -0.5),
        'pe_cmp_k': nrm(ks[17], (DEPTH, CMP_LEN, D_HEAD), 0.1),
        'w_cmp_k1': nrm(ks[18], (DEPTH, CMP_LEN * D_HEAD, CMP_HIDDEN), (CMP_LEN * D_HEAD) ** -0.5),
        'w_cmp_k2': nrm(ks[19], (DEPTH, CMP_HIDDEN, D_HEAD), CMP_HIDDEN ** -0.5),
        'pe_cmp_v': nrm(ks[20], (DEPTH, CMP_LEN, D_HEAD), 0.1),
        'w_cmp_v1': nrm(ks[21], (DEPTH, CMP_LEN * D_HEAD, CMP_HIDDEN), (CMP_LEN * D_HEAD) ** -0.5),
        'w_cmp_v2': nrm(ks[22], (DEPTH, CMP_HIDDEN, D_HEAD), CMP_HIDDEN ** -0.5),
        'w_br_a': nrm(ks[23], (DEPTH, MOBA_WIDTH, D_MODEL), MOBA_WIDTH ** -0.5),
        'w_br_b': nrm(ks[24], (DEPTH, NSA_WIDTH, D_MODEL), NSA_WIDTH ** -0.5),
        'w_out': nrm(ks[25], (DEPTH, D_MODEL, D_MODEL), D_MODEL ** -0.5),
        'norm_ffn_g': 1.0 + nrm(ks[26], (DEPTH, D_MODEL), 0.02),
        'w_router_grp': nrm(ks[27], (DEPTH, D_MODEL, N_GROUPS), D_MODEL ** -0.5),
        'w_router_exp': nrm(ks[28], (DEPTH, D_MODEL, N_EXPERTS), D_MODEL ** -0.5),
        'w_expert_in': nrm(ks[29], (DEPTH, N_EXPERTS, D_MODEL, 2 * D_EXPERT), D_MODEL ** -0.5),
        'w_expert_out': nrm(ks[30], (DEPTH, N_EXPERTS, D_EXPERT, D_MODEL), D_EXPERT ** -0.5),
        'norm_final_g': 1.0 + nrm(ks[31], (D_MODEL,), 0.02),
    }


def reference(x_prompt, x_sample, c_prompt, c_sample, cache_moba_k, cache_moba_v,
              cache_nsa_cmp_k, cache_nsa_cmp_v, cache_nsa_slc_k, cache_nsa_slc_v,
              state_nsa_win_k, state_nsa_win_v, page_table, w_ada, b_ada, norm_mix_g, w_in,
              pe_cmp_k, w_cmp_k1, w_cmp_k2, pe_cmp_v, w_cmp_v1, w_cmp_v2, w_br_a, w_br_b, w_out,
              norm_ffn_g, w_router_grp, w_router_exp, w_expert_in, w_expert_out, norm_final_g):
    t_p = x_prompt.shape[1]
    t_s = x_sample.shape[1]
    past_len = page_table.shape[1] * cache_moba_k.shape[2]
    pos_p = jnp.arange(t_p, dtype=jnp.int32)
    pos_s = past_len + jnp.arange(t_s, dtype=jnp.int32)
    xp, xs = x_prompt, x_sample
    st_p, st_s = [], []
    for l in range(DEPTH):
        lw = {'w_ada': w_ada[l], 'b_ada': b_ada[l], 'norm_mix_g': norm_mix_g[l], 'w_in': w_in[l],
              'pe_cmp_k': pe_cmp_k[l], 'w_cmp_k1': w_cmp_k1[l], 'w_cmp_k2': w_cmp_k2[l],
              'pe_cmp_v': pe_cmp_v[l], 'w_cmp_v1': w_cmp_v1[l], 'w_cmp_v2': w_cmp_v2[l],
              'w_br_a': w_br_a[l], 'w_br_b': w_br_b[l], 'w_out': w_out[l], 'norm_ffn_g': norm_ffn_g[l],
              'w_router_grp': w_router_grp[l], 'w_router_exp': w_router_exp[l],
              'w_expert_in': w_expert_in[l], 'w_expert_out': w_expert_out[l]}
        xp, new_p = layer(xp, c_prompt, pos_p, None, lw)
        past = (gather_pages(cache_moba_k[l], page_table), gather_pages(cache_moba_v[l], page_table),
                gather_pages(cache_nsa_cmp_k[l], page_table), gather_pages(cache_nsa_cmp_v[l], page_table),
                gather_pages(cache_nsa_slc_k[l], page_table), gather_pages(cache_nsa_slc_v[l], page_table),
                state_nsa_win_k[l], state_nsa_win_v[l])
        xs, new_s = layer(xs, c_sample, pos_s, past, lw)
        st_p.append(new_p)
        st_s.append(new_s)
    sp = [jnp.stack(a, axis=0) for a in zip(*st_p)]
    ss = [jnp.stack(a, axis=0) for a in zip(*st_s)]
    y_prompt = rms_norm(xp, norm_final_g)
    y_sample = rms_norm(xs, norm_final_g)
    return (y_prompt, y_sample, sp[0], sp[1], sp[2], sp[3], sp[4], sp[5], sp[6], sp[7],
            ss[0], ss[1], ss[2], ss[3], ss[4], ss[5], ss[6], ss[7])
```

```python
import functools

import jax
import jax.numpy as jnp
from jax import lax
from jax.experimental import pallas as pl
from jax.experimental.pallas import tpu as pltpu

D_MODEL = 1024
D_HEAD = 64
HALF = D_HEAD // 2
MOBA_HEADS = 8
MOBA_BLOCK = 256
MOBA_TOPK = 3
NSA_HEADS = 8
NSA_KV_HEADS = 2
NSA_GROUP = NSA_HEADS // NSA_KV_HEADS
CMP_LEN = 32
CMP_STRIDE = 16
CMP_HIDDEN = 128
SLC_BLOCK = 64
SLC_TOPN = 16
WINDOW = 512
N_GROUPS = 4
EXPERTS_PER_GROUP = 8
N_EXPERTS = N_GROUPS * EXPERTS_PER_GROUP
D_EXPERT = 256
ROPE_THETA = 10000.0
EPS = 1e-6
NEG_INF = -1e30
BIG = 1e30
TINY = 1e-30
MOBA_WIDTH = MOBA_HEADS * D_HEAD
NSA_WIDTH = NSA_HEADS * D_HEAD
NSA_KV_WIDTH = NSA_KV_HEADS * D_HEAD
SCALE = D_HEAD ** -0.5

LANES = 128
VMEM_LIMIT = 48 * 1024 * 1024

F32 = jnp.float32
BF16 = jnp.bfloat16
HIGHEST = lax.Precision.HIGHEST


def _params(*sem):
    return pltpu.CompilerParams(dimension_semantics=sem, vmem_limit_bytes=VMEM_LIMIT)


def _dot(a, b):
    return jnp.dot(a, b, preferred_element_type=F32)


def _dot_nt(a, b, precision=None):
    return lax.dot_general(a, b, (((1,), (1,)), ((), ())), precision=precision,
                           preferred_element_type=F32)


def _sigmoid(x):
    return 1.0 / (1.0 + jnp.exp(-x))


def _iota(shape, dim):
    return lax.broadcasted_iota(jnp.int32, shape, dim)


def _ada_kernel(c_ref, w_ref, b_ref, o_ref):
    c = c_ref[...]
    s = c * _sigmoid(c)
    o_ref[...] = _dot(s.astype(BF16), w_ref[...].astype(BF16)) + b_ref[...]


def _ada(c, w, b):
    n = c.shape[0]
    tn = 1024
    return pl.pallas_call(
        _ada_kernel,
        grid=(w.shape[1] // tn,),
        in_specs=[pl.BlockSpec((n, D_MODEL), lambda j: (0, 0)),
                  pl.BlockSpec((D_MODEL, tn), lambda j: (0, j)),
                  pl.BlockSpec((1, tn), lambda j: (0, j))],
        out_specs=pl.BlockSpec((n, tn), lambda j: (0, j)),
        out_shape=jax.ShapeDtypeStruct((n, w.shape[1]), F32),
        compiler_params=_params("arbitrary"),
        name="ada",
    )(c, w, b)


_IN_GROUPS = ((2 * MOBA_WIDTH, True),
              (MOBA_WIDTH, False),
              (NSA_WIDTH, True),
              (3 * NSA_KV_WIDTH, True),
              (3 * NSA_KV_WIDTH, False),
              (LANES, False),
              (2 * D_MODEL, False))
_IN_COLS_PAD = sum(w for w, _ in _IN_GROUPS)


def _norm_mod(x, g, sc, sh):
    ms = jnp.mean(x * x, axis=-1, keepdims=True)
    y = x * lax.rsqrt(ms + EPS) * g
    return y * (1.0 + sc) + sh


def _inproj_kernel(x_ref, sc_ref, sh_ref, g_ref, cos_ref, sin_ref, w_ref, *out_refs):
    h = _norm_mod(x_ref[...], g_ref[...], sc_ref[0], sh_ref[0]).astype(BF16)
    cos = cos_ref[...]
    sin = sin_ref[...]
    first_half = (_iota(cos.shape, 1) & (D_HEAD - 1)) < HALF

    def rope(y):
        rot = jnp.where(first_half, pltpu.roll(y, LANES - HALF, 1), pltpu.roll(y, HALF, 1))
        return y * cos + rot * sin

    col = 0
    for out_ref, (width, rotary) in zip(out_refs, _IN_GROUPS):
        chunk = min(width, 512)
        for c in range(0, width, chunk):
            cw = min(chunk, width - c)
            y = _dot(h, w_ref[:, col + c:col + c + cw])
            for s in range(0, cw, LANES):
                piece = y[:, s:s + LANES]
                out_ref[:, c + s:c + s + LANES] = rope(piece) if rotary else piece
        col += width


def _inproj(x, sc, sh, g, cos, sin, w, tm):
    n = x.shape[0]
    nb, r, _ = sc.shape
    tiles_per_b = (n // nb) // tm
    tab_tiles = cos.shape[0] // tm
    row = lambda i: (i, 0)
    mod = lambda i: (i // tiles_per_b, 0, 0)
    tab = lambda i: (i % tab_tiles, 0)
    return pl.pallas_call(
        _inproj_kernel,
        grid=(n // tm,),
        in_specs=[pl.BlockSpec((tm, D_MODEL), row),
                  pl.BlockSpec((1, r, D_MODEL), mod),
                  pl.BlockSpec((1, r, D_MODEL), mod),
                  pl.BlockSpec((1, D_MODEL), lambda i: (0, 0)),
                  pl.BlockSpec((tm, LANES), tab),
                  pl.BlockSpec((tm, LANES), tab),
                  pl.BlockSpec((D_MODEL, _IN_COLS_PAD), lambda i: (0, 0))],
        out_specs=[pl.BlockSpec((tm, wd), row) for wd, _ in _IN_GROUPS],
        out_shape=[jax.ShapeDtypeStruct((n, wd), F32) for wd, _ in _IN_GROUPS],
        compiler_params=_params("arbitrary"),
        name="inproj",
    )(x, sc, sh, g, cos, sin, w)


def _reorder_w_in(w_in):
    kv0 = 3 * MOBA_WIDTH + NSA_WIDTH
    kvs = [w_in[:, kv0 + i * NSA_KV_WIDTH:kv0 + (i + 1) * NSA_KV_WIDTH] for i in range(6)]
    g0 = kv0 + 6 * NSA_KV_WIDTH
    ng = 3 * NSA_HEADS
    gate = jnp.pad(w_in[:, g0:g0 + ng], ((0, 0), (0, LANES - ng)))
    parts = [w_in[:, :kv0], kvs[0], kvs[2], kvs[4], kvs[1], kvs[3], kvs[5], gate, w_in[:, g0 + ng:]]
    return jnp.concatenate(parts, axis=1).astype(BF16)


def _rope_tables(pos):
    inv = ROPE_THETA ** (-jnp.arange(HALF, dtype=F32) / HALF)
    ang = pos.astype(F32)[:, None] * inv[None, :]
    cos = jnp.cos(ang)
    sin = jnp.sin(ang)
    cos = jnp.concatenate([cos, cos, cos, cos], axis=1)
    sin = jnp.concatenate([-sin, sin, -sin, sin], axis=1)
    return cos, sin


def _rank_select(score, n_cols, n_keep):
    lane = _iota(score.shape, 1)
    rank = jnp.zeros(score.shape, jnp.int32)
    for jp in range(n_cols):
        col = score[:, jp:jp + 1]
        beats = (col > score) | ((col == score) & (lane > jp))
        rank = rank + beats.astype(jnp.int32)
    return rank < n_keep


def _flash(q, k_ref, v_ref, lane0, c_lo, c_hi, tk, mask_fn, heads):
    rows = q.shape[0]
    tq = rows // heads

    def body(c, carry):
        m, l, acc = carry
        k = k_ref[0, pl.ds(pl.multiple_of(c * tk, tk), tk), lane0:lane0 + D_HEAD].astype(BF16)
        v = v_ref[0, pl.ds(pl.multiple_of(c * tk, tk), tk), lane0:lane0 + D_HEAD].astype(BF16)
        s = _dot_nt(q, k)
        msk = mask_fn(c)
        if heads > 1:
            msk = jnp.broadcast_to(msk[None], (heads, tq, tk)).reshape(rows, tk)
        s = jnp.where(msk, s, NEG_INF)
        m_new = jnp.maximum(m, jnp.max(s, axis=-1, keepdims=True))
        p = jnp.where(msk, jnp.exp(s - m_new), 0.0)
        alpha = jnp.exp(m - m_new)
        l = alpha * l + jnp.sum(p, axis=-1, keepdims=True)
        acc = alpha * acc + _dot(p.astype(BF16), v)
        return m_new, l, acc

    init = (jnp.full((rows, 1), NEG_INF, F32), jnp.zeros((rows, 1), F32), jnp.zeros((rows, D_HEAD), F32))
    m, l, acc = lax.fori_loop(c_lo, c_hi, body, init)
    return acc / jnp.maximum(l, TINY)


def _moba_p_kernel(q_ref, k_ref, v_ref, o_ref):
    qi = pl.program_id(2)
    tq = MOBA_BLOCK
    nblk = k_ref.shape[1] // MOBA_BLOCK
    q2 = q_ref[0]
    kmean = jnp.mean(k_ref[0].reshape(nblk, MOBA_BLOCK, LANES), axis=1)
    kmean = jnp.concatenate([kmean, jnp.zeros((LANES - nblk, LANES), F32)], axis=0)
    lane = _iota((tq, LANES), 1)
    klane = _iota((LANES, LANES), 1)
    causal = _iota((tq, tq), 0) >= _iota((tq, tq), 1)
    outs = []
    for hh in range(LANES // D_HEAD):
        km = jnp.where((klane >= hh * D_HEAD) & (klane < (hh + 1) * D_HEAD), kmean, 0.0)
        gate = _dot_nt(q2, km, precision=HIGHEST)
        gate = jnp.where(lane < qi, gate, NEG_INF)
        sel = (_rank_select(gate, nblk, MOBA_TOPK) & (lane < qi)).astype(F32)
        qh = (q2[:, hh * D_HEAD:(hh + 1) * D_HEAD] * SCALE).astype(BF16)

        def mask_fn(c, sel=sel):
            selcol = jnp.sum(jnp.where(lane == c, sel, 0.0), axis=1, keepdims=True) > 0.5
            return (causal & (c == qi)) | (selcol & (c != qi))

        outs.append(_flash(qh, k_ref, v_ref, hh * D_HEAD, 0, qi + 1, MOBA_BLOCK, mask_fn, 1))
    o_ref[0] = jnp.concatenate(outs, axis=1)


def _moba_p(qk, v, batch, seq):
    qk3 = qk.reshape(batch, seq, 2 * MOBA_WIDTH)
    v3 = v.reshape(batch, seq, MOBA_WIDTH)
    pairs = MOBA_WIDTH // LANES
    out = pl.pallas_call(
        _moba_p_kernel,
        grid=(batch, pairs, seq // MOBA_BLOCK),
        in_specs=[pl.BlockSpec((1, MOBA_BLOCK, LANES), lambda b, h, i: (b, i, h)),
                  pl.BlockSpec((1, seq, LANES), lambda b, h, i: (b, 0, pairs + h)),
                  pl.BlockSpec((1, seq, LANES), lambda b, h, i: (b, 0, h))],
        out_specs=pl.BlockSpec((1, MOBA_BLOCK, LANES), lambda b, h, i: (b, i, h)),
        out_shape=jax.ShapeDtypeStruct((batch, seq, MOBA_WIDTH), F32),
        compiler_params=_params("arbitrary", "arbitrary", "arbitrary"),
        name="moba_prompt",
    )(qk3, qk3, v3)
    return out.reshape(batch * seq, MOBA_WIDTH)


def _gelu(x):
    return 0.5 * x * (1.0 + jnp.tanh(0.7978845608028654 * (x + 0.044715 * x * x * x)))


def _compress_p_kernel(seg_ref, pea_ref, peb_ref, wa_ref, wb_ref, w2_ref, o_ref):
    seg = seg_ref[0]
    a = _dot((seg + pea_ref[...]).astype(BF16), wa_ref[...])
    b = _dot((seg + peb_ref[...]).astype(BF16), wb_ref[...])
    nseg = seg.shape[0]
    hid = a + pltpu.roll(b, nseg - 1, 0)
    o_ref[0] = _dot(_gelu(hid).astype(BF16), w2_ref[...])


def _compress_weights(pe, w1, w2):
    g = NSA_KV_HEADS
    eye = jnp.eye(g, dtype=F32)
    w1r = w1.reshape(CMP_LEN, D_HEAD, CMP_HIDDEN)

    def half(lo):
        w = w1r[lo:lo + CMP_STRIDE]
        wbd = jnp.einsum('ldf,gh->lgdhf', w, eye)
        p = jnp.broadcast_to(pe[lo:lo + CMP_STRIDE, None, :], (CMP_STRIDE, g, D_HEAD))
        return wbd.reshape(CMP_STRIDE, g * D_HEAD, g * CMP_HIDDEN).astype(BF16), p.reshape(CMP_STRIDE, g * D_HEAD)

    wa, pea = half(0)
    wb, peb = half(CMP_STRIDE)
    w2bd = jnp.einsum('fd,gh->gfhd', w2, eye).reshape(g * CMP_HIDDEN, g * D_HEAD).astype(BF16)
    return wa, wb, pea, peb, w2bd


def _compress_p(rows, cw, batch, seq):
    wa, wb, pea, peb, w2bd = cw
    nseg = seq // CMP_STRIDE
    width = CMP_STRIDE * NSA_KV_WIDTH
    seg = rows.reshape(batch, nseg, width)
    full = lambda b: (0, 0)
    return pl.pallas_call(
        _compress_p_kernel,
        grid=(batch,),
        in_specs=[pl.BlockSpec((1, nseg, width), lambda b: (b, 0, 0)),
                  pl.BlockSpec((1, width), full), pl.BlockSpec((1, width), full),
                  pl.BlockSpec((width, NSA_KV_HEADS * CMP_HIDDEN), full),
                  pl.BlockSpec((width, NSA_KV_HEADS * CMP_HIDDEN), full),
                  pl.BlockSpec((NSA_KV_HEADS * CMP_HIDDEN, NSA_KV_WIDTH), full)],
        out_specs=pl.BlockSpec((1, nseg, NSA_KV_WIDTH), lambda b: (b, 0, 0)),
        out_shape=jax.ShapeDtypeStruct((batch, nseg, NSA_KV_WIDTH), F32),
        compiler_params=_params("arbitrary"),
        name="compress_prompt",
    )(seg, pea.reshape(1, width), peb.reshape(1, width), wa.reshape(width, -1), wb.reshape(width, -1), w2bd)


NSA_TQ = 128
NSA_TK = 256


def _overlap_matrix(nc_pad, nslc_pad):
    cs = jnp.arange(nc_pad)[:, None] * CMP_STRIDE
    ss = jnp.arange(nslc_pad)[None, :] * SLC_BLOCK
    return ((cs < ss + SLC_BLOCK) & (cs + CMP_LEN > ss)).astype(F32)


def _nsa_p_kernel(q_ref, kc_ref, vc_ref, ks_ref, vs_ref, kw_ref, vw_ref, gate_ref, ov_ref, o_ref):
    tq, tk = NSA_TQ, NSA_TK
    q0 = pl.program_id(1) * tq
    nc_pad = kc_ref.shape[1]
    pos = q0 + _iota((tq, 1), 0)
    gates = _sigmoid(gate_ref[0])
    lane = _iota((tq, LANES), 1)
    kcol = _iota((tq, tk), 1)
    ecol = _iota((LANES, tk), 1)
    erow = _iota((LANES, tk), 0)
    n_i = _iota((tq, nc_pad), 1)
    valid_c = (n_i < nc_pad - 1) & (n_i * CMP_STRIDE + (CMP_LEN - 1) <= pos)
    valid_c4 = jnp.broadcast_to(valid_c[None], (NSA_GROUP, tq, nc_pad)).reshape(NSA_GROUP * tq, nc_pad)
    q_blk = pos >> 6
    c_end = (q0 + tq - 1) // tk + 1
    c_win = jnp.maximum(q0 - WINDOW, 0) // tk

    for g in range(NSA_KV_HEADS):
        lane0 = g * D_HEAD
        qs = jnp.concatenate(
            [q_ref[0, :, (g * NSA_GROUP + r) * D_HEAD:(g * NSA_GROUP + r + 1) * D_HEAD] * SCALE
             for r in range(NSA_GROUP)], axis=0).astype(BF16)
        kc = kc_ref[0, :, lane0:lane0 + D_HEAD].astype(BF16)
        vc = vc_ref[0, :, lane0:lane0 + D_HEAD].astype(BF16)
        s = jnp.where(valid_c4, _dot_nt(qs, kc), NEG_INF)
        m = jnp.max(s, axis=-1, keepdims=True)
        p = jnp.where(valid_c4, jnp.exp(s - m), 0.0)
        p = p / jnp.maximum(jnp.sum(p, axis=-1, keepdims=True), TINY)
        o_c = _dot(p.astype(BF16), vc)
        psum = jnp.sum(p.reshape(NSA_GROUP, tq, nc_pad), axis=0)
        imp = jnp.dot(psum, ov_ref[...], precision=HIGHEST, preferred_element_type=F32)
        imp = jnp.where(lane == q_blk, BIG, jnp.where(lane < q_blk, imp, NEG_INF))
        nslc = ks_ref.shape[1] // SLC_BLOCK
        sel = (_rank_select(imp, nslc, SLC_TOPN) & (lane <= q_blk)).astype(BF16)

        def mask_slc(c, sel=sel):
            expand = (((c * tk + ecol) >> 6) == erow).astype(BF16)
            return (_dot(sel, expand) > 0.5) & (c * tk + kcol <= pos)

        o_s = _flash(qs, ks_ref, vs_ref, lane0, 0, c_end, tk, mask_slc, NSA_GROUP)

        def mask_win(c):
            dist = pos - (c * tk + kcol)
            return (dist >= 0) & (dist <= WINDOW)

        o_w = _flash(qs, kw_ref, vw_ref, lane0, c_win, c_end, tk, mask_win, NSA_GROUP)

        for r in range(NSA_GROUP):
            hd = g * NSA_GROUP + r
            rs = slice(r * tq, (r + 1) * tq)
            o = (gates[:, 3 * hd:3 * hd + 1] * o_c[rs] + gates[:, 3 * hd + 1:3 * hd + 2] * o_s[rs]
                 + gates[:, 3 * hd + 2:3 * hd + 3] * o_w[rs])
            o_ref[0, :, hd * D_HEAD:(hd + 1) * D_HEAD] = o


def _nsa_p(q, kn, vn, kcmp, vcmp, gate, batch, seq):
    q3 = q.reshape(batch, seq, NSA_WIDTH)
    kn3 = kn.reshape(batch, seq, 3 * NSA_KV_WIDTH)
    vn3 = vn.reshape(batch, seq, 3 * NSA_KV_WIDTH)
    g3 = gate.reshape(batch, seq, LANES)
    nc_pad = kcmp.shape[1]
    ov = _overlap_matrix(nc_pad, LANES)
    tile = lambda b, t: (b, t, 0)
    cmp_spec = pl.BlockSpec((1, nc_pad, NSA_KV_WIDTH), lambda b, t: (b, 0, 0))
    out = pl.pallas_call(
        _nsa_p_kernel,
        grid=(batch, seq // NSA_TQ),
        in_specs=[pl.BlockSpec((1, NSA_TQ, NSA_WIDTH), tile), cmp_spec, cmp_spec,
                  pl.BlockSpec((1, seq, LANES), lambda b, t: (b, 0, 1)),
                  pl.BlockSpec((1, seq, LANES), lambda b, t: (b, 0, 1)),
                  pl.BlockSpec((1, seq, LANES), lambda b, t: (b, 0, 2)),
                  pl.BlockSpec((1, seq, LANES), lambda b, t: (b, 0, 2)),
                  pl.BlockSpec((1, NSA_TQ, LANES), tile),
                  pl.BlockSpec((nc_pad, LANES), lambda b, t: (0, 0))],
        out_specs=pl.BlockSpec((1, NSA_TQ, NSA_WIDTH), tile),
        out_shape=jax.ShapeDtypeStruct((batch, seq, NSA_WIDTH), F32),
        compiler_params=_params("arbitrary", "arbitrary"),
        name="nsa_prompt",
    )(q3, kcmp, vcmp, kn3, vn3, kn3, vn3, g3, ov)
    return out.reshape(batch * seq, NSA_WIDTH)


def _merge_kernel(oa_ref, ob_ref, mg_ref, x_ref, g1_ref, wa_ref, wb_ref, wo_ref, o_ref):
    a = _dot(oa_ref[...].astype(BF16), wa_ref[...])
    b = _dot(ob_ref[...].astype(BF16), wb_ref[...])
    mix = _sigmoid(mg_ref[:, :D_MODEL]) * a + _sigmoid(mg_ref[:, D_MODEL:]) * b
    o_ref[...] = x_ref[...] + g1_ref[0] * _dot(mix.astype(BF16), wo_ref[...])


def _merge(oa, ob, mg, x, g1, wa, wb, wo, tm):
    n = x.shape[0]
    nb, r, _ = g1.shape
    tiles_per_b = (n // nb) // tm
    row = lambda i: (i, 0)
    full = lambda i: (0, 0)
    return pl.pallas_call(
        _merge_kernel,
        grid=(n // tm,),
        in_specs=[pl.BlockSpec((tm, MOBA_WIDTH), row), pl.BlockSpec((tm, NSA_WIDTH), row),
                  pl.BlockSpec((tm, 2 * D_MODEL), row), pl.BlockSpec((tm, D_MODEL), row),
                  pl.BlockSpec((1, r, D_MODEL), lambda i: (i // tiles_per_b, 0, 0)),
                  pl.BlockSpec((MOBA_WIDTH, D_MODEL), full), pl.BlockSpec((NSA_WIDTH, D_MODEL), full),
                  pl.BlockSpec((D_MODEL, D_MODEL), full)],
        out_specs=pl.BlockSpec((tm, D_MODEL), row),
        out_shape=jax.ShapeDtypeStruct((n, D_MODEL), F32),
        compiler_params=_params("arbitrary"),
        name="merge",
    )(oa, ob, mg, x, g1, wa, wb, wo)


def _route(logits):
    lane = _iota(logits.shape, 1)
    is_grp = lane < N_GROUPS
    lg = jnp.where(is_grp, logits, NEG_INF)
    mg = jnp.max(lg, axis=-1, keepdims=True)
    pg = jnp.where(is_grp, jnp.exp(lg - mg), 0.0)
    pg = pg / jnp.sum(pg, axis=-1, keepdims=True)
    g_w = jnp.max(pg, axis=-1, keepdims=True)
    g_sel = jnp.min(jnp.where(is_grp & (pg == g_w), lane, LANES), axis=-1, keepdims=True)
    e_lane = lane - N_GROUPS
    in_grp = (e_lane >= 0) & (e_lane < N_EXPERTS) & ((e_lane >> 3) == g_sel)
    le = jnp.where(in_grp, logits, NEG_INF)
    me = jnp.max(le, axis=-1, keepdims=True)
    pe = jnp.where(in_grp, jnp.exp(le - me), 0.0)
    pe = pe / jnp.sum(pe, axis=-1, keepdims=True)
    v1 = jnp.max(pe, axis=-1, keepdims=True)
    i1 = jnp.min(jnp.where(in_grp & (pe == v1), lane, LANES), axis=-1, keepdims=True)
    rest = in_grp & (lane != i1)
    pr = jnp.where(rest, pe, -1.0)
    v2 = jnp.max(pr, axis=-1, keepdims=True)
    i2 = jnp.min(jnp.where(rest & (pr == v2), lane, LANES), axis=-1, keepdims=True)
    tot = v1 + v2
    comb = jnp.where(lane == i1, v1 / tot, 0.0) + jnp.where(lane == i2, v2 / tot, 0.0)
    comb = comb * g_w
    return pltpu.roll(comb, LANES - N_GROUPS, 1)


def _moe_kernel(x_ref, sc_ref, sh_ref, g2_ref, gn_ref, gf_ref, wr_ref, win_ref, wout_ref, o_ref,
                h_scr, comb_scr, acc_scr):
    e = pl.program_id(1)

    @pl.when(e == 0)
    def _():
        h = _norm_mod(x_ref[...], gn_ref[...], sc_ref[0], sh_ref[0])
        hb = h.astype(BF16)
        h_scr[...] = hb
        logits = jnp.dot(h, wr_ref[...], precision=HIGHEST, preferred_element_type=F32)
        comb_scr[...] = _route(logits)
        acc_scr[...] = jnp.zeros_like(acc_scr)

    hid = _dot(h_scr[...], win_ref[0])
    a = hid[:, :D_EXPERT]
    b = hid[:, D_EXPERT:]
    comb = comb_scr[...]
    w = jnp.sum(jnp.where(_iota(comb.shape, 1) == e, comb, 0.0), axis=1, keepdims=True)
    act = a * _sigmoid(a) * b * w
    acc_scr[...] += _dot(act.astype(BF16), wout_ref[0])

    @pl.when(e == N_EXPERTS - 1)
    def _():
        y = x_ref[...] + g2_ref[0] * acc_scr[...]
        ms = jnp.mean(y * y, axis=-1, keepdims=True)
        o_ref[...] = y * lax.rsqrt(ms + EPS) * gf_ref[...]


def _moe(x, sc, sh, g2, gn, gf, wr, w_ein, w_eout, tm):
    n = x.shape[0]
    nb, r, _ = sc.shape
    tiles_per_b = (n // nb) // tm
    row = lambda i, e: (i, 0)
    mod = lambda i, e: (i // tiles_per_b, 0, 0)
    full = lambda i, e: (0, 0)
    return pl.pallas_call(
        _moe_kernel,
        grid=(n // tm, N_EXPERTS),
        in_specs=[pl.BlockSpec((tm, D_MODEL), row),
                  pl.BlockSpec((1, r, D_MODEL), mod), pl.BlockSpec((1, r, D_MODEL), mod),
                  pl.BlockSpec((1, r, D_MODEL), mod),
                  pl.BlockSpec((1, D_MODEL), full), pl.BlockSpec((1, D_MODEL), full),
                  pl.BlockSpec((D_MODEL, LANES), full),
                  pl.BlockSpec((1, D_MODEL, 2 * D_EXPERT), lambda i, e: (e, 0, 0)),
                  pl.BlockSpec((1, D_EXPERT, D_MODEL), lambda i, e: (e, 0, 0))],
        out_specs=pl.BlockSpec((tm, D_MODEL), row),
        out_shape=jax.ShapeDtypeStruct((n, D_MODEL), F32),
        scratch_shapes=[pltpu.VMEM((tm, D_MODEL), BF16), pltpu.VMEM((tm, LANES), F32),
                        pltpu.VMEM((tm, D_MODEL), F32)],
        compiler_params=_params("arbitrary", "arbitrary"),
        name="moe",
    )(x, sc, sh, g2, gn, gf, wr, w_ein, w_eout)


PAGE = 128
MOBA_PAGES_PER_STEP = 4
NSA_PAGES_PER_STEP = 8


def _page_view(cache):
    n_phys, page, heads, dh = cache.shape
    return jnp.transpose(cache, (0, 2, 3, 1)).reshape(n_phys, heads * dh, page)


def _page_specs(rows, n_pages, per_step):
    def spec(u):
        return pl.BlockSpec((1, rows, PAGE), lambda b, s, pt: (pt[b * n_pages + s * per_step + u], 0, 0))
    return [spec(u) for u in range(per_step)]


def _head_diag(full, heads):
    rows = full.shape[0]
    head = _iota((rows, D_HEAD), 0) // (rows // heads)
    out = jnp.zeros((rows, D_HEAD), F32)
    for h in range(heads):
        out = out + jnp.where(head == h, full[:, h * D_HEAD:(h + 1) * D_HEAD], 0.0)
    return out


def _moba_s_kernel(pt_ref, qbd_ref, *refs):
    per = MOBA_PAGES_PER_STEP
    k_refs, v_refs = refs[:per], refs[per:2 * per]
    m_ref, l_ref, ks_ref, o_ref = refs[2 * per:]
    s = pl.program_id(1)

    @pl.when(s == 0)
    def _():
        m_ref[...] = jnp.zeros_like(m_ref)
        l_ref[...] = jnp.zeros_like(l_ref)
        ks_ref[...] = jnp.zeros_like(ks_ref)

    qbd = qbd_ref[0]
    lane_q = _iota(m_ref.shape[1:], 1)
    lane_k = _iota(ks_ref.shape[1:], 1)
    for u in range(per):
        page = s * per + u
        kt = k_refs[u][0]
        vt = v_refs[u][0]
        sc = _dot(qbd, kt.astype(BF16))
        m = jnp.max(sc, axis=-1, keepdims=True)
        p = jnp.exp(sc - m)
        l = jnp.sum(p, axis=-1, keepdims=True)
        o_ref[0, u] = _head_diag(_dot_nt(p.astype(BF16), vt.astype(BF16)), MOBA_HEADS)
        ksum = jnp.sum(kt, axis=-1, keepdims=True)
        m_ref[0] = jnp.where(lane_q == page, m, m_ref[0])
        l_ref[0] = jnp.where(lane_q == page, l, l_ref[0])
        ks_ref[0] = jnp.where(lane_k == page, ksum, ks_ref[0])


def _moba_s_pass(pt_flat, qbd, kt_pages, vt_pages, batch, n_pages):
    per = MOBA_PAGES_PER_STEP
    rows = qbd.shape[1]
    stat = lambda b, s, pt: (b, 0, 0)
    return pl.pallas_call(
        _moba_s_kernel,
        grid_spec=pltpu.PrefetchScalarGridSpec(
            num_scalar_prefetch=1,
            grid=(batch, n_pages // per),
            in_specs=[pl.BlockSpec((1, rows, MOBA_WIDTH), stat)]
            + _page_specs(MOBA_WIDTH, n_pages, per) + _page_specs(MOBA_WIDTH, n_pages, per),
            out_specs=[pl.BlockSpec((1, rows, LANES), stat), pl.BlockSpec((1, rows, LANES), stat),
                       pl.BlockSpec((1, MOBA_WIDTH, LANES), stat),
                       pl.BlockSpec((1, per, rows, D_HEAD), lambda b, s, pt: (b, s, 0, 0))]),
        out_shape=[jax.ShapeDtypeStruct((batch, rows, LANES), F32), jax.ShapeDtypeStruct((batch, rows, LANES), F32),
                   jax.ShapeDtypeStruct((batch, MOBA_WIDTH, LANES), F32),
                   jax.ShapeDtypeStruct((batch, n_pages, rows, D_HEAD), F32)],
        compiler_params=_params("arbitrary", "arbitrary"),
        name="moba_decode_pages",
    )(pt_flat, qbd, *([kt_pages] * per), *([vt_pages] * per))


def _moba_s_combine_kernel(m_ref, l_ref, ks_ref, o_ref, qf_ref, qbd_ref, kn_ref, vn_ref, out_ref, *, n_pages, ts):
    rows = m_ref.shape[1]
    pages_per_blk = MOBA_BLOCK // PAGE
    nblk = n_pages // pages_per_blk
    lane = _iota((rows, LANES), 1)
    pr = _iota((LANES, LANES), 0)
    pc = _iota((LANES, LANES), 1)
    page_to_blk = ((pr // pages_per_blk) == pc).astype(F32)
    blk_to_page = (pr == (pc // pages_per_blk)).astype(BF16)
    kmean = jnp.dot(ks_ref[0], page_to_blk, precision=HIGHEST, preferred_element_type=F32) * (1.0 / MOBA_BLOCK)
    gate = jnp.dot(qf_ref[0], kmean, precision=HIGHEST, preferred_element_type=F32)
    gate = jnp.where(lane < nblk, gate, NEG_INF)
    sel = (_rank_select(gate, nblk, MOBA_TOPK) & (lane < nblk)).astype(BF16)
    selp = _dot(sel, blk_to_page) > 0.5
    qbd = qbd_ref[0]
    s_own = _dot_nt(qbd, kn_ref[0].astype(BF16))
    valid_own = lane <= (_iota((rows, LANES), 0) % ts)
    s_own = jnp.where(valid_own, s_own, NEG_INF)
    m_all = jnp.where(selp, m_ref[0], NEG_INF)
    big_m = jnp.maximum(jnp.max(m_all, axis=-1, keepdims=True), jnp.max(s_own, axis=-1, keepdims=True))
    wgt = jnp.where(selp, jnp.exp(m_ref[0] - big_m), 0.0)
    p_own = jnp.where(valid_own, jnp.exp(s_own - big_m), 0.0)
    denom = jnp.sum(wgt * l_ref[0], axis=-1, keepdims=True) + jnp.sum(p_own, axis=-1, keepdims=True)
    num = _head_diag(_dot(p_own.astype(BF16), vn_ref[0].astype(BF16)), MOBA_HEADS)
    for p in range(n_pages):
        num = num + wgt[:, p:p + 1] * o_ref[0, p]
    out_ref[0] = num / jnp.maximum(denom, TINY)


def _moba_s_combine(m, l, ks, o, qf, qbd, kn, vn, n_pages, ts):
    batch, rows, _ = m.shape
    b3 = lambda b: (b, 0, 0)
    return pl.pallas_call(
        functools.partial(_moba_s_combine_kernel, n_pages=n_pages, ts=ts),
        grid=(batch,),
        in_specs=[pl.BlockSpec((1, rows, LANES), b3), pl.BlockSpec((1, rows, LANES), b3),
                  pl.BlockSpec((1, MOBA_WIDTH, LANES), b3),
                  pl.BlockSpec((1, n_pages, rows, D_HEAD), lambda b: (b, 0, 0, 0)),
                  pl.BlockSpec((1, rows, MOBA_WIDTH), b3), pl.BlockSpec((1, rows, MOBA_WIDTH), b3),
                  pl.BlockSpec((1, LANES, MOBA_WIDTH), b3), pl.BlockSpec((1, LANES, MOBA_WIDTH), b3)],
        out_specs=pl.BlockSpec((1, rows, D_HEAD), b3),
        out_shape=jax.ShapeDtypeStruct((batch, rows, D_HEAD), F32),
        compiler_params=_params("arbitrary"),
        name="moba_decode_combine",
    )(m, l, ks, o, qf, qbd, kn, vn)


def _block_diag_q(q, batch, ts, heads):
    q4 = q.reshape(batch, ts, heads, D_HEAD)
    eye = jnp.eye(heads, dtype=q.dtype)
    return jnp.einsum('bchd,hk->bhckd', q4, eye).reshape(batch, heads * ts, heads * D_HEAD)


def _pad_rows(a, batch, ts):
    a3 = a.reshape(batch, ts, a.shape[-1])
    return jnp.pad(a3, ((0, 0), (0, LANES - ts), (0, 0)))


def _compress_s_kernel(pt_ref, *refs):
    per = NSA_PAGES_PER_STEP
    page_refs = refs[:per]
    pea_ref, peb_ref, wa_ref, wb_ref, w2_ref, o_ref, x_scr, a_scr, b_scr = refs[per:]
    s = pl.program_id(1)
    nseg = per * PAGE // CMP_STRIDE
    for u in range(per):
        x_scr[u * PAGE:(u + 1) * PAGE, :] = page_refs[u][0].T
    a = jnp.zeros((nseg, a_scr.shape[1]), F32)
    b = jnp.zeros((nseg, b_scr.shape[1]), F32)
    for l in range(CMP_STRIDE):
        xl = x_scr[pl.ds(l, nseg, stride=CMP_STRIDE), :]
        a = a + _dot((xl + pea_ref[l:l + 1, :]).astype(BF16), wa_ref[l])
        b = b + _dot((xl + peb_ref[l:l + 1, :]).astype(BF16), wb_ref[l])
    a_scr[pl.ds(pl.multiple_of(s * nseg, nseg), nseg), :] = a
    b_scr[pl.ds(pl.multiple_of(s * nseg, nseg), nseg), :] = b

    @pl.when(s == pl.num_programs(1) - 1)
    def _():
        total = a_scr.shape[0]
        hid = a_scr[...] + pltpu.roll(b_scr[...], total - 1, 0)
        o_ref[0] = _dot(_gelu(hid).astype(BF16), w2_ref[...])


def _compress_s(pt_flat, pages, cw, batch, n_pages):
    wa, wb, pea, peb, w2bd = cw
    per = NSA_PAGES_PER_STEP
    total = n_pages * PAGE // CMP_STRIDE
    hidden = NSA_KV_HEADS * CMP_HIDDEN
    full2 = lambda b, s, pt: (0, 0)
    full3 = lambda b, s, pt: (0, 0, 0)
    return pl.pallas_call(
        _compress_s_kernel,
        grid_spec=pltpu.PrefetchScalarGridSpec(
            num_scalar_prefetch=1,
            grid=(batch, n_pages // per),
            in_specs=_page_specs(NSA_KV_WIDTH, n_pages, per)
            + [pl.BlockSpec((CMP_STRIDE, NSA_KV_WIDTH), full2), pl.BlockSpec((CMP_STRIDE, NSA_KV_WIDTH), full2),
               pl.BlockSpec((CMP_STRIDE, NSA_KV_WIDTH, hidden), full3),
               pl.BlockSpec((CMP_STRIDE, NSA_KV_WIDTH, hidden), full3),
               pl.BlockSpec((hidden, NSA_KV_WIDTH), full2)],
            out_specs=pl.BlockSpec((1, total, NSA_KV_WIDTH), lambda b, s, pt: (b, 0, 0)),
            scratch_shapes=[pltpu.VMEM((per * PAGE, NSA_KV_WIDTH), F32), pltpu.VMEM((total, hidden), F32),
                            pltpu.VMEM((total, hidden), F32)]),
        out_shape=jax.ShapeDtypeStruct((batch, total, NSA_KV_WIDTH), F32),
        compiler_params=_params("arbitrary", "arbitrary"),
        name="compress_decode",
    )(pt_flat, *([pages] * per), pea, peb, wa, wb, w2bd)


def _stack_group_q(q_ref, g):
    return jnp.concatenate(
        [q_ref[0, :, (g * NSA_GROUP + r) * D_HEAD:(g * NSA_GROUP + r + 1) * D_HEAD] * SCALE
         for r in range(NSA_GROUP)], axis=0).astype(BF16)


def _nsa_s_kernel(q_ref, kc_ref, vc_ref, wk_ref, wv_ref, kn_ref, vn_ref, gate_ref, ov_ref, part_ref, sel_ref, *, ts):
    rows = NSA_GROUP * ts
    nc_pad = kc_ref.shape[1]
    wlen = wk_ref.shape[3]
    gates = _sigmoid(gate_ref[0])
    n_i = _iota((rows, nc_pad), 1)
    valid_c = n_i < nc_pad - 1
    c_of_row = _iota((rows, 1), 0) % ts
    valid_w = _iota((rows, wlen), 1) >= c_of_row
    valid_n = _iota((rows, LANES), 1) <= c_of_row
    for g in range(NSA_KV_HEADS):
        lane0 = g * D_HEAD
        qs = _stack_group_q(q_ref, g)
        s = jnp.where(valid_c, _dot_nt(qs, kc_ref[0, :, lane0:lane0 + D_HEAD].astype(BF16)), NEG_INF)
        m = jnp.max(s, axis=-1, keepdims=True)
        p = jnp.where(valid_c, jnp.exp(s - m), 0.0)
        p = p / jnp.maximum(jnp.sum(p, axis=-1, keepdims=True), TINY)
        o_c = _dot(p.astype(BF16), vc_ref[0, :, lane0:lane0 + D_HEAD].astype(BF16))
        psum = jnp.sum(p.reshape(NSA_GROUP, ts, nc_pad), axis=0)
        imp = jnp.dot(psum, ov_ref[...], precision=HIGHEST, preferred_element_type=F32)
        sel_ref[0, g] = _rank_select(imp, LANES, SLC_TOPN - 1).astype(F32)
        s_w = jnp.where(valid_w, _dot(qs, wk_ref[0, g].astype(BF16)), NEG_INF)
        s_n = jnp.where(valid_n, _dot_nt(qs, kn_ref[0, :, lane0:lane0 + D_HEAD].astype(BF16)), NEG_INF)
        m = jnp.maximum(jnp.max(s_w, axis=-1, keepdims=True), jnp.max(s_n, axis=-1, keepdims=True))
        p_w = jnp.where(valid_w, jnp.exp(s_w - m), 0.0)
        p_n = jnp.where(valid_n, jnp.exp(s_n - m), 0.0)
        den = jnp.sum(p_w, axis=-1, keepdims=True) + jnp.sum(p_n, axis=-1, keepdims=True)
        o_w = (_dot_nt(p_w.astype(BF16), wv_ref[0, g].astype(BF16))
               + _dot(p_n.astype(BF16), vn_ref[0, :, lane0:lane0 + D_HEAD].astype(BF16))) / jnp.maximum(den, TINY)
        for r in range(NSA_GROUP):
            hd = g * NSA_GROUP + r
            rs = slice(r * ts, (r + 1) * ts)
            part_ref[0, :, hd * D_HEAD:(hd + 1) * D_HEAD] = (
                gates[:, 3 * hd:3 * hd + 1] * o_c[rs] + gates[:, 3 * hd + 2:3 * hd + 3] * o_w[rs])


def _nsa_s(q3, kcmp, vcmp, wk_t, wv_t, kn_w, vn_w, gate3, ts):
    batch = q3.shape[0]
    nc_pad = kcmp.shape[1]
    wlen = wk_t.shape[3]
    ov = _overlap_matrix(nc_pad, LANES)
    b3 = lambda b: (b, 0, 0)
    b4 = lambda b: (b, 0, 0, 0)
    return pl.pallas_call(
        functools.partial(_nsa_s_kernel, ts=ts),
        grid=(batch,),
        in_specs=[pl.BlockSpec((1, ts, NSA_WIDTH), b3),
                  pl.BlockSpec((1, nc_pad, NSA_KV_WIDTH), b3), pl.BlockSpec((1, nc_pad, NSA_KV_WIDTH), b3),
                  pl.BlockSpec((1, NSA_KV_HEADS, D_HEAD, wlen), b4), pl.BlockSpec((1, NSA_KV_HEADS, D_HEAD, wlen), b4),
                  pl.BlockSpec((1, LANES, NSA_KV_WIDTH), b3), pl.BlockSpec((1, LANES, NSA_KV_WIDTH), b3),
                  pl.BlockSpec((1, ts, LANES), b3),
                  pl.BlockSpec((nc_pad, LANES), lambda b: (0, 0))],
        out_specs=[pl.BlockSpec((1, ts, NSA_WIDTH), b3), pl.BlockSpec((1, NSA_KV_HEADS, ts, LANES), b4)],
        out_shape=[jax.ShapeDtypeStruct((batch, ts, NSA_WIDTH), F32),
                   jax.ShapeDtypeStruct((batch, NSA_KV_HEADS, ts, LANES), F32)],
        compiler_params=_params("arbitrary"),
        name="nsa_decode_cmp_win",
    )(q3, kcmp, vcmp, wk_t, wv_t, kn_w, vn_w, gate3, ov)


def _slc_s_kernel(pt_ref, q_ref, sel_ref, *refs, ts):
    per = NSA_PAGES_PER_STEP
    k_refs, v_refs = refs[:per], refs[per:2 * per]
    kn_ref, vn_ref, gate_ref, part_ref, o_ref, m_scr, l_scr, acc_scr = refs[2 * per:]
    s = pl.program_id(1)
    rows = NSA_GROUP * ts
    blocks_per_page = PAGE // SLC_BLOCK

    @pl.when(s == 0)
    def _():
        m_scr[...] = jnp.full_like(m_scr, NEG_INF)
        l_scr[...] = jnp.zeros_like(l_scr)
        acc_scr[...] = jnp.zeros_like(acc_scr)

    lane_t = _iota((ts, LANES), 1)

    def update(g, sc, msk, pv_fn):
        sc = jnp.where(msk, sc, NEG_INF)
        m_old = m_scr[g]
        m_new = jnp.maximum(m_old, jnp.max(sc, axis=-1, keepdims=True))
        p = jnp.where(msk, jnp.exp(sc - m_new), 0.0)
        alpha = jnp.exp(m_old - m_new)
        l_scr[g] = alpha * l_scr[g] + jnp.sum(p, axis=-1, keepdims=True)
        acc_scr[g] = alpha * acc_scr[g] + pv_fn(p.astype(BF16))
        m_scr[g] = m_new

    qs = [_stack_group_q(q_ref, g) for g in range(NSA_KV_HEADS)]
    for u in range(per):
        page = s * per + u
        kt = k_refs[u][0]
        vt = v_refs[u][0]
        for g in range(NSA_KV_HEADS):
            selg = sel_ref[0, g]
            msk = jnp.zeros((ts, LANES), jnp.bool_)
            for hb in range(blocks_per_page):
                col = jnp.sum(jnp.where(lane_t == page * blocks_per_page + hb, selg, 0.0), axis=-1, keepdims=True)
                msk = msk | ((col > 0.5) & ((lane_t // SLC_BLOCK) == hb))
            msk = jnp.broadcast_to(msk[None], (NSA_GROUP, ts, LANES)).reshape(rows, LANES)
            ktg = kt[g * D_HEAD:(g + 1) * D_HEAD, :].astype(BF16)
            vtg = vt[g * D_HEAD:(g + 1) * D_HEAD, :].astype(BF16)
            update(g, _dot(qs[g], ktg), msk, lambda pb, vtg=vtg: _dot_nt(pb, vtg))

    @pl.when(s == pl.num_programs(1) - 1)
    def _():
        gates = _sigmoid(gate_ref[0])
        valid_n = _iota((rows, LANES), 1) <= (_iota((rows, 1), 0) % ts)
        for g in range(NSA_KV_HEADS):
            lane0 = g * D_HEAD
            kn = kn_ref[0, :, lane0:lane0 + D_HEAD].astype(BF16)
            vn = vn_ref[0, :, lane0:lane0 + D_HEAD].astype(BF16)
            update(g, _dot_nt(qs[g], kn), valid_n, lambda pb, vn=vn: _dot(pb, vn))
            o_s = acc_scr[g] / jnp.maximum(l_scr[g], TINY)
            for r in range(NSA_GROUP):
                hd = g * NSA_GROUP + r
                cols = slice(hd * D_HEAD, (hd + 1) * D_HEAD)
                o_ref[0, :, cols] = part_ref[0, :, cols] + gates[:, 3 * hd + 1:3 * hd + 2] * o_s[r * ts:(r + 1) * ts]


def _slc_s(pt_flat, q3, sel, k_pages, v_pages, kn_s, vn_s, gate3, part, n_pages, ts):
    batch = q3.shape[0]
    per = NSA_PAGES_PER_STEP
    rows = NSA_GROUP * ts
    b3 = lambda b, s, pt: (b, 0, 0)
    return pl.pallas_call(
        functools.partial(_slc_s_kernel, ts=ts),
        grid_spec=pltpu.PrefetchScalarGridSpec(
            num_scalar_prefetch=1,
            grid=(batch, n_pages // per),
            in_specs=[pl.BlockSpec((1, ts, NSA_WIDTH), b3),
                      pl.BlockSpec((1, NSA_KV_HEADS, ts, LANES), lambda b, s, pt: (b, 0, 0, 0))]
            + _page_specs(NSA_KV_WIDTH, n_pages, per) + _page_specs(NSA_KV_WIDTH, n_pages, per)
            + [pl.BlockSpec((1, LANES, NSA_KV_WIDTH), b3), pl.BlockSpec((1, LANES, NSA_KV_WIDTH), b3),
               pl.BlockSpec((1, ts, LANES), b3), pl.BlockSpec((1, ts, NSA_WIDTH), b3)],
            out_specs=pl.BlockSpec((1, ts, NSA_WIDTH), b3),
            scratch_shapes=[pltpu.VMEM((NSA_KV_HEADS, rows, 1), F32), pltpu.VMEM((NSA_KV_HEADS, rows, 1), F32),
                            pltpu.VMEM((NSA_KV_HEADS, rows, D_HEAD), F32)]),
        out_shape=jax.ShapeDtypeStruct((batch, ts, NSA_WIDTH), F32),
        compiler_params=_params("arbitrary", "arbitrary"),
        name="nsa_decode_slc",
    )(pt_flat, q3, sel, *([k_pages] * per), *([v_pages] * per), kn_s, vn_s, gate3, part)


def _prep_weights(w_ada, b_ada, norm_mix_g, w_in, pe_cmp_k, w_cmp_k1, w_cmp_k2, pe_cmp_v, w_cmp_v1, w_cmp_v2,
                  w_br_a, w_br_b, w_out, norm_ffn_g, w_router_grp, w_router_exp, w_expert_in, w_expert_out,
                  norm_final_g):
    wr = jnp.concatenate([w_router_grp, w_router_exp], axis=1)
    wr = jnp.pad(wr, ((0, 0), (0, LANES - wr.shape[1])))
    return dict(
        w_ada=w_ada, b_ada=b_ada.reshape(1, -1), g_mix=norm_mix_g.reshape(1, -1),
        w_in=_reorder_w_in(w_in),
        cmp_k=_compress_weights(pe_cmp_k, w_cmp_k1, w_cmp_k2),
        cmp_v=_compress_weights(pe_cmp_v, w_cmp_v1, w_cmp_v2),
        w_br_a=w_br_a.astype(BF16), w_br_b=w_br_b.astype(BF16), w_out=w_out.astype(BF16),
        g_ffn=norm_ffn_g.reshape(1, -1), wr=wr,
        w_ein=w_expert_in.astype(BF16), w_eout=w_expert_out.astype(BF16),
        g_final=norm_final_g.reshape(1, -1))


def _prompt_layer(x, mod, w, batch, seq):
    sh1, sc1, g1, sh2, sc2, g2 = mod
    cos, sin = _rope_tables(jnp.arange(seq, dtype=jnp.int32))
    qk_a, v_a, q_b, k_n, v_n, gate, mg = _inproj(x, sc1, sh1, w['g_mix'], cos, sin, w['w_in'], 256)
    o_a = _moba_p(qk_a, v_a, batch, seq)
    kcmp = _compress_p(k_n[:, :NSA_KV_WIDTH], w['cmp_k'], batch, seq)
    vcmp = _compress_p(v_n[:, :NSA_KV_WIDTH], w['cmp_v'], batch, seq)
    o_b = _nsa_p(q_b, k_n, v_n, kcmp, vcmp, gate, batch, seq)
    x1 = _merge(o_a, o_b, mg, x, g1, w['w_br_a'], w['w_br_b'], w['w_out'], 256)
    y = _moe(x1, sc2, sh2, g2, w['g_ffn'], w['g_final'], w['wr'], w['w_ein'], w['w_eout'], 512)
    return y, (qk_a, v_a, k_n, v_n)


def _sample_layer(x, mod, w, caches, win_state, page_table, batch, ts):
    sh1, sc1, g1, sh2, sc2, g2 = mod
    moba_k, moba_v, cmp_k, cmp_v, slc_k, slc_v = caches
    win_k, win_v = win_state
    n_pages = page_table.shape[1]
    assert moba_k.shape[1] == PAGE and win_k.shape[1] == WINDOW and ts <= LANES
    n = batch * ts
    pos = n_pages * PAGE + (jnp.arange(n, dtype=jnp.int32) % ts)
    cos, sin = _rope_tables(pos)
    qk_a, v_a, q_b, k_n, v_n, gate, mg = _inproj(x, sc1, sh1, w['g_mix'], cos, sin, w['w_in'], n)
    pt_flat = page_table.reshape(-1)
    kv = NSA_KV_WIDTH
    qf = _block_diag_q(qk_a[:, :MOBA_WIDTH], batch, ts, MOBA_HEADS)
    qbd = (qf * SCALE).astype(BF16)
    m, l, ks, o = _moba_s_pass(pt_flat, qbd, _page_view(moba_k), _page_view(moba_v), batch, n_pages)
    o_a = _moba_s_combine(m, l, ks, o, qf, qbd, _pad_rows(qk_a[:, MOBA_WIDTH:], batch, ts), _pad_rows(v_a, batch, ts),
                          n_pages, ts)
    o_a = o_a.reshape(batch, MOBA_HEADS, ts, D_HEAD).transpose(0, 2, 1, 3).reshape(n, MOBA_WIDTH)
    kcmp = _compress_s(pt_flat, _page_view(cmp_k), w['cmp_k'], batch, n_pages)
    vcmp = _compress_s(pt_flat, _page_view(cmp_v), w['cmp_v'], batch, n_pages)
    q3 = q_b.reshape(batch, ts, NSA_WIDTH)
    gate3 = gate.reshape(batch, ts, LANES)
    part, sel = _nsa_s(q3, kcmp, vcmp, jnp.transpose(win_k, (0, 2, 3, 1)), jnp.transpose(win_v, (0, 2, 3, 1)),
                       _pad_rows(k_n[:, 2 * kv:], batch, ts), _pad_rows(v_n[:, 2 * kv:], batch, ts), gate3, ts)
    o_b = _slc_s(pt_flat, q3, sel, _page_view(slc_k), _page_view(slc_v),
                 _pad_rows(k_n[:, kv:2 * kv], batch, ts), _pad_rows(v_n[:, kv:2 * kv], batch, ts), gate3, part,
                 n_pages, ts).reshape(n, NSA_WIDTH)
    x1 = _merge(o_a, o_b, mg, x, g1, w['w_br_a'], w['w_br_b'], w['w_out'], n)
    y = _moe(x1, sc2, sh2, g2, w['g_ffn'], w['g_final'], w['wr'], w['w_ein'], w['w_eout'], n)
    return y, (qk_a, v_a, k_n, v_n)


def kernel(x_prompt, x_sample, c_prompt, c_sample, cache_moba_k, cache_moba_v, cache_nsa_cmp_k, cache_nsa_cmp_v,
           cache_nsa_slc_k, cache_nsa_slc_v, state_nsa_win_k, state_nsa_win_v, page_table, w_ada, b_ada, norm_mix_g,
           w_in, pe_cmp_k, w_cmp_k1, w_cmp_k2, pe_cmp_v, w_cmp_v1, w_cmp_v2, w_br_a, w_br_b, w_out, norm_ffn_g,
           w_router_grp, w_router_exp, w_expert_in, w_expert_out, norm_final_g):
    bp, tp, _ = x_prompt.shape
    bs, ts, _ = x_sample.shape
    w = _prep_weights(w_ada[0], b_ada[0], norm_mix_g[0], w_in[0], pe_cmp_k[0], w_cmp_k1[0], w_cmp_k2[0], pe_cmp_v[0],
                      w_cmp_v1[0], w_cmp_v2[0], w_br_a[0], w_br_b[0], w_out[0], norm_ffn_g[0], w_router_grp[0],
                      w_router_exp[0], w_expert_in[0], w_expert_out[0], norm_final_g)
    mod = _ada(jnp.concatenate([c_prompt, c_sample], axis=0), w['w_ada'], w['b_ada'])
    mod_p = [m.reshape(bp, 1, D_MODEL) for m in jnp.split(mod[:bp], 6, axis=-1)]
    y_p, (qk_a, v_a, k_n, v_n) = _prompt_layer(x_prompt.reshape(bp * tp, D_MODEL), mod_p, w, bp, tp)

    mod_s = [jnp.repeat(m, ts, axis=0).reshape(1, bs * ts, D_MODEL) for m in jnp.split(mod[bp:], 6, axis=-1)]
    caches = (cache_moba_k[0], cache_moba_v[0], cache_nsa_cmp_k[0], cache_nsa_cmp_v[0], cache_nsa_slc_k[0],
              cache_nsa_slc_v[0])
    y_s, new_s = _sample_layer(x_sample.reshape(bs * ts, D_MODEL), mod_s, w, caches,
                               (state_nsa_win_k[0], state_nsa_win_v[0]), page_table, bs, ts)

    kv = NSA_KV_WIDTH

    def new_rows(new, b, t):
        qk, v, k_n, v_n = new
        rows = lambda a, heads: a.reshape(1, b, t, heads, D_HEAD)
        return (rows(qk[:, MOBA_WIDTH:], MOBA_HEADS), rows(v, MOBA_HEADS),
                rows(k_n[:, :kv], NSA_KV_HEADS), rows(v_n[:, :kv], NSA_KV_HEADS),
                rows(k_n[:, kv:2 * kv], NSA_KV_HEADS), rows(v_n[:, kv:2 * kv], NSA_KV_HEADS),
                rows(k_n[:, 2 * kv:], NSA_KV_HEADS), rows(v_n[:, 2 * kv:], NSA_KV_HEADS))

    wb = state_nsa_win_k.shape[2]
    outs_p = new_rows((qk_a, v_a, k_n, v_n), bp, tp)
    outs_p = outs_p[:6] + (outs_p[6][:, :, tp - wb:], outs_p[7][:, :, tp - wb:])
    outs_s = new_rows(new_s, bs, ts)
    win_k = jnp.concatenate([state_nsa_win_k, outs_s[6]], axis=2)[:, :, ts:]
    win_v = jnp.concatenate([state_nsa_win_v, outs_s[7]], axis=2)[:, :, ts:]
    outs_s = outs_s[:6] + (win_k, win_v)
    return (y_p.reshape(bp, tp, D_MODEL), y_s.reshape(bs, ts, D_MODEL)) + outs_p + outs_s
```

```python
import functools

import jax
import jax.numpy as jnp
from jax import lax
from jax.experimental import pallas as pl
from jax.experimental.pallas import tpu as pltpu

D_MODEL = 1024
D_HEAD = 64
HALF = D_HEAD // 2
MOBA_HEADS = 8
MOBA_BLOCK = 256
MOBA_TOPK = 3
NSA_HEADS = 8
NSA_KV_HEADS = 2
NSA_GROUP = NSA_HEADS // NSA_KV_HEADS
CMP_LEN = 32
CMP_STRIDE = 16
CMP_HIDDEN = 128
SLC_BLOCK = 64
SLC_TOPN = 16
WINDOW = 512
N_GROUPS = 4
EXPERTS_PER_GROUP = 8
N_EXPERTS = N_GROUPS * EXPERTS_PER_GROUP
D_EXPERT = 256
ROPE_THETA = 10000.0
EPS = 1e-6
NEG_INF = -1e30
BIG = 1e30
TINY = 1e-30
MOBA_WIDTH = MOBA_HEADS * D_HEAD
NSA_WIDTH = NSA_HEADS * D_HEAD
NSA_KV_WIDTH = NSA_KV_HEADS * D_HEAD
SCALE = D_HEAD ** -0.5

LANES = 128
VMEM_LIMIT = 48 * 1024 * 1024

F32 = jnp.float32
BF16 = jnp.bfloat16
HIGHEST = lax.Precision.HIGHEST


def _params(*sem):
    return pltpu.CompilerParams(dimension_semantics=sem, vmem_limit_bytes=VMEM_LIMIT)


def _dot(a, b):
    return jnp.dot(a, b, preferred_element_type=F32)


def _dot_nt(a, b, precision=None):
    return lax.dot_general(a, b, (((1,), (1,)), ((), ())), precision=precision,
                           preferred_element_type=F32)


def _sigmoid(x):
    return 1.0 / (1.0 + jnp.exp(-x))


def _iota(shape, dim):
    return lax.broadcasted_iota(jnp.int32, shape, dim)


def _ada_kernel(c_ref, w_ref, b_ref, o_ref):
    c = c_ref[...]
    s = c * _sigmoid(c)
    o_ref[...] = _dot(s.astype(BF16), w_ref[...].astype(BF16)) + b_ref[...]


def _ada(c, w, b):
    n = c.shape[0]
    tn = 1024
    return pl.pallas_call(
        _ada_kernel,
        grid=(w.shape[1] // tn,),
        in_specs=[pl.BlockSpec((n, D_MODEL), lambda j: (0, 0)),
                  pl.BlockSpec((D_MODEL, tn), lambda j: (0, j)),
                  pl.BlockSpec((1, tn), lambda j: (0, j))],
        out_specs=pl.BlockSpec((n, tn), lambda j: (0, j)),
        out_shape=jax.ShapeDtypeStruct((n, w.shape[1]), F32),
        compiler_params=_params("arbitrary"),
        name="ada",
    )(c, w, b)


_IN_GROUPS = ((2 * MOBA_WIDTH, True),
              (MOBA_WIDTH, False),
              (NSA_WIDTH, True),
              (3 * NSA_KV_WIDTH, True),
              (3 * NSA_KV_WIDTH, False),
              (LANES, False),
              (2 * D_MODEL, False))
_IN_COLS_PAD = sum(w for w, _ in _IN_GROUPS)


def _norm_mod(x, g, sc, sh):
    ms = jnp.mean(x * x, axis=-1, keepdims=True)
    y = x * lax.rsqrt(ms + EPS) * g
    return y * (1.0 + sc) + sh


def _inproj_kernel(x_ref, sc_ref, sh_ref, g_ref, cos_ref, sin_ref, w_ref, *out_refs):
    h = _norm_mod(x_ref[...], g_ref[...], sc_ref[0], sh_ref[0]).astype(BF16)
    cos = cos_ref[...]
    sin = sin_ref[...]
    first_half = (_iota(cos.shape, 1) & (D_HEAD - 1)) < HALF

    def rope(y):
        rot = jnp.where(first_half, pltpu.roll(y, LANES - HALF, 1), pltpu.roll(y, HALF, 1))
        return y * cos + rot * sin

    col = 0
    for out_ref, (width, rotary) in zip(out_refs, _IN_GROUPS):
        chunk = min(width, 512)
        for c in range(0, width, chunk):
            cw = min(chunk, width - c)
            y = _dot(h, w_ref[:, col + c:col + c + cw])
            for s in range(0, cw, LANES):
                piece = y[:, s:s + LANES]
                out_ref[:, c + s:c + s + LANES] = rope(piece) if rotary else piece
        col += width


def _inproj(x, sc, sh, g, cos, sin, w, tm):
    n = x.shape[0]
    nb, r, _ = sc.shape
    tiles_per_b = (n // nb) // tm
    tab_tiles = cos.shape[0] // tm
    row = lambda i: (i, 0)
    mod = lambda i: (i // tiles_per_b, 0, 0)
    tab = lambda i: (i % tab_tiles, 0)
    return pl.pallas_call(
        _inproj_kernel,
        grid=(n // tm,),
        in_specs=[pl.BlockSpec((tm, D_MODEL), row),
                  pl.BlockSpec((1, r, D_MODEL), mod),
                  pl.BlockSpec((1, r, D_MODEL), mod),
                  pl.BlockSpec((1, D_MODEL), lambda i: (0, 0)),
                  pl.BlockSpec((tm, LANES), tab),
                  pl.BlockSpec((tm, LANES), tab),
                  pl.BlockSpec((D_MODEL, _IN_COLS_PAD), lambda i: (0, 0))],
        out_specs=[pl.BlockSpec((tm, wd), row) for wd, _ in _IN_GROUPS],
        out_shape=[jax.ShapeDtypeStruct((n, wd), F32) for wd, _ in _IN_GROUPS],
        compiler_params=_params("arbitrary"),
        name="inproj",
    )(x, sc, sh, g, cos, sin, w)


def _reorder_w_in(w_in):
    kv0 = 3 * MOBA_WIDTH + NSA_WIDTH
    kvs = [w_in[:, kv0 + i * NSA_KV_WIDTH:kv0 + (i + 1) * NSA_KV_WIDTH] for i in range(6)]
    g0 = kv0 + 6 * NSA_KV_WIDTH
    ng = 3 * NSA_HEADS
    gate = jnp.pad(w_in[:, g0:g0 + ng], ((0, 0), (0, LANES - ng)))
    parts = [w_in[:, :kv0], kvs[0], kvs[2], kvs[4], kvs[1], kvs[3], kvs[5], gate, w_in[:, g0 + ng:]]
    return jnp.concatenate(parts, axis=1).astype(BF16)


def _rope_tables(pos):
    inv = ROPE_THETA ** (-jnp.arange(HALF, dtype=F32) / HALF)
    ang = pos.astype(F32)[:, None] * inv[None, :]
    cos = jnp.cos(ang)
    sin = jnp.sin(ang)
    cos = jnp.concatenate([cos, cos, cos, cos], axis=1)
    sin = jnp.concatenate([-sin, sin, -sin, sin], axis=1)
    return cos, sin


def _rank_select(score, n_cols, n_keep):
    lane = _iota(score.shape, 1)
    rank = jnp.zeros(score.shape, jnp.int32)
    for jp in range(n_cols):
        col = score[:, jp:jp + 1]
        beats = (col > score) | ((col == score) & (lane > jp))
        rank = rank + beats.astype(jnp.int32)
    return rank < n_keep


def _rank_select_t(score, n_rows, n_keep):
    row = _iota(score.shape, 0)
    rank = jnp.zeros(score.shape, jnp.int32)
    for jp in range(n_rows):
        r = score[jp:jp + 1, :]
        beats = (r > score) | ((r == score) & (row > jp))
        rank = rank + beats.astype(jnp.int32)
    return rank < n_keep


SUBLANES = 8


def _fold_rows(x, op):
    return op(x.reshape(x.shape[0] // SUBLANES, SUBLANES, x.shape[1]), axis=0)


def _attend_t(make_scores, vt_fn, own_c, lo, hi, width, n_chains):
    score_fn = make_scores(None)

    def max_step(c, own, m):
        return tuple(jnp.maximum(mi, _fold_rows(s, jnp.max)) for mi, s in zip(m, score_fn(c, own)))

    m = max_step(own_c, True, tuple(jnp.full((SUBLANES, width), NEG_INF, F32) for _ in range(n_chains)))
    m = lax.fori_loop(lo, hi, lambda c, mm: max_step(c, False, mm), m)
    shifted_fn = make_scores([jnp.max(mi, axis=0, keepdims=True) for mi in m])

    def acc_step(c, own, carry):
        out = []
        for (l, acc), s, vt in zip(carry, shifted_fn(c, own), vt_fn(c)):
            p = jnp.exp(s)
            out.append((l + _fold_rows(p, jnp.sum), acc + _dot(vt, p.astype(BF16))))
        return tuple(out)

    zero = tuple((jnp.zeros((SUBLANES, width), F32), jnp.zeros((D_HEAD, width), F32)) for _ in range(n_chains))
    carry = acc_step(own_c, True, zero)
    carry = lax.fori_loop(lo, hi, lambda c, cr: acc_step(c, False, cr), carry)
    return [acc / jnp.maximum(jnp.sum(l, axis=0, keepdims=True), TINY) for l, acc in carry]


AUG_ONE = 0
AUG_BLK = SUBLANES


def _key_aug(n_keys, block, n_blocks):
    lane = _iota((n_keys, D_HEAD), 1)
    blk = _iota((n_keys, D_HEAD), 0) // block
    hot = (lane == AUG_ONE) | ((lane >= AUG_BLK) & (lane < AUG_BLK + n_blocks) & (lane - AUG_BLK == blk))
    return jnp.where(hot, 1.0, 0.0).astype(BF16)


def _query_aug(qt, shift, bias):
    width = qt.shape[1]
    top = jnp.zeros((SUBLANES, width), F32)
    if shift is not None:
        top = jnp.where(_iota((SUBLANES, width), 0) == AUG_ONE, -shift, 0.0)
    parts = [qt, top]
    used = SUBLANES
    if bias is not None:
        parts.append(bias)
        used += bias.shape[0]
    parts.append(jnp.zeros((D_HEAD - used, width), F32))
    return jnp.concatenate(parts, axis=0).astype(BF16)


MOBA_STEP_WIDTH = 512


def _moba_p_kernel(q_ref, k_ref, v_ref, o_ref, ka_scr, vt_scr, km_scr):
    qi = pl.program_id(2)
    tq = tk = MOBA_BLOCK
    seq = k_ref.shape[1]
    nblk = seq // MOBA_BLOCK
    width = q_ref.shape[2]
    nh = width // D_HEAD

    @pl.when(qi == 0)
    def _():
        kf = k_ref[0]
        km_scr[...] = jnp.mean(kf.reshape(nblk, MOBA_BLOCK, width), axis=1)
        aug = _key_aug(seq, MOBA_BLOCK, nblk)
        for hh in range(nh):
            ka_scr[hh] = jnp.concatenate([kf[:, hh * D_HEAD:(hh + 1) * D_HEAD].astype(BF16), aug], axis=1)
        for j in range(nblk):
            vt_scr[j] = v_ref[0, j * tk:(j + 1) * tk, :].T.astype(BF16)

    q2t = q_ref[0].T
    km = km_scr[...]
    klane = _iota(km.shape, 1)
    blk = _iota((nblk, tq), 0)
    causal_t = _iota((tk, tq), 0) <= _iota((tk, tq), 1)
    qts, biases = [], []
    for hh in range(nh):
        kmh = jnp.where((klane >= hh * D_HEAD) & (klane < (hh + 1) * D_HEAD), km, 0.0)
        gate = jnp.dot(kmh, q2t, precision=HIGHEST, preferred_element_type=F32)
        gate = jnp.where(blk < qi, gate, NEG_INF)
        keep = _rank_select_t(gate, nblk, MOBA_TOPK) & (blk < qi)
        biases.append(jnp.where(keep, 0.0, NEG_INF))
        qts.append(q2t[hh * D_HEAD:(hh + 1) * D_HEAD, :] * SCALE)

    def make_scores(shift):
        sh = [None] * nh if shift is None else shift
        q_own = [_query_aug(qts[hh], sh[hh], None) for hh in range(nh)]
        q_past = [_query_aug(qts[hh], sh[hh], biases[hh]) for hh in range(nh)]

        def score_fn(c, own):
            out = []
            for hh in range(nh):
                kj = ka_scr[hh, pl.ds(pl.multiple_of(c * tk, tk), tk), :]
                s = _dot(kj, q_own[hh] if own else q_past[hh])
                out.append(jnp.where(causal_t, s, NEG_INF) if own else s)
            return out

        return score_fn

    def values(c):
        return [vt_scr[c, hh * D_HEAD:(hh + 1) * D_HEAD, :] for hh in range(nh)]

    o_ref[0] = jnp.concatenate(_attend_t(make_scores, values, qi, 0, qi, tq, nh), axis=0).T


def _moba_p(qk, v, batch, seq):
    qk3 = qk.reshape(batch, seq, 2 * MOBA_WIDTH)
    v3 = v.reshape(batch, seq, MOBA_WIDTH)
    sw = MOBA_STEP_WIDTH
    pairs = MOBA_WIDTH // sw
    nblk = seq // MOBA_BLOCK
    out = pl.pallas_call(
        _moba_p_kernel,
        grid=(batch, pairs, nblk),
        in_specs=[pl.BlockSpec((1, MOBA_BLOCK, sw), lambda b, h, i: (b, i, h)),
                  pl.BlockSpec((1, seq, sw), lambda b, h, i: (b, 0, pairs + h)),
                  pl.BlockSpec((1, seq, sw), lambda b, h, i: (b, 0, h))],
        out_specs=pl.BlockSpec((1, MOBA_BLOCK, sw), lambda b, h, i: (b, i, h)),
        out_shape=jax.ShapeDtypeStruct((batch, seq, MOBA_WIDTH), F32),
        scratch_shapes=[pltpu.VMEM((sw // D_HEAD, seq, 2 * D_HEAD), BF16), pltpu.VMEM((nblk, sw, MOBA_BLOCK), BF16),
                        pltpu.VMEM((nblk, sw), F32)],
        compiler_params=_params("arbitrary", "arbitrary", "arbitrary"),
        name="moba_prompt",
    )(qk3, qk3, v3)
    return out.reshape(batch * seq, MOBA_WIDTH)


def _gelu(x):
    return 0.5 * x * (1.0 + jnp.tanh(0.7978845608028654 * (x + 0.044715 * x * x * x)))


def _compress_p_kernel(seg_ref, pea_ref, peb_ref, wa_ref, wb_ref, w2_ref, o_ref):
    seg = seg_ref[0]
    a = _dot((seg + pea_ref[...]).astype(BF16), wa_ref[...])
    b = _dot((seg + peb_ref[...]).astype(BF16), wb_ref[...])
    nseg = seg.shape[0]
    hid = a + pltpu.roll(b, nseg - 1, 0)
    o_ref[0] = _dot(_gelu(hid).astype(BF16), w2_ref[...])


def _compress_weights(pe, w1, w2):
    g = NSA_KV_HEADS
    eye = jnp.eye(g, dtype=F32)
    w1r = w1.reshape(CMP_LEN, D_HEAD, CMP_HIDDEN)

    def half(lo):
        w = w1r[lo:lo + CMP_STRIDE]
        wbd = jnp.einsum('ldf,gh->lgdhf', w, eye)
        p = jnp.broadcast_to(pe[lo:lo + CMP_STRIDE, None, :], (CMP_STRIDE, g, D_HEAD))
        return wbd.reshape(CMP_STRIDE, g * D_HEAD, g * CMP_HIDDEN).astype(BF16), p.reshape(CMP_STRIDE, g * D_HEAD)

    wa, pea = half(0)
    wb, peb = half(CMP_STRIDE)
    w2bd = jnp.einsum('fd,gh->gfhd', w2, eye).reshape(g * CMP_HIDDEN, g * D_HEAD).astype(BF16)
    return wa, wb, pea, peb, w2bd


def _compress_p(rows, cw, batch, seq):
    wa, wb, pea, peb, w2bd = cw
    nseg = seq // CMP_STRIDE
    width = CMP_STRIDE * NSA_KV_WIDTH
    seg = rows.reshape(batch, nseg, width)
    full = lambda b: (0, 0)
    return pl.pallas_call(
        _compress_p_kernel,
        grid=(batch,),
        in_specs=[pl.BlockSpec((1, nseg, width), lambda b: (b, 0, 0)),
                  pl.BlockSpec((1, width), full), pl.BlockSpec((1, width), full),
                  pl.BlockSpec((width, NSA_KV_HEADS * CMP_HIDDEN), full),
                  pl.BlockSpec((width, NSA_KV_HEADS * CMP_HIDDEN), full),
                  pl.BlockSpec((NSA_KV_HEADS * CMP_HIDDEN, NSA_KV_WIDTH), full)],
        out_specs=pl.BlockSpec((1, nseg, NSA_KV_WIDTH), lambda b: (b, 0, 0)),
        out_shape=jax.ShapeDtypeStruct((batch, nseg, NSA_KV_WIDTH), F32),
        compiler_params=_params("arbitrary"),
        name="compress_prompt",
    )(seg, pea.reshape(1, width), peb.reshape(1, width), wa.reshape(width, -1), wb.reshape(width, -1), w2bd)


NSA_TQ = 128
NSA_TK = 256


def _overlap_matrix(nc_pad, nslc_pad):
    cs = jnp.arange(nc_pad)[:, None] * CMP_STRIDE
    ss = jnp.arange(nslc_pad)[None, :] * SLC_BLOCK
    return ((cs < ss + SLC_BLOCK) & (cs + CMP_LEN > ss)).astype(F32)


def _nsa_p_kernel(q_ref, kc_ref, vc_ref, ks_ref, vs_ref, kw_ref, vw_ref, gate_ref, ovt_ref, o_ref,
                  ksa_scr, kwa_scr, vst_scr, vwt_scr):
    tq, tk = NSA_TQ, NSA_TK
    t = pl.program_id(1)
    q0 = t * tq
    seq = ks_ref.shape[1]
    nc_pad = kc_ref.shape[1]
    nslc = seq // SLC_BLOCK
    width = NSA_GROUP * tq

    @pl.when(t == 0)
    def _():
        aug_s = _key_aug(seq, SLC_BLOCK, nslc)
        aug_w = _key_aug(seq, SLC_BLOCK, 0)
        for g in range(NSA_KV_HEADS):
            rows = slice(g * D_HEAD, (g + 1) * D_HEAD)
            ksa_scr[g] = jnp.concatenate([ks_ref[0, :, rows].astype(BF16), aug_s], axis=1)
            kwa_scr[g] = jnp.concatenate([kw_ref[0, :, rows].astype(BF16), aug_w], axis=1)
        for j in range(seq // tk):
            vst_scr[j] = vs_ref[0, j * tk:(j + 1) * tk, :].T.astype(BF16)
            vwt_scr[j] = vw_ref[0, j * tk:(j + 1) * tk, :].T.astype(BF16)

    qt_all = q_ref[0].T
    gates_t = _sigmoid(gate_ref[0]).T
    kct = kc_ref[0].astype(BF16)
    vct = vc_ref[0].T.astype(BF16)
    pos = q0 + _iota((1, tq), 1)
    n_i = _iota((nc_pad, tq), 0)
    valid_c = (n_i < nc_pad - 1) & (n_i * CMP_STRIDE + (CMP_LEN - 1) <= pos)
    valid_c4 = jnp.concatenate([valid_c] * NSA_GROUP, axis=1)
    q_blk = pos >> 6
    jrow = _iota((nslc, tq), 0)
    krow = _iota((tk, tq), 0)
    cd = q0 // tk
    c_win = jnp.maximum(q0 - WINDOW, 0) // tk

    def tile(b):
        return jnp.concatenate([b] * NSA_GROUP, axis=1)

    def win_bias(c):
        dist = pos - (c * tk + krow)
        return jnp.where((dist >= 0) & (dist <= WINDOW), 0.0, NEG_INF)

    qts, o_cs, biases = [], [], []
    for g in range(NSA_KV_HEADS):
        rows = slice(g * D_HEAD, (g + 1) * D_HEAD)
        qt = jnp.concatenate(
            [qt_all[(g * NSA_GROUP + r) * D_HEAD:(g * NSA_GROUP + r + 1) * D_HEAD, :] * SCALE
             for r in range(NSA_GROUP)], axis=1)
        qts.append(qt)
        s = jnp.where(valid_c4, _dot(kct[:, rows], qt.astype(BF16)), NEG_INF)
        m = jnp.max(s, axis=0, keepdims=True)
        p = jnp.where(valid_c4, jnp.exp(s - m), 0.0)
        p = p / jnp.maximum(jnp.sum(p, axis=0, keepdims=True), TINY)
        o_cs.append(_dot(vct[rows, :], p.astype(BF16)))
        psum = p[:, :tq]
        for r in range(1, NSA_GROUP):
            psum = psum + p[:, r * tq:(r + 1) * tq]
        imp = jnp.dot(ovt_ref[...], psum, precision=HIGHEST, preferred_element_type=F32)
        imp = jnp.where(jrow == q_blk, BIG, jnp.where(jrow < q_blk, imp, NEG_INF))
        keep = _rank_select_t(imp, nslc, SLC_TOPN) & (jrow <= q_blk)
        biases.append(tile(jnp.where(keep, 0.0, NEG_INF)))

    causal_own = tile(cd * tk + krow <= pos)

    def make_slc_scores(shift):
        sh = [None] * NSA_KV_HEADS if shift is None else shift
        qa = [_query_aug(qts[g], sh[g], biases[g]) for g in range(NSA_KV_HEADS)]

        def score_fn(c, own):
            out = []
            for g in range(NSA_KV_HEADS):
                s = _dot(ksa_scr[g, pl.ds(pl.multiple_of(c * tk, tk), tk), :], qa[g])
                out.append(jnp.where(causal_own, s, NEG_INF) if own else s)
            return out

        return score_fn

    def make_win_scores(shift):
        sh = [None] * NSA_KV_HEADS if shift is None else shift
        qa = [_query_aug(qts[g], sh[g], None) for g in range(NSA_KV_HEADS)]

        def score_fn(c, own):
            wb = tile(win_bias(c))
            return [_dot(kwa_scr[g, pl.ds(pl.multiple_of(c * tk, tk), tk), :], qa[g]) + wb
                    for g in range(NSA_KV_HEADS)]

        return score_fn

    def values(vt_scr):
        return lambda c: [vt_scr[c, g * D_HEAD:(g + 1) * D_HEAD, :] for g in range(NSA_KV_HEADS)]

    o_ss = _attend_t(make_slc_scores, values(vst_scr), cd, 0, cd, width, NSA_KV_HEADS)
    o_ws = _attend_t(make_win_scores, values(vwt_scr), cd, c_win, cd, width, NSA_KV_HEADS)
    outs = []
    for g in range(NSA_KV_HEADS):
        o_c, o_s, o_w = o_cs[g], o_ss[g], o_ws[g]
        for r in range(NSA_GROUP):
            hd = g * NSA_GROUP + r
            cs = slice(r * tq, (r + 1) * tq)
            outs.append(gates_t[3 * hd:3 * hd + 1, :] * o_c[:, cs] + gates_t[3 * hd + 1:3 * hd + 2, :] * o_s[:, cs]
                        + gates_t[3 * hd + 2:3 * hd + 3, :] * o_w[:, cs])
    o_ref[0] = jnp.concatenate(outs, axis=0).T


def _nsa_p(q, kn, vn, kcmp, vcmp, gate, batch, seq):
    q3 = q.reshape(batch, seq, NSA_WIDTH)
    kn3 = kn.reshape(batch, seq, 3 * NSA_KV_WIDTH)
    vn3 = vn.reshape(batch, seq, 3 * NSA_KV_WIDTH)
    g3 = gate.reshape(batch, seq, LANES)
    nc_pad = kcmp.shape[1]
    nslc = seq // SLC_BLOCK
    nchunk = seq // NSA_TK
    ovt = _overlap_matrix(nc_pad, nslc).T
    tile = lambda b, t: (b, t, 0)
    cmp_spec = pl.BlockSpec((1, nc_pad, NSA_KV_WIDTH), lambda b, t: (b, 0, 0))
    out = pl.pallas_call(
        _nsa_p_kernel,
        grid=(batch, seq // NSA_TQ),
        in_specs=[pl.BlockSpec((1, NSA_TQ, NSA_WIDTH), tile), cmp_spec, cmp_spec,
                  pl.BlockSpec((1, seq, LANES), lambda b, t: (b, 0, 1)),
                  pl.BlockSpec((1, seq, LANES), lambda b, t: (b, 0, 1)),
                  pl.BlockSpec((1, seq, LANES), lambda b, t: (b, 0, 2)),
                  pl.BlockSpec((1, seq, LANES), lambda b, t: (b, 0, 2)),
                  pl.BlockSpec((1, NSA_TQ, LANES), tile),
                  pl.BlockSpec((nslc, nc_pad), lambda b, t: (0, 0))],
        out_specs=pl.BlockSpec((1, NSA_TQ, NSA_WIDTH), tile),
        out_shape=jax.ShapeDtypeStruct((batch, seq, NSA_WIDTH), F32),
        scratch_shapes=[pltpu.VMEM((NSA_KV_HEADS, seq, 2 * D_HEAD), BF16),
                        pltpu.VMEM((NSA_KV_HEADS, seq, 2 * D_HEAD), BF16),
                        pltpu.VMEM((nchunk, LANES, NSA_TK), BF16), pltpu.VMEM((nchunk, LANES, NSA_TK), BF16)],
        compiler_params=_params("arbitrary", "arbitrary"),
        name="nsa_prompt",
    )(q3, kcmp, vcmp, kn3, vn3, kn3, vn3, g3, ovt)
    return out.reshape(batch * seq, NSA_WIDTH)


def _merge_kernel(oa_ref, ob_ref, mg_ref, x_ref, g1_ref, wa_ref, wb_ref, wo_ref, o_ref):
    a = _dot(oa_ref[...].astype(BF16), wa_ref[...])
    b = _dot(ob_ref[...].astype(BF16), wb_ref[...])
    mix = _sigmoid(mg_ref[:, :D_MODEL]) * a + _sigmoid(mg_ref[:, D_MODEL:]) * b
    o_ref[...] = x_ref[...] + g1_ref[0] * _dot(mix.astype(BF16), wo_ref[...])


def _merge(oa, ob, mg, x, g1, wa, wb, wo, tm):
    n = x.shape[0]
    nb, r, _ = g1.shape
    tiles_per_b = (n // nb) // tm
    row = lambda i: (i, 0)
    full = lambda i: (0, 0)
    return pl.pallas_call(
        _merge_kernel,
        grid=(n // tm,),
        in_specs=[pl.BlockSpec((tm, MOBA_WIDTH), row), pl.BlockSpec((tm, NSA_WIDTH), row),
                  pl.BlockSpec((tm, 2 * D_MODEL), row), pl.BlockSpec((tm, D_MODEL), row),
                  pl.BlockSpec((1, r, D_MODEL), lambda i: (i // tiles_per_b, 0, 0)),
                  pl.BlockSpec((MOBA_WIDTH, D_MODEL), full), pl.BlockSpec((NSA_WIDTH, D_MODEL), full),
                  pl.BlockSpec((D_MODEL, D_MODEL), full)],
        out_specs=pl.BlockSpec((tm, D_MODEL), row),
        out_shape=jax.ShapeDtypeStruct((n, D_MODEL), F32),
        compiler_params=_params("arbitrary"),
        name="merge",
    )(oa, ob, mg, x, g1, wa, wb, wo)


def _route(logits):
    lane = _iota(logits.shape, 1)
    is_grp = lane < N_GROUPS
    lg = jnp.where(is_grp, logits, NEG_INF)
    mg = jnp.max(lg, axis=-1, keepdims=True)
    pg = jnp.where(is_grp, jnp.exp(lg - mg), 0.0)
    pg = pg / jnp.sum(pg, axis=-1, keepdims=True)
    g_w = jnp.max(pg, axis=-1, keepdims=True)
    g_sel = jnp.min(jnp.where(is_grp & (pg == g_w), lane, LANES), axis=-1, keepdims=True)
    e_lane = lane - N_GROUPS
    in_grp = (e_lane >= 0) & (e_lane < N_EXPERTS) & ((e_lane >> 3) == g_sel)
    le = jnp.where(in_grp, logits, NEG_INF)
    me = jnp.max(le, axis=-1, keepdims=True)
    pe = jnp.where(in_grp, jnp.exp(le - me), 0.0)
    pe = pe / jnp.sum(pe, axis=-1, keepdims=True)
    v1 = jnp.max(pe, axis=-1, keepdims=True)
    i1 = jnp.min(jnp.where(in_grp & (pe == v1), lane, LANES), axis=-1, keepdims=True)
    rest = in_grp & (lane != i1)
    pr = jnp.where(rest, pe, -1.0)
    v2 = jnp.max(pr, axis=-1, keepdims=True)
    i2 = jnp.min(jnp.where(rest & (pr == v2), lane, LANES), axis=-1, keepdims=True)
    tot = v1 + v2
    comb = jnp.where(lane == i1, v1 / tot, 0.0) + jnp.where(lane == i2, v2 / tot, 0.0)
    comb = comb * g_w
    return pltpu.roll(comb, LANES - N_GROUPS, 1)


def _moe_kernel(x_ref, sc_ref, sh_ref, g2_ref, gn_ref, gf_ref, wr_ref, win_ref, wout_ref, o_ref,
                h_scr, comb_scr, acc_scr):
    e = pl.program_id(1)

    @pl.when(e == 0)
    def _():
        h = _norm_mod(x_ref[...], gn_ref[...], sc_ref[0], sh_ref[0])
        hb = h.astype(BF16)
        h_scr[...] = hb
        logits = jnp.dot(h, wr_ref[...], precision=HIGHEST, preferred_element_type=F32)
        comb_scr[...] = _route(logits)
        acc_scr[...] = jnp.zeros_like(acc_scr)

    hid = _dot(h_scr[...], win_ref[0])
    a = hid[:, :D_EXPERT]
    b = hid[:, D_EXPERT:]
    comb = comb_scr[...]
    w = jnp.sum(jnp.where(_iota(comb.shape, 1) == e, comb, 0.0), axis=1, keepdims=True)
    act = a * _sigmoid(a) * b * w
    acc_scr[...] += _dot(act.astype(BF16), wout_ref[0])

    @pl.when(e == N_EXPERTS - 1)
    def _():
        y = x_ref[...] + g2_ref[0] * acc_scr[...]
        ms = jnp.mean(y * y, axis=-1, keepdims=True)
        o_ref[...] = y * lax.rsqrt(ms + EPS) * gf_ref[...]


def _moe(x, sc, sh, g2, gn, gf, wr, w_ein, w_eout, tm):
    n = x.shape[0]
    nb, r, _ = sc.shape
    tiles_per_b = (n // nb) // tm
    row = lambda i, e: (i, 0)
    mod = lambda i, e: (i // tiles_per_b, 0, 0)
    full = lambda i, e: (0, 0)
    return pl.pallas_call(
        _moe_kernel,
        grid=(n // tm, N_EXPERTS),
        in_specs=[pl.BlockSpec((tm, D_MODEL), row),
                  pl.BlockSpec((1, r, D_MODEL), mod), pl.BlockSpec((1, r, D_MODEL), mod),
                  pl.BlockSpec((1, r, D_MODEL), mod),
                  pl.BlockSpec((1, D_MODEL), full), pl.BlockSpec((1, D_MODEL), full),
                  pl.BlockSpec((D_MODEL, LANES), full),
                  pl.BlockSpec((1, D_MODEL, 2 * D_EXPERT), lambda i, e: (e, 0, 0)),
                  pl.BlockSpec((1, D_EXPERT, D_MODEL), lambda i, e: (e, 0, 0))],
        out_specs=pl.BlockSpec((tm, D_MODEL), row),
        out_shape=jax.ShapeDtypeStruct((n, D_MODEL), F32),
        scratch_shapes=[pltpu.VMEM((tm, D_MODEL), BF16), pltpu.VMEM((tm, LANES), F32),
                        pltpu.VMEM((tm, D_MODEL), F32)],
        compiler_params=_params("arbitrary", "arbitrary"),
        name="moe",
    )(x, sc, sh, g2, gn, gf, wr, w_ein, w_eout)


PAGE = 128
MOBA_PAGES_PER_STEP = 4
NSA_PAGES_PER_STEP = 8


def _page_view(cache):
    n_phys, page, heads, dh = cache.shape
    return jnp.transpose(cache, (0, 2, 3, 1)).reshape(n_phys, heads * dh, page)


def _page_specs(rows, n_pages, per_step):
    def spec(u):
        return pl.BlockSpec((1, rows, PAGE), lambda b, s, pt: (pt[b * n_pages + s * per_step + u], 0, 0))
    return [spec(u) for u in range(per_step)]


def _head_diag(full, heads):
    rows = full.shape[0]
    head = _iota((rows, D_HEAD), 0) // (rows // heads)
    out = jnp.zeros((rows, D_HEAD), F32)
    for h in range(heads):
        out = out + jnp.where(head == h, full[:, h * D_HEAD:(h + 1) * D_HEAD], 0.0)
    return out


def _moba_s_kernel(pt_ref, qbd_ref, *refs):
    per = MOBA_PAGES_PER_STEP
    k_refs, v_refs = refs[:per], refs[per:2 * per]
    m_ref, l_ref, ks_ref, o_ref = refs[2 * per:]
    s = pl.program_id(1)

    @pl.when(s == 0)
    def _():
        m_ref[...] = jnp.zeros_like(m_ref)
        l_ref[...] = jnp.zeros_like(l_ref)
        ks_ref[...] = jnp.zeros_like(ks_ref)

    qbd = qbd_ref[0]
    lane_q = _iota(m_ref.shape[1:], 1)
    lane_k = _iota(ks_ref.shape[1:], 1)
    for u in range(per):
        page = s * per + u
        kt = k_refs[u][0]
        vt = v_refs[u][0]
        sc = _dot(qbd, kt.astype(BF16))
        m = jnp.max(sc, axis=-1, keepdims=True)
        p = jnp.exp(sc - m)
        l = jnp.sum(p, axis=-1, keepdims=True)
        o_ref[0, u] = _head_diag(_dot_nt(p.astype(BF16), vt.astype(BF16)), MOBA_HEADS)
        ksum = jnp.sum(kt, axis=-1, keepdims=True)
        m_ref[0] = jnp.where(lane_q == page, m, m_ref[0])
        l_ref[0] = jnp.where(lane_q == page, l, l_ref[0])
        ks_ref[0] = jnp.where(lane_k == page, ksum, ks_ref[0])


def _moba_s_pass(pt_flat, qbd, kt_pages, vt_pages, batch, n_pages):
    per = MOBA_PAGES_PER_STEP
    rows = qbd.shape[1]
    stat = lambda b, s, pt: (b, 0, 0)
    return pl.pallas_call(
        _moba_s_kernel,
        grid_spec=pltpu.PrefetchScalarGridSpec(
            num_scalar_prefetch=1,
            grid=(batch, n_pages // per),
            in_specs=[pl.BlockSpec((1, rows, MOBA_WIDTH), stat)]
            + _page_specs(MOBA_WIDTH, n_pages, per) + _page_specs(MOBA_WIDTH, n_pages, per),
            out_specs=[pl.BlockSpec((1, rows, LANES), stat), pl.BlockSpec((1, rows, LANES), stat),
                       pl.BlockSpec((1, MOBA_WIDTH, LANES), stat),
                       pl.BlockSpec((1, per, rows, D_HEAD), lambda b, s, pt: (b, s, 0, 0))]),
        out_shape=[jax.ShapeDtypeStruct((batch, rows, LANES), F32), jax.ShapeDtypeStruct((batch, rows, LANES), F32),
                   jax.ShapeDtypeStruct((batch, MOBA_WIDTH, LANES), F32),
                   jax.ShapeDtypeStruct((batch, n_pages, rows, D_HEAD), F32)],
        compiler_params=_params("arbitrary", "arbitrary"),
        name="moba_decode_pages",
    )(pt_flat, qbd, *([kt_pages] * per), *([vt_pages] * per))


def _moba_s_combine_kernel(m_ref, l_ref, ks_ref, o_ref, qf_ref, qbd_ref, kn_ref, vn_ref, out_ref, *, n_pages, ts):
    rows = m_ref.shape[1]
    pages_per_blk = MOBA_BLOCK // PAGE
    nblk = n_pages // pages_per_blk
    lane = _iota((rows, LANES), 1)
    pr = _iota((LANES, LANES), 0)
    pc = _iota((LANES, LANES), 1)
    page_to_blk = ((pr // pages_per_blk) == pc).astype(F32)
    blk_to_page = (pr == (pc // pages_per_blk)).astype(BF16)
    kmean = jnp.dot(ks_ref[0], page_to_blk, precision=HIGHEST, preferred_element_type=F32) * (1.0 / MOBA_BLOCK)
    gate = jnp.dot(qf_ref[0], kmean, precision=HIGHEST, preferred_element_type=F32)
    gate = jnp.where(lane < nblk, gate, NEG_INF)
    sel = (_rank_select(gate, nblk, MOBA_TOPK) & (lane < nblk)).astype(BF16)
    selp = _dot(sel, blk_to_page) > 0.5
    qbd = qbd_ref[0]
    s_own = _dot_nt(qbd, kn_ref[0].astype(BF16))
    valid_own = lane <= (_iota((rows, LANES), 0) % ts)
    s_own = jnp.where(valid_own, s_own, NEG_INF)
    m_all = jnp.where(selp, m_ref[0], NEG_INF)
    big_m = jnp.maximum(jnp.max(m_all, axis=-1, keepdims=True), jnp.max(s_own, axis=-1, keepdims=True))
    wgt = jnp.where(selp, jnp.exp(m_ref[0] - big_m), 0.0)
    p_own = jnp.where(valid_own, jnp.exp(s_own - big_m), 0.0)
    denom = jnp.sum(wgt * l_ref[0], axis=-1, keepdims=True) + jnp.sum(p_own, axis=-1, keepdims=True)
    num = _head_diag(_dot(p_own.astype(BF16), vn_ref[0].astype(BF16)), MOBA_HEADS)
    for p in range(n_pages):
        num = num + wgt[:, p:p + 1] * o_ref[0, p]
    out_ref[0] = num / jnp.maximum(denom, TINY)


def _moba_s_combine(m, l, ks, o, qf, qbd, kn, vn, n_pages, ts):
    batch, rows, _ = m.shape
    b3 = lambda b: (b, 0, 0)
    return pl.pallas_call(
        functools.partial(_moba_s_combine_kernel, n_pages=n_pages, ts=ts),
        grid=(batch,),
        in_specs=[pl.BlockSpec((1, rows, LANES), b3), pl.BlockSpec((1, rows, LANES), b3),
                  pl.BlockSpec((1, MOBA_WIDTH, LANES), b3),
                  pl.BlockSpec((1, n_pages, rows, D_HEAD), lambda b: (b, 0, 0, 0)),
                  pl.BlockSpec((1, rows, MOBA_WIDTH), b3), pl.BlockSpec((1, rows, MOBA_WIDTH), b3),
                  pl.BlockSpec((1, LANES, MOBA_WIDTH), b3), pl.BlockSpec((1, LANES, MOBA_WIDTH), b3)],
        out_specs=pl.BlockSpec((1, rows, D_HEAD), b3),
        out_shape=jax.ShapeDtypeStruct((batch, rows, D_HEAD), F32),
        compiler_params=_params("arbitrary"),
        name="moba_decode_combine",
    )(m, l, ks, o, qf, qbd, kn, vn)


def _block_diag_q(q, batch, ts, heads):
    q4 = q.reshape(batch, ts, heads, D_HEAD)
    eye = jnp.eye(heads, dtype=q.dtype)
    return jnp.einsum('bchd,hk->bhckd', q4, eye).reshape(batch, heads * ts, heads * D_HEAD)


def _pad_rows(a, batch, ts):
    a3 = a.reshape(batch, ts, a.shape[-1])
    return jnp.pad(a3, ((0, 0), (0, LANES - ts), (0, 0)))


def _compress_s_kernel(pt_ref, *refs):
    per = NSA_PAGES_PER_STEP
    page_refs = refs[:per]
    pea_ref, peb_ref, wa_ref, wb_ref, w2_ref, o_ref, x_scr, a_scr, b_scr = refs[per:]
    s = pl.program_id(1)
    nseg = per * PAGE // CMP_STRIDE
    for u in range(per):
        x_scr[u * PAGE:(u + 1) * PAGE, :] = page_refs[u][0].T
    a = jnp.zeros((nseg, a_scr.shape[1]), F32)
    b = jnp.zeros((nseg, b_scr.shape[1]), F32)
    for l in range(CMP_STRIDE):
        xl = x_scr[pl.ds(l, nseg, stride=CMP_STRIDE), :]
        a = a + _dot((xl + pea_ref[l:l + 1, :]).astype(BF16), wa_ref[l])
        b = b + _dot((xl + peb_ref[l:l + 1, :]).astype(BF16), wb_ref[l])
    a_scr[pl.ds(pl.multiple_of(s * nseg, nseg), nseg), :] = a
    b_scr[pl.ds(pl.multiple_of(s * nseg, nseg), nseg), :] = b

    @pl.when(s == pl.num_programs(1) - 1)
    def _():
        total = a_scr.shape[0]
        hid = a_scr[...] + pltpu.roll(b_scr[...], total - 1, 0)
        o_ref[0] = _dot(_gelu(hid).astype(BF16), w2_ref[...])


def _compress_s(pt_flat, pages, cw, batch, n_pages):
    wa, wb, pea, peb, w2bd = cw
    per = NSA_PAGES_PER_STEP
    total = n_pages * PAGE // CMP_STRIDE
    hidden = NSA_KV_HEADS * CMP_HIDDEN
    full2 = lambda b, s, pt: (0, 0)
    full3 = lambda b, s, pt: (0, 0, 0)
    return pl.pallas_call(
        _compress_s_kernel,
        grid_spec=pltpu.PrefetchScalarGridSpec(
            num_scalar_prefetch=1,
            grid=(batch, n_pages // per),
            in_specs=_page_specs(NSA_KV_WIDTH, n_pages, per)
            + [pl.BlockSpec((CMP_STRIDE, NSA_KV_WIDTH), full2), pl.BlockSpec((CMP_STRIDE, NSA_KV_WIDTH), full2),
               pl.BlockSpec((CMP_STRIDE, NSA_KV_WIDTH, hidden), full3),
               pl.BlockSpec((CMP_STRIDE, NSA_KV_WIDTH, hidden), full3),
               pl.BlockSpec((hidden, NSA_KV_WIDTH), full2)],
            out_specs=pl.BlockSpec((1, total, NSA_KV_WIDTH), lambda b, s, pt: (b, 0, 0)),
            scratch_shapes=[pltpu.VMEM((per * PAGE, NSA_KV_WIDTH), F32), pltpu.VMEM((total, hidden), F32),
                            pltpu.VMEM((total, hidden), F32)]),
        out_shape=jax.ShapeDtypeStruct((batch, total, NSA_KV_WIDTH), F32),
        compiler_params=_params("arbitrary", "arbitrary"),
        name="compress_decode",
    )(pt_flat, *([pages] * per), pea, peb, wa, wb, w2bd)


def _stack_group_q(q_ref, g):
    return jnp.concatenate(
        [q_ref[0, :, (g * NSA_GROUP + r) * D_HEAD:(g * NSA_GROUP + r + 1) * D_HEAD] * SCALE
         for r in range(NSA_GROUP)], axis=0).astype(BF16)


def _nsa_s_kernel(q_ref, kc_ref, vc_ref, wk_ref, wv_ref, kn_ref, vn_ref, gate_ref, ov_ref, part_ref, sel_ref, *, ts):
    rows = NSA_GROUP * ts
    nc_pad = kc_ref.shape[1]
    wlen = wk_ref.shape[3]
    gates = _sigmoid(gate_ref[0])
    n_i = _iota((rows, nc_pad), 1)
    valid_c = n_i < nc_pad - 1
    c_of_row = _iota((rows, 1), 0) % ts
    valid_w = _iota((rows, wlen), 1) >= c_of_row
    valid_n = _iota((rows, LANES), 1) <= c_of_row
    for g in range(NSA_KV_HEADS):
        lane0 = g * D_HEAD
        qs = _stack_group_q(q_ref, g)
        s = jnp.where(valid_c, _dot_nt(qs, kc_ref[0, :, lane0:lane0 + D_HEAD].astype(BF16)), NEG_INF)
        m = jnp.max(s, axis=-1, keepdims=True)
        p = jnp.where(valid_c, jnp.exp(s - m), 0.0)
        p = p / jnp.maximum(jnp.sum(p, axis=-1, keepdims=True), TINY)
        o_c = _dot(p.astype(BF16), vc_ref[0, :, lane0:lane0 + D_HEAD].astype(BF16))
        psum = jnp.sum(p.reshape(NSA_GROUP, ts, nc_pad), axis=0)
        imp = jnp.dot(psum, ov_ref[...], precision=HIGHEST, preferred_element_type=F32)
        sel_ref[0, g] = _rank_select(imp, LANES, SLC_TOPN - 1).astype(F32)
        s_w = jnp.where(valid_w, _dot(qs, wk_ref[0, g].astype(BF16)), NEG_INF)
        s_n = jnp.where(valid_n, _dot_nt(qs, kn_ref[0, :, lane0:lane0 + D_HEAD].astype(BF16)), NEG_INF)
        m = jnp.maximum(jnp.max(s_w, axis=-1, keepdims=True), jnp.max(s_n, axis=-1, keepdims=True))
        p_w = jnp.where(valid_w, jnp.exp(s_w - m), 0.0)
        p_n = jnp.where(valid_n, jnp.exp(s_n - m), 0.0)
        den = jnp.sum(p_w, axis=-1, keepdims=True) + jnp.sum(p_n, axis=-1, keepdims=True)
        o_w = (_dot_nt(p_w.astype(BF16), wv_ref[0, g].astype(BF16))
               + _dot(p_n.astype(BF16), vn_ref[0, :, lane0:lane0 + D_HEAD].astype(BF16))) / jnp.maximum(den, TINY)
        for r in range(NSA_GROUP):
            hd = g * NSA_GROUP + r
            rs = slice(r * ts, (r + 1) * ts)
            part_ref[0, :, hd * D_HEAD:(hd + 1) * D_HEAD] = (
                gates[:, 3 * hd:3 * hd + 1] * o_c[rs] + gates[:, 3 * hd + 2:3 * hd + 3] * o_w[rs])


def _nsa_s(q3, kcmp, vcmp, wk_t, wv_t, kn_w, vn_w, gate3, ts):
    batch = q3.shape[0]
    nc_pad = kcmp.shape[1]
    wlen = wk_t.shape[3]
    ov = _overlap_matrix(nc_pad, LANES)
    b3 = lambda b: (b, 0, 0)
    b4 = lambda b: (b, 0, 0, 0)
    return pl.pallas_call(
        functools.partial(_nsa_s_kernel, ts=ts),
        grid=(batch,),
        in_specs=[pl.BlockSpec((1, ts, NSA_WIDTH), b3),
                  pl.BlockSpec((1, nc_pad, NSA_KV_WIDTH), b3), pl.BlockSpec((1, nc_pad, NSA_KV_WIDTH), b3),
                  pl.BlockSpec((1, NSA_KV_HEADS, D_HEAD, wlen), b4), pl.BlockSpec((1, NSA_KV_HEADS, D_HEAD, wlen), b4),
                  pl.BlockSpec((1, LANES, NSA_KV_WIDTH), b3), pl.BlockSpec((1, LANES, NSA_KV_WIDTH), b3),
                  pl.BlockSpec((1, ts, LANES), b3),
                  pl.BlockSpec((nc_pad, LANES), lambda b: (0, 0))],
        out_specs=[pl.BlockSpec((1, ts, NSA_WIDTH), b3), pl.BlockSpec((1, NSA_KV_HEADS, ts, LANES), b4)],
        out_shape=[jax.ShapeDtypeStruct((batch, ts, NSA_WIDTH), F32),
                   jax.ShapeDtypeStruct((batch, NSA_KV_HEADS, ts, LANES), F32)],
        compiler_params=_params("arbitrary"),
        name="nsa_decode_cmp_win",
    )(q3, kcmp, vcmp, wk_t, wv_t, kn_w, vn_w, gate3, ov)


def _slc_s_kernel(pt_ref, q_ref, sel_ref, *refs, ts):
    per = NSA_PAGES_PER_STEP
    k_refs, v_refs = refs[:per], refs[per:2 * per]
    kn_ref, vn_ref, gate_ref, part_ref, o_ref, m_scr, l_scr, acc_scr = refs[2 * per:]
    s = pl.program_id(1)
    rows = NSA_GROUP * ts
    blocks_per_page = PAGE // SLC_BLOCK

    @pl.when(s == 0)
    def _():
        m_scr[...] = jnp.full_like(m_scr, NEG_INF)
        l_scr[...] = jnp.zeros_like(l_scr)
        acc_scr[...] = jnp.zeros_like(acc_scr)

    lane_t = _iota((ts, LANES), 1)

    def update(g, sc, msk, pv_fn):
        sc = jnp.where(msk, sc, NEG_INF)
        m_old = m_scr[g]
        m_new = jnp.maximum(m_old, jnp.max(sc, axis=-1, keepdims=True))
        p = jnp.where(msk, jnp.exp(sc - m_new), 0.0)
        alpha = jnp.exp(m_old - m_new)
        l_scr[g] = alpha * l_scr[g] + jnp.sum(p, axis=-1, keepdims=True)
        acc_scr[g] = alpha * acc_scr[g] + pv_fn(p.astype(BF16))
        m_scr[g] = m_new

    qs = [_stack_group_q(q_ref, g) for g in range(NSA_KV_HEADS)]
    for u in range(per):
        page = s * per + u
        kt = k_refs[u][0]
        vt = v_refs[u][0]
        for g in range(NSA_KV_HEADS):
            selg = sel_ref[0, g]
            msk = jnp.zeros((ts, LANES), jnp.bool_)
            for hb in range(blocks_per_page):
                col = jnp.sum(jnp.where(lane_t == page * blocks_per_page + hb, selg, 0.0), axis=-1, keepdims=True)
                msk = msk | ((col > 0.5) & ((lane_t // SLC_BLOCK) == hb))
            msk = jnp.broadcast_to(msk[None], (NSA_GROUP, ts, LANES)).reshape(rows, LANES)
            ktg = kt[g * D_HEAD:(g + 1) * D_HEAD, :].astype(BF16)
            vtg = vt[g * D_HEAD:(g + 1) * D_HEAD, :].astype(BF16)
            update(g, _dot(qs[g], ktg), msk, lambda pb, vtg=vtg: _dot_nt(pb, vtg))

    @pl.when(s == pl.num_programs(1) - 1)
    def _():
        gates = _sigmoid(gate_ref[0])
        valid_n = _iota((rows, LANES), 1) <= (_iota((rows, 1), 0) % ts)
        for g in range(NSA_KV_HEADS):
            lane0 = g * D_HEAD
            kn = kn_ref[0, :, lane0:lane0 + D_HEAD].astype(BF16)
            vn = vn_ref[0, :, lane0:lane0 + D_HEAD].astype(BF16)
            update(g, _dot_nt(qs[g], kn), valid_n, lambda pb, vn=vn: _dot(pb, vn))
            o_s = acc_scr[g] / jnp.maximum(l_scr[g], TINY)
            for r in range(NSA_GROUP):
                hd = g * NSA_GROUP + r
                cols = slice(hd * D_HEAD, (hd + 1) * D_HEAD)
                o_ref[0, :, cols] = part_ref[0, :, cols] + gates[:, 3 * hd + 1:3 * hd + 2] * o_s[r * ts:(r + 1) * ts]


def _slc_s(pt_flat, q3, sel, k_pages, v_pages, kn_s, vn_s, gate3, part, n_pages, ts):
    batch = q3.shape[0]
    per = NSA_PAGES_PER_STEP
    rows = NSA_GROUP * ts
    b3 = lambda b, s, pt: (b, 0, 0)
    return pl.pallas_call(
        functools.partial(_slc_s_kernel, ts=ts),
        grid_spec=pltpu.PrefetchScalarGridSpec(
            num_scalar_prefetch=1,
            grid=(batch, n_pages // per),
            in_specs=[pl.BlockSpec((1, ts, NSA_WIDTH), b3),
                      pl.BlockSpec((1, NSA_KV_HEADS, ts, LANES), lambda b, s, pt: (b, 0, 0, 0))]
            + _page_specs(NSA_KV_WIDTH, n_pages, per) + _page_specs(NSA_KV_WIDTH, n_pages, per)
            + [pl.BlockSpec((1, LANES, NSA_KV_WIDTH), b3), pl.BlockSpec((1, LANES, NSA_KV_WIDTH), b3),
               pl.BlockSpec((1, ts, LANES), b3), pl.BlockSpec((1, ts, NSA_WIDTH), b3)],
            out_specs=pl.BlockSpec((1, ts, NSA_WIDTH), b3),
            scratch_shapes=[pltpu.VMEM((NSA_KV_HEADS, rows, 1), F32), pltpu.VMEM((NSA_KV_HEADS, rows, 1), F32),
                            pltpu.VMEM((NSA_KV_HEADS, rows, D_HEAD), F32)]),
        out_shape=jax.ShapeDtypeStruct((batch, ts, NSA_WIDTH), F32),
        compiler_params=_params("arbitrary", "arbitrary"),
        name="nsa_decode_slc",
    )(pt_flat, q3, sel, *([k_pages] * per), *([v_pages] * per), kn_s, vn_s, gate3, part)


def _prep_weights(w_ada, b_ada, norm_mix_g, w_in, pe_cmp_k, w_cmp_k1, w_cmp_k2, pe_cmp_v, w_cmp_v1, w_cmp_v2,
                  w_br_a, w_br_b, w_out, norm_ffn_g, w_router_grp, w_router_exp, w_expert_in, w_expert_out,
                  norm_final_g):
    wr = jnp.concatenate([w_router_grp, w_router_exp], axis=1)
    wr = jnp.pad(wr, ((0, 0), (0, LANES - wr.shape[1])))
    return dict(
        w_ada=w_ada, b_ada=b_ada.reshape(1, -1), g_mix=norm_mix_g.reshape(1, -1),
        w_in=_reorder_w_in(w_in),
        cmp_k=_compress_weights(pe_cmp_k, w_cmp_k1, w_cmp_k2),
        cmp_v=_compress_weights(pe_cmp_v, w_cmp_v1, w_cmp_v2),
        w_br_a=w_br_a.astype(BF16), w_br_b=w_br_b.astype(BF16), w_out=w_out.astype(BF16),
        g_ffn=norm_ffn_g.reshape(1, -1), wr=wr,
        w_ein=w_expert_in.astype(BF16), w_eout=w_expert_out.astype(BF16),
        g_final=norm_final_g.reshape(1, -1))


def _prompt_layer(x, mod, w, batch, seq):
    sh1, sc1, g1, sh2, sc2, g2 = mod
    cos, sin = _rope_tables(jnp.arange(seq, dtype=jnp.int32))
    qk_a, v_a, q_b, k_n, v_n, gate, mg = _inproj(x, sc1, sh1, w['g_mix'], cos, sin, w['w_in'], 256)
    o_a = _moba_p(qk_a, v_a, batch, seq)
    kcmp = _compress_p(k_n[:, :NSA_KV_WIDTH], w['cmp_k'], batch, seq)
    vcmp = _compress_p(v_n[:, :NSA_KV_WIDTH], w['cmp_v'], batch, seq)
    o_b = _nsa_p(q_b, k_n, v_n, kcmp, vcmp, gate, batch, seq)
    x1 = _merge(o_a, o_b, mg, x, g1, w['w_br_a'], w['w_br_b'], w['w_out'], 256)
    y = _moe(x1, sc2, sh2, g2, w['g_ffn'], w['g_final'], w['wr'], w['w_ein'], w['w_eout'], 512)
    return y, (qk_a, v_a, k_n, v_n)


def _sample_layer(x, mod, w, caches, win_state, page_table, batch, ts):
    sh1, sc1, g1, sh2, sc2, g2 = mod
    moba_k, moba_v, cmp_k, cmp_v, slc_k, slc_v = caches
    win_k, win_v = win_state
    n_pages = page_table.shape[1]
    assert moba_k.shape[1] == PAGE and win_k.shape[1] == WINDOW and ts <= LANES
    n = batch * ts
    pos = n_pages * PAGE + (jnp.arange(n, dtype=jnp.int32) % ts)
    cos, sin = _rope_tables(pos)
    qk_a, v_a, q_b, k_n, v_n, gate, mg = _inproj(x, sc1, sh1, w['g_mix'], cos, sin, w['w_in'], n)
    pt_flat = page_table.reshape(-1)
    kv = NSA_KV_WIDTH
    qf = _block_diag_q(qk_a[:, :MOBA_WIDTH], batch, ts, MOBA_HEADS)
    qbd = (qf * SCALE).astype(BF16)
    m, l, ks, o = _moba_s_pass(pt_flat, qbd, _page_view(moba_k), _page_view(moba_v), batch, n_pages)
    o_a = _moba_s_combine(m, l, ks, o, qf, qbd, _pad_rows(qk_a[:, MOBA_WIDTH:], batch, ts), _pad_rows(v_a, batch, ts),
                          n_pages, ts)
    o_a = o_a.reshape(batch, MOBA_HEADS, ts, D_HEAD).transpose(0, 2, 1, 3).reshape(n, MOBA_WIDTH)
    kcmp = _compress_s(pt_flat, _page_view(cmp_k), w['cmp_k'], batch, n_pages)
    vcmp = _compress_s(pt_flat, _page_view(cmp_v), w['cmp_v'], batch, n_pages)
    q3 = q_b.reshape(batch, ts, NSA_WIDTH)
    gate3 = gate.reshape(batch, ts, LANES)
    part, sel = _nsa_s(q3, kcmp, vcmp, jnp.transpose(win_k, (0, 2, 3, 1)), jnp.transpose(win_v, (0, 2, 3, 1)),
                       _pad_rows(k_n[:, 2 * kv:], batch, ts), _pad_rows(v_n[:, 2 * kv:], batch, ts), gate3, ts)
    o_b = _slc_s(pt_flat, q3, sel, _page_view(slc_k), _page_view(slc_v),
                 _pad_rows(k_n[:, kv:2 * kv], batch, ts), _pad_rows(v_n[:, kv:2 * kv], batch, ts), gate3, part,
                 n_pages, ts).reshape(n, NSA_WIDTH)
    x1 = _merge(o_a, o_b, mg, x, g1, w['w_br_a'], w['w_br_b'], w['w_out'], n)
    y = _moe(x1, sc2, sh2, g2, w['g_ffn'], w['g_final'], w['wr'], w['w_ein'], w['w_eout'], n)
    return y, (qk_a, v_a, k_n, v_n)


def kernel(x_prompt, x_sample, c_prompt, c_sample, cache_moba_k, cache_moba_v, cache_nsa_cmp_k, cache_nsa_cmp_v,
           cache_nsa_slc_k, cache_nsa_slc_v, state_nsa_win_k, state_nsa_win_v, page_table, w_ada, b_ada, norm_mix_g,
           w_in, pe_cmp_k, w_cmp_k1, w_cmp_k2, pe_cmp_v, w_cmp_v1, w_cmp_v2, w_br_a, w_br_b, w_out, norm_ffn_g,
           w_router_grp, w_router_exp, w_expert_in, w_expert_out, norm_final_g):
    bp, tp, _ = x_prompt.shape
    bs, ts, _ = x_sample.shape
    w = _prep_weights(w_ada[0], b_ada[0], norm_mix_g[0], w_in[0], pe_cmp_k[0], w_cmp_k1[0], w_cmp_k2[0], pe_cmp_v[0],
                      w_cmp_v1[0], w_cmp_v2[0], w_br_a[0], w_br_b[0], w_out[0], norm_ffn_g[0], w_router_grp[0],
                      w_router_exp[0], w_expert_in[0], w_expert_out[0], norm_final_g)
    mod = _ada(jnp.concatenate([c_prompt, c_sample], axis=0), w['w_ada'], w['b_ada'])
    mod_p = [m.reshape(bp, 1, D_MODEL) for m in jnp.split(mod[:bp], 6, axis=-1)]
    y_p, (qk_a, v_a, k_n, v_n) = _prompt_layer(x_prompt.reshape(bp * tp, D_MODEL), mod_p, w, bp, tp)

    mod_s = [jnp.repeat(m, ts, axis=0).reshape(1, bs * ts, D_MODEL) for m in jnp.split(mod[bp:], 6, axis=-1)]
    caches = (cache_moba_k[0], cache_moba_v[0], cache_nsa_cmp_k[0], cache_nsa_cmp_v[0], cache_nsa_slc_k[0],
              cache_nsa_slc_v[0])
    y_s, new_s = _sample_layer(x_sample.reshape(bs * ts, D_MODEL), mod_s, w, caches,
                               (state_nsa_win_k[0], state_nsa_win_v[0]), page_table, bs, ts)

    kv = NSA_KV_WIDTH

    def new_rows(new, b, t):
        qk, v, k_n, v_n = new
        rows = lambda a, heads: a.reshape(1, b, t, heads, D_HEAD)
        return (rows(qk[:, MOBA_WIDTH:], MOBA_HEADS), rows(v, MOBA_HEADS),
                rows(k_n[:, :kv], NSA_KV_HEADS), rows(v_n[:, :kv], NSA_KV_HEADS),
                rows(k_n[:, kv:2 * kv], NSA_KV_HEADS), rows(v_n[:, kv:2 * kv], NSA_KV_HEADS),
                rows(k_n[:, 2 * kv:], NSA_KV_HEADS), rows(v_n[:, 2 * kv:], NSA_KV_HEADS))

    wb = state_nsa_win_k.shape[2]
    outs_p = new_rows((qk_a, v_a, k_n, v_n), bp, tp)
    outs_p = outs_p[:6] + (outs_p[6][:, :, tp - wb:], outs_p[7][:, :, tp - wb:])
    outs_s = new_rows(new_s, bs, ts)
    win_k = jnp.concatenate([state_nsa_win_k, outs_s[6]], axis=2)[:, :, ts:]
    win_v = jnp.concatenate([state_nsa_win_v, outs_s[7]], axis=2)[:, :, ts:]
    outs_s = outs_s[:6] + (win_k, win_v)
    return (y_p.reshape(bp, tp, D_MODEL), y_s.reshape(bs, ts, D_MODEL)) + outs_p + outs_s
```

```python
import functools

import jax
import jax.numpy as jnp
from jax import lax
from jax.experimental import pallas as pl
from jax.experimental.pallas import tpu as pltpu

D_MODEL = 1024
D_HEAD = 64
HALF = D_HEAD // 2
MOBA_HEADS = 8
MOBA_BLOCK = 256
MOBA_TOPK = 3
NSA_HEADS = 8
NSA_KV_HEADS = 2
NSA_GROUP = NSA_HEADS // NSA_KV_HEADS
CMP_LEN = 32
CMP_STRIDE = 16
CMP_HIDDEN = 128
SLC_BLOCK = 64
SLC_TOPN = 16
WINDOW = 512
N_GROUPS = 4
EXPERTS_PER_GROUP = 8
N_EXPERTS = N_GROUPS * EXPERTS_PER_GROUP
D_EXPERT = 256
ROPE_THETA = 10000.0
EPS = 1e-6
NEG_INF = -1e30
BIG = 1e30
TINY = 1e-30
MOBA_WIDTH = MOBA_HEADS * D_HEAD
NSA_WIDTH = NSA_HEADS * D_HEAD
NSA_KV_WIDTH = NSA_KV_HEADS * D_HEAD
SCALE = D_HEAD ** -0.5

LANES = 128
VMEM_LIMIT = 48 * 1024 * 1024

F32 = jnp.float32
BF16 = jnp.bfloat16
HIGHEST = lax.Precision.HIGHEST


def _params(*sem):
    return pltpu.CompilerParams(dimension_semantics=sem, vmem_limit_bytes=VMEM_LIMIT)


def _dot(a, b):
    return jnp.dot(a, b, preferred_element_type=F32)


def _dot_nt(a, b, precision=None):
    return lax.dot_general(a, b, (((1,), (1,)), ((), ())), precision=precision,
                           preferred_element_type=F32)


def _sigmoid(x):
    return 1.0 / (1.0 + jnp.exp(-x))


def _iota(shape, dim):
    return lax.broadcasted_iota(jnp.int32, shape, dim)


def _ada_kernel(c_ref, w_ref, b_ref, o_ref):
    c = c_ref[...]
    s = c * _sigmoid(c)
    o_ref[...] = _dot(s.astype(BF16), w_ref[...].astype(BF16)) + b_ref[...]


def _ada(c, w, b):
    n = c.shape[0]
    tn = 1024
    return pl.pallas_call(
        _ada_kernel,
        grid=(w.shape[1] // tn,),
        in_specs=[pl.BlockSpec((n, D_MODEL), lambda j: (0, 0)),
                  pl.BlockSpec((D_MODEL, tn), lambda j: (0, j)),
                  pl.BlockSpec((1, tn), lambda j: (0, j))],
        out_specs=pl.BlockSpec((n, tn), lambda j: (0, j)),
        out_shape=jax.ShapeDtypeStruct((n, w.shape[1]), F32),
        compiler_params=_params("arbitrary"),
        name="ada",
    )(c, w, b)


_IN_GROUPS = ((2 * MOBA_WIDTH, True),
              (MOBA_WIDTH, False),
              (NSA_WIDTH, True),
              (3 * NSA_KV_WIDTH, True),
              (3 * NSA_KV_WIDTH, False),
              (LANES, False),
              (2 * D_MODEL, False))
_IN_COLS_PAD = sum(w for w, _ in _IN_GROUPS)


def _norm_mod(x, g, sc, sh):
    ms = jnp.mean(x * x, axis=-1, keepdims=True)
    y = x * lax.rsqrt(ms + EPS) * g
    return y * (1.0 + sc) + sh


def _inproj_kernel(x_ref, sc_ref, sh_ref, g_ref, cos_ref, sin_ref, w_ref, *out_refs):
    h = _norm_mod(x_ref[...], g_ref[...], sc_ref[0], sh_ref[0]).astype(BF16)
    cos = cos_ref[...]
    sin = sin_ref[...]
    first_half = (_iota(cos.shape, 1) & (D_HEAD - 1)) < HALF

    def rope(y):
        rot = jnp.where(first_half, pltpu.roll(y, LANES - HALF, 1), pltpu.roll(y, HALF, 1))
        return y * cos + rot * sin

    col = 0
    for out_ref, (width, rotary) in zip(out_refs, _IN_GROUPS):
        chunk = min(width, 512)
        for c in range(0, width, chunk):
            cw = min(chunk, width - c)
            y = _dot(h, w_ref[:, col + c:col + c + cw])
            for s in range(0, cw, LANES):
                piece = y[:, s:s + LANES]
                out_ref[:, c + s:c + s + LANES] = rope(piece) if rotary else piece
        col += width


def _inproj(x, sc, sh, g, cos, sin, w, tm):
    n = x.shape[0]
    nb, r, _ = sc.shape
    tiles_per_b = (n // nb) // tm
    tab_tiles = cos.shape[0] // tm
    row = lambda i: (i, 0)
    mod = lambda i: (i // tiles_per_b, 0, 0)
    tab = lambda i: (i % tab_tiles, 0)
    return pl.pallas_call(
        _inproj_kernel,
        grid=(n // tm,),
        in_specs=[pl.BlockSpec((tm, D_MODEL), row),
                  pl.BlockSpec((1, r, D_MODEL), mod),
                  pl.BlockSpec((1, r, D_MODEL), mod),
                  pl.BlockSpec((1, D_MODEL), lambda i: (0, 0)),
                  pl.BlockSpec((tm, LANES), tab),
                  pl.BlockSpec((tm, LANES), tab),
                  pl.BlockSpec((D_MODEL, _IN_COLS_PAD), lambda i: (0, 0))],
        out_specs=[pl.BlockSpec((tm, wd), row) for wd, _ in _IN_GROUPS],
        out_shape=[jax.ShapeDtypeStruct((n, wd), F32) for wd, _ in _IN_GROUPS],
        compiler_params=_params("arbitrary"),
        name="inproj",
    )(x, sc, sh, g, cos, sin, w)


def _reorder_w_in(w_in):
    kv0 = 3 * MOBA_WIDTH + NSA_WIDTH
    kvs = [w_in[:, kv0 + i * NSA_KV_WIDTH:kv0 + (i + 1) * NSA_KV_WIDTH] for i in range(6)]
    g0 = kv0 + 6 * NSA_KV_WIDTH
    ng = 3 * NSA_HEADS
    gate = jnp.pad(w_in[:, g0:g0 + ng], ((0, 0), (0, LANES - ng)))
    parts = [w_in[:, :kv0], kvs[0], kvs[2], kvs[4], kvs[1], kvs[3], kvs[5], gate, w_in[:, g0 + ng:]]
    return jnp.concatenate(parts, axis=1).astype(BF16)


def _rope_tables(pos):
    inv = ROPE_THETA ** (-jnp.arange(HALF, dtype=F32) / HALF)
    ang = pos.astype(F32)[:, None] * inv[None, :]
    cos = jnp.cos(ang)
    sin = jnp.sin(ang)
    cos = jnp.concatenate([cos, cos, cos, cos], axis=1)
    sin = jnp.concatenate([-sin, sin, -sin, sin], axis=1)
    return cos, sin


def _rank_select(score, n_cols, n_keep):
    lane = _iota(score.shape, 1)
    rank = jnp.zeros(score.shape, jnp.int32)
    for jp in range(n_cols):
        col = score[:, jp:jp + 1]
        beats = (col > score) | ((col == score) & (lane > jp))
        rank = rank + beats.astype(jnp.int32)
    return rank < n_keep


def _rank_select_t(score, n_rows, n_keep):
    row = _iota(score.shape, 0)
    rank = jnp.zeros(score.shape, jnp.int32)
    for jp in range(n_rows):
        r = score[jp:jp + 1, :]
        beats = (r > score) | ((r == score) & (row > jp))
        rank = rank + beats.astype(jnp.int32)
    return rank < n_keep


SUBLANES = 8


def _fold_rows(x, op):
    return op(x.reshape(x.shape[0] // SUBLANES, SUBLANES, x.shape[1]), axis=0)


def _attend_t(make_scores, vt_fn, own_c, lo, hi, width, n_chains):
    score_fn = make_scores(None)

    def max_step(c, own, m):
        return tuple(jnp.maximum(mi, _fold_rows(s, jnp.max)) for mi, s in zip(m, score_fn(c, own)))

    m = max_step(own_c, True, tuple(jnp.full((SUBLANES, width), NEG_INF, F32) for _ in range(n_chains)))
    m = lax.fori_loop(lo, hi, lambda c, mm: max_step(c, False, mm), m)
    shifted_fn = make_scores([jnp.max(mi, axis=0, keepdims=True) for mi in m])

    def acc_step(c, own, carry):
        out = []
        for (l, acc), s, vt in zip(carry, shifted_fn(c, own), vt_fn(c)):
            p = jnp.exp(s)
            out.append((l + _fold_rows(p, jnp.sum), acc + _dot(vt, p.astype(BF16))))
        return tuple(out)

    zero = tuple((jnp.zeros((SUBLANES, width), F32), jnp.zeros((D_HEAD, width), F32)) for _ in range(n_chains))
    carry = acc_step(own_c, True, zero)
    carry = lax.fori_loop(lo, hi, lambda c, cr: acc_step(c, False, cr), carry)
    return [acc / jnp.maximum(jnp.sum(l, axis=0, keepdims=True), TINY) for l, acc in carry]


AUG_ONE = 0
AUG_BLK = SUBLANES


def _key_aug(n_keys, block, n_blocks):
    lane = _iota((n_keys, D_HEAD), 1)
    blk = _iota((n_keys, D_HEAD), 0) // block
    hot = (lane == AUG_ONE) | ((lane >= AUG_BLK) & (lane < AUG_BLK + n_blocks) & (lane - AUG_BLK == blk))
    return jnp.where(hot, 1.0, 0.0).astype(BF16)


def _query_aug(qt, shift, bias):
    width = qt.shape[1]
    top = jnp.zeros((SUBLANES, width), F32)
    if shift is not None:
        top = jnp.where(_iota((SUBLANES, width), 0) == AUG_ONE, -shift, 0.0)
    parts = [qt, top]
    used = SUBLANES
    if bias is not None:
        parts.append(bias)
        used += bias.shape[0]
    parts.append(jnp.zeros((D_HEAD - used, width), F32))
    return jnp.concatenate(parts, axis=0).astype(BF16)


MOBA_STEP_WIDTH = 512


def _moba_p_kernel(q_ref, k_ref, v_ref, o_ref, ka_scr, vt_scr, km_scr):
    qi = pl.program_id(2)
    tq = tk = MOBA_BLOCK
    seq = k_ref.shape[1]
    nblk = seq // MOBA_BLOCK
    width = q_ref.shape[2]
    nh = width // D_HEAD

    @pl.when(qi == 0)
    def _():
        kf = k_ref[0]
        km_scr[...] = jnp.mean(kf.reshape(nblk, MOBA_BLOCK, width), axis=1)
        aug = _key_aug(seq, MOBA_BLOCK, nblk)
        for hh in range(nh):
            ka_scr[hh] = jnp.concatenate([kf[:, hh * D_HEAD:(hh + 1) * D_HEAD].astype(BF16), aug], axis=1)
        for j in range(nblk):
            vt_scr[j] = v_ref[0, j * tk:(j + 1) * tk, :].T.astype(BF16)

    q2t = q_ref[0].T
    km = km_scr[...]
    klane = _iota(km.shape, 1)
    blk = _iota((nblk, tq), 0)
    causal_t = _iota((tk, tq), 0) <= _iota((tk, tq), 1)
    qts, biases = [], []
    for hh in range(nh):
        kmh = jnp.where((klane >= hh * D_HEAD) & (klane < (hh + 1) * D_HEAD), km, 0.0)
        gate = jnp.dot(kmh, q2t, precision=HIGHEST, preferred_element_type=F32)
        gate = jnp.where(blk < qi, gate, NEG_INF)
        keep = _rank_select_t(gate, nblk, MOBA_TOPK) & (blk < qi)
        biases.append(jnp.where(keep, 0.0, NEG_INF))
        qts.append(q2t[hh * D_HEAD:(hh + 1) * D_HEAD, :] * SCALE)

    def make_scores(shift):
        sh = [None] * nh if shift is None else shift
        q_own = [_query_aug(qts[hh], sh[hh], None) for hh in range(nh)]
        q_past = [_query_aug(qts[hh], sh[hh], biases[hh]) for hh in range(nh)]

        def score_fn(c, own):
            out = []
            for hh in range(nh):
                kj = ka_scr[hh, pl.ds(pl.multiple_of(c * tk, tk), tk), :]
                s = _dot(kj, q_own[hh] if own else q_past[hh])
                out.append(jnp.where(causal_t, s, NEG_INF) if own else s)
            return out

        return score_fn

    def values(c):
        return [vt_scr[c, hh * D_HEAD:(hh + 1) * D_HEAD, :] for hh in range(nh)]

    o_ref[0] = jnp.concatenate(_attend_t(make_scores, values, qi, 0, qi, tq, nh), axis=0).T


def _moba_p(qk, v, batch, seq):
    qk3 = qk.reshape(batch, seq, 2 * MOBA_WIDTH)
    v3 = v.reshape(batch, seq, MOBA_WIDTH)
    sw = MOBA_STEP_WIDTH
    pairs = MOBA_WIDTH // sw
    nblk = seq // MOBA_BLOCK
    out = pl.pallas_call(
        _moba_p_kernel,
        grid=(batch, pairs, nblk),
        in_specs=[pl.BlockSpec((1, MOBA_BLOCK, sw), lambda b, h, i: (b, i, h)),
                  pl.BlockSpec((1, seq, sw), lambda b, h, i: (b, 0, pairs + h)),
                  pl.BlockSpec((1, seq, sw), lambda b, h, i: (b, 0, h))],
        out_specs=pl.BlockSpec((1, MOBA_BLOCK, sw), lambda b, h, i: (b, i, h)),
        out_shape=jax.ShapeDtypeStruct((batch, seq, MOBA_WIDTH), F32),
        scratch_shapes=[pltpu.VMEM((sw // D_HEAD, seq, 2 * D_HEAD), BF16), pltpu.VMEM((nblk, sw, MOBA_BLOCK), BF16),
                        pltpu.VMEM((nblk, sw), F32)],
        compiler_params=_params("arbitrary", "arbitrary", "arbitrary"),
        name="moba_prompt",
    )(qk3, qk3, v3)
    return out.reshape(batch * seq, MOBA_WIDTH)


def _gelu(x):
    return 0.5 * x * (1.0 + jnp.tanh(0.7978845608028654 * (x + 0.044715 * x * x * x)))


def _compress_p_kernel(seg_ref, pea_ref, peb_ref, wa_ref, wb_ref, w2_ref, o_ref):
    seg = seg_ref[0]
    a = _dot((seg + pea_ref[...]).astype(BF16), wa_ref[...])
    b = _dot((seg + peb_ref[...]).astype(BF16), wb_ref[...])
    nseg = seg.shape[0]
    hid = a + pltpu.roll(b, nseg - 1, 0)
    o_ref[0] = _dot(_gelu(hid).astype(BF16), w2_ref[...])


def _compress_weights(pe, w1, w2):
    g = NSA_KV_HEADS
    eye = jnp.eye(g, dtype=F32)
    w1r = w1.reshape(CMP_LEN, D_HEAD, CMP_HIDDEN)

    def half(lo):
        w = w1r[lo:lo + CMP_STRIDE]
        wbd = jnp.einsum('ldf,gh->lgdhf', w, eye)
        p = jnp.broadcast_to(pe[lo:lo + CMP_STRIDE, None, :], (CMP_STRIDE, g, D_HEAD))
        return wbd.reshape(CMP_STRIDE, g * D_HEAD, g * CMP_HIDDEN).astype(BF16), p.reshape(CMP_STRIDE, g * D_HEAD)

    wa, pea = half(0)
    wb, peb = half(CMP_STRIDE)
    w2bd = jnp.einsum('fd,gh->gfhd', w2, eye).reshape(g * CMP_HIDDEN, g * D_HEAD).astype(BF16)
    return wa, wb, pea, peb, w2bd


def _compress_p(rows, cw, batch, seq):
    wa, wb, pea, peb, w2bd = cw
    nseg = seq // CMP_STRIDE
    width = CMP_STRIDE * NSA_KV_WIDTH
    seg = rows.reshape(batch, nseg, width)
    full = lambda b: (0, 0)
    return pl.pallas_call(
        _compress_p_kernel,
        grid=(batch,),
        in_specs=[pl.BlockSpec((1, nseg, width), lambda b: (b, 0, 0)),
                  pl.BlockSpec((1, width), full), pl.BlockSpec((1, width), full),
                  pl.BlockSpec((width, NSA_KV_HEADS * CMP_HIDDEN), full),
                  pl.BlockSpec((width, NSA_KV_HEADS * CMP_HIDDEN), full),
                  pl.BlockSpec((NSA_KV_HEADS * CMP_HIDDEN, NSA_KV_WIDTH), full)],
        out_specs=pl.BlockSpec((1, nseg, NSA_KV_WIDTH), lambda b: (b, 0, 0)),
        out_shape=jax.ShapeDtypeStruct((batch, nseg, NSA_KV_WIDTH), F32),
        compiler_params=_params("arbitrary"),
        name="compress_prompt",
    )(seg, pea.reshape(1, width), peb.reshape(1, width), wa.reshape(width, -1), wb.reshape(width, -1), w2bd)


NSA_TQ = 128
NSA_TK = 256


def _overlap_matrix(nc_pad, nslc_pad):
    cs = jnp.arange(nc_pad)[:, None] * CMP_STRIDE
    ss = jnp.arange(nslc_pad)[None, :] * SLC_BLOCK
    return ((cs < ss + SLC_BLOCK) & (cs + CMP_LEN > ss)).astype(F32)


def _nsa_p_kernel(q_ref, kc_ref, vc_ref, ks_ref, vs_ref, kw_ref, vw_ref, gate_ref, ovt_ref, o_ref,
                  ksa_scr, kwa_scr, vst_scr, vwt_scr):
    tq, tk = NSA_TQ, NSA_TK
    t = pl.program_id(1)
    q0 = t * tq
    seq = ks_ref.shape[1]
    nc_pad = kc_ref.shape[1]
    nslc = seq // SLC_BLOCK
    width = NSA_GROUP * tq

    @pl.when(t == 0)
    def _():
        aug_s = _key_aug(seq, SLC_BLOCK, nslc)
        aug_w = _key_aug(seq, SLC_BLOCK, 0)
        for g in range(NSA_KV_HEADS):
            rows = slice(g * D_HEAD, (g + 1) * D_HEAD)
            ksa_scr[g] = jnp.concatenate([ks_ref[0, :, rows].astype(BF16), aug_s], axis=1)
            kwa_scr[g] = jnp.concatenate([kw_ref[0, :, rows].astype(BF16), aug_w], axis=1)
        for j in range(seq // tk):
            vst_scr[j] = vs_ref[0, j * tk:(j + 1) * tk, :].T.astype(BF16)
            vwt_scr[j] = vw_ref[0, j * tk:(j + 1) * tk, :].T.astype(BF16)

    qt_all = q_ref[0].T
    gates_t = _sigmoid(gate_ref[0]).T
    kct = kc_ref[0].astype(BF16)
    vct = vc_ref[0].T.astype(BF16)
    pos = q0 + _iota((1, tq), 1)
    n_i = _iota((nc_pad, tq), 0)
    valid_c = (n_i < nc_pad - 1) & (n_i * CMP_STRIDE + (CMP_LEN - 1) <= pos)
    valid_c4 = jnp.concatenate([valid_c] * NSA_GROUP, axis=1)
    q_blk = pos >> 6
    jrow = _iota((nslc, tq), 0)
    krow = _iota((tk, tq), 0)
    cd = q0 // tk
    c_win = jnp.maximum(q0 - WINDOW, 0) // tk

    def tile(b):
        return jnp.concatenate([b] * NSA_GROUP, axis=1)

    def win_bias(c):
        dist = pos - (c * tk + krow)
        return jnp.where((dist >= 0) & (dist <= WINDOW), 0.0, NEG_INF)

    qts, o_cs, biases = [], [], []
    for g in range(NSA_KV_HEADS):
        rows = slice(g * D_HEAD, (g + 1) * D_HEAD)
        qt = jnp.concatenate(
            [qt_all[(g * NSA_GROUP + r) * D_HEAD:(g * NSA_GROUP + r + 1) * D_HEAD, :] * SCALE
             for r in range(NSA_GROUP)], axis=1)
        qts.append(qt)
        s = jnp.where(valid_c4, _dot(kct[:, rows], qt.astype(BF16)), NEG_INF)
        m = jnp.max(s, axis=0, keepdims=True)
        p = jnp.where(valid_c4, jnp.exp(s - m), 0.0)
        p = p / jnp.maximum(jnp.sum(p, axis=0, keepdims=True), TINY)
        o_cs.append(_dot(vct[rows, :], p.astype(BF16)))
        psum = p[:, :tq]
        for r in range(1, NSA_GROUP):
            psum = psum + p[:, r * tq:(r + 1) * tq]
        imp = jnp.dot(ovt_ref[...], psum, precision=HIGHEST, preferred_element_type=F32)
        imp = jnp.where(jrow == q_blk, BIG, jnp.where(jrow < q_blk, imp, NEG_INF))
        keep = _rank_select_t(imp, nslc, SLC_TOPN) & (jrow <= q_blk)
        biases.append(tile(jnp.where(keep, 0.0, NEG_INF)))

    causal_own = tile(cd * tk + krow <= pos)

    def make_slc_scores(shift):
        sh = [None] * NSA_KV_HEADS if shift is None else shift
        qa = [_query_aug(qts[g], sh[g], biases[g]) for g in range(NSA_KV_HEADS)]

        def score_fn(c, own):
            out = []
            for g in range(NSA_KV_HEADS):
                s = _dot(ksa_scr[g, pl.ds(pl.multiple_of(c * tk, tk), tk), :], qa[g])
                out.append(jnp.where(causal_own, s, NEG_INF) if own else s)
            return out

        return score_fn

    def make_win_scores(shift):
        sh = [None] * NSA_KV_HEADS if shift is None else shift
        qa = [_query_aug(qts[g], sh[g], None) for g in range(NSA_KV_HEADS)]

        def score_fn(c, own):
            wb = tile(win_bias(c))
            return [_dot(kwa_scr[g, pl.ds(pl.multiple_of(c * tk, tk), tk), :], qa[g]) + wb
                    for g in range(NSA_KV_HEADS)]

        return score_fn

    def values(vt_scr):
        return lambda c: [vt_scr[c, g * D_HEAD:(g + 1) * D_HEAD, :] for g in range(NSA_KV_HEADS)]

    o_ss = _attend_t(make_slc_scores, values(vst_scr), cd, 0, cd, width, NSA_KV_HEADS)
    o_ws = _attend_t(make_win_scores, values(vwt_scr), cd, c_win, cd, width, NSA_KV_HEADS)
    outs = []
    for g in range(NSA_KV_HEADS):
        o_c, o_s, o_w = o_cs[g], o_ss[g], o_ws[g]
        for r in range(NSA_GROUP):
            hd = g * NSA_GROUP + r
            cs = slice(r * tq, (r + 1) * tq)
            outs.append(gates_t[3 * hd:3 * hd + 1, :] * o_c[:, cs] + gates_t[3 * hd + 1:3 * hd + 2, :] * o_s[:, cs]
                        + gates_t[3 * hd + 2:3 * hd + 3, :] * o_w[:, cs])
    o_ref[0] = jnp.concatenate(outs, axis=0).T


def _nsa_p(q, kn, vn, kcmp, vcmp, gate, batch, seq):
    q3 = q.reshape(batch, seq, NSA_WIDTH)
    kn3 = kn.reshape(batch, seq, 3 * NSA_KV_WIDTH)
    vn3 = vn.reshape(batch, seq, 3 * NSA_KV_WIDTH)
    g3 = gate.reshape(batch, seq, LANES)
    nc_pad = kcmp.shape[1]
    nslc = seq // SLC_BLOCK
    nchunk = seq // NSA_TK
    ovt = _overlap_matrix(nc_pad, nslc).T
    tile = lambda b, t: (b, t, 0)
    cmp_spec = pl.BlockSpec((1, nc_pad, NSA_KV_WIDTH), lambda b, t: (b, 0, 0))
    out = pl.pallas_call(
        _nsa_p_kernel,
        grid=(batch, seq // NSA_TQ),
        in_specs=[pl.BlockSpec((1, NSA_TQ, NSA_WIDTH), tile), cmp_spec, cmp_spec,
                  pl.BlockSpec((1, seq, LANES), lambda b, t: (b, 0, 1)),
                  pl.BlockSpec((1, seq, LANES), lambda b, t: (b, 0, 1)),
                  pl.BlockSpec((1, seq, LANES), lambda b, t: (b, 0, 2)),
                  pl.BlockSpec((1, seq, LANES), lambda b, t: (b, 0, 2)),
                  pl.BlockSpec((1, NSA_TQ, LANES), tile),
                  pl.BlockSpec((nslc, nc_pad), lambda b, t: (0, 0))],
        out_specs=pl.BlockSpec((1, NSA_TQ, NSA_WIDTH), tile),
        out_shape=jax.ShapeDtypeStruct((batch, seq, NSA_WIDTH), F32),
        scratch_shapes=[pltpu.VMEM((NSA_KV_HEADS, seq, 2 * D_HEAD), BF16),
                        pltpu.VMEM((NSA_KV_HEADS, seq, 2 * D_HEAD), BF16),
                        pltpu.VMEM((nchunk, LANES, NSA_TK), BF16), pltpu.VMEM((nchunk, LANES, NSA_TK), BF16)],
        compiler_params=_params("arbitrary", "arbitrary"),
        name="nsa_prompt",
    )(q3, kcmp, vcmp, kn3, vn3, kn3, vn3, g3, ovt)
    return out.reshape(batch * seq, NSA_WIDTH)


def _merge_kernel(oa_ref, ob_ref, mg_ref, x_ref, g1_ref, wa_ref, wb_ref, wo_ref, o_ref):
    a = _dot(oa_ref[...].astype(BF16), wa_ref[...])
    b = _dot(ob_ref[...].astype(BF16), wb_ref[...])
    mix = _sigmoid(mg_ref[:, :D_MODEL]) * a + _sigmoid(mg_ref[:, D_MODEL:]) * b
    o_ref[...] = x_ref[...] + g1_ref[0] * _dot(mix.astype(BF16), wo_ref[...])


def _merge(oa, ob, mg, x, g1, wa, wb, wo, tm):
    n = x.shape[0]
    nb, r, _ = g1.shape
    tiles_per_b = (n // nb) // tm
    row = lambda i: (i, 0)
    full = lambda i: (0, 0)
    return pl.pallas_call(
        _merge_kernel,
        grid=(n // tm,),
        in_specs=[pl.BlockSpec((tm, MOBA_WIDTH), row), pl.BlockSpec((tm, NSA_WIDTH), row),
                  pl.BlockSpec((tm, 2 * D_MODEL), row), pl.BlockSpec((tm, D_MODEL), row),
                  pl.BlockSpec((1, r, D_MODEL), lambda i: (i // tiles_per_b, 0, 0)),
                  pl.BlockSpec((MOBA_WIDTH, D_MODEL), full), pl.BlockSpec((NSA_WIDTH, D_MODEL), full),
                  pl.BlockSpec((D_MODEL, D_MODEL), full)],
        out_specs=pl.BlockSpec((tm, D_MODEL), row),
        out_shape=jax.ShapeDtypeStruct((n, D_MODEL), F32),
        compiler_params=_params("arbitrary"),
        name="merge",
    )(oa, ob, mg, x, g1, wa, wb, wo)


def _route(logits):
    lane = _iota(logits.shape, 1)
    is_grp = lane < N_GROUPS
    lg = jnp.where(is_grp, logits, NEG_INF)
    mg = jnp.max(lg, axis=-1, keepdims=True)
    pg = jnp.where(is_grp, jnp.exp(lg - mg), 0.0)
    pg = pg / jnp.sum(pg, axis=-1, keepdims=True)
    g_w = jnp.max(pg, axis=-1, keepdims=True)
    g_sel = jnp.min(jnp.where(is_grp & (pg == g_w), lane, LANES), axis=-1, keepdims=True)
    e_lane = lane - N_GROUPS
    in_grp = (e_lane >= 0) & (e_lane < N_EXPERTS) & ((e_lane >> 3) == g_sel)
    le = jnp.where(in_grp, logits, NEG_INF)
    me = jnp.max(le, axis=-1, keepdims=True)
    pe = jnp.where(in_grp, jnp.exp(le - me), 0.0)
    pe = pe / jnp.sum(pe, axis=-1, keepdims=True)
    v1 = jnp.max(pe, axis=-1, keepdims=True)
    i1 = jnp.min(jnp.where(in_grp & (pe == v1), lane, LANES), axis=-1, keepdims=True)
    rest = in_grp & (lane != i1)
    pr = jnp.where(rest, pe, -1.0)
    v2 = jnp.max(pr, axis=-1, keepdims=True)
    i2 = jnp.min(jnp.where(rest & (pr == v2), lane, LANES), axis=-1, keepdims=True)
    tot = v1 + v2
    comb = jnp.where(lane == i1, v1 / tot, 0.0) + jnp.where(lane == i2, v2 / tot, 0.0)
    comb = comb * g_w
    return pltpu.roll(comb, LANES - N_GROUPS, 1)


MOE_EXPERTS_PER_STEP = 8


def _moe_kernel(x_ref, sc_ref, sh_ref, g2_ref, gn_ref, gf_ref, wr_ref, win_ref, wout_ref, o_ref,
                h_scr, comb_scr, acc_scr):
    e = pl.program_id(1)

    @pl.when(e == 0)
    def _():
        h = _norm_mod(x_ref[...], gn_ref[...], sc_ref[0], sh_ref[0])
        hb = h.astype(BF16)
        h_scr[...] = hb
        logits = jnp.dot(h, wr_ref[...], precision=HIGHEST, preferred_element_type=F32)
        comb_scr[...] = _route(logits)
        acc_scr[...] = jnp.zeros_like(acc_scr)

    per = win_ref.shape[0]
    hb = h_scr[...]
    comb = comb_scr[...]
    lane = _iota(comb.shape, 1)
    acts = []
    for j in range(per):
        hid = _dot(hb, win_ref[j])
        a = hid[:, :D_EXPERT]
        b = hid[:, D_EXPERT:]
        w = jnp.sum(jnp.where(lane == e * per + j, comb, 0.0), axis=1, keepdims=True)
        acts.append((a * _sigmoid(a) * b * w).astype(BF16))
    act = jnp.concatenate(acts, axis=1)
    acc_scr[...] += _dot(act, wout_ref[...].reshape(per * D_EXPERT, D_MODEL))

    @pl.when(e == pl.num_programs(1) - 1)
    def _():
        y = x_ref[...] + g2_ref[0] * acc_scr[...]
        ms = jnp.mean(y * y, axis=-1, keepdims=True)
        o_ref[...] = y * lax.rsqrt(ms + EPS) * gf_ref[...]


def _moe(x, sc, sh, g2, gn, gf, wr, w_ein, w_eout, tm):
    n = x.shape[0]
    nb, r, _ = sc.shape
    tiles_per_b = (n // nb) // tm
    row = lambda i, e: (i, 0)
    mod = lambda i, e: (i // tiles_per_b, 0, 0)
    full = lambda i, e: (0, 0)
    per = MOE_EXPERTS_PER_STEP
    return pl.pallas_call(
        _moe_kernel,
        grid=(n // tm, N_EXPERTS // per),
        in_specs=[pl.BlockSpec((tm, D_MODEL), row),
                  pl.BlockSpec((1, r, D_MODEL), mod), pl.BlockSpec((1, r, D_MODEL), mod),
                  pl.BlockSpec((1, r, D_MODEL), mod),
                  pl.BlockSpec((1, D_MODEL), full), pl.BlockSpec((1, D_MODEL), full),
                  pl.BlockSpec((D_MODEL, LANES), full),
                  pl.BlockSpec((per, D_MODEL, 2 * D_EXPERT), lambda i, e: (e, 0, 0)),
                  pl.BlockSpec((per, D_EXPERT, D_MODEL), lambda i, e: (e, 0, 0))],
        out_specs=pl.BlockSpec((tm, D_MODEL), row),
        out_shape=jax.ShapeDtypeStruct((n, D_MODEL), F32),
        scratch_shapes=[pltpu.VMEM((tm, D_MODEL), BF16), pltpu.VMEM((tm, LANES), F32),
                        pltpu.VMEM((tm, D_MODEL), F32)],
        compiler_params=_params("arbitrary", "arbitrary"),
        name="moe",
    )(x, sc, sh, g2, gn, gf, wr, w_ein, w_eout)


PAGE = 128
MOBA_PAGES_PER_STEP = 8
NSA_PAGES_PER_STEP = 8


def _page_view(cache):
    n_phys, page, heads, dh = cache.shape
    return jnp.transpose(cache, (0, 2, 3, 1)).reshape(n_phys, heads * dh, page)


def _page_specs(rows, n_pages, per_step):
    def spec(u):
        return pl.BlockSpec((1, rows, PAGE), lambda b, s, pt: (pt[b * n_pages + s * per_step + u], 0, 0))
    return [spec(u) for u in range(per_step)]


def _head_diag(full, heads):
    rows = full.shape[0]
    head = _iota((rows, D_HEAD), 0) // (rows // heads)
    out = jnp.zeros((rows, D_HEAD), F32)
    for h in range(heads):
        out = out + jnp.where(head == h, full[:, h * D_HEAD:(h + 1) * D_HEAD], 0.0)
    return out


def _moba_s_kernel(pt_ref, qbd_ref, *refs):
    per = MOBA_PAGES_PER_STEP
    k_refs, v_refs = refs[:per], refs[per:2 * per]
    m_ref, l_ref, ks_ref, o_ref = refs[2 * per:]
    s = pl.program_id(1)

    @pl.when(s == 0)
    def _():
        m_ref[...] = jnp.zeros_like(m_ref)
        l_ref[...] = jnp.zeros_like(l_ref)
        ks_ref[...] = jnp.zeros_like(ks_ref)

    qbd = qbd_ref[0]
    lane_q = _iota(m_ref.shape[1:], 1)
    lane_k = _iota(ks_ref.shape[1:], 1)
    ppb = MOBA_BLOCK // PAGE
    m_all, l_all, ks_all = m_ref[0], l_ref[0], ks_ref[0]
    kts = [k_refs[u][0] for u in range(per)]
    sc_all = _dot(qbd, jnp.concatenate([kt.astype(BF16) for kt in kts], axis=1))
    for j in range(per // ppb):
        blk = s * (per // ppb) + j
        sc = sc_all[:, j * MOBA_BLOCK:(j + 1) * MOBA_BLOCK]
        m = jnp.max(sc, axis=-1, keepdims=True)
        p = jnp.exp(sc - m)
        l = jnp.sum(p, axis=-1, keepdims=True)
        vt = jnp.concatenate([v_refs[j * ppb + t][0].astype(BF16) for t in range(ppb)], axis=1)
        o_ref[0, j] = _head_diag(_dot_nt(p.astype(BF16), vt), MOBA_HEADS)
        kb = kts[j * ppb]
        for t in range(1, ppb):
            kb = kb + kts[j * ppb + t]
        ksum = jnp.sum(kb, axis=-1, keepdims=True)
        m_all = jnp.where(lane_q == blk, m, m_all)
        l_all = jnp.where(lane_q == blk, l, l_all)
        ks_all = jnp.where(lane_k == blk, ksum, ks_all)
    m_ref[0] = m_all
    l_ref[0] = l_all
    ks_ref[0] = ks_all


def _moba_s_pass(pt_flat, qbd, kt_pages, vt_pages, batch, n_pages):
    per = MOBA_PAGES_PER_STEP
    rows = qbd.shape[1]
    ppb = MOBA_BLOCK // PAGE
    stat = lambda b, s, pt: (b, 0, 0)
    return pl.pallas_call(
        _moba_s_kernel,
        grid_spec=pltpu.PrefetchScalarGridSpec(
            num_scalar_prefetch=1,
            grid=(batch, n_pages // per),
            in_specs=[pl.BlockSpec((1, rows, MOBA_WIDTH), stat)]
            + _page_specs(MOBA_WIDTH, n_pages, per) + _page_specs(MOBA_WIDTH, n_pages, per),
            out_specs=[pl.BlockSpec((1, rows, LANES), stat), pl.BlockSpec((1, rows, LANES), stat),
                       pl.BlockSpec((1, MOBA_WIDTH, LANES), stat),
                       pl.BlockSpec((1, per // ppb, rows, D_HEAD), lambda b, s, pt: (b, s, 0, 0))]),
        out_shape=[jax.ShapeDtypeStruct((batch, rows, LANES), F32), jax.ShapeDtypeStruct((batch, rows, LANES), F32),
                   jax.ShapeDtypeStruct((batch, MOBA_WIDTH, LANES), F32),
                   jax.ShapeDtypeStruct((batch, n_pages // ppb, rows, D_HEAD), F32)],
        compiler_params=_params("arbitrary", "arbitrary"),
        name="moba_decode_pages",
    )(pt_flat, qbd, *([kt_pages] * per), *([vt_pages] * per))


def _moba_s_combine_kernel(m_ref, l_ref, ks_ref, o_ref, qf_ref, qbd_ref, kn_ref, vn_ref, out_ref, *, n_pages, ts):
    rows = m_ref.shape[1]
    nblk = n_pages // (MOBA_BLOCK // PAGE)
    lane = _iota((rows, LANES), 1)
    kmean = ks_ref[0] * (1.0 / MOBA_BLOCK)
    gate = jnp.dot(qf_ref[0], kmean, precision=HIGHEST, preferred_element_type=F32)
    gate = jnp.where(lane < nblk, gate, NEG_INF)
    selp = _rank_select(gate, nblk, MOBA_TOPK) & (lane < nblk)
    qbd = qbd_ref[0]
    s_own = _dot_nt(qbd, kn_ref[0].astype(BF16))
    valid_own = lane <= (_iota((rows, LANES), 0) % ts)
    s_own = jnp.where(valid_own, s_own, NEG_INF)
    m_all = jnp.where(selp, m_ref[0], NEG_INF)
    big_m = jnp.maximum(jnp.max(m_all, axis=-1, keepdims=True), jnp.max(s_own, axis=-1, keepdims=True))
    wgt = jnp.where(selp, jnp.exp(m_ref[0] - big_m), 0.0)
    p_own = jnp.where(valid_own, jnp.exp(s_own - big_m), 0.0)
    denom = jnp.sum(wgt * l_ref[0], axis=-1, keepdims=True) + jnp.sum(p_own, axis=-1, keepdims=True)
    num = _head_diag(_dot(p_own.astype(BF16), vn_ref[0].astype(BF16)), MOBA_HEADS)
    for j in range(nblk):
        num = num + wgt[:, j:j + 1] * o_ref[0, j]
    out_ref[0] = num / jnp.maximum(denom, TINY)


def _moba_s_combine(m, l, ks, o, qf, qbd, kn, vn, n_pages, ts):
    batch, rows, _ = m.shape
    b3 = lambda b: (b, 0, 0)
    return pl.pallas_call(
        functools.partial(_moba_s_combine_kernel, n_pages=n_pages, ts=ts),
        grid=(batch,),
        in_specs=[pl.BlockSpec((1, rows, LANES), b3), pl.BlockSpec((1, rows, LANES), b3),
                  pl.BlockSpec((1, MOBA_WIDTH, LANES), b3),
                  pl.BlockSpec((1, o.shape[1], rows, D_HEAD), lambda b: (b, 0, 0, 0)),
                  pl.BlockSpec((1, rows, MOBA_WIDTH), b3), pl.BlockSpec((1, rows, MOBA_WIDTH), b3),
                  pl.BlockSpec((1, LANES, MOBA_WIDTH), b3), pl.BlockSpec((1, LANES, MOBA_WIDTH), b3)],
        out_specs=pl.BlockSpec((1, rows, D_HEAD), b3),
        out_shape=jax.ShapeDtypeStruct((batch, rows, D_HEAD), F32),
        compiler_params=_params("arbitrary"),
        name="moba_decode_combine",
    )(m, l, ks, o, qf, qbd, kn, vn)


def _block_diag_q(q, batch, ts, heads):
    q4 = q.reshape(batch, ts, heads, D_HEAD)
    eye = jnp.eye(heads, dtype=q.dtype)
    return jnp.einsum('bchd,hk->bhckd', q4, eye).reshape(batch, heads * ts, heads * D_HEAD)


def _pad_rows(a, batch, ts):
    a3 = a.reshape(batch, ts, a.shape[-1])
    return jnp.pad(a3, ((0, 0), (0, LANES - ts), (0, 0)))


def _compress_s_kernel(pt_ref, *refs):
    per = NSA_PAGES_PER_STEP
    page_refs = refs[:per]
    pea_ref, peb_ref, wa_ref, wb_ref, w2_ref, o_ref, x_scr, a_scr, b_scr = refs[per:]
    s = pl.program_id(1)
    nseg = per * PAGE // CMP_STRIDE
    for u in range(per):
        x_scr[u * PAGE:(u + 1) * PAGE, :] = page_refs[u][0].T
    a = jnp.zeros((nseg, a_scr.shape[1]), F32)
    b = jnp.zeros((nseg, b_scr.shape[1]), F32)
    for l in range(CMP_STRIDE):
        xl = x_scr[pl.ds(l, nseg, stride=CMP_STRIDE), :]
        a = a + _dot((xl + pea_ref[l:l + 1, :]).astype(BF16), wa_ref[l])
        b = b + _dot((xl + peb_ref[l:l + 1, :]).astype(BF16), wb_ref[l])
    a_scr[pl.ds(pl.multiple_of(s * nseg, nseg), nseg), :] = a
    b_scr[pl.ds(pl.multiple_of(s * nseg, nseg), nseg), :] = b

    @pl.when(s == pl.num_programs(1) - 1)
    def _():
        total = a_scr.shape[0]
        hid = a_scr[...] + pltpu.roll(b_scr[...], total - 1, 0)
        o_ref[0] = _dot(_gelu(hid).astype(BF16), w2_ref[...])


def _compress_s(pt_flat, pages, cw, batch, n_pages):
    wa, wb, pea, peb, w2bd = cw
    per = NSA_PAGES_PER_STEP
    total = n_pages * PAGE // CMP_STRIDE
    hidden = NSA_KV_HEADS * CMP_HIDDEN
    full2 = lambda b, s, pt: (0, 0)
    full3 = lambda b, s, pt: (0, 0, 0)
    return pl.pallas_call(
        _compress_s_kernel,
        grid_spec=pltpu.PrefetchScalarGridSpec(
            num_scalar_prefetch=1,
            grid=(batch, n_pages // per),
            in_specs=_page_specs(NSA_KV_WIDTH, n_pages, per)
            + [pl.BlockSpec((CMP_STRIDE, NSA_KV_WIDTH), full2), pl.BlockSpec((CMP_STRIDE, NSA_KV_WIDTH), full2),
               pl.BlockSpec((CMP_STRIDE, NSA_KV_WIDTH, hidden), full3),
               pl.BlockSpec((CMP_STRIDE, NSA_KV_WIDTH, hidden), full3),
               pl.BlockSpec((hidden, NSA_KV_WIDTH), full2)],
            out_specs=pl.BlockSpec((1, total, NSA_KV_WIDTH), lambda b, s, pt: (b, 0, 0)),
            scratch_shapes=[pltpu.VMEM((per * PAGE, NSA_KV_WIDTH), F32), pltpu.VMEM((total, hidden), F32),
                            pltpu.VMEM((total, hidden), F32)]),
        out_shape=jax.ShapeDtypeStruct((batch, total, NSA_KV_WIDTH), F32),
        compiler_params=_params("arbitrary", "arbitrary"),
        name="compress_decode",
    )(pt_flat, *([pages] * per), pea, peb, wa, wb, w2bd)


def _stack_group_q(q_ref, g):
    return jnp.concatenate(
        [q_ref[0, :, (g * NSA_GROUP + r) * D_HEAD:(g * NSA_GROUP + r + 1) * D_HEAD] * SCALE
         for r in range(NSA_GROUP)], axis=0).astype(BF16)


def _nsa_s_kernel(q_ref, kc_ref, vc_ref, wk_ref, wv_ref, kn_ref, vn_ref, gate_ref, ov_ref, ex_ref, part_ref, sel_ref,
                  *, ts):
    rows = NSA_GROUP * ts
    nc_pad = kc_ref.shape[1]
    wlen = wk_ref.shape[3]
    gates = _sigmoid(gate_ref[0])
    n_i = _iota((rows, nc_pad), 1)
    valid_c = n_i < nc_pad - 1
    c_of_row = _iota((rows, 1), 0) % ts
    valid_w = _iota((rows, wlen), 1) >= c_of_row
    valid_n = _iota((rows, LANES), 1) <= c_of_row
    for g in range(NSA_KV_HEADS):
        lane0 = g * D_HEAD
        qs = _stack_group_q(q_ref, g)
        s = jnp.where(valid_c, _dot_nt(qs, kc_ref[0, :, lane0:lane0 + D_HEAD].astype(BF16)), NEG_INF)
        m = jnp.max(s, axis=-1, keepdims=True)
        p = jnp.where(valid_c, jnp.exp(s - m), 0.0)
        p = p / jnp.maximum(jnp.sum(p, axis=-1, keepdims=True), TINY)
        o_c = _dot(p.astype(BF16), vc_ref[0, :, lane0:lane0 + D_HEAD].astype(BF16))
        psum = jnp.sum(p.reshape(NSA_GROUP, ts, nc_pad), axis=0)
        imp = jnp.dot(psum, ov_ref[...], precision=HIGHEST, preferred_element_type=F32)
        keep = _rank_select(imp, LANES, SLC_TOPN - 1).astype(BF16)
        sel_ref[0, g] = jnp.where(_dot(keep, ex_ref[...]) > 0.5, 0.0, NEG_INF)
        s_w = jnp.where(valid_w, _dot(qs, wk_ref[0, g].astype(BF16)), NEG_INF)
        s_n = jnp.where(valid_n, _dot_nt(qs, kn_ref[0, :, lane0:lane0 + D_HEAD].astype(BF16)), NEG_INF)
        m = jnp.maximum(jnp.max(s_w, axis=-1, keepdims=True), jnp.max(s_n, axis=-1, keepdims=True))
        p_w = jnp.where(valid_w, jnp.exp(s_w - m), 0.0)
        p_n = jnp.where(valid_n, jnp.exp(s_n - m), 0.0)
        den = jnp.sum(p_w, axis=-1, keepdims=True) + jnp.sum(p_n, axis=-1, keepdims=True)
        o_w = (_dot_nt(p_w.astype(BF16), wv_ref[0, g].astype(BF16))
               + _dot(p_n.astype(BF16), vn_ref[0, :, lane0:lane0 + D_HEAD].astype(BF16))) / jnp.maximum(den, TINY)
        for r in range(NSA_GROUP):
            hd = g * NSA_GROUP + r
            rs = slice(r * ts, (r + 1) * ts)
            part_ref[0, :, hd * D_HEAD:(hd + 1) * D_HEAD] = (
                gates[:, 3 * hd:3 * hd + 1] * o_c[rs] + gates[:, 3 * hd + 2:3 * hd + 3] * o_w[rs])


def _nsa_s(q3, kcmp, vcmp, wk_t, wv_t, kn_w, vn_w, gate3, ts):
    batch = q3.shape[0]
    nc_pad = kcmp.shape[1]
    wlen = wk_t.shape[3]
    ov = _overlap_matrix(nc_pad, LANES)
    n_keys = LANES * SLC_BLOCK
    expand = (jnp.arange(LANES)[:, None] == jnp.arange(n_keys)[None, :] // SLC_BLOCK).astype(BF16)
    b3 = lambda b: (b, 0, 0)
    b4 = lambda b: (b, 0, 0, 0)
    return pl.pallas_call(
        functools.partial(_nsa_s_kernel, ts=ts),
        grid=(batch,),
        in_specs=[pl.BlockSpec((1, ts, NSA_WIDTH), b3),
                  pl.BlockSpec((1, nc_pad, NSA_KV_WIDTH), b3), pl.BlockSpec((1, nc_pad, NSA_KV_WIDTH), b3),
                  pl.BlockSpec((1, NSA_KV_HEADS, D_HEAD, wlen), b4), pl.BlockSpec((1, NSA_KV_HEADS, D_HEAD, wlen), b4),
                  pl.BlockSpec((1, LANES, NSA_KV_WIDTH), b3), pl.BlockSpec((1, LANES, NSA_KV_WIDTH), b3),
                  pl.BlockSpec((1, ts, LANES), b3),
                  pl.BlockSpec((nc_pad, LANES), lambda b: (0, 0)),
                  pl.BlockSpec((LANES, n_keys), lambda b: (0, 0))],
        out_specs=[pl.BlockSpec((1, ts, NSA_WIDTH), b3), pl.BlockSpec((1, NSA_KV_HEADS, ts, n_keys), b4)],
        out_shape=[jax.ShapeDtypeStruct((batch, ts, NSA_WIDTH), F32),
                   jax.ShapeDtypeStruct((batch, NSA_KV_HEADS, ts, n_keys), F32)],
        compiler_params=_params("arbitrary"),
        name="nsa_decode_cmp_win",
    )(q3, kcmp, vcmp, wk_t, wv_t, kn_w, vn_w, gate3, ov, expand)


def _slc_s_kernel(pt_ref, q_ref, bias_ref, *refs, ts):
    per = NSA_PAGES_PER_STEP
    k_refs, v_refs = refs[:per], refs[per:2 * per]
    kn_ref, vn_ref, gate_ref, part_ref, o_ref, m_scr, l_scr, acc_scr = refs[2 * per:]
    s = pl.program_id(1)
    rows = NSA_GROUP * ts
    zero = jnp.zeros((rows, D_HEAD), BF16)
    q2 = jnp.concatenate([jnp.concatenate([_stack_group_q(q_ref, 0), zero], axis=1),
                          jnp.concatenate([zero, _stack_group_q(q_ref, 1)], axis=1)], axis=0)
    in_g0 = _iota((NSA_KV_HEADS * rows, D_HEAD), 0) < rows

    def own_group(full):
        return jnp.where(in_g0, full[:, :D_HEAD], full[:, D_HEAD:])

    @pl.when(s == 0)
    def _():
        valid_n = _iota((NSA_KV_HEADS * rows, LANES), 1) <= (_iota((NSA_KV_HEADS * rows, 1), 0) % ts)
        sc = jnp.where(valid_n, _dot_nt(q2, kn_ref[0].astype(BF16)), NEG_INF)
        m = jnp.max(sc, axis=-1, keepdims=True)
        p = jnp.exp(sc - m)
        m_scr[...] = m
        l_scr[...] = jnp.sum(p, axis=-1, keepdims=True)
        acc_scr[...] = own_group(_dot(p.astype(BF16), vn_ref[0].astype(BF16)))

    kt = jnp.concatenate([k_refs[u][0].astype(BF16) for u in range(per)], axis=1)
    vt = jnp.concatenate([v_refs[u][0].astype(BF16) for u in range(per)], axis=1)
    bias = jnp.concatenate([bias_ref[0, g] for g in range(NSA_KV_HEADS) for _ in range(NSA_GROUP)], axis=0)
    sc = _dot(q2, kt) + bias
    m_old = m_scr[...]
    m_new = jnp.maximum(m_old, jnp.max(sc, axis=-1, keepdims=True))
    pf = jnp.exp(sc - m_new)
    alpha = jnp.exp(m_old - m_new)
    l_scr[...] = alpha * l_scr[...] + jnp.sum(pf, axis=-1, keepdims=True)
    acc_scr[...] = alpha * acc_scr[...] + own_group(_dot_nt(pf.astype(BF16), vt))
    m_scr[...] = m_new

    @pl.when(s == pl.num_programs(1) - 1)
    def _():
        gates = _sigmoid(gate_ref[0])
        o_s = acc_scr[...] / jnp.maximum(l_scr[...], TINY)
        for hd in range(NSA_HEADS):
            cols = slice(hd * D_HEAD, (hd + 1) * D_HEAD)
            o_ref[0, :, cols] = part_ref[0, :, cols] + gates[:, 3 * hd + 1:3 * hd + 2] * o_s[hd * ts:(hd + 1) * ts]


def _slc_s(pt_flat, q3, sel_bias, k_pages, v_pages, kn_s, vn_s, gate3, part, n_pages, ts):
    batch = q3.shape[0]
    per = NSA_PAGES_PER_STEP
    rows = NSA_GROUP * ts
    b3 = lambda b, s, pt: (b, 0, 0)
    return pl.pallas_call(
        functools.partial(_slc_s_kernel, ts=ts),
        grid_spec=pltpu.PrefetchScalarGridSpec(
            num_scalar_prefetch=1,
            grid=(batch, n_pages // per),
            in_specs=[pl.BlockSpec((1, ts, NSA_WIDTH), b3),
                      pl.BlockSpec((1, NSA_KV_HEADS, ts, per * PAGE), lambda b, s, pt: (b, 0, 0, s))]
            + _page_specs(NSA_KV_WIDTH, n_pages, per) + _page_specs(NSA_KV_WIDTH, n_pages, per)
            + [pl.BlockSpec((1, LANES, NSA_KV_WIDTH), b3), pl.BlockSpec((1, LANES, NSA_KV_WIDTH), b3),
               pl.BlockSpec((1, ts, LANES), b3), pl.BlockSpec((1, ts, NSA_WIDTH), b3)],
            out_specs=pl.BlockSpec((1, ts, NSA_WIDTH), b3),
            scratch_shapes=[pltpu.VMEM((NSA_KV_HEADS * rows, 1), F32), pltpu.VMEM((NSA_KV_HEADS * rows, 1), F32),
                            pltpu.VMEM((NSA_KV_HEADS * rows, D_HEAD), F32)]),
        out_shape=jax.ShapeDtypeStruct((batch, ts, NSA_WIDTH), F32),
        compiler_params=_params("arbitrary", "arbitrary"),
        name="nsa_decode_slc",
    )(pt_flat, q3, sel_bias, *([k_pages] * per), *([v_pages] * per), kn_s, vn_s, gate3, part)


def _prep_weights(w_ada, b_ada, norm_mix_g, w_in, pe_cmp_k, w_cmp_k1, w_cmp_k2, pe_cmp_v, w_cmp_v1, w_cmp_v2,
                  w_br_a, w_br_b, w_out, norm_ffn_g, w_router_grp, w_router_exp, w_expert_in, w_expert_out,
                  norm_final_g):
    wr = jnp.concatenate([w_router_grp, w_router_exp], axis=1)
    wr = jnp.pad(wr, ((0, 0), (0, LANES - wr.shape[1])))
    return dict(
        w_ada=w_ada, b_ada=b_ada.reshape(1, -1), g_mix=norm_mix_g.reshape(1, -1),
        w_in=_reorder_w_in(w_in),
        cmp_k=_compress_weights(pe_cmp_k, w_cmp_k1, w_cmp_k2),
        cmp_v=_compress_weights(pe_cmp_v, w_cmp_v1, w_cmp_v2),
        w_br_a=w_br_a.astype(BF16), w_br_b=w_br_b.astype(BF16), w_out=w_out.astype(BF16),
        g_ffn=norm_ffn_g.reshape(1, -1), wr=wr,
        w_ein=w_expert_in.astype(BF16), w_eout=w_expert_out.astype(BF16),
        g_final=norm_final_g.reshape(1, -1))


def _prompt_layer(x, mod, w, batch, seq):
    sh1, sc1, g1, sh2, sc2, g2 = mod
    cos, sin = _rope_tables(jnp.arange(seq, dtype=jnp.int32))
    qk_a, v_a, q_b, k_n, v_n, gate, mg = _inproj(x, sc1, sh1, w['g_mix'], cos, sin, w['w_in'], 256)
    o_a = _moba_p(qk_a, v_a, batch, seq)
    kcmp = _compress_p(k_n[:, :NSA_KV_WIDTH], w['cmp_k'], batch, seq)
    vcmp = _compress_p(v_n[:, :NSA_KV_WIDTH], w['cmp_v'], batch, seq)
    o_b = _nsa_p(q_b, k_n, v_n, kcmp, vcmp, gate, batch, seq)
    x1 = _merge(o_a, o_b, mg, x, g1, w['w_br_a'], w['w_br_b'], w['w_out'], 256)
    y = _moe(x1, sc2, sh2, g2, w['g_ffn'], w['g_final'], w['wr'], w['w_ein'], w['w_eout'], 512)
    return y, (qk_a, v_a, k_n, v_n)


def _sample_layer(x, mod, w, caches, win_state, page_table, batch, ts):
    sh1, sc1, g1, sh2, sc2, g2 = mod
    moba_k, moba_v, cmp_k, cmp_v, slc_k, slc_v = caches
    win_k, win_v = win_state
    n_pages = page_table.shape[1]
    assert moba_k.shape[1] == PAGE and win_k.shape[1] == WINDOW and ts <= LANES
    assert n_pages * PAGE == LANES * SLC_BLOCK and n_pages * PAGE // MOBA_BLOCK <= LANES
    n = batch * ts
    pos = n_pages * PAGE + (jnp.arange(n, dtype=jnp.int32) % ts)
    cos, sin = _rope_tables(pos)
    qk_a, v_a, q_b, k_n, v_n, gate, mg = _inproj(x, sc1, sh1, w['g_mix'], cos, sin, w['w_in'], n)
    pt_flat = page_table.reshape(-1)
    kv = NSA_KV_WIDTH
    qf = _block_diag_q(qk_a[:, :MOBA_WIDTH], batch, ts, MOBA_HEADS)
    qbd = (qf * SCALE).astype(BF16)
    m, l, ks, o = _moba_s_pass(pt_flat, qbd, _page_view(moba_k), _page_view(moba_v), batch, n_pages)
    o_a = _moba_s_combine(m, l, ks, o, qf, qbd, _pad_rows(qk_a[:, MOBA_WIDTH:], batch, ts), _pad_rows(v_a, batch, ts),
                          n_pages, ts)
    o_a = o_a.reshape(batch, MOBA_HEADS, ts, D_HEAD).transpose(0, 2, 1, 3).reshape(n, MOBA_WIDTH)
    kcmp = _compress_s(pt_flat, _page_view(cmp_k), w['cmp_k'], batch, n_pages)
    vcmp = _compress_s(pt_flat, _page_view(cmp_v), w['cmp_v'], batch, n_pages)
    q3 = q_b.reshape(batch, ts, NSA_WIDTH)
    gate3 = gate.reshape(batch, ts, LANES)
    part, sel = _nsa_s(q3, kcmp, vcmp, jnp.transpose(win_k, (0, 2, 3, 1)), jnp.transpose(win_v, (0, 2, 3, 1)),
                       _pad_rows(k_n[:, 2 * kv:], batch, ts), _pad_rows(v_n[:, 2 * kv:], batch, ts), gate3, ts)
    o_b = _slc_s(pt_flat, q3, sel, _page_view(slc_k), _page_view(slc_v),
                 _pad_rows(k_n[:, kv:2 * kv], batch, ts), _pad_rows(v_n[:, kv:2 * kv], batch, ts), gate3, part,
                 n_pages, ts).reshape(n, NSA_WIDTH)
    x1 = _merge(o_a, o_b, mg, x, g1, w['w_br_a'], w['w_br_b'], w['w_out'], n)
    y = _moe(x1, sc2, sh2, g2, w['g_ffn'], w['g_final'], w['wr'], w['w_ein'], w['w_eout'], n)
    return y, (qk_a, v_a, k_n, v_n)


def kernel(x_prompt, x_sample, c_prompt, c_sample, cache_moba_k, cache_moba_v, cache_nsa_cmp_k, cache_nsa_cmp_v,
           cache_nsa_slc_k, cache_nsa_slc_v, state_nsa_win_k, state_nsa_win_v, page_table, w_ada, b_ada, norm_mix_g,
           w_in, pe_cmp_k, w_cmp_k1, w_cmp_k2, pe_cmp_v, w_cmp_v1, w_cmp_v2, w_br_a, w_br_b, w_out, norm_ffn_g,
           w_router_grp, w_router_exp, w_expert_in, w_expert_out, norm_final_g):
    bp, tp, _ = x_prompt.shape
    bs, ts, _ = x_sample.shape
    w = _prep_weights(w_ada[0], b_ada[0], norm_mix_g[0], w_in[0], pe_cmp_k[0], w_cmp_k1[0], w_cmp_k2[0], pe_cmp_v[0],
                      w_cmp_v1[0], w_cmp_v2[0], w_br_a[0], w_br_b[0], w_out[0], norm_ffn_g[0], w_router_grp[0],
                      w_router_exp[0], w_expert_in[0], w_expert_out[0], norm_final_g)
    mod = _ada(jnp.concatenate([c_prompt, c_sample], axis=0), w['w_ada'], w['b_ada'])
    mod_p = [m.reshape(bp, 1, D_MODEL) for m in jnp.split(mod[:bp], 6, axis=-1)]
    y_p, (qk_a, v_a, k_n, v_n) = _prompt_layer(x_prompt.reshape(bp * tp, D_MODEL), mod_p, w, bp, tp)

    mod_s = [jnp.repeat(m, ts, axis=0).reshape(1, bs * ts, D_MODEL) for m in jnp.split(mod[bp:], 6, axis=-1)]
    caches = (cache_moba_k[0], cache_moba_v[0], cache_nsa_cmp_k[0], cache_nsa_cmp_v[0], cache_nsa_slc_k[0],
              cache_nsa_slc_v[0])
    y_s, new_s = _sample_layer(x_sample.reshape(bs * ts, D_MODEL), mod_s, w, caches,
                               (state_nsa_win_k[0], state_nsa_win_v[0]), page_table, bs, ts)

    kv = NSA_KV_WIDTH

    def new_rows(new, b, t):
        qk, v, k_n, v_n = new
        rows = lambda a, heads: a.reshape(1, b, t, heads, D_HEAD)
        return (rows(qk[:, MOBA_WIDTH:], MOBA_HEADS), rows(v, MOBA_HEADS),
                rows(k_n[:, :kv], NSA_KV_HEADS), rows(v_n[:, :kv], NSA_KV_HEADS),
                rows(k_n[:, kv:2 * kv], NSA_KV_HEADS), rows(v_n[:, kv:2 * kv], NSA_KV_HEADS),
                rows(k_n[:, 2 * kv:], NSA_KV_HEADS), rows(v_n[:, 2 * kv:], NSA_KV_HEADS))

    wb = state_nsa_win_k.shape[2]
    outs_p = new_rows((qk_a, v_a, k_n, v_n), bp, tp)
    outs_p = outs_p[:6] + (outs_p[6][:, :, tp - wb:], outs_p[7][:, :, tp - wb:])
    outs_s = new_rows(new_s, bs, ts)
    win_k = jnp.concatenate([state_nsa_win_k, outs_s[6]], axis=2)[:, :, ts:]
    win_v = jnp.concatenate([state_nsa_win_v, outs_s[7]], axis=2)[:, :, ts:]
    outs_s = outs_s[:6] + (win_k, win_v)
    return (y_p.reshape(bp, tp, D_MODEL), y_s.reshape(bs, ts, D_MODEL)) + outs_p + outs_s
```

```python
import functools

import jax
import jax.numpy as jnp
from jax import lax
from jax.experimental import pallas as pl
from jax.experimental.pallas import tpu as pltpu

D_MODEL = 1024
D_HEAD = 64
HALF = D_HEAD // 2
MOBA_HEADS = 8
MOBA_BLOCK = 256
MOBA_TOPK = 3
NSA_HEADS = 8
NSA_KV_HEADS = 2
NSA_GROUP = NSA_HEADS // NSA_KV_HEADS
CMP_LEN = 32
CMP_STRIDE = 16
CMP_HIDDEN = 128
SLC_BLOCK = 64
SLC_TOPN = 16
WINDOW = 512
N_GROUPS = 4
EXPERTS_PER_GROUP = 8
N_EXPERTS = N_GROUPS * EXPERTS_PER_GROUP
D_EXPERT = 256
ROPE_THETA = 10000.0
EPS = 1e-6
NEG_INF = -1e30
BIG = 1e30
TINY = 1e-30
MOBA_WIDTH = MOBA_HEADS * D_HEAD
NSA_WIDTH = NSA_HEADS * D_HEAD
NSA_KV_WIDTH = NSA_KV_HEADS * D_HEAD
SCALE = D_HEAD ** -0.5

LANES = 128
VMEM_LIMIT = 48 * 1024 * 1024

F32 = jnp.float32
BF16 = jnp.bfloat16
HIGHEST = lax.Precision.HIGHEST


def _params(*sem):
    return pltpu.CompilerParams(dimension_semantics=sem, vmem_limit_bytes=VMEM_LIMIT)


def _dot(a, b):
    return jnp.dot(a, b, preferred_element_type=F32)


def _dot_nt(a, b, precision=None):
    return lax.dot_general(a, b, (((1,), (1,)), ((), ())), precision=precision,
                           preferred_element_type=F32)


def _sigmoid(x):
    return 1.0 / (1.0 + jnp.exp(-x))


def _iota(shape, dim):
    return lax.broadcasted_iota(jnp.int32, shape, dim)


def _ada_kernel(c_ref, w_ref, b_ref, o_ref):
    c = c_ref[...]
    s = c * _sigmoid(c)
    o_ref[...] = _dot(s.astype(BF16), w_ref[...].astype(BF16)) + b_ref[...]


def _ada(c, w, b):
    n = c.shape[0]
    tn = 1024
    return pl.pallas_call(
        _ada_kernel,
        grid=(w.shape[1] // tn,),
        in_specs=[pl.BlockSpec((n, D_MODEL), lambda j: (0, 0)),
                  pl.BlockSpec((D_MODEL, tn), lambda j: (0, j)),
                  pl.BlockSpec((1, tn), lambda j: (0, j))],
        out_specs=pl.BlockSpec((n, tn), lambda j: (0, j)),
        out_shape=jax.ShapeDtypeStruct((n, w.shape[1]), F32),
        compiler_params=_params("arbitrary"),
        name="ada",
    )(c, w, b)


_IN_GROUPS = ((2 * MOBA_WIDTH, True),
              (MOBA_WIDTH, False),
              (NSA_WIDTH, True),
              (3 * NSA_KV_WIDTH, True),
              (3 * NSA_KV_WIDTH, False),
              (LANES, False),
              (2 * D_MODEL, False))
_IN_COLS_PAD = sum(w for w, _ in _IN_GROUPS)


def _norm_mod(x, g, sc, sh):
    ms = jnp.mean(x * x, axis=-1, keepdims=True)
    y = x * lax.rsqrt(ms + EPS) * g
    return y * (1.0 + sc) + sh


def _inproj_kernel(x_ref, sc_ref, sh_ref, g_ref, cos_ref, sin_ref, w_ref, *out_refs):
    h = _norm_mod(x_ref[...], g_ref[...], sc_ref[0], sh_ref[0]).astype(BF16)
    cos = cos_ref[...]
    sin = sin_ref[...]
    first_half = (_iota(cos.shape, 1) & (D_HEAD - 1)) < HALF

    def rope(y):
        rot = jnp.where(first_half, pltpu.roll(y, LANES - HALF, 1), pltpu.roll(y, HALF, 1))
        return y * cos + rot * sin

    n_groups = len(_IN_GROUPS)
    t_refs = dict(zip(_IN_T_GROUPS, out_refs[n_groups:]))
    col = 0
    for gi, (out_ref, (width, rotary)) in enumerate(zip(out_refs, _IN_GROUPS)):
        chunk = min(width, 512)
        for c in range(0, width, chunk):
            cw = min(chunk, width - c)
            y = _dot(h, w_ref[:, col + c:col + c + cw])
            for s in range(0, cw, LANES):
                piece = y[:, s:s + LANES]
                piece = rope(piece) if rotary else piece
                out_ref[:, c + s:c + s + LANES] = piece
                if gi in t_refs and c + s >= _IN_T_GROUPS[gi]:
                    t0 = c + s - _IN_T_GROUPS[gi]
                    t_refs[gi][0, t0:t0 + LANES, :] = piece.T
        col += width


_IN_T_GROUPS = {0: MOBA_WIDTH, 1: 0, 3: 0, 4: 0}


def _inproj(x, sc, sh, g, cos, sin, w, tm, transposed=False):
    n = x.shape[0]
    nb, r, _ = sc.shape
    tiles_per_b = (n // nb) // tm
    tab_tiles = cos.shape[0] // tm
    row = lambda i: (i, 0)
    mod = lambda i: (i // tiles_per_b, 0, 0)
    tab = lambda i: (i % tab_tiles, 0)
    out_specs = [pl.BlockSpec((tm, wd), row) for wd, _ in _IN_GROUPS]
    out_shape = [jax.ShapeDtypeStruct((n, wd), F32) for wd, _ in _IN_GROUPS]
    if transposed:
        for gi, first in _IN_T_GROUPS.items():
            cols = _IN_GROUPS[gi][0] - first
            out_specs.append(pl.BlockSpec((1, cols, tm), lambda i: (i // tiles_per_b, 0, i % tiles_per_b)))
            out_shape.append(jax.ShapeDtypeStruct((nb, cols, n // nb), F32))
    return pl.pallas_call(
        _inproj_kernel,
        grid=(n // tm,),
        in_specs=[pl.BlockSpec((tm, D_MODEL), row),
                  pl.BlockSpec((1, r, D_MODEL), mod),
                  pl.BlockSpec((1, r, D_MODEL), mod),
                  pl.BlockSpec((1, D_MODEL), lambda i: (0, 0)),
                  pl.BlockSpec((tm, LANES), tab),
                  pl.BlockSpec((tm, LANES), tab),
                  pl.BlockSpec((D_MODEL, _IN_COLS_PAD), lambda i: (0, 0))],
        out_specs=out_specs,
        out_shape=out_shape,
        compiler_params=_params("arbitrary"),
        name="inproj",
    )(x, sc, sh, g, cos, sin, w)


def _reorder_w_in(w_in):
    kv0 = 3 * MOBA_WIDTH + NSA_WIDTH
    kvs = [w_in[:, kv0 + i * NSA_KV_WIDTH:kv0 + (i + 1) * NSA_KV_WIDTH] for i in range(6)]
    g0 = kv0 + 6 * NSA_KV_WIDTH
    ng = 3 * NSA_HEADS
    gate = jnp.pad(w_in[:, g0:g0 + ng], ((0, 0), (0, LANES - ng)))
    parts = [w_in[:, :kv0], kvs[0], kvs[2], kvs[4], kvs[1], kvs[3], kvs[5], gate, w_in[:, g0 + ng:]]
    return jnp.concatenate(parts, axis=1).astype(BF16)


def _rope_tables(pos):
    inv = ROPE_THETA ** (-jnp.arange(HALF, dtype=F32) / HALF)
    ang = pos.astype(F32)[:, None] * inv[None, :]
    cos = jnp.cos(ang)
    sin = jnp.sin(ang)
    cos = jnp.concatenate([cos, cos, cos, cos], axis=1)
    sin = jnp.concatenate([-sin, sin, -sin, sin], axis=1)
    return cos, sin


def _rank_select(score, n_cols, n_keep):
    lane = _iota(score.shape, 1)
    rank = jnp.zeros(score.shape, jnp.int32)
    for jp in range(n_cols):
        col = score[:, jp:jp + 1]
        beats = (col > score) | ((col == score) & (lane > jp))
        rank = rank + beats.astype(jnp.int32)
    return rank < n_keep


def _rank_select_t(score, n_rows, n_keep):
    row = _iota(score.shape, 0)
    rank = jnp.zeros(score.shape, jnp.int32)
    for jp in range(n_rows):
        r = score[jp:jp + 1, :]
        beats = (r > score) | ((r == score) & (row > jp))
        rank = rank + beats.astype(jnp.int32)
    return rank < n_keep


SUBLANES = 8


def _fold_rows(x, op):
    return op(x.reshape(x.shape[0] // SUBLANES, SUBLANES, x.shape[1]), axis=0)


def _attend_t(make_scores, vt_fn, own_c, lo, hi, width, n_chains):
    score_fn = make_scores(None)

    def max_step(c, own, m):
        return tuple(jnp.maximum(mi, _fold_rows(s, jnp.max)) for mi, s in zip(m, score_fn(c, own)))

    m = max_step(own_c, True, tuple(jnp.full((SUBLANES, width), NEG_INF, F32) for _ in range(n_chains)))
    m = lax.fori_loop(lo, hi, lambda c, mm: max_step(c, False, mm), m)
    shifted_fn = make_scores([jnp.max(mi, axis=0, keepdims=True) for mi in m])

    def acc_step(c, own, carry):
        out = []
        for (l, acc), s, vt in zip(carry, shifted_fn(c, own), vt_fn(c)):
            p = jnp.exp(s)
            out.append((l + _fold_rows(p, jnp.sum), acc + _dot(vt, p.astype(BF16))))
        return tuple(out)

    zero = tuple((jnp.zeros((SUBLANES, width), F32), jnp.zeros((D_HEAD, width), F32)) for _ in range(n_chains))
    carry = acc_step(own_c, True, zero)
    carry = lax.fori_loop(lo, hi, lambda c, cr: acc_step(c, False, cr), carry)
    return [acc / jnp.maximum(jnp.sum(l, axis=0, keepdims=True), TINY) for l, acc in carry]


AUG_ONE = 0
AUG_BLK = SUBLANES


def _key_aug(n_keys, block, n_blocks):
    lane = _iota((n_keys, D_HEAD), 1)
    blk = _iota((n_keys, D_HEAD), 0) // block
    hot = (lane == AUG_ONE) | ((lane >= AUG_BLK) & (lane < AUG_BLK + n_blocks) & (lane - AUG_BLK == blk))
    return jnp.where(hot, 1.0, 0.0).astype(BF16)


def _query_aug(qt, shift, bias):
    width = qt.shape[1]
    top = jnp.zeros((SUBLANES, width), F32)
    if shift is not None:
        top = jnp.where(_iota((SUBLANES, width), 0) == AUG_ONE, -shift, 0.0)
    parts = [qt, top]
    used = SUBLANES
    if bias is not None:
        parts.append(bias)
        used += bias.shape[0]
    parts.append(jnp.zeros((D_HEAD - used, width), F32))
    return jnp.concatenate(parts, axis=0).astype(BF16)


MOBA_STEP_WIDTH = 512


def _moba_p_kernel(q_ref, k_ref, v_ref, o_ref, ka_scr, vt_scr, km_scr):
    qi = pl.program_id(2)
    tq = tk = MOBA_BLOCK
    seq = k_ref.shape[1]
    nblk = seq // MOBA_BLOCK
    width = q_ref.shape[2]
    nh = width // D_HEAD

    @pl.when(qi == 0)
    def _():
        kf = k_ref[0]
        km_scr[...] = jnp.mean(kf.reshape(nblk, MOBA_BLOCK, width), axis=1)
        aug = _key_aug(seq, MOBA_BLOCK, nblk)
        for hh in range(nh):
            ka_scr[hh] = jnp.concatenate([kf[:, hh * D_HEAD:(hh + 1) * D_HEAD].astype(BF16), aug], axis=1)
        for j in range(nblk):
            vt_scr[j] = v_ref[0, :, j * tk:(j + 1) * tk].astype(BF16)

    q2t = q_ref[0].T
    km = km_scr[...]
    klane = _iota(km.shape, 1)
    blk = _iota((nblk, tq), 0)
    causal_t = _iota((tk, tq), 0) <= _iota((tk, tq), 1)
    qts, biases = [], []
    for hh in range(nh):
        kmh = jnp.where((klane >= hh * D_HEAD) & (klane < (hh + 1) * D_HEAD), km, 0.0)
        gate = jnp.dot(kmh, q2t, precision=HIGHEST, preferred_element_type=F32)
        gate = jnp.where(blk < qi, gate, NEG_INF)
        keep = _rank_select_t(gate, nblk, MOBA_TOPK) & (blk < qi)
        biases.append(jnp.where(keep, 0.0, NEG_INF))
        qts.append(q2t[hh * D_HEAD:(hh + 1) * D_HEAD, :] * SCALE)

    def make_scores(shift):
        sh = [None] * nh if shift is None else shift
        q_own = [_query_aug(qts[hh], sh[hh], None) for hh in range(nh)]
        q_past = [_query_aug(qts[hh], sh[hh], biases[hh]) for hh in range(nh)]

        def score_fn(c, own):
            out = []
            for hh in range(nh):
                kj = ka_scr[hh, pl.ds(pl.multiple_of(c * tk, tk), tk), :]
                s = _dot(kj, q_own[hh] if own else q_past[hh])
                out.append(jnp.where(causal_t, s, NEG_INF) if own else s)
            return out

        return score_fn

    def values(c):
        return [vt_scr[c, hh * D_HEAD:(hh + 1) * D_HEAD, :] for hh in range(nh)]

    o_ref[0] = jnp.concatenate(_attend_t(make_scores, values, qi, 0, qi, tq, nh), axis=0).T


def _moba_p(qk, vt, batch, seq):
    qk3 = qk.reshape(batch, seq, 2 * MOBA_WIDTH)
    sw = MOBA_STEP_WIDTH
    pairs = MOBA_WIDTH // sw
    nblk = seq // MOBA_BLOCK
    out = pl.pallas_call(
        _moba_p_kernel,
        grid=(batch, pairs, nblk),
        in_specs=[pl.BlockSpec((1, MOBA_BLOCK, sw), lambda b, h, i: (b, i, h)),
                  pl.BlockSpec((1, seq, sw), lambda b, h, i: (b, 0, pairs + h)),
                  pl.BlockSpec((1, sw, seq), lambda b, h, i: (b, h, 0))],
        out_specs=pl.BlockSpec((1, MOBA_BLOCK, sw), lambda b, h, i: (b, i, h)),
        out_shape=jax.ShapeDtypeStruct((batch, seq, MOBA_WIDTH), F32),
        scratch_shapes=[pltpu.VMEM((sw // D_HEAD, seq, 2 * D_HEAD), BF16), pltpu.VMEM((nblk, sw, MOBA_BLOCK), BF16),
                        pltpu.VMEM((nblk, sw), F32)],
        compiler_params=_params("arbitrary", "arbitrary", "arbitrary"),
        name="moba_prompt",
    )(qk3, qk3, vt)
    return out.reshape(batch * seq, MOBA_WIDTH)


def _gelu(x):
    return 0.5 * x * (1.0 + jnp.tanh(0.7978845608028654 * (x + 0.044715 * x * x * x)))


def _compress_p_kernel(seg_ref, pea_ref, peb_ref, wa_ref, wb_ref, w2_ref, o_ref):
    seg = seg_ref[0]
    a = _dot((seg + pea_ref[...]).astype(BF16), wa_ref[...])
    b = _dot((seg + peb_ref[...]).astype(BF16), wb_ref[...])
    nseg = seg.shape[0]
    hid = a + pltpu.roll(b, nseg - 1, 0)
    o_ref[0] = _dot(_gelu(hid).astype(BF16), w2_ref[...])


def _compress_weights(pe, w1, w2):
    g = NSA_KV_HEADS
    eye = jnp.eye(g, dtype=F32)
    w1r = w1.reshape(CMP_LEN, D_HEAD, CMP_HIDDEN)

    def half(lo):
        w = w1r[lo:lo + CMP_STRIDE]
        wbd = jnp.einsum('ldf,gh->lgdhf', w, eye)
        p = jnp.broadcast_to(pe[lo:lo + CMP_STRIDE, None, :], (CMP_STRIDE, g, D_HEAD))
        return wbd.reshape(CMP_STRIDE, g * D_HEAD, g * CMP_HIDDEN).astype(BF16), p.reshape(CMP_STRIDE, g * D_HEAD)

    wa, pea = half(0)
    wb, peb = half(CMP_STRIDE)
    w2bd = jnp.einsum('fd,gh->gfhd', w2, eye).reshape(g * CMP_HIDDEN, g * D_HEAD).astype(BF16)
    return wa, wb, pea, peb, w2bd


def _compress_p(rows, cw, batch, seq):
    wa, wb, pea, peb, w2bd = cw
    nseg = seq // CMP_STRIDE
    width = CMP_STRIDE * NSA_KV_WIDTH
    seg = rows.reshape(batch, nseg, width)
    full = lambda b: (0, 0)
    return pl.pallas_call(
        _compress_p_kernel,
        grid=(batch,),
        in_specs=[pl.BlockSpec((1, nseg, width), lambda b: (b, 0, 0)),
                  pl.BlockSpec((1, width), full), pl.BlockSpec((1, width), full),
                  pl.BlockSpec((width, NSA_KV_HEADS * CMP_HIDDEN), full),
                  pl.BlockSpec((width, NSA_KV_HEADS * CMP_HIDDEN), full),
                  pl.BlockSpec((NSA_KV_HEADS * CMP_HIDDEN, NSA_KV_WIDTH), full)],
        out_specs=pl.BlockSpec((1, nseg, NSA_KV_WIDTH), lambda b: (b, 0, 0)),
        out_shape=jax.ShapeDtypeStruct((batch, nseg, NSA_KV_WIDTH), F32),
        compiler_params=_params("arbitrary"),
        name="compress_prompt",
    )(seg, pea.reshape(1, width), peb.reshape(1, width), wa.reshape(width, -1), wb.reshape(width, -1), w2bd)


NSA_TQ = 128
NSA_TK = 256


def _overlap_matrix(nc_pad, nslc_pad):
    cs = jnp.arange(nc_pad)[:, None] * CMP_STRIDE
    ss = jnp.arange(nslc_pad)[None, :] * SLC_BLOCK
    return ((cs < ss + SLC_BLOCK) & (cs + CMP_LEN > ss)).astype(F32)


def _nsa_p_kernel(q_ref, kc_ref, vc_ref, ks_ref, vs_ref, kw_ref, vw_ref, gate_ref, ovt_ref, o_ref,
                  ksa_scr, kwa_scr, vst_scr, vwt_scr):
    tq, tk = NSA_TQ, NSA_TK
    t = pl.program_id(1)
    q0 = t * tq
    seq = ks_ref.shape[1]
    nc_pad = kc_ref.shape[1]
    nslc = seq // SLC_BLOCK
    width = NSA_GROUP * tq

    @pl.when(t == 0)
    def _():
        aug_s = _key_aug(seq, SLC_BLOCK, nslc)
        aug_w = _key_aug(seq, SLC_BLOCK, 0)
        for g in range(NSA_KV_HEADS):
            rows = slice(g * D_HEAD, (g + 1) * D_HEAD)
            ksa_scr[g] = jnp.concatenate([ks_ref[0, :, rows].astype(BF16), aug_s], axis=1)
            kwa_scr[g] = jnp.concatenate([kw_ref[0, :, rows].astype(BF16), aug_w], axis=1)
        for j in range(seq // tk):
            vst_scr[j] = vs_ref[0, :, j * tk:(j + 1) * tk].astype(BF16)
            vwt_scr[j] = vw_ref[0, :, j * tk:(j + 1) * tk].astype(BF16)

    qt_all = q_ref[0].T
    gates_t = _sigmoid(gate_ref[0]).T
    kct = kc_ref[0].astype(BF16)
    vct = vc_ref[0].T.astype(BF16)
    pos = q0 + _iota((1, tq), 1)
    n_i = _iota((nc_pad, tq), 0)
    valid_c = (n_i < nc_pad - 1) & (n_i * CMP_STRIDE + (CMP_LEN - 1) <= pos)
    valid_c4 = jnp.concatenate([valid_c] * NSA_GROUP, axis=1)
    q_blk = pos >> 6
    jrow = _iota((nslc, tq), 0)
    krow = _iota((tk, tq), 0)
    cd = q0 // tk
    c_win = jnp.maximum(q0 - WINDOW, 0) // tk

    def tile(b):
        return jnp.concatenate([b] * NSA_GROUP, axis=1)

    def win_bias(c):
        dist = pos - (c * tk + krow)
        return jnp.where((dist >= 0) & (dist <= WINDOW), 0.0, NEG_INF)

    qts, o_cs, biases = [], [], []
    for g in range(NSA_KV_HEADS):
        rows = slice(g * D_HEAD, (g + 1) * D_HEAD)
        qt = jnp.concatenate(
            [qt_all[(g * NSA_GROUP + r) * D_HEAD:(g * NSA_GROUP + r + 1) * D_HEAD, :] * SCALE
             for r in range(NSA_GROUP)], axis=1)
        qts.append(qt)
        s = jnp.where(valid_c4, _dot(kct[:, rows], qt.astype(BF16)), NEG_INF)
        m = jnp.max(s, axis=0, keepdims=True)
        p = jnp.where(valid_c4, jnp.exp(s - m), 0.0)
        p = p / jnp.maximum(jnp.sum(p, axis=0, keepdims=True), TINY)
        o_cs.append(_dot(vct[rows, :], p.astype(BF16)))
        psum = p[:, :tq]
        for r in range(1, NSA_GROUP):
            psum = psum + p[:, r * tq:(r + 1) * tq]
        imp = jnp.dot(ovt_ref[...], psum, precision=HIGHEST, preferred_element_type=F32)
        imp = jnp.where(jrow == q_blk, BIG, jnp.where(jrow < q_blk, imp, NEG_INF))
        keep = _rank_select_t(imp, nslc, SLC_TOPN) & (jrow <= q_blk)
        biases.append(tile(jnp.where(keep, 0.0, NEG_INF)))

    causal_own = tile(cd * tk + krow <= pos)

    def make_slc_scores(shift):
        sh = [None] * NSA_KV_HEADS if shift is None else shift
        qa = [_query_aug(qts[g], sh[g], biases[g]) for g in range(NSA_KV_HEADS)]

        def score_fn(c, own):
            out = []
            for g in range(NSA_KV_HEADS):
                s = _dot(ksa_scr[g, pl.ds(pl.multiple_of(c * tk, tk), tk), :], qa[g])
                out.append(jnp.where(causal_own, s, NEG_INF) if own else s)
            return out

        return score_fn

    def make_win_scores(shift):
        sh = [None] * NSA_KV_HEADS if shift is None else shift
        qa = [_query_aug(qts[g], sh[g], None) for g in range(NSA_KV_HEADS)]

        def score_fn(c, own):
            wb = tile(win_bias(c))
            return [_dot(kwa_scr[g, pl.ds(pl.multiple_of(c * tk, tk), tk), :], qa[g]) + wb
                    for g in range(NSA_KV_HEADS)]

        return score_fn

    def values(vt_scr):
        return lambda c: [vt_scr[c, g * D_HEAD:(g + 1) * D_HEAD, :] for g in range(NSA_KV_HEADS)]

    o_ss = _attend_t(make_slc_scores, values(vst_scr), cd, 0, cd, width, NSA_KV_HEADS)
    o_ws = _attend_t(make_win_scores, values(vwt_scr), cd, c_win, cd, width, NSA_KV_HEADS)
    outs = []
    for g in range(NSA_KV_HEADS):
        o_c, o_s, o_w = o_cs[g], o_ss[g], o_ws[g]
        for r in range(NSA_GROUP):
            hd = g * NSA_GROUP + r
            cs = slice(r * tq, (r + 1) * tq)
            outs.append(gates_t[3 * hd:3 * hd + 1, :] * o_c[:, cs] + gates_t[3 * hd + 1:3 * hd + 2, :] * o_s[:, cs]
                        + gates_t[3 * hd + 2:3 * hd + 3, :] * o_w[:, cs])
    o_ref[0] = jnp.concatenate(outs, axis=0).T


def _nsa_p(q, kn, vnt, kcmp, vcmp, gate, batch, seq):
    q3 = q.reshape(batch, seq, NSA_WIDTH)
    kn3 = kn.reshape(batch, seq, 3 * NSA_KV_WIDTH)
    g3 = gate.reshape(batch, seq, LANES)
    nc_pad = kcmp.shape[1]
    nslc = seq // SLC_BLOCK
    nchunk = seq // NSA_TK
    ovt = _overlap_matrix(nc_pad, nslc).T
    tile = lambda b, t: (b, t, 0)
    cmp_spec = pl.BlockSpec((1, nc_pad, NSA_KV_WIDTH), lambda b, t: (b, 0, 0))
    out = pl.pallas_call(
        _nsa_p_kernel,
        grid=(batch, seq // NSA_TQ),
        in_specs=[pl.BlockSpec((1, NSA_TQ, NSA_WIDTH), tile), cmp_spec, cmp_spec,
                  pl.BlockSpec((1, seq, LANES), lambda b, t: (b, 0, 1)),
                  pl.BlockSpec((1, LANES, seq), lambda b, t: (b, 1, 0)),
                  pl.BlockSpec((1, seq, LANES), lambda b, t: (b, 0, 2)),
                  pl.BlockSpec((1, LANES, seq), lambda b, t: (b, 2, 0)),
                  pl.BlockSpec((1, NSA_TQ, LANES), tile),
                  pl.BlockSpec((nslc, nc_pad), lambda b, t: (0, 0))],
        out_specs=pl.BlockSpec((1, NSA_TQ, NSA_WIDTH), tile),
        out_shape=jax.ShapeDtypeStruct((batch, seq, NSA_WIDTH), F32),
        scratch_shapes=[pltpu.VMEM((NSA_KV_HEADS, seq, 2 * D_HEAD), BF16),
                        pltpu.VMEM((NSA_KV_HEADS, seq, 2 * D_HEAD), BF16),
                        pltpu.VMEM((nchunk, LANES, NSA_TK), BF16), pltpu.VMEM((nchunk, LANES, NSA_TK), BF16)],
        compiler_params=_params("arbitrary", "arbitrary"),
        name="nsa_prompt",
    )(q3, kcmp, vcmp, kn3, vnt, kn3, vnt, g3, ovt)
    return out.reshape(batch * seq, NSA_WIDTH)


def _merge_kernel(oa_ref, ob_ref, mg_ref, x_ref, g1_ref, wa_ref, wb_ref, wo_ref, o_ref):
    a = _dot(oa_ref[...].astype(BF16), wa_ref[...])
    b = _dot(ob_ref[...].astype(BF16), wb_ref[...])
    mix = _sigmoid(mg_ref[:, :D_MODEL]) * a + _sigmoid(mg_ref[:, D_MODEL:]) * b
    o_ref[...] = x_ref[...] + g1_ref[0] * _dot(mix.astype(BF16), wo_ref[...])


def _merge(oa, ob, mg, x, g1, wa, wb, wo, tm):
    n = x.shape[0]
    nb, r, _ = g1.shape
    tiles_per_b = (n // nb) // tm
    row = lambda i: (i, 0)
    full = lambda i: (0, 0)
    return pl.pallas_call(
        _merge_kernel,
        grid=(n // tm,),
        in_specs=[pl.BlockSpec((tm, MOBA_WIDTH), row), pl.BlockSpec((tm, NSA_WIDTH), row),
                  pl.BlockSpec((tm, 2 * D_MODEL), row), pl.BlockSpec((tm, D_MODEL), row),
                  pl.BlockSpec((1, r, D_MODEL), lambda i: (i // tiles_per_b, 0, 0)),
                  pl.BlockSpec((MOBA_WIDTH, D_MODEL), full), pl.BlockSpec((NSA_WIDTH, D_MODEL), full),
                  pl.BlockSpec((D_MODEL, D_MODEL), full)],
        out_specs=pl.BlockSpec((tm, D_MODEL), row),
        out_shape=jax.ShapeDtypeStruct((n, D_MODEL), F32),
        compiler_params=_params("arbitrary"),
        name="merge",
    )(oa, ob, mg, x, g1, wa, wb, wo)


def _route(logits):
    lane = _iota(logits.shape, 1)
    is_grp = lane < N_GROUPS
    lg = jnp.where(is_grp, logits, NEG_INF)
    mg = jnp.max(lg, axis=-1, keepdims=True)
    pg = jnp.where(is_grp, jnp.exp(lg - mg), 0.0)
    pg = pg / jnp.sum(pg, axis=-1, keepdims=True)
    g_w = jnp.max(pg, axis=-1, keepdims=True)
    g_sel = jnp.min(jnp.where(is_grp & (pg == g_w), lane, LANES), axis=-1, keepdims=True)
    e_lane = lane - N_GROUPS
    in_grp = (e_lane >= 0) & (e_lane < N_EXPERTS) & ((e_lane >> 3) == g_sel)
    le = jnp.where(in_grp, logits, NEG_INF)
    me = jnp.max(le, axis=-1, keepdims=True)
    pe = jnp.where(in_grp, jnp.exp(le - me), 0.0)
    pe = pe / jnp.sum(pe, axis=-1, keepdims=True)
    v1 = jnp.max(pe, axis=-1, keepdims=True)
    i1 = jnp.min(jnp.where(in_grp & (pe == v1), lane, LANES), axis=-1, keepdims=True)
    rest = in_grp & (lane != i1)
    pr = jnp.where(rest, pe, -1.0)
    v2 = jnp.max(pr, axis=-1, keepdims=True)
    i2 = jnp.min(jnp.where(rest & (pr == v2), lane, LANES), axis=-1, keepdims=True)
    tot = v1 + v2
    comb = jnp.where(lane == i1, v1 / tot, 0.0) + jnp.where(lane == i2, v2 / tot, 0.0)
    comb = comb * g_w
    return pltpu.roll(comb, LANES - N_GROUPS, 1)


MOE_EXPERTS_PER_STEP = 8


def _moe_kernel(x_ref, sc_ref, sh_ref, g2_ref, gn_ref, gf_ref, wr_ref, win_ref, wout_ref, o_ref,
                h_scr, comb_scr, acc_scr):
    e = pl.program_id(1)

    @pl.when(e == 0)
    def _():
        h = _norm_mod(x_ref[...], gn_ref[...], sc_ref[0], sh_ref[0])
        hb = h.astype(BF16)
        h_scr[...] = hb
        logits = jnp.dot(h, wr_ref[...], precision=HIGHEST, preferred_element_type=F32)
        comb_scr[...] = _route(logits)
        acc_scr[...] = jnp.zeros_like(acc_scr)

    per = win_ref.shape[0]
    hb = h_scr[...]
    comb = comb_scr[...]
    lane = _iota(comb.shape, 1)
    acts = []
    for j in range(per):
        hid = _dot(hb, win_ref[j])
        a = hid[:, :D_EXPERT]
        b = hid[:, D_EXPERT:]
        w = jnp.sum(jnp.where(lane == e * per + j, comb, 0.0), axis=1, keepdims=True)
        acts.append((a * _sigmoid(a) * b * w).astype(BF16))
    act = jnp.concatenate(acts, axis=1)
    acc_scr[...] += _dot(act, wout_ref[...].reshape(per * D_EXPERT, D_MODEL))

    @pl.when(e == pl.num_programs(1) - 1)
    def _():
        y = x_ref[...] + g2_ref[0] * acc_scr[...]
        ms = jnp.mean(y * y, axis=-1, keepdims=True)
        o_ref[...] = y * lax.rsqrt(ms + EPS) * gf_ref[...]


def _moe(x, sc, sh, g2, gn, gf, wr, w_ein, w_eout, tm):
    n = x.shape[0]
    nb, r, _ = sc.shape
    tiles_per_b = (n // nb) // tm
    row = lambda i, e: (i, 0)
    mod = lambda i, e: (i // tiles_per_b, 0, 0)
    full = lambda i, e: (0, 0)
    per = MOE_EXPERTS_PER_STEP
    return pl.pallas_call(
        _moe_kernel,
        grid=(n // tm, N_EXPERTS // per),
        in_specs=[pl.BlockSpec((tm, D_MODEL), row),
                  pl.BlockSpec((1, r, D_MODEL), mod), pl.BlockSpec((1, r, D_MODEL), mod),
                  pl.BlockSpec((1, r, D_MODEL), mod),
                  pl.BlockSpec((1, D_MODEL), full), pl.BlockSpec((1, D_MODEL), full),
                  pl.BlockSpec((D_MODEL, LANES), full),
                  pl.BlockSpec((per, D_MODEL, 2 * D_EXPERT), lambda i, e: (e, 0, 0)),
                  pl.BlockSpec((per, D_EXPERT, D_MODEL), lambda i, e: (e, 0, 0))],
        out_specs=pl.BlockSpec((tm, D_MODEL), row),
        out_shape=jax.ShapeDtypeStruct((n, D_MODEL), F32),
        scratch_shapes=[pltpu.VMEM((tm, D_MODEL), BF16), pltpu.VMEM((tm, LANES), F32),
                        pltpu.VMEM((tm, D_MODEL), F32)],
        compiler_params=_params("arbitrary", "arbitrary"),
        name="moe",
    )(x, sc, sh, g2, gn, gf, wr, w_ein, w_eout)


PAGE = 128
MOBA_PAGES_PER_STEP = 8
NSA_PAGES_PER_STEP = 16
COMPRESS_PAGES_PER_STEP = 32


def _page_view(cache):
    n_phys, page, heads, dh = cache.shape
    return jnp.transpose(cache, (0, 2, 3, 1)).reshape(n_phys, heads * dh, page)


def _page_specs(rows, n_pages, per_step):
    def spec(u):
        return pl.BlockSpec((1, rows, PAGE), lambda b, s, pt: (pt[b * n_pages + s * per_step + u], 0, 0))
    return [spec(u) for u in range(per_step)]


def _head_diag(full, heads):
    rows = full.shape[0]
    head = _iota((rows, D_HEAD), 0) // (rows // heads)
    out = jnp.zeros((rows, D_HEAD), F32)
    for h in range(heads):
        out = out + jnp.where(head == h, full[:, h * D_HEAD:(h + 1) * D_HEAD], 0.0)
    return out


def _moba_s_kernel(pt_ref, qbd_ref, *refs):
    per = MOBA_PAGES_PER_STEP
    k_refs, v_refs = refs[:per], refs[per:2 * per]
    m_ref, l_ref, ks_ref, o_ref = refs[2 * per:]
    s = pl.program_id(1)

    @pl.when(s == 0)
    def _():
        m_ref[...] = jnp.zeros_like(m_ref)
        l_ref[...] = jnp.zeros_like(l_ref)
        ks_ref[...] = jnp.zeros_like(ks_ref)

    qbd = qbd_ref[0]
    lane_q = _iota(m_ref.shape[1:], 1)
    lane_k = _iota(ks_ref.shape[1:], 1)
    ppb = MOBA_BLOCK // PAGE
    m_all, l_all, ks_all = m_ref[0], l_ref[0], ks_ref[0]
    kts = [k_refs[u][0] for u in range(per)]
    sc_all = _dot(qbd, jnp.concatenate([kt.astype(BF16) for kt in kts], axis=1))
    for j in range(per // ppb):
        blk = s * (per // ppb) + j
        sc = sc_all[:, j * MOBA_BLOCK:(j + 1) * MOBA_BLOCK]
        m = jnp.max(sc, axis=-1, keepdims=True)
        p = jnp.exp(sc - m)
        l = jnp.sum(p, axis=-1, keepdims=True)
        vt = jnp.concatenate([v_refs[j * ppb + t][0].astype(BF16) for t in range(ppb)], axis=1)
        o_ref[0, j] = _head_diag(_dot_nt(p.astype(BF16), vt), MOBA_HEADS)
        kb = kts[j * ppb]
        for t in range(1, ppb):
            kb = kb + kts[j * ppb + t]
        ksum = jnp.sum(kb, axis=-1, keepdims=True)
        m_all = jnp.where(lane_q == blk, m, m_all)
        l_all = jnp.where(lane_q == blk, l, l_all)
        ks_all = jnp.where(lane_k == blk, ksum, ks_all)
    m_ref[0] = m_all
    l_ref[0] = l_all
    ks_ref[0] = ks_all


def _moba_s_pass(pt_flat, qbd, kt_pages, vt_pages, batch, n_pages):
    per = MOBA_PAGES_PER_STEP
    rows = qbd.shape[1]
    ppb = MOBA_BLOCK // PAGE
    stat = lambda b, s, pt: (b, 0, 0)
    return pl.pallas_call(
        _moba_s_kernel,
        grid_spec=pltpu.PrefetchScalarGridSpec(
            num_scalar_prefetch=1,
            grid=(batch, n_pages // per),
            in_specs=[pl.BlockSpec((1, rows, MOBA_WIDTH), stat)]
            + _page_specs(MOBA_WIDTH, n_pages, per) + _page_specs(MOBA_WIDTH, n_pages, per),
            out_specs=[pl.BlockSpec((1, rows, LANES), stat), pl.BlockSpec((1, rows, LANES), stat),
                       pl.BlockSpec((1, MOBA_WIDTH, LANES), stat),
                       pl.BlockSpec((1, per // ppb, rows, D_HEAD), lambda b, s, pt: (b, s, 0, 0))]),
        out_shape=[jax.ShapeDtypeStruct((batch, rows, LANES), F32), jax.ShapeDtypeStruct((batch, rows, LANES), F32),
                   jax.ShapeDtypeStruct((batch, MOBA_WIDTH, LANES), F32),
                   jax.ShapeDtypeStruct((batch, n_pages // ppb, rows, D_HEAD), F32)],
        compiler_params=_params("arbitrary", "arbitrary"),
        name="moba_decode_pages",
    )(pt_flat, qbd, *([kt_pages] * per), *([vt_pages] * per))


def _moba_s_combine_kernel(m_ref, l_ref, ks_ref, o_ref, qf_ref, qbd_ref, kn_ref, vn_ref, out_ref, *, n_pages, ts):
    rows = m_ref.shape[1]
    nblk = n_pages // (MOBA_BLOCK // PAGE)
    lane = _iota((rows, LANES), 1)
    kmean = ks_ref[0] * (1.0 / MOBA_BLOCK)
    gate = jnp.dot(qf_ref[0], kmean, precision=HIGHEST, preferred_element_type=F32)
    gate = jnp.where(lane < nblk, gate, NEG_INF)
    selp = _rank_select(gate, nblk, MOBA_TOPK) & (lane < nblk)
    qbd = qbd_ref[0]
    s_own = _dot_nt(qbd, kn_ref[0].astype(BF16))
    valid_own = lane <= (_iota((rows, LANES), 0) % ts)
    s_own = jnp.where(valid_own, s_own, NEG_INF)
    m_all = jnp.where(selp, m_ref[0], NEG_INF)
    big_m = jnp.maximum(jnp.max(m_all, axis=-1, keepdims=True), jnp.max(s_own, axis=-1, keepdims=True))
    wgt = jnp.where(selp, jnp.exp(m_ref[0] - big_m), 0.0)
    p_own = jnp.where(valid_own, jnp.exp(s_own - big_m), 0.0)
    denom = jnp.sum(wgt * l_ref[0], axis=-1, keepdims=True) + jnp.sum(p_own, axis=-1, keepdims=True)
    num = _head_diag(_dot(p_own.astype(BF16), vn_ref[0].astype(BF16)), MOBA_HEADS)
    for j in range(nblk):
        num = num + wgt[:, j:j + 1] * o_ref[0, j]
    out_ref[0] = num / jnp.maximum(denom, TINY)


def _moba_s_combine(m, l, ks, o, qf, qbd, kn, vn, n_pages, ts):
    batch, rows, _ = m.shape
    b3 = lambda b: (b, 0, 0)
    return pl.pallas_call(
        functools.partial(_moba_s_combine_kernel, n_pages=n_pages, ts=ts),
        grid=(batch,),
        in_specs=[pl.BlockSpec((1, rows, LANES), b3), pl.BlockSpec((1, rows, LANES), b3),
                  pl.BlockSpec((1, MOBA_WIDTH, LANES), b3),
                  pl.BlockSpec((1, o.shape[1], rows, D_HEAD), lambda b: (b, 0, 0, 0)),
                  pl.BlockSpec((1, rows, MOBA_WIDTH), b3), pl.BlockSpec((1, rows, MOBA_WIDTH), b3),
                  pl.BlockSpec((1, LANES, MOBA_WIDTH), b3), pl.BlockSpec((1, LANES, MOBA_WIDTH), b3)],
        out_specs=pl.BlockSpec((1, rows, D_HEAD), b3),
        out_shape=jax.ShapeDtypeStruct((batch, rows, D_HEAD), F32),
        compiler_params=_params("arbitrary"),
        name="moba_decode_combine",
    )(m, l, ks, o, qf, qbd, kn, vn)


def _block_diag_q(q, batch, ts, heads):
    q4 = q.reshape(batch, ts, heads, D_HEAD)
    eye = jnp.eye(heads, dtype=q.dtype)
    return jnp.einsum('bchd,hk->bhckd', q4, eye).reshape(batch, heads * ts, heads * D_HEAD)


def _pad_rows(a, batch, ts):
    a3 = a.reshape(batch, ts, a.shape[-1])
    return jnp.pad(a3, ((0, 0), (0, LANES - ts), (0, 0)))


def _compress_s_kernel(pt_ref, *refs):
    per = COMPRESS_PAGES_PER_STEP
    page_refs = refs[:per]
    pea_ref, peb_ref, wa_ref, wb_ref, w2_ref, o_ref, x_scr, a_scr, b_scr = refs[per:]
    s = pl.program_id(1)
    nseg = per * PAGE // CMP_STRIDE
    for u in range(per):
        x_scr[u * PAGE:(u + 1) * PAGE, :] = page_refs[u][0].T
    xs = [x_scr[pl.ds(l, nseg, stride=CMP_STRIDE), :] for l in range(CMP_STRIDE)]
    xa = jnp.concatenate([(xs[l] + pea_ref[l:l + 1, :]).astype(BF16) for l in range(CMP_STRIDE)], axis=1)
    xb = jnp.concatenate([(xs[l] + peb_ref[l:l + 1, :]).astype(BF16) for l in range(CMP_STRIDE)], axis=1)
    a_scr[pl.ds(pl.multiple_of(s * nseg, nseg), nseg), :] = _dot(xa, wa_ref[...])
    b_scr[pl.ds(pl.multiple_of(s * nseg, nseg), nseg), :] = _dot(xb, wb_ref[...])

    @pl.when(s == pl.num_programs(1) - 1)
    def _():
        total = a_scr.shape[0]
        hid = a_scr[...] + pltpu.roll(b_scr[...], total - 1, 0)
        o_ref[0] = _dot(_gelu(hid).astype(BF16), w2_ref[...])


def _compress_s(pt_flat, pages, cw, batch, n_pages):
    wa, wb, pea, peb, w2bd = cw
    per = COMPRESS_PAGES_PER_STEP
    total = n_pages * PAGE // CMP_STRIDE
    hidden = NSA_KV_HEADS * CMP_HIDDEN
    width = CMP_STRIDE * NSA_KV_WIDTH
    full2 = lambda b, s, pt: (0, 0)
    return pl.pallas_call(
        _compress_s_kernel,
        grid_spec=pltpu.PrefetchScalarGridSpec(
            num_scalar_prefetch=1,
            grid=(batch, n_pages // per),
            in_specs=_page_specs(NSA_KV_WIDTH, n_pages, per)
            + [pl.BlockSpec((CMP_STRIDE, NSA_KV_WIDTH), full2), pl.BlockSpec((CMP_STRIDE, NSA_KV_WIDTH), full2),
               pl.BlockSpec((width, hidden), full2), pl.BlockSpec((width, hidden), full2),
               pl.BlockSpec((hidden, NSA_KV_WIDTH), full2)],
            out_specs=pl.BlockSpec((1, total, NSA_KV_WIDTH), lambda b, s, pt: (b, 0, 0)),
            scratch_shapes=[pltpu.VMEM((per * PAGE, NSA_KV_WIDTH), F32), pltpu.VMEM((total, hidden), F32),
                            pltpu.VMEM((total, hidden), F32)]),
        out_shape=jax.ShapeDtypeStruct((batch, total, NSA_KV_WIDTH), F32),
        compiler_params=_params("arbitrary", "arbitrary"),
        name="compress_decode",
    )(pt_flat, *([pages] * per), pea, peb, wa.reshape(width, hidden), wb.reshape(width, hidden), w2bd)


def _stack_group_q(q_ref, g):
    return jnp.concatenate(
        [q_ref[0, :, (g * NSA_GROUP + r) * D_HEAD:(g * NSA_GROUP + r + 1) * D_HEAD] * SCALE
         for r in range(NSA_GROUP)], axis=0).astype(BF16)


def _nsa_s_kernel(q_ref, kc_ref, vc_ref, wk_ref, wv_ref, kn_ref, vn_ref, gate_ref, ov_ref, ex_ref, part_ref, sel_ref,
                  *, ts):
    rows = NSA_GROUP * ts
    nc_pad = kc_ref.shape[1]
    wlen = wk_ref.shape[3]
    gates = _sigmoid(gate_ref[0])
    n_i = _iota((rows, nc_pad), 1)
    valid_c = n_i < nc_pad - 1
    c_of_row = _iota((rows, 1), 0) % ts
    valid_w = _iota((rows, wlen), 1) >= c_of_row
    valid_n = _iota((rows, LANES), 1) <= c_of_row
    for g in range(NSA_KV_HEADS):
        lane0 = g * D_HEAD
        qs = _stack_group_q(q_ref, g)
        s = jnp.where(valid_c, _dot_nt(qs, kc_ref[0, :, lane0:lane0 + D_HEAD].astype(BF16)), NEG_INF)
        m = jnp.max(s, axis=-1, keepdims=True)
        p = jnp.where(valid_c, jnp.exp(s - m), 0.0)
        p = p / jnp.maximum(jnp.sum(p, axis=-1, keepdims=True), TINY)
        o_c = _dot(p.astype(BF16), vc_ref[0, :, lane0:lane0 + D_HEAD].astype(BF16))
        psum = jnp.sum(p.reshape(NSA_GROUP, ts, nc_pad), axis=0)
        imp = jnp.dot(psum, ov_ref[...], precision=HIGHEST, preferred_element_type=F32)
        keep = _rank_select(imp, LANES, SLC_TOPN - 1).astype(BF16)
        sel_ref[0, g] = jnp.where(_dot(keep, ex_ref[...]) > 0.5, 0.0, NEG_INF)
        s_w = jnp.where(valid_w, _dot(qs, wk_ref[0, g].astype(BF16)), NEG_INF)
        s_n = jnp.where(valid_n, _dot_nt(qs, kn_ref[0, :, lane0:lane0 + D_HEAD].astype(BF16)), NEG_INF)
        m = jnp.maximum(jnp.max(s_w, axis=-1, keepdims=True), jnp.max(s_n, axis=-1, keepdims=True))
        p_w = jnp.where(valid_w, jnp.exp(s_w - m), 0.0)
        p_n = jnp.where(valid_n, jnp.exp(s_n - m), 0.0)
        den = jnp.sum(p_w, axis=-1, keepdims=True) + jnp.sum(p_n, axis=-1, keepdims=True)
        o_w = (_dot_nt(p_w.astype(BF16), wv_ref[0, g].astype(BF16))
               + _dot(p_n.astype(BF16), vn_ref[0, :, lane0:lane0 + D_HEAD].astype(BF16))) / jnp.maximum(den, TINY)
        for r in range(NSA_GROUP):
            hd = g * NSA_GROUP + r
            rs = slice(r * ts, (r + 1) * ts)
            part_ref[0, :, hd * D_HEAD:(hd + 1) * D_HEAD] = (
                gates[:, 3 * hd:3 * hd + 1] * o_c[rs] + gates[:, 3 * hd + 2:3 * hd + 3] * o_w[rs])


def _nsa_s(q3, kcmp, vcmp, wk_t, wv_t, kn_w, vn_w, gate3, ts):
    batch = q3.shape[0]
    nc_pad = kcmp.shape[1]
    wlen = wk_t.shape[3]
    ov = _overlap_matrix(nc_pad, LANES)
    n_keys = LANES * SLC_BLOCK
    expand = (jnp.arange(LANES)[:, None] == jnp.arange(n_keys)[None, :] // SLC_BLOCK).astype(BF16)
    b3 = lambda b: (b, 0, 0)
    b4 = lambda b: (b, 0, 0, 0)
    return pl.pallas_call(
        functools.partial(_nsa_s_kernel, ts=ts),
        grid=(batch,),
        in_specs=[pl.BlockSpec((1, ts, NSA_WIDTH), b3),
                  pl.BlockSpec((1, nc_pad, NSA_KV_WIDTH), b3), pl.BlockSpec((1, nc_pad, NSA_KV_WIDTH), b3),
                  pl.BlockSpec((1, NSA_KV_HEADS, D_HEAD, wlen), b4), pl.BlockSpec((1, NSA_KV_HEADS, D_HEAD, wlen), b4),
                  pl.BlockSpec((1, LANES, NSA_KV_WIDTH), b3), pl.BlockSpec((1, LANES, NSA_KV_WIDTH), b3),
                  pl.BlockSpec((1, ts, LANES), b3),
                  pl.BlockSpec((nc_pad, LANES), lambda b: (0, 0)),
                  pl.BlockSpec((LANES, n_keys), lambda b: (0, 0))],
        out_specs=[pl.BlockSpec((1, ts, NSA_WIDTH), b3), pl.BlockSpec((1, NSA_KV_HEADS, ts, n_keys), b4)],
        out_shape=[jax.ShapeDtypeStruct((batch, ts, NSA_WIDTH), F32),
                   jax.ShapeDtypeStruct((batch, NSA_KV_HEADS, ts, n_keys), F32)],
        compiler_params=_params("arbitrary"),
        name="nsa_decode_cmp_win",
    )(q3, kcmp, vcmp, wk_t, wv_t, kn_w, vn_w, gate3, ov, expand)


def _slc_s_kernel(pt_ref, q_ref, bias_ref, *refs, ts):
    per = NSA_PAGES_PER_STEP
    k_refs, v_refs = refs[:per], refs[per:2 * per]
    kn_ref, vn_ref, gate_ref, part_ref, o_ref, m_scr, l_scr, acc_scr = refs[2 * per:]
    s = pl.program_id(1)
    rows = NSA_GROUP * ts
    zero = jnp.zeros((rows, D_HEAD), BF16)
    q2 = jnp.concatenate([jnp.concatenate([_stack_group_q(q_ref, 0), zero], axis=1),
                          jnp.concatenate([zero, _stack_group_q(q_ref, 1)], axis=1)], axis=0)
    in_g0 = _iota((NSA_KV_HEADS * rows, D_HEAD), 0) < rows

    def own_group(full):
        return jnp.where(in_g0, full[:, :D_HEAD], full[:, D_HEAD:])

    @pl.when(s == 0)
    def _():
        valid_n = _iota((NSA_KV_HEADS * rows, LANES), 1) <= (_iota((NSA_KV_HEADS * rows, 1), 0) % ts)
        sc = jnp.where(valid_n, _dot_nt(q2, kn_ref[0].astype(BF16)), NEG_INF)
        m = jnp.max(sc, axis=-1, keepdims=True)
        p = jnp.exp(sc - m)
        m_scr[...] = m
        l_scr[...] = jnp.sum(p, axis=-1, keepdims=True)
        acc_scr[...] = own_group(_dot(p.astype(BF16), vn_ref[0].astype(BF16)))

    kt = jnp.concatenate([k_refs[u][0].astype(BF16) for u in range(per)], axis=1)
    vt = jnp.concatenate([v_refs[u][0].astype(BF16) for u in range(per)], axis=1)
    bias = jnp.concatenate([bias_ref[0, g] for g in range(NSA_KV_HEADS) for _ in range(NSA_GROUP)], axis=0)
    sc = _dot(q2, kt) + bias
    m_old = m_scr[...]
    m_new = jnp.maximum(m_old, jnp.max(sc, axis=-1, keepdims=True))
    pf = jnp.exp(sc - m_new)
    alpha = jnp.exp(m_old - m_new)
    l_scr[...] = alpha * l_scr[...] + jnp.sum(pf, axis=-1, keepdims=True)
    acc_scr[...] = alpha * acc_scr[...] + own_group(_dot_nt(pf.astype(BF16), vt))
    m_scr[...] = m_new

    @pl.when(s == pl.num_programs(1) - 1)
    def _():
        gates = _sigmoid(gate_ref[0])
        o_s = acc_scr[...] / jnp.maximum(l_scr[...], TINY)
        for hd in range(NSA_HEADS):
            cols = slice(hd * D_HEAD, (hd + 1) * D_HEAD)
            o_ref[0, :, cols] = part_ref[0, :, cols] + gates[:, 3 * hd + 1:3 * hd + 2] * o_s[hd * ts:(hd + 1) * ts]


def _slc_s(pt_flat, q3, sel_bias, k_pages, v_pages, kn_s, vn_s, gate3, part, n_pages, ts):
    batch = q3.shape[0]
    per = NSA_PAGES_PER_STEP
    rows = NSA_GROUP * ts
    b3 = lambda b, s, pt: (b, 0, 0)
    return pl.pallas_call(
        functools.partial(_slc_s_kernel, ts=ts),
        grid_spec=pltpu.PrefetchScalarGridSpec(
            num_scalar_prefetch=1,
            grid=(batch, n_pages // per),
            in_specs=[pl.BlockSpec((1, ts, NSA_WIDTH), b3),
                      pl.BlockSpec((1, NSA_KV_HEADS, ts, per * PAGE), lambda b, s, pt: (b, 0, 0, s))]
            + _page_specs(NSA_KV_WIDTH, n_pages, per) + _page_specs(NSA_KV_WIDTH, n_pages, per)
            + [pl.BlockSpec((1, LANES, NSA_KV_WIDTH), b3), pl.BlockSpec((1, LANES, NSA_KV_WIDTH), b3),
               pl.BlockSpec((1, ts, LANES), b3), pl.BlockSpec((1, ts, NSA_WIDTH), b3)],
            out_specs=pl.BlockSpec((1, ts, NSA_WIDTH), b3),
            scratch_shapes=[pltpu.VMEM((NSA_KV_HEADS * rows, 1), F32), pltpu.VMEM((NSA_KV_HEADS * rows, 1), F32),
                            pltpu.VMEM((NSA_KV_HEADS * rows, D_HEAD), F32)]),
        out_shape=jax.ShapeDtypeStruct((batch, ts, NSA_WIDTH), F32),
        compiler_params=_params("arbitrary", "arbitrary"),
        name="nsa_decode_slc",
    )(pt_flat, q3, sel_bias, *([k_pages] * per), *([v_pages] * per), kn_s, vn_s, gate3, part)


def _prep_weights(w_ada, b_ada, norm_mix_g, w_in, pe_cmp_k, w_cmp_k1, w_cmp_k2, pe_cmp_v, w_cmp_v1, w_cmp_v2,
                  w_br_a, w_br_b, w_out, norm_ffn_g, w_router_grp, w_router_exp, w_expert_in, w_expert_out,
                  norm_final_g):
    wr = jnp.concatenate([w_router_grp, w_router_exp], axis=1)
    wr = jnp.pad(wr, ((0, 0), (0, LANES - wr.shape[1])))
    return dict(
        w_ada=w_ada, b_ada=b_ada.reshape(1, -1), g_mix=norm_mix_g.reshape(1, -1),
        w_in=_reorder_w_in(w_in),
        cmp_k=_compress_weights(pe_cmp_k, w_cmp_k1, w_cmp_k2),
        cmp_v=_compress_weights(pe_cmp_v, w_cmp_v1, w_cmp_v2),
        w_br_a=w_br_a.astype(BF16), w_br_b=w_br_b.astype(BF16), w_out=w_out.astype(BF16),
        g_ffn=norm_ffn_g.reshape(1, -1), wr=wr,
        w_ein=w_expert_in.astype(BF16), w_eout=w_expert_out.astype(BF16),
        g_final=norm_final_g.reshape(1, -1))


def _prompt_layer(x, mod, w, batch, seq):
    sh1, sc1, g1, sh2, sc2, g2 = mod
    cos, sin = _rope_tables(jnp.arange(seq, dtype=jnp.int32))
    qk_a, _, q_b, k_n, v_n, gate, mg, kt_a, vt_a, kt_n, vt_n = _inproj(
        x, sc1, sh1, w['g_mix'], cos, sin, w['w_in'], 256, transposed=True)
    o_a = _moba_p(qk_a, vt_a, batch, seq)
    kcmp = _compress_p(k_n[:, :NSA_KV_WIDTH], w['cmp_k'], batch, seq)
    vcmp = _compress_p(v_n[:, :NSA_KV_WIDTH], w['cmp_v'], batch, seq)
    o_b = _nsa_p(q_b, k_n, vt_n, kcmp, vcmp, gate, batch, seq)
    x1 = _merge(o_a, o_b, mg, x, g1, w['w_br_a'], w['w_br_b'], w['w_out'], 256)
    y = _moe(x1, sc2, sh2, g2, w['g_ffn'], w['g_final'], w['wr'], w['w_ein'], w['w_eout'], 512)
    return y, (kt_a, vt_a, kt_n, vt_n)


def _sample_layer(x, mod, w, caches, win_state, page_table, batch, ts):
    sh1, sc1, g1, sh2, sc2, g2 = mod
    moba_k, moba_v, cmp_k, cmp_v, slc_k, slc_v = caches
    win_k, win_v = win_state
    n_pages = page_table.shape[1]
    assert moba_k.shape[1] == PAGE and win_k.shape[1] == WINDOW and ts <= LANES
    assert n_pages * PAGE == LANES * SLC_BLOCK and n_pages * PAGE // MOBA_BLOCK <= LANES
    n = batch * ts
    pos = n_pages * PAGE + (jnp.arange(n, dtype=jnp.int32) % ts)
    cos, sin = _rope_tables(pos)
    qk_a, v_a, q_b, k_n, v_n, gate, mg = _inproj(x, sc1, sh1, w['g_mix'], cos, sin, w['w_in'], n)
    pt_flat = page_table.reshape(-1)
    kv = NSA_KV_WIDTH
    qf = _block_diag_q(qk_a[:, :MOBA_WIDTH], batch, ts, MOBA_HEADS)
    qbd = (qf * SCALE).astype(BF16)
    m, l, ks, o = _moba_s_pass(pt_flat, qbd, _page_view(moba_k), _page_view(moba_v), batch, n_pages)
    o_a = _moba_s_combine(m, l, ks, o, qf, qbd, _pad_rows(qk_a[:, MOBA_WIDTH:], batch, ts), _pad_rows(v_a, batch, ts),
                          n_pages, ts)
    o_a = o_a.reshape(batch, MOBA_HEADS, ts, D_HEAD).transpose(0, 2, 1, 3).reshape(n, MOBA_WIDTH)
    kcmp = _compress_s(pt_flat, _page_view(cmp_k), w['cmp_k'], batch, n_pages)
    vcmp = _compress_s(pt_flat, _page_view(cmp_v), w['cmp_v'], batch, n_pages)
    q3 = q_b.reshape(batch, ts, NSA_WIDTH)
    gate3 = gate.reshape(batch, ts, LANES)
    part, sel = _nsa_s(q3, kcmp, vcmp, jnp.transpose(win_k, (0, 2, 3, 1)), jnp.transpose(win_v, (0, 2, 3, 1)),
                       _pad_rows(k_n[:, 2 * kv:], batch, ts), _pad_rows(v_n[:, 2 * kv:], batch, ts), gate3, ts)
    o_b = _slc_s(pt_flat, q3, sel, _page_view(slc_k), _page_view(slc_v),
                 _pad_rows(k_n[:, kv:2 * kv], batch, ts), _pad_rows(v_n[:, kv:2 * kv], batch, ts), gate3, part,
                 n_pages, ts).reshape(n, NSA_WIDTH)
    x1 = _merge(o_a, o_b, mg, x, g1, w['w_br_a'], w['w_br_b'], w['w_out'], n)
    y = _moe(x1, sc2, sh2, g2, w['g_ffn'], w['g_final'], w['wr'], w['w_ein'], w['w_eout'], n)
    return y, (qk_a, v_a, k_n, v_n)


def kernel(x_prompt, x_sample, c_prompt, c_sample, cache_moba_k, cache_moba_v, cache_nsa_cmp_k, cache_nsa_cmp_v,
           cache_nsa_slc_k, cache_nsa_slc_v, state_nsa_win_k, state_nsa_win_v, page_table, w_ada, b_ada, norm_mix_g,
           w_in, pe_cmp_k, w_cmp_k1, w_cmp_k2, pe_cmp_v, w_cmp_v1, w_cmp_v2, w_br_a, w_br_b, w_out, norm_ffn_g,
           w_router_grp, w_router_exp, w_expert_in, w_expert_out, norm_final_g):
    bp, tp, _ = x_prompt.shape
    bs, ts, _ = x_sample.shape
    w = _prep_weights(w_ada[0], b_ada[0], norm_mix_g[0], w_in[0], pe_cmp_k[0], w_cmp_k1[0], w_cmp_k2[0], pe_cmp_v[0],
                      w_cmp_v1[0], w_cmp_v2[0], w_br_a[0], w_br_b[0], w_out[0], norm_ffn_g[0], w_router_grp[0],
                      w_router_exp[0], w_expert_in[0], w_expert_out[0], norm_final_g)
    mod = _ada(jnp.concatenate([c_prompt, c_sample], axis=0), w['w_ada'], w['b_ada'])
    mod_p = [m.reshape(bp, 1, D_MODEL) for m in jnp.split(mod[:bp], 6, axis=-1)]
    y_p, new_p = _prompt_layer(x_prompt.reshape(bp * tp, D_MODEL), mod_p, w, bp, tp)

    mod_s = [jnp.repeat(m, ts, axis=0).reshape(1, bs * ts, D_MODEL) for m in jnp.split(mod[bp:], 6, axis=-1)]
    caches = (cache_moba_k[0], cache_moba_v[0], cache_nsa_cmp_k[0], cache_nsa_cmp_v[0], cache_nsa_slc_k[0],
              cache_nsa_slc_v[0])
    y_s, new_s = _sample_layer(x_sample.reshape(bs * ts, D_MODEL), mod_s, w, caches,
                               (state_nsa_win_k[0], state_nsa_win_v[0]), page_table, bs, ts)

    kv = NSA_KV_WIDTH

    def new_rows(new, b, t):
        qk, v, k_n, v_n = new
        rows = lambda a, heads: a.reshape(1, b, t, heads, D_HEAD)
        return (rows(qk[:, MOBA_WIDTH:], MOBA_HEADS), rows(v, MOBA_HEADS),
                rows(k_n[:, :kv], NSA_KV_HEADS), rows(v_n[:, :kv], NSA_KV_HEADS),
                rows(k_n[:, kv:2 * kv], NSA_KV_HEADS), rows(v_n[:, kv:2 * kv], NSA_KV_HEADS),
                rows(k_n[:, 2 * kv:], NSA_KV_HEADS), rows(v_n[:, 2 * kv:], NSA_KV_HEADS))

    def new_rows_t(new, b, t):
        kt_a, vt_a, kt_n, vt_n = new
        rows = lambda a: jnp.transpose(a.reshape(1, b, a.shape[1] // D_HEAD, D_HEAD, a.shape[2]), (0, 1, 4, 2, 3))
        return (rows(kt_a), rows(vt_a), rows(kt_n[:, :kv]), rows(vt_n[:, :kv]),
                rows(kt_n[:, kv:2 * kv]), rows(vt_n[:, kv:2 * kv]),
                rows(kt_n[:, 2 * kv:, t - wb:]), rows(vt_n[:, 2 * kv:, t - wb:]))

    wb = state_nsa_win_k.shape[2]
    outs_p = new_rows_t(new_p, bp, tp)
    outs_s = new_rows(new_s, bs, ts)
    win_k = jnp.concatenate([state_nsa_win_k, outs_s[6]], axis=2)[:, :, ts:]
    win_v = jnp.concatenate([state_nsa_win_v, outs_s[7]], axis=2)[:, :, ts:]
    outs_s = outs_s[:6] + (win_k, win_v)
    return (y_p.reshape(bp, tp, D_MODEL), y_s.reshape(bs, ts, D_MODEL)) + outs_p + outs_s
```

```python
import functools

import jax
import jax.numpy as jnp
from jax import lax
from jax.experimental import pallas as pl
from jax.experimental.pallas import tpu as pltpu

D_MODEL = 1024
D_HEAD = 64
HALF = D_HEAD // 2
MOBA_HEADS = 8
MOBA_BLOCK = 256
MOBA_TOPK = 3
NSA_HEADS = 8
NSA_KV_HEADS = 2
NSA_GROUP = NSA_HEADS // NSA_KV_HEADS
CMP_LEN = 32
CMP_STRIDE = 16
CMP_HIDDEN = 128
SLC_BLOCK = 64
SLC_TOPN = 16
WINDOW = 512
N_GROUPS = 4
EXPERTS_PER_GROUP = 8
N_EXPERTS = N_GROUPS * EXPERTS_PER_GROUP
D_EXPERT = 256
ROPE_THETA = 10000.0
EPS = 1e-6
NEG_INF = -1e30
BIG = 1e30
TINY = 1e-30
MOBA_WIDTH = MOBA_HEADS * D_HEAD
NSA_WIDTH = NSA_HEADS * D_HEAD
NSA_KV_WIDTH = NSA_KV_HEADS * D_HEAD
SCALE = D_HEAD ** -0.5

LANES = 128
VMEM_LIMIT = 48 * 1024 * 1024

F32 = jnp.float32
BF16 = jnp.bfloat16
HIGHEST = lax.Precision.HIGHEST


def _params(*sem):
    return pltpu.CompilerParams(dimension_semantics=sem, vmem_limit_bytes=VMEM_LIMIT)


def _dot(a, b):
    return jnp.dot(a, b, preferred_element_type=F32)


def _dot_nt(a, b, precision=None):
    return lax.dot_general(a, b, (((1,), (1,)), ((), ())), precision=precision,
                           preferred_element_type=F32)


def _sigmoid(x):
    return 1.0 / (1.0 + jnp.exp(-x))


def _iota(shape, dim):
    return lax.broadcasted_iota(jnp.int32, shape, dim)


def _ada_kernel(c_ref, w_ref, b_ref, o_ref):
    c = c_ref[...]
    s = c * _sigmoid(c)
    o_ref[...] = _dot(s.astype(BF16), w_ref[...].astype(BF16)) + b_ref[...]


def _ada(c, w, b):
    n = c.shape[0]
    tn = 1024
    return pl.pallas_call(
        _ada_kernel,
        grid=(w.shape[1] // tn,),
        in_specs=[pl.BlockSpec((n, D_MODEL), lambda j: (0, 0)),
                  pl.BlockSpec((D_MODEL, tn), lambda j: (0, j)),
                  pl.BlockSpec((1, tn), lambda j: (0, j))],
        out_specs=pl.BlockSpec((n, tn), lambda j: (0, j)),
        out_shape=jax.ShapeDtypeStruct((n, w.shape[1]), F32),
        compiler_params=_params("arbitrary"),
        name="ada",
    )(c, w, b)


_IN_GROUPS = ((2 * MOBA_WIDTH, True),
              (MOBA_WIDTH, False),
              (NSA_WIDTH, True),
              (3 * NSA_KV_WIDTH, True),
              (3 * NSA_KV_WIDTH, False),
              (LANES, False),
              (2 * D_MODEL, False))
_IN_COLS_PAD = sum(w for w, _ in _IN_GROUPS)


def _norm_mod(x, g, sc, sh):
    ms = jnp.mean(x * x, axis=-1, keepdims=True)
    y = x * lax.rsqrt(ms + EPS) * g
    return y * (1.0 + sc) + sh


def _inproj_kernel(x_ref, sc_ref, sh_ref, g_ref, cos_ref, sin_ref, w_ref, *out_refs):
    h = _norm_mod(x_ref[...], g_ref[...], sc_ref[0], sh_ref[0]).astype(BF16)
    cos = cos_ref[...]
    sin = sin_ref[...]
    first_half = (_iota(cos.shape, 1) & (D_HEAD - 1)) < HALF

    def rope(y):
        rot = jnp.where(first_half, pltpu.roll(y, LANES - HALF, 1), pltpu.roll(y, HALF, 1))
        return y * cos + rot * sin

    n_groups = len(_IN_GROUPS)
    t_refs = dict(zip(_IN_T_GROUPS, out_refs[n_groups:]))
    col = 0
    for gi, (out_ref, (width, rotary)) in enumerate(zip(out_refs, _IN_GROUPS)):
        chunk = min(width, 512)
        for c in range(0, width, chunk):
            cw = min(chunk, width - c)
            y = _dot(h, w_ref[:, col + c:col + c + cw])
            for s in range(0, cw, LANES):
                piece = y[:, s:s + LANES]
                piece = rope(piece) if rotary else piece
                out_ref[:, c + s:c + s + LANES] = piece
                if gi in t_refs and c + s >= _IN_T_GROUPS[gi]:
                    t0 = c + s - _IN_T_GROUPS[gi]
                    t_refs[gi][0, t0:t0 + LANES, :] = piece.T
        col += width


_IN_T_GROUPS = {0: MOBA_WIDTH, 1: 0, 3: 0, 4: 0}


def _inproj(x, sc, sh, g, cos, sin, w, tm, transposed=False):
    n = x.shape[0]
    nb, r, _ = sc.shape
    tiles_per_b = (n // nb) // tm
    tab_tiles = cos.shape[0] // tm
    row = lambda i: (i, 0)
    mod = lambda i: (i // tiles_per_b, 0, 0)
    tab = lambda i: (i % tab_tiles, 0)
    out_specs = [pl.BlockSpec((tm, wd), row) for wd, _ in _IN_GROUPS]
    out_shape = [jax.ShapeDtypeStruct((n, wd), F32) for wd, _ in _IN_GROUPS]
    if transposed:
        for gi, first in _IN_T_GROUPS.items():
            cols = _IN_GROUPS[gi][0] - first
            out_specs.append(pl.BlockSpec((1, cols, tm), lambda i: (i // tiles_per_b, 0, i % tiles_per_b)))
            out_shape.append(jax.ShapeDtypeStruct((nb, cols, n // nb), F32))
    return pl.pallas_call(
        _inproj_kernel,
        grid=(n // tm,),
        in_specs=[pl.BlockSpec((tm, D_MODEL), row),
                  pl.BlockSpec((1, r, D_MODEL), mod),
                  pl.BlockSpec((1, r, D_MODEL), mod),
                  pl.BlockSpec((1, D_MODEL), lambda i: (0, 0)),
                  pl.BlockSpec((tm, LANES), tab),
                  pl.BlockSpec((tm, LANES), tab),
                  pl.BlockSpec((D_MODEL, _IN_COLS_PAD), lambda i: (0, 0))],
        out_specs=out_specs,
        out_shape=out_shape,
        compiler_params=_params("arbitrary"),
        name="inproj",
    )(x, sc, sh, g, cos, sin, w)


def _reorder_w_in(w_in):
    kv0 = 3 * MOBA_WIDTH + NSA_WIDTH
    kvs = [w_in[:, kv0 + i * NSA_KV_WIDTH:kv0 + (i + 1) * NSA_KV_WIDTH] for i in range(6)]
    g0 = kv0 + 6 * NSA_KV_WIDTH
    ng = 3 * NSA_HEADS
    gate = jnp.pad(w_in[:, g0:g0 + ng], ((0, 0), (0, LANES - ng)))
    parts = [w_in[:, :kv0], kvs[0], kvs[2], kvs[4], kvs[1], kvs[3], kvs[5], gate, w_in[:, g0 + ng:]]
    return jnp.concatenate(parts, axis=1).astype(BF16)


def _rope_tables(pos):
    inv = ROPE_THETA ** (-jnp.arange(HALF, dtype=F32) / HALF)
    ang = pos.astype(F32)[:, None] * inv[None, :]
    cos = jnp.cos(ang)
    sin = jnp.sin(ang)
    cos = jnp.concatenate([cos, cos, cos, cos], axis=1)
    sin = jnp.concatenate([-sin, sin, -sin, sin], axis=1)
    return cos, sin


def _rank_select(score, n_cols, n_keep):
    lane = _iota(score.shape, 1)
    rank = jnp.zeros(score.shape, jnp.int32)
    for jp in range(n_cols):
        col = score[:, jp:jp + 1]
        beats = (col > score) | ((col == score) & (lane > jp))
        rank = rank + beats.astype(jnp.int32)
    return rank < n_keep


def _rank_select_t(score, n_rows, n_keep):
    row = _iota(score.shape, 0)
    rank = jnp.zeros(score.shape, jnp.int32)
    for jp in range(n_rows):
        r = score[jp:jp + 1, :]
        beats = (r > score) | ((r == score) & (row > jp))
        rank = rank + beats.astype(jnp.int32)
    return rank < n_keep


SUBLANES = 8


def _fold_rows(x, op):
    return op(x.reshape(x.shape[0] // SUBLANES, SUBLANES, x.shape[1]), axis=0)


def _attend_t(make_scores, vt_fn, own_c, lo, hi, width, n_chains):
    score_fn = make_scores(None)

    def max_step(c, own, m):
        return tuple(jnp.maximum(mi, _fold_rows(s, jnp.max)) for mi, s in zip(m, score_fn(c, own)))

    m = max_step(own_c, True, tuple(jnp.full((SUBLANES, width), NEG_INF, F32) for _ in range(n_chains)))
    m = lax.fori_loop(lo, hi, lambda c, mm: max_step(c, False, mm), m)
    shifted_fn = make_scores([jnp.max(mi, axis=0, keepdims=True) for mi in m])

    def acc_step(c, own, carry):
        out = []
        for (l, acc), s, vt in zip(carry, shifted_fn(c, own), vt_fn(c)):
            p = jnp.exp(s)
            out.append((l + _fold_rows(p, jnp.sum), acc + _dot(vt, p.astype(BF16))))
        return tuple(out)

    zero = tuple((jnp.zeros((SUBLANES, width), F32), jnp.zeros((D_HEAD, width), F32)) for _ in range(n_chains))
    carry = acc_step(own_c, True, zero)
    carry = lax.fori_loop(lo, hi, lambda c, cr: acc_step(c, False, cr), carry)
    return [acc / jnp.maximum(jnp.sum(l, axis=0, keepdims=True), TINY) for l, acc in carry]


AUG_ONES = 2
AUG_BLK = SUBLANES


def _key_aug(n_keys, block, n_blocks):
    lane = _iota((n_keys, D_HEAD), 1)
    blk = _iota((n_keys, D_HEAD), 0) // block
    hot = (lane < AUG_ONES) | ((lane >= AUG_BLK) & (lane < AUG_BLK + n_blocks) & (lane - AUG_BLK == blk))
    return jnp.where(hot, 1.0, 0.0).astype(BF16)


def _query_aug(qt, shift, bias):
    width = qt.shape[1]
    top = jnp.zeros((SUBLANES, width), F32)
    if shift is not None:
        hi = shift.astype(BF16).astype(F32)
        row = _iota((SUBLANES, width), 0)
        top = jnp.where(row == 0, -hi, jnp.where(row == 1, hi - shift, 0.0))
    parts = [qt, top]
    used = SUBLANES
    if bias is not None:
        parts.append(bias)
        used += bias.shape[0]
    parts.append(jnp.zeros((D_HEAD - used, width), F32))
    return jnp.concatenate(parts, axis=0).astype(BF16)


MOBA_STEP_WIDTH = 512


def _moba_p_kernel(q_ref, k_ref, v_ref, o_ref, ka_scr, vt_scr, km_scr):
    qi = pl.program_id(2)
    tq = tk = MOBA_BLOCK
    seq = k_ref.shape[1]
    nblk = seq // MOBA_BLOCK
    width = q_ref.shape[2]
    nh = width // D_HEAD

    @pl.when(qi == 0)
    def _():
        kf = k_ref[0]
        km_scr[...] = jnp.mean(kf.reshape(nblk, MOBA_BLOCK, width), axis=1)
        aug = _key_aug(seq, MOBA_BLOCK, nblk)
        for hh in range(nh):
            ka_scr[hh] = jnp.concatenate([kf[:, hh * D_HEAD:(hh + 1) * D_HEAD].astype(BF16), aug], axis=1)
        for j in range(nblk):
            vt_scr[j] = v_ref[0, :, j * tk:(j + 1) * tk].astype(BF16)

    q2t = q_ref[0].T
    km = km_scr[...]
    klane = _iota(km.shape, 1)
    blk = _iota((nblk, tq), 0)
    causal_t = _iota((tk, tq), 0) <= _iota((tk, tq), 1)
    qts, biases = [], []
    for hh in range(nh):
        kmh = jnp.where((klane >= hh * D_HEAD) & (klane < (hh + 1) * D_HEAD), km, 0.0)
        gate = jnp.dot(kmh, q2t, precision=HIGHEST, preferred_element_type=F32)
        gate = jnp.where(blk < qi, gate, NEG_INF)
        keep = _rank_select_t(gate, nblk, MOBA_TOPK) & (blk < qi)
        biases.append(jnp.where(keep, 0.0, NEG_INF))
        qts.append(q2t[hh * D_HEAD:(hh + 1) * D_HEAD, :] * SCALE)

    def make_scores(shift):
        sh = [None] * nh if shift is None else shift
        q_own = [_query_aug(qts[hh], sh[hh], None) for hh in range(nh)]
        q_past = [_query_aug(qts[hh], sh[hh], biases[hh]) for hh in range(nh)]

        def score_fn(c, own):
            out = []
            for hh in range(nh):
                kj = ka_scr[hh, pl.ds(pl.multiple_of(c * tk, tk), tk), :]
                s = _dot(kj, q_own[hh] if own else q_past[hh])
                out.append(jnp.where(causal_t, s, NEG_INF) if own else s)
            return out

        return score_fn

    def values(c):
        return [vt_scr[c, hh * D_HEAD:(hh + 1) * D_HEAD, :] for hh in range(nh)]

    o_ref[0] = jnp.concatenate(_attend_t(make_scores, values, qi, 0, qi, tq, nh), axis=0).T


def _moba_p(qk, vt, batch, seq):
    qk3 = qk.reshape(batch, seq, 2 * MOBA_WIDTH)
    sw = MOBA_STEP_WIDTH
    pairs = MOBA_WIDTH // sw
    nblk = seq // MOBA_BLOCK
    out = pl.pallas_call(
        _moba_p_kernel,
        grid=(batch, pairs, nblk),
        in_specs=[pl.BlockSpec((1, MOBA_BLOCK, sw), lambda b, h, i: (b, i, h)),
                  pl.BlockSpec((1, seq, sw), lambda b, h, i: (b, 0, pairs + h)),
                  pl.BlockSpec((1, sw, seq), lambda b, h, i: (b, h, 0))],
        out_specs=pl.BlockSpec((1, MOBA_BLOCK, sw), lambda b, h, i: (b, i, h)),
        out_shape=jax.ShapeDtypeStruct((batch, seq, MOBA_WIDTH), F32),
        scratch_shapes=[pltpu.VMEM((sw // D_HEAD, seq, 2 * D_HEAD), BF16), pltpu.VMEM((nblk, sw, MOBA_BLOCK), BF16),
                        pltpu.VMEM((nblk, sw), F32)],
        compiler_params=_params("arbitrary", "arbitrary", "arbitrary"),
        name="moba_prompt",
    )(qk3, qk3, vt)
    return out.reshape(batch * seq, MOBA_WIDTH)


def _gelu(x):
    return 0.5 * x * (1.0 + jnp.tanh(0.7978845608028654 * (x + 0.044715 * x * x * x)))


def _compress_p_kernel(seg_ref, pea_ref, peb_ref, wa_ref, wb_ref, w2_ref, o_ref):
    seg = seg_ref[0]
    a = _dot((seg + pea_ref[...]).astype(BF16), wa_ref[...])
    b = _dot((seg + peb_ref[...]).astype(BF16), wb_ref[...])
    nseg = seg.shape[0]
    hid = a + pltpu.roll(b, nseg - 1, 0)
    o_ref[0] = _dot(_gelu(hid).astype(BF16), w2_ref[...])


def _compress_weights(pe, w1, w2):
    g = NSA_KV_HEADS
    eye = jnp.eye(g, dtype=F32)
    w1r = w1.reshape(CMP_LEN, D_HEAD, CMP_HIDDEN)

    def half(lo):
        w = w1r[lo:lo + CMP_STRIDE]
        wbd = jnp.einsum('ldf,gh->lgdhf', w, eye)
        p = jnp.broadcast_to(pe[lo:lo + CMP_STRIDE, None, :], (CMP_STRIDE, g, D_HEAD))
        return wbd.reshape(CMP_STRIDE, g * D_HEAD, g * CMP_HIDDEN).astype(BF16), p.reshape(CMP_STRIDE, g * D_HEAD)

    wa, pea = half(0)
    wb, peb = half(CMP_STRIDE)
    w2bd = jnp.einsum('fd,gh->gfhd', w2, eye).reshape(g * CMP_HIDDEN, g * D_HEAD).astype(BF16)
    return wa, wb, pea, peb, w2bd


def _compress_p(rows, cw, batch, seq):
    wa, wb, pea, peb, w2bd = cw
    nseg = seq // CMP_STRIDE
    width = CMP_STRIDE * NSA_KV_WIDTH
    seg = rows.reshape(batch, nseg, width)
    full = lambda b: (0, 0)
    return pl.pallas_call(
        _compress_p_kernel,
        grid=(batch,),
        in_specs=[pl.BlockSpec((1, nseg, width), lambda b: (b, 0, 0)),
                  pl.BlockSpec((1, width), full), pl.BlockSpec((1, width), full),
                  pl.BlockSpec((width, NSA_KV_HEADS * CMP_HIDDEN), full),
                  pl.BlockSpec((width, NSA_KV_HEADS * CMP_HIDDEN), full),
                  pl.BlockSpec((NSA_KV_HEADS * CMP_HIDDEN, NSA_KV_WIDTH), full)],
        out_specs=pl.BlockSpec((1, nseg, NSA_KV_WIDTH), lambda b: (b, 0, 0)),
        out_shape=jax.ShapeDtypeStruct((batch, nseg, NSA_KV_WIDTH), F32),
        compiler_params=_params("arbitrary"),
        name="compress_prompt",
    )(seg, pea.reshape(1, width), peb.reshape(1, width), wa.reshape(width, -1), wb.reshape(width, -1), w2bd)


NSA_TQ = 256
NSA_TK = 256


def _overlap_matrix(nc_pad, nslc_pad):
    cs = jnp.arange(nc_pad)[:, None] * CMP_STRIDE
    ss = jnp.arange(nslc_pad)[None, :] * SLC_BLOCK
    return ((cs < ss + SLC_BLOCK) & (cs + CMP_LEN > ss)).astype(F32)


def _nsa_p_kernel(q_ref, kc_ref, vc_ref, ks_ref, vs_ref, kw_ref, vw_ref, gate_ref, ovt_ref, o_ref,
                  ksa_scr, kwa_scr, vst_scr, vwt_scr):
    tq, tk = NSA_TQ, NSA_TK
    t = pl.program_id(1)
    q0 = t * tq
    seq = ks_ref.shape[1]
    nc_pad = kc_ref.shape[1]
    nslc = seq // SLC_BLOCK
    width = NSA_GROUP * tq

    @pl.when(t == 0)
    def _():
        aug_s = _key_aug(seq, SLC_BLOCK, nslc)
        aug_w = _key_aug(seq, SLC_BLOCK, 0)
        for g in range(NSA_KV_HEADS):
            rows = slice(g * D_HEAD, (g + 1) * D_HEAD)
            ksa_scr[g] = jnp.concatenate([ks_ref[0, :, rows].astype(BF16), aug_s], axis=1)
            kwa_scr[g] = jnp.concatenate([kw_ref[0, :, rows].astype(BF16), aug_w], axis=1)
        for j in range(seq // tk):
            vst_scr[j] = vs_ref[0, :, j * tk:(j + 1) * tk].astype(BF16)
            vwt_scr[j] = vw_ref[0, :, j * tk:(j + 1) * tk].astype(BF16)

    qt_all = q_ref[0].T
    gates_t = _sigmoid(gate_ref[0]).T
    kct = kc_ref[0].astype(BF16)
    vct = vc_ref[0].T.astype(BF16)
    pos = q0 + _iota((1, tq), 1)
    n_i = _iota((nc_pad, tq), 0)
    valid_c = (n_i < nc_pad - 1) & (n_i * CMP_STRIDE + (CMP_LEN - 1) <= pos)
    valid_c4 = jnp.concatenate([valid_c] * NSA_GROUP, axis=1)
    q_blk = pos >> 6
    jrow = _iota((nslc, tq), 0)
    krow = _iota((tk, tq), 0)
    cd = q0 // tk
    c_win = jnp.maximum(q0 - WINDOW, 0) // tk

    def tile(b):
        return jnp.concatenate([b] * NSA_GROUP, axis=1)

    def win_bias(c):
        dist = pos - (c * tk + krow)
        return jnp.where((dist >= 0) & (dist <= WINDOW), 0.0, NEG_INF)

    qts, o_cs, biases = [], [], []
    for g in range(NSA_KV_HEADS):
        rows = slice(g * D_HEAD, (g + 1) * D_HEAD)
        qt = jnp.concatenate(
            [qt_all[(g * NSA_GROUP + r) * D_HEAD:(g * NSA_GROUP + r + 1) * D_HEAD, :] * SCALE
             for r in range(NSA_GROUP)], axis=1)
        qts.append(qt)
        s = jnp.where(valid_c4, _dot(kct[:, rows], qt.astype(BF16)), NEG_INF)
        m = jnp.max(s, axis=0, keepdims=True)
        p = jnp.where(valid_c4, jnp.exp(s - m), 0.0)
        p = p / jnp.maximum(jnp.sum(p, axis=0, keepdims=True), TINY)
        o_cs.append(_dot(vct[rows, :], p.astype(BF16)))
        psum = p[:, :tq]
        for r in range(1, NSA_GROUP):
            psum = psum + p[:, r * tq:(r + 1) * tq]
        imp = jnp.dot(ovt_ref[...], psum, precision=HIGHEST, preferred_element_type=F32)
        imp = jnp.where(jrow == q_blk, BIG, jnp.where(jrow < q_blk, imp, NEG_INF))
        keep = _rank_select_t(imp, nslc, SLC_TOPN) & (jrow <= q_blk)
        biases.append(tile(jnp.where(keep, 0.0, NEG_INF)))

    causal_own = tile(cd * tk + krow <= pos)

    def make_slc_scores(shift):
        sh = [None] * NSA_KV_HEADS if shift is None else shift
        qa = [_query_aug(qts[g], sh[g], biases[g]) for g in range(NSA_KV_HEADS)]

        def score_fn(c, own):
            out = []
            for g in range(NSA_KV_HEADS):
                s = _dot(ksa_scr[g, pl.ds(pl.multiple_of(c * tk, tk), tk), :], qa[g])
                out.append(jnp.where(causal_own, s, NEG_INF) if own else s)
            return out

        return score_fn

    def make_win_scores(shift):
        sh = [None] * NSA_KV_HEADS if shift is None else shift
        qa = [_query_aug(qts[g], sh[g], None) for g in range(NSA_KV_HEADS)]

        def score_fn(c, own):
            wb = tile(win_bias(c))
            return [_dot(kwa_scr[g, pl.ds(pl.multiple_of(c * tk, tk), tk), :], qa[g]) + wb
                    for g in range(NSA_KV_HEADS)]

        return score_fn

    def values(vt_scr):
        return lambda c: [vt_scr[c, g * D_HEAD:(g + 1) * D_HEAD, :] for g in range(NSA_KV_HEADS)]

    o_ss = _attend_t(make_slc_scores, values(vst_scr), cd, 0, cd, width, NSA_KV_HEADS)
    o_ws = _attend_t(make_win_scores, values(vwt_scr), cd, c_win, cd, width, NSA_KV_HEADS)
    outs = []
    for g in range(NSA_KV_HEADS):
        o_c, o_s, o_w = o_cs[g], o_ss[g], o_ws[g]
        for r in range(NSA_GROUP):
            hd = g * NSA_GROUP + r
            cs = slice(r * tq, (r + 1) * tq)
            outs.append(gates_t[3 * hd:3 * hd + 1, :] * o_c[:, cs] + gates_t[3 * hd + 1:3 * hd + 2, :] * o_s[:, cs]
                        + gates_t[3 * hd + 2:3 * hd + 3, :] * o_w[:, cs])
    o_ref[0] = jnp.concatenate(outs, axis=0).T


def _nsa_p(q, kn, vnt, kcmp, vcmp, gate, batch, seq):
    q3 = q.reshape(batch, seq, NSA_WIDTH)
    kn3 = kn.reshape(batch, seq, 3 * NSA_KV_WIDTH)
    g3 = gate.reshape(batch, seq, LANES)
    nc_pad = kcmp.shape[1]
    nslc = seq // SLC_BLOCK
    nchunk = seq // NSA_TK
    ovt = _overlap_matrix(nc_pad, nslc).T
    tile = lambda b, t: (b, t, 0)
    cmp_spec = pl.BlockSpec((1, nc_pad, NSA_KV_WIDTH), lambda b, t: (b, 0, 0))
    out = pl.pallas_call(
        _nsa_p_kernel,
        grid=(batch, seq // NSA_TQ),
        in_specs=[pl.BlockSpec((1, NSA_TQ, NSA_WIDTH), tile), cmp_spec, cmp_spec,
                  pl.BlockSpec((1, seq, LANES), lambda b, t: (b, 0, 1)),
                  pl.BlockSpec((1, LANES, seq), lambda b, t: (b, 1, 0)),
                  pl.BlockSpec((1, seq, LANES), lambda b, t: (b, 0, 2)),
                  pl.BlockSpec((1, LANES, seq), lambda b, t: (b, 2, 0)),
                  pl.BlockSpec((1, NSA_TQ, LANES), tile),
                  pl.BlockSpec((nslc, nc_pad), lambda b, t: (0, 0))],
        out_specs=pl.BlockSpec((1, NSA_TQ, NSA_WIDTH), tile),
        out_shape=jax.ShapeDtypeStruct((batch, seq, NSA_WIDTH), F32),
        scratch_shapes=[pltpu.VMEM((NSA_KV_HEADS, seq, 2 * D_HEAD), BF16),
                        pltpu.VMEM((NSA_KV_HEADS, seq, 2 * D_HEAD), BF16),
                        pltpu.VMEM((nchunk, LANES, NSA_TK), BF16), pltpu.VMEM((nchunk, LANES, NSA_TK), BF16)],
        compiler_params=_params("arbitrary", "arbitrary"),
        name="nsa_prompt",
    )(q3, kcmp, vcmp, kn3, vnt, kn3, vnt, g3, ovt)
    return out.reshape(batch * seq, NSA_WIDTH)


def _merge_kernel(oa_ref, ob_ref, mg_ref, x_ref, g1_ref, wa_ref, wb_ref, wo_ref, o_ref):
    a = _dot(oa_ref[...].astype(BF16), wa_ref[...])
    b = _dot(ob_ref[...].astype(BF16), wb_ref[...])
    mix = _sigmoid(mg_ref[:, :D_MODEL]) * a + _sigmoid(mg_ref[:, D_MODEL:]) * b
    o_ref[...] = x_ref[...] + g1_ref[0] * _dot(mix.astype(BF16), wo_ref[...])


def _merge(oa, ob, mg, x, g1, wa, wb, wo, tm):
    n = x.shape[0]
    nb, r, _ = g1.shape
    tiles_per_b = (n // nb) // tm
    row = lambda i: (i, 0)
    full = lambda i: (0, 0)
    return pl.pallas_call(
        _merge_kernel,
        grid=(n // tm,),
        in_specs=[pl.BlockSpec((tm, MOBA_WIDTH), row), pl.BlockSpec((tm, NSA_WIDTH), row),
                  pl.BlockSpec((tm, 2 * D_MODEL), row), pl.BlockSpec((tm, D_MODEL), row),
                  pl.BlockSpec((1, r, D_MODEL), lambda i: (i // tiles_per_b, 0, 0)),
                  pl.BlockSpec((MOBA_WIDTH, D_MODEL), full), pl.BlockSpec((NSA_WIDTH, D_MODEL), full),
                  pl.BlockSpec((D_MODEL, D_MODEL), full)],
        out_specs=pl.BlockSpec((tm, D_MODEL), row),
        out_shape=jax.ShapeDtypeStruct((n, D_MODEL), F32),
        compiler_params=_params("arbitrary"),
        name="merge",
    )(oa, ob, mg, x, g1, wa, wb, wo)


def _route(logits):
    lane = _iota(logits.shape, 1)
    is_grp = lane < N_GROUPS
    lg = jnp.where(is_grp, logits, NEG_INF)
    mg = jnp.max(lg, axis=-1, keepdims=True)
    pg = jnp.where(is_grp, jnp.exp(lg - mg), 0.0)
    pg = pg / jnp.sum(pg, axis=-1, keepdims=True)
    g_w = jnp.max(pg, axis=-1, keepdims=True)
    g_sel = jnp.min(jnp.where(is_grp & (pg == g_w), lane, LANES), axis=-1, keepdims=True)
    e_lane = lane - N_GROUPS
    in_grp = (e_lane >= 0) & (e_lane < N_EXPERTS) & ((e_lane >> 3) == g_sel)
    le = jnp.where(in_grp, logits, NEG_INF)
    me = jnp.max(le, axis=-1, keepdims=True)
    pe = jnp.where(in_grp, jnp.exp(le - me), 0.0)
    pe = pe / jnp.sum(pe, axis=-1, keepdims=True)
    v1 = jnp.max(pe, axis=-1, keepdims=True)
    i1 = jnp.min(jnp.where(in_grp & (pe == v1), lane, LANES), axis=-1, keepdims=True)
    rest = in_grp & (lane != i1)
    pr = jnp.where(rest, pe, -1.0)
    v2 = jnp.max(pr, axis=-1, keepdims=True)
    i2 = jnp.min(jnp.where(rest & (pr == v2), lane, LANES), axis=-1, keepdims=True)
    tot = v1 + v2
    comb = jnp.where(lane == i1, v1 / tot, 0.0) + jnp.where(lane == i2, v2 / tot, 0.0)
    comb = comb * g_w
    return pltpu.roll(comb, LANES - N_GROUPS, 1)


MOE_EXPERTS_PER_STEP = 8


def _moe_kernel(x_ref, sc_ref, sh_ref, g2_ref, gn_ref, gf_ref, wr_ref, win_ref, wout_ref, o_ref,
                h_scr, comb_scr, acc_scr):
    e = pl.program_id(1)

    @pl.when(e == 0)
    def _():
        h = _norm_mod(x_ref[...], gn_ref[...], sc_ref[0], sh_ref[0])
        hb = h.astype(BF16)
        h_scr[...] = hb
        h_lo = (h - hb.astype(F32)).astype(BF16)
        logits = _dot(hb, wr_ref[0]) + _dot(hb, wr_ref[1]) + _dot(h_lo, wr_ref[0])
        comb_scr[...] = _route(logits)
        acc_scr[...] = jnp.zeros_like(acc_scr)

    per = win_ref.shape[0]
    hb = h_scr[...]
    comb = comb_scr[...]
    lane = _iota(comb.shape, 1)
    acts = []
    for j in range(per):
        hid = _dot(hb, win_ref[j])
        a = hid[:, :D_EXPERT]
        b = hid[:, D_EXPERT:]
        w = jnp.sum(jnp.where(lane == e * per + j, comb, 0.0), axis=1, keepdims=True)
        acts.append((a * _sigmoid(a) * b * w).astype(BF16))
    act = jnp.concatenate(acts, axis=1)
    acc_scr[...] += _dot(act, wout_ref[...].reshape(per * D_EXPERT, D_MODEL))

    @pl.when(e == pl.num_programs(1) - 1)
    def _():
        y = x_ref[...] + g2_ref[0] * acc_scr[...]
        ms = jnp.mean(y * y, axis=-1, keepdims=True)
        o_ref[...] = y * lax.rsqrt(ms + EPS) * gf_ref[...]


def _moe(x, sc, sh, g2, gn, gf, wr, w_ein, w_eout, tm):
    n = x.shape[0]
    nb, r, _ = sc.shape
    tiles_per_b = (n // nb) // tm
    row = lambda i, e: (i, 0)
    mod = lambda i, e: (i // tiles_per_b, 0, 0)
    full = lambda i, e: (0, 0)
    per = MOE_EXPERTS_PER_STEP
    return pl.pallas_call(
        _moe_kernel,
        grid=(n // tm, N_EXPERTS // per),
        in_specs=[pl.BlockSpec((tm, D_MODEL), row),
                  pl.BlockSpec((1, r, D_MODEL), mod), pl.BlockSpec((1, r, D_MODEL), mod),
                  pl.BlockSpec((1, r, D_MODEL), mod),
                  pl.BlockSpec((1, D_MODEL), full), pl.BlockSpec((1, D_MODEL), full),
                  pl.BlockSpec((2, D_MODEL, LANES), lambda i, e: (0, 0, 0)),
                  pl.BlockSpec((per, D_MODEL, 2 * D_EXPERT), lambda i, e: (e, 0, 0)),
                  pl.BlockSpec((per, D_EXPERT, D_MODEL), lambda i, e: (e, 0, 0))],
        out_specs=pl.BlockSpec((tm, D_MODEL), row),
        out_shape=jax.ShapeDtypeStruct((n, D_MODEL), F32),
        scratch_shapes=[pltpu.VMEM((tm, D_MODEL), BF16), pltpu.VMEM((tm, LANES), F32),
                        pltpu.VMEM((tm, D_MODEL), F32)],
        compiler_params=_params("arbitrary", "arbitrary"),
        name="moe",
    )(x, sc, sh, g2, gn, gf, wr, w_ein, w_eout)


PAGE = 128
MOBA_PAGES_PER_STEP = 8
NSA_PAGES_PER_STEP = 16
COMPRESS_PAGES_PER_STEP = 32


def _page_view(cache):
    n_phys, page, heads, dh = cache.shape
    return jnp.transpose(cache, (0, 2, 3, 1)).reshape(n_phys, heads * dh, page)


def _page_specs(rows, n_pages, per_step):
    def spec(u):
        return pl.BlockSpec((1, rows, PAGE), lambda b, s, pt: (pt[b * n_pages + s * per_step + u], 0, 0))
    return [spec(u) for u in range(per_step)]


def _head_diag(full, heads):
    rows = full.shape[0]
    head = _iota((rows, D_HEAD), 0) // (rows // heads)
    out = jnp.zeros((rows, D_HEAD), F32)
    for h in range(heads):
        out = out + jnp.where(head == h, full[:, h * D_HEAD:(h + 1) * D_HEAD], 0.0)
    return out


def _moba_s_kernel(pt_ref, qbd_ref, *refs):
    per = MOBA_PAGES_PER_STEP
    k_refs, v_refs = refs[:per], refs[per:2 * per]
    m_ref, l_ref, ks_ref, o_ref = refs[2 * per:]
    s = pl.program_id(1)

    @pl.when(s == 0)
    def _():
        m_ref[...] = jnp.zeros_like(m_ref)
        l_ref[...] = jnp.zeros_like(l_ref)
        ks_ref[...] = jnp.zeros_like(ks_ref)

    qbd = qbd_ref[0]
    lane_q = _iota(m_ref.shape[1:], 1)
    lane_k = _iota(ks_ref.shape[1:], 1)
    ppb = MOBA_BLOCK // PAGE
    m_all, l_all, ks_all = m_ref[0], l_ref[0], ks_ref[0]
    kts = [k_refs[u][0] for u in range(per)]
    sc_all = _dot(qbd, jnp.concatenate([kt.astype(BF16) for kt in kts], axis=1))
    for j in range(per // ppb):
        blk = s * (per // ppb) + j
        sc = sc_all[:, j * MOBA_BLOCK:(j + 1) * MOBA_BLOCK]
        m = jnp.max(sc, axis=-1, keepdims=True)
        p = jnp.exp(sc - m)
        l = jnp.sum(p, axis=-1, keepdims=True)
        vt = jnp.concatenate([v_refs[j * ppb + t][0].astype(BF16) for t in range(ppb)], axis=1)
        o_ref[0, j] = _head_diag(_dot_nt(p.astype(BF16), vt), MOBA_HEADS)
        kb = kts[j * ppb]
        for t in range(1, ppb):
            kb = kb + kts[j * ppb + t]
        ksum = jnp.sum(kb, axis=-1, keepdims=True)
        m_all = jnp.where(lane_q == blk, m, m_all)
        l_all = jnp.where(lane_q == blk, l, l_all)
        ks_all = jnp.where(lane_k == blk, ksum, ks_all)
    m_ref[0] = m_all
    l_ref[0] = l_all
    ks_ref[0] = ks_all


def _moba_s_pass(pt_flat, qbd, kt_pages, vt_pages, batch, n_pages):
    per = MOBA_PAGES_PER_STEP
    rows = qbd.shape[1]
    ppb = MOBA_BLOCK // PAGE
    stat = lambda b, s, pt: (b, 0, 0)
    return pl.pallas_call(
        _moba_s_kernel,
        grid_spec=pltpu.PrefetchScalarGridSpec(
            num_scalar_prefetch=1,
            grid=(batch, n_pages // per),
            in_specs=[pl.BlockSpec((1, rows, MOBA_WIDTH), stat)]
            + _page_specs(MOBA_WIDTH, n_pages, per) + _page_specs(MOBA_WIDTH, n_pages, per),
            out_specs=[pl.BlockSpec((1, rows, LANES), stat), pl.BlockSpec((1, rows, LANES), stat),
                       pl.BlockSpec((1, MOBA_WIDTH, LANES), stat),
                       pl.BlockSpec((1, per // ppb, rows, D_HEAD), lambda b, s, pt: (b, s, 0, 0))]),
        out_shape=[jax.ShapeDtypeStruct((batch, rows, LANES), F32), jax.ShapeDtypeStruct((batch, rows, LANES), F32),
                   jax.ShapeDtypeStruct((batch, MOBA_WIDTH, LANES), F32),
                   jax.ShapeDtypeStruct((batch, n_pages // ppb, rows, D_HEAD), F32)],
        compiler_params=_params("arbitrary", "arbitrary"),
        name="moba_decode_pages",
    )(pt_flat, qbd, *([kt_pages] * per), *([vt_pages] * per))


def _moba_s_combine_kernel(m_ref, l_ref, ks_ref, o_ref, qf_ref, qbd_ref, kn_ref, vn_ref, out_ref, *, n_pages, ts):
    rows = m_ref.shape[1]
    nblk = n_pages // (MOBA_BLOCK // PAGE)
    lane = _iota((rows, LANES), 1)
    kmean = ks_ref[0] * (1.0 / MOBA_BLOCK)
    gate = jnp.dot(qf_ref[0], kmean, precision=HIGHEST, preferred_element_type=F32)
    gate = jnp.where(lane < nblk, gate, NEG_INF)
    selp = _rank_select(gate, nblk, MOBA_TOPK) & (lane < nblk)
    qbd = qbd_ref[0]
    s_own = _dot_nt(qbd, kn_ref[0].astype(BF16))
    valid_own = lane <= (_iota((rows, LANES), 0) % ts)
    s_own = jnp.where(valid_own, s_own, NEG_INF)
    m_all = jnp.where(selp, m_ref[0], NEG_INF)
    big_m = jnp.maximum(jnp.max(m_all, axis=-1, keepdims=True), jnp.max(s_own, axis=-1, keepdims=True))
    wgt = jnp.where(selp, jnp.exp(m_ref[0] - big_m), 0.0)
    p_own = jnp.where(valid_own, jnp.exp(s_own - big_m), 0.0)
    denom = jnp.sum(wgt * l_ref[0], axis=-1, keepdims=True) + jnp.sum(p_own, axis=-1, keepdims=True)
    num = _head_diag(_dot(p_own.astype(BF16), vn_ref[0].astype(BF16)), MOBA_HEADS)
    for j in range(nblk):
        num = num + wgt[:, j:j + 1] * o_ref[0, j]
    out_ref[0] = num / jnp.maximum(denom, TINY)


def _moba_s_combine(m, l, ks, o, qf, qbd, kn, vn, n_pages, ts):
    batch, rows, _ = m.shape
    b3 = lambda b: (b, 0, 0)
    return pl.pallas_call(
        functools.partial(_moba_s_combine_kernel, n_pages=n_pages, ts=ts),
        grid=(batch,),
        in_specs=[pl.BlockSpec((1, rows, LANES), b3), pl.BlockSpec((1, rows, LANES), b3),
                  pl.BlockSpec((1, MOBA_WIDTH, LANES), b3),
                  pl.BlockSpec((1, o.shape[1], rows, D_HEAD), lambda b: (b, 0, 0, 0)),
                  pl.BlockSpec((1, rows, MOBA_WIDTH), b3), pl.BlockSpec((1, rows, MOBA_WIDTH), b3),
                  pl.BlockSpec((1, LANES, MOBA_WIDTH), b3), pl.BlockSpec((1, LANES, MOBA_WIDTH), b3)],
        out_specs=pl.BlockSpec((1, rows, D_HEAD), b3),
        out_shape=jax.ShapeDtypeStruct((batch, rows, D_HEAD), F32),
        compiler_params=_params("arbitrary"),
        name="moba_decode_combine",
    )(m, l, ks, o, qf, qbd, kn, vn)


def _block_diag_q(q, batch, ts, heads):
    q4 = q.reshape(batch, ts, heads, D_HEAD)
    eye = jnp.eye(heads, dtype=q.dtype)
    return jnp.einsum('bchd,hk->bhckd', q4, eye).reshape(batch, heads * ts, heads * D_HEAD)


def _pad_rows(a, batch, ts):
    a3 = a.reshape(batch, ts, a.shape[-1])
    return jnp.pad(a3, ((0, 0), (0, LANES - ts), (0, 0)))


def _compress_s_kernel(pt_ref, *refs):
    per = COMPRESS_PAGES_PER_STEP
    page_refs = refs[:per]
    pea_ref, peb_ref, wa_ref, wb_ref, w2_ref, o_ref, x_scr, a_scr, b_scr = refs[per:]
    s = pl.program_id(1)
    nseg = per * PAGE // CMP_STRIDE
    for u in range(per):
        x_scr[u * PAGE:(u + 1) * PAGE, :] = page_refs[u][0].T
    xs = [x_scr[pl.ds(l, nseg, stride=CMP_STRIDE), :] for l in range(CMP_STRIDE)]
    xa = jnp.concatenate([(xs[l] + pea_ref[l:l + 1, :]).astype(BF16) for l in range(CMP_STRIDE)], axis=1)
    xb = jnp.concatenate([(xs[l] + peb_ref[l:l + 1, :]).astype(BF16) for l in range(CMP_STRIDE)], axis=1)
    a_scr[pl.ds(pl.multiple_of(s * nseg, nseg), nseg), :] = _dot(xa, wa_ref[...])
    b_scr[pl.ds(pl.multiple_of(s * nseg, nseg), nseg), :] = _dot(xb, wb_ref[...])

    @pl.when(s == pl.num_programs(1) - 1)
    def _():
        total = a_scr.shape[0]
        hid = a_scr[...] + pltpu.roll(b_scr[...], total - 1, 0)
        o_ref[0] = _dot(_gelu(hid).astype(BF16), w2_ref[...])


def _compress_s(pt_flat, pages, cw, batch, n_pages):
    wa, wb, pea, peb, w2bd = cw
    per = COMPRESS_PAGES_PER_STEP
    total = n_pages * PAGE // CMP_STRIDE
    hidden = NSA_KV_HEADS * CMP_HIDDEN
    width = CMP_STRIDE * NSA_KV_WIDTH
    full2 = lambda b, s, pt: (0, 0)
    return pl.pallas_call(
        _compress_s_kernel,
        grid_spec=pltpu.PrefetchScalarGridSpec(
            num_scalar_prefetch=1,
            grid=(batch, n_pages // per),
            in_specs=_page_specs(NSA_KV_WIDTH, n_pages, per)
            + [pl.BlockSpec((CMP_STRIDE, NSA_KV_WIDTH), full2), pl.BlockSpec((CMP_STRIDE, NSA_KV_WIDTH), full2),
               pl.BlockSpec((width, hidden), full2), pl.BlockSpec((width, hidden), full2),
               pl.BlockSpec((hidden, NSA_KV_WIDTH), full2)],
            out_specs=pl.BlockSpec((1, total, NSA_KV_WIDTH), lambda b, s, pt: (b, 0, 0)),
            scratch_shapes=[pltpu.VMEM((per * PAGE, NSA_KV_WIDTH), F32), pltpu.VMEM((total, hidden), F32),
                            pltpu.VMEM((total, hidden), F32)]),
        out_shape=jax.ShapeDtypeStruct((batch, total, NSA_KV_WIDTH), F32),
        compiler_params=_params("arbitrary", "arbitrary"),
        name="compress_decode",
    )(pt_flat, *([pages] * per), pea, peb, wa.reshape(width, hidden), wb.reshape(width, hidden), w2bd)


def _stack_group_q(q_ref, g):
    return jnp.concatenate(
        [q_ref[0, :, (g * NSA_GROUP + r) * D_HEAD:(g * NSA_GROUP + r + 1) * D_HEAD] * SCALE
         for r in range(NSA_GROUP)], axis=0).astype(BF16)


def _nsa_s_kernel(q_ref, kc_ref, vc_ref, wk_ref, wv_ref, kn_ref, vn_ref, gate_ref, ov_ref, ex_ref, part_ref, sel_ref,
                  *, ts):
    rows = NSA_GROUP * ts
    nc_pad = kc_ref.shape[1]
    wlen = wk_ref.shape[3]
    gates = _sigmoid(gate_ref[0])
    n_i = _iota((rows, nc_pad), 1)
    valid_c = n_i < nc_pad - 1
    c_of_row = _iota((rows, 1), 0) % ts
    valid_w = _iota((rows, wlen), 1) >= c_of_row
    valid_n = _iota((rows, LANES), 1) <= c_of_row
    for g in range(NSA_KV_HEADS):
        lane0 = g * D_HEAD
        qs = _stack_group_q(q_ref, g)
        s = jnp.where(valid_c, _dot_nt(qs, kc_ref[0, :, lane0:lane0 + D_HEAD].astype(BF16)), NEG_INF)
        m = jnp.max(s, axis=-1, keepdims=True)
        p = jnp.where(valid_c, jnp.exp(s - m), 0.0)
        p = p / jnp.maximum(jnp.sum(p, axis=-1, keepdims=True), TINY)
        o_c = _dot(p.astype(BF16), vc_ref[0, :, lane0:lane0 + D_HEAD].astype(BF16))
        psum = jnp.sum(p.reshape(NSA_GROUP, ts, nc_pad), axis=0)
        imp = jnp.dot(psum, ov_ref[...], precision=HIGHEST, preferred_element_type=F32)
        keep = _rank_select(imp, LANES, SLC_TOPN - 1).astype(BF16)
        sel_ref[0, g] = jnp.where(_dot(keep, ex_ref[...]) > 0.5, 0.0, NEG_INF)
        s_w = jnp.where(valid_w, _dot(qs, wk_ref[0, g].astype(BF16)), NEG_INF)
        s_n = jnp.where(valid_n, _dot_nt(qs, kn_ref[0, :, lane0:lane0 + D_HEAD].astype(BF16)), NEG_INF)
        m = jnp.maximum(jnp.max(s_w, axis=-1, keepdims=True), jnp.max(s_n, axis=-1, keepdims=True))
        p_w = jnp.where(valid_w, jnp.exp(s_w - m), 0.0)
        p_n = jnp.where(valid_n, jnp.exp(s_n - m), 0.0)
        den = jnp.sum(p_w, axis=-1, keepdims=True) + jnp.sum(p_n, axis=-1, keepdims=True)
        o_w = (_dot_nt(p_w.astype(BF16), wv_ref[0, g].astype(BF16))
               + _dot(p_n.astype(BF16), vn_ref[0, :, lane0:lane0 + D_HEAD].astype(BF16))) / jnp.maximum(den, TINY)
        for r in range(NSA_GROUP):
            hd = g * NSA_GROUP + r
            rs = slice(r * ts, (r + 1) * ts)
            part_ref[0, :, hd * D_HEAD:(hd + 1) * D_HEAD] = (
                gates[:, 3 * hd:3 * hd + 1] * o_c[rs] + gates[:, 3 * hd + 2:3 * hd + 3] * o_w[rs])


def _nsa_s(q3, kcmp, vcmp, wk_t, wv_t, kn_w, vn_w, gate3, ts):
    batch = q3.shape[0]
    nc_pad = kcmp.shape[1]
    wlen = wk_t.shape[3]
    ov = _overlap_matrix(nc_pad, LANES)
    n_keys = LANES * SLC_BLOCK
    expand = (jnp.arange(LANES)[:, None] == jnp.arange(n_keys)[None, :] // SLC_BLOCK).astype(BF16)
    b3 = lambda b: (b, 0, 0)
    b4 = lambda b: (b, 0, 0, 0)
    return pl.pallas_call(
        functools.partial(_nsa_s_kernel, ts=ts),
        grid=(batch,),
        in_specs=[pl.BlockSpec((1, ts, NSA_WIDTH), b3),
                  pl.BlockSpec((1, nc_pad, NSA_KV_WIDTH), b3), pl.BlockSpec((1, nc_pad, NSA_KV_WIDTH), b3),
                  pl.BlockSpec((1, NSA_KV_HEADS, D_HEAD, wlen), b4), pl.BlockSpec((1, NSA_KV_HEADS, D_HEAD, wlen), b4),
                  pl.BlockSpec((1, LANES, NSA_KV_WIDTH), b3), pl.BlockSpec((1, LANES, NSA_KV_WIDTH), b3),
                  pl.BlockSpec((1, ts, LANES), b3),
                  pl.BlockSpec((nc_pad, LANES), lambda b: (0, 0)),
                  pl.BlockSpec((LANES, n_keys), lambda b: (0, 0))],
        out_specs=[pl.BlockSpec((1, ts, NSA_WIDTH), b3), pl.BlockSpec((1, NSA_KV_HEADS, ts, n_keys), b4)],
        out_shape=[jax.ShapeDtypeStruct((batch, ts, NSA_WIDTH), F32),
                   jax.ShapeDtypeStruct((batch, NSA_KV_HEADS, ts, n_keys), F32)],
        compiler_params=_params("arbitrary"),
        name="nsa_decode_cmp_win",
    )(q3, kcmp, vcmp, wk_t, wv_t, kn_w, vn_w, gate3, ov, expand)


def _slc_s_kernel(pt_ref, q_ref, bias_ref, *refs, ts):
    per = NSA_PAGES_PER_STEP
    k_refs, v_refs = refs[:per], refs[per:2 * per]
    kn_ref, vn_ref, gate_ref, part_ref, o_ref, m_scr, l_scr, acc_scr = refs[2 * per:]
    s = pl.program_id(1)
    rows = NSA_GROUP * ts
    zero = jnp.zeros((rows, D_HEAD), BF16)
    q2 = jnp.concatenate([jnp.concatenate([_stack_group_q(q_ref, 0), zero], axis=1),
                          jnp.concatenate([zero, _stack_group_q(q_ref, 1)], axis=1)], axis=0)
    in_g0 = _iota((NSA_KV_HEADS * rows, D_HEAD), 0) < rows

    def own_group(full):
        return jnp.where(in_g0, full[:, :D_HEAD], full[:, D_HEAD:])

    @pl.when(s == 0)
    def _():
        valid_n = _iota((NSA_KV_HEADS * rows, LANES), 1) <= (_iota((NSA_KV_HEADS * rows, 1), 0) % ts)
        sc = jnp.where(valid_n, _dot_nt(q2, kn_ref[0].astype(BF16)), NEG_INF)
        m = jnp.max(sc, axis=-1, keepdims=True)
        p = jnp.exp(sc - m)
        m_scr[...] = m
        l_scr[...] = jnp.sum(p, axis=-1, keepdims=True)
        acc_scr[...] = own_group(_dot(p.astype(BF16), vn_ref[0].astype(BF16)))

    kt = jnp.concatenate([k_refs[u][0].astype(BF16) for u in range(per)], axis=1)
    vt = jnp.concatenate([v_refs[u][0].astype(BF16) for u in range(per)], axis=1)
    bias = jnp.concatenate([bias_ref[0, g] for g in range(NSA_KV_HEADS) for _ in range(NSA_GROUP)], axis=0)
    sc = _dot(q2, kt) + bias
    m_old = m_scr[...]
    m_new = jnp.maximum(m_old, jnp.max(sc, axis=-1, keepdims=True))
    pf = jnp.exp(sc - m_new)
    alpha = jnp.exp(m_old - m_new)
    l_scr[...] = alpha * l_scr[...] + jnp.sum(pf, axis=-1, keepdims=True)
    acc_scr[...] = alpha * acc_scr[...] + own_group(_dot_nt(pf.astype(BF16), vt))
    m_scr[...] = m_new

    @pl.when(s == pl.num_programs(1) - 1)
    def _():
        gates = _sigmoid(gate_ref[0])
        o_s = acc_scr[...] / jnp.maximum(l_scr[...], TINY)
        for hd in range(NSA_HEADS):
            cols = slice(hd * D_HEAD, (hd + 1) * D_HEAD)
            o_ref[0, :, cols] = part_ref[0, :, cols] + gates[:, 3 * hd + 1:3 * hd + 2] * o_s[hd * ts:(hd + 1) * ts]


def _slc_s(pt_flat, q3, sel_bias, k_pages, v_pages, kn_s, vn_s, gate3, part, n_pages, ts):
    batch = q3.shape[0]
    per = NSA_PAGES_PER_STEP
    rows = NSA_GROUP * ts
    b3 = lambda b, s, pt: (b, 0, 0)
    return pl.pallas_call(
        functools.partial(_slc_s_kernel, ts=ts),
        grid_spec=pltpu.PrefetchScalarGridSpec(
            num_scalar_prefetch=1,
            grid=(batch, n_pages // per),
            in_specs=[pl.BlockSpec((1, ts, NSA_WIDTH), b3),
                      pl.BlockSpec((1, NSA_KV_HEADS, ts, per * PAGE), lambda b, s, pt: (b, 0, 0, s))]
            + _page_specs(NSA_KV_WIDTH, n_pages, per) + _page_specs(NSA_KV_WIDTH, n_pages, per)
            + [pl.BlockSpec((1, LANES, NSA_KV_WIDTH), b3), pl.BlockSpec((1, LANES, NSA_KV_WIDTH), b3),
               pl.BlockSpec((1, ts, LANES), b3), pl.BlockSpec((1, ts, NSA_WIDTH), b3)],
            out_specs=pl.BlockSpec((1, ts, NSA_WIDTH), b3),
            scratch_shapes=[pltpu.VMEM((NSA_KV_HEADS * rows, 1), F32), pltpu.VMEM((NSA_KV_HEADS * rows, 1), F32),
                            pltpu.VMEM((NSA_KV_HEADS * rows, D_HEAD), F32)]),
        out_shape=jax.ShapeDtypeStruct((batch, ts, NSA_WIDTH), F32),
        compiler_params=_params("arbitrary", "arbitrary"),
        name="nsa_decode_slc",
    )(pt_flat, q3, sel_bias, *([k_pages] * per), *([v_pages] * per), kn_s, vn_s, gate3, part)


def _prep_weights(w_ada, b_ada, norm_mix_g, w_in, pe_cmp_k, w_cmp_k1, w_cmp_k2, pe_cmp_v, w_cmp_v1, w_cmp_v2,
                  w_br_a, w_br_b, w_out, norm_ffn_g, w_router_grp, w_router_exp, w_expert_in, w_expert_out,
                  norm_final_g):
    wr = jnp.concatenate([w_router_grp, w_router_exp], axis=1)
    wr = jnp.pad(wr, ((0, 0), (0, LANES - wr.shape[1])))
    wr_hi = wr.astype(BF16)
    wr = jnp.stack([wr_hi, (wr - wr_hi.astype(F32)).astype(BF16)])
    return dict(
        w_ada=w_ada, b_ada=b_ada.reshape(1, -1), g_mix=norm_mix_g.reshape(1, -1),
        w_in=_reorder_w_in(w_in),
        cmp_k=_compress_weights(pe_cmp_k, w_cmp_k1, w_cmp_k2),
        cmp_v=_compress_weights(pe_cmp_v, w_cmp_v1, w_cmp_v2),
        w_br_a=w_br_a.astype(BF16), w_br_b=w_br_b.astype(BF16), w_out=w_out.astype(BF16),
        g_ffn=norm_ffn_g.reshape(1, -1), wr=wr,
        w_ein=w_expert_in.astype(BF16), w_eout=w_expert_out.astype(BF16),
        g_final=norm_final_g.reshape(1, -1))


def _prompt_layer(x, mod, w, batch, seq):
    sh1, sc1, g1, sh2, sc2, g2 = mod
    cos, sin = _rope_tables(jnp.arange(seq, dtype=jnp.int32))
    qk_a, _, q_b, k_n, v_n, gate, mg, kt_a, vt_a, kt_n, vt_n = _inproj(
        x, sc1, sh1, w['g_mix'], cos, sin, w['w_in'], 256, transposed=True)
    o_a = _moba_p(qk_a, vt_a, batch, seq)
    kcmp = _compress_p(k_n[:, :NSA_KV_WIDTH], w['cmp_k'], batch, seq)
    vcmp = _compress_p(v_n[:, :NSA_KV_WIDTH], w['cmp_v'], batch, seq)
    o_b = _nsa_p(q_b, k_n, vt_n, kcmp, vcmp, gate, batch, seq)
    x1 = _merge(o_a, o_b, mg, x, g1, w['w_br_a'], w['w_br_b'], w['w_out'], 256)
    y = _moe(x1, sc2, sh2, g2, w['g_ffn'], w['g_final'], w['wr'], w['w_ein'], w['w_eout'], 512)
    return y, (kt_a, vt_a, kt_n, vt_n)


def _sample_layer(x, mod, w, caches, win_state, page_table, batch, ts):
    sh1, sc1, g1, sh2, sc2, g2 = mod
    moba_k, moba_v, cmp_k, cmp_v, slc_k, slc_v = caches
    win_k, win_v = win_state
    n_pages = page_table.shape[1]
    assert moba_k.shape[1] == PAGE and win_k.shape[1] == WINDOW and ts <= LANES
    assert n_pages * PAGE == LANES * SLC_BLOCK and n_pages * PAGE // MOBA_BLOCK <= LANES
    n = batch * ts
    pos = n_pages * PAGE + (jnp.arange(n, dtype=jnp.int32) % ts)
    cos, sin = _rope_tables(pos)
    qk_a, v_a, q_b, k_n, v_n, gate, mg = _inproj(x, sc1, sh1, w['g_mix'], cos, sin, w['w_in'], n)
    pt_flat = page_table.reshape(-1)
    kv = NSA_KV_WIDTH
    qf = _block_diag_q(qk_a[:, :MOBA_WIDTH], batch, ts, MOBA_HEADS)
    qbd = (qf * SCALE).astype(BF16)
    m, l, ks, o = _moba_s_pass(pt_flat, qbd, _page_view(moba_k), _page_view(moba_v), batch, n_pages)
    o_a = _moba_s_combine(m, l, ks, o, qf, qbd, _pad_rows(qk_a[:, MOBA_WIDTH:], batch, ts), _pad_rows(v_a, batch, ts),
                          n_pages, ts)
    o_a = o_a.reshape(batch, MOBA_HEADS, ts, D_HEAD).transpose(0, 2, 1, 3).reshape(n, MOBA_WIDTH)
    kcmp = _compress_s(pt_flat, _page_view(cmp_k), w['cmp_k'], batch, n_pages)
    vcmp = _compress_s(pt_flat, _page_view(cmp_v), w['cmp_v'], batch, n_pages)
    q3 = q_b.reshape(batch, ts, NSA_WIDTH)
    gate3 = gate.reshape(batch, ts, LANES)
    part, sel = _nsa_s(q3, kcmp, vcmp, jnp.transpose(win_k, (0, 2, 3, 1)), jnp.transpose(win_v, (0, 2, 3, 1)),
                       _pad_rows(k_n[:, 2 * kv:], batch, ts), _pad_rows(v_n[:, 2 * kv:], batch, ts), gate3, ts)
    o_b = _slc_s(pt_flat, q3, sel, _page_view(slc_k), _page_view(slc_v),
                 _pad_rows(k_n[:, kv:2 * kv], batch, ts), _pad_rows(v_n[:, kv:2 * kv], batch, ts), gate3, part,
                 n_pages, ts).reshape(n, NSA_WIDTH)
    x1 = _merge(o_a, o_b, mg, x, g1, w['w_br_a'], w['w_br_b'], w['w_out'], n)
    y = _moe(x1, sc2, sh2, g2, w['g_ffn'], w['g_final'], w['wr'], w['w_ein'], w['w_eout'], n)
    return y, (qk_a, v_a, k_n, v_n)


def kernel(x_prompt, x_sample, c_prompt, c_sample, cache_moba_k, cache_moba_v, cache_nsa_cmp_k, cache_nsa_cmp_v,
           cache_nsa_slc_k, cache_nsa_slc_v, state_nsa_win_k, state_nsa_win_v, page_table, w_ada, b_ada, norm_mix_g,
           w_in, pe_cmp_k, w_cmp_k1, w_cmp_k2, pe_cmp_v, w_cmp_v1, w_cmp_v2, w_br_a, w_br_b, w_out, norm_ffn_g,
           w_router_grp, w_router_exp, w_expert_in, w_expert_out, norm_final_g):
    bp, tp, _ = x_prompt.shape
    bs, ts, _ = x_sample.shape
    w = _prep_weights(w_ada[0], b_ada[0], norm_mix_g[0], w_in[0], pe_cmp_k[0], w_cmp_k1[0], w_cmp_k2[0], pe_cmp_v[0],
                      w_cmp_v1[0], w_cmp_v2[0], w_br_a[0], w_br_b[0], w_out[0], norm_ffn_g[0], w_router_grp[0],
                      w_router_exp[0], w_expert_in[0], w_expert_out[0], norm_final_g)
    mod = _ada(jnp.concatenate([c_prompt, c_sample], axis=0), w['w_ada'], w['b_ada'])
    mod_p = [m.reshape(bp, 1, D_MODEL) for m in jnp.split(mod[:bp], 6, axis=-1)]
    y_p, new_p = _prompt_layer(x_prompt.reshape(bp * tp, D_MODEL), mod_p, w, bp, tp)

    mod_s = [jnp.repeat(m, ts, axis=0).reshape(1, bs * ts, D_MODEL) for m in jnp.split(mod[bp:], 6, axis=-1)]
    caches = (cache_moba_k[0], cache_moba_v[0], cache_nsa_cmp_k[0], cache_nsa_cmp_v[0], cache_nsa_slc_k[0],
              cache_nsa_slc_v[0])
    y_s, new_s = _sample_layer(x_sample.reshape(bs * ts, D_MODEL), mod_s, w, caches,
                               (state_nsa_win_k[0], state_nsa_win_v[0]), page_table, bs, ts)

    kv = NSA_KV_WIDTH

    def new_rows(new, b, t):
        qk, v, k_n, v_n = new
        rows = lambda a, heads: a.reshape(1, b, t, heads, D_HEAD)
        return (rows(qk[:, MOBA_WIDTH:], MOBA_HEADS), rows(v, MOBA_HEADS),
                rows(k_n[:, :kv], NSA_KV_HEADS), rows(v_n[:, :kv], NSA_KV_HEADS),
                rows(k_n[:, kv:2 * kv], NSA_KV_HEADS), rows(v_n[:, kv:2 * kv], NSA_KV_HEADS),
                rows(k_n[:, 2 * kv:], NSA_KV_HEADS), rows(v_n[:, 2 * kv:], NSA_KV_HEADS))

    def new_rows_t(new, b, t):
        kt_a, vt_a, kt_n, vt_n = new
        rows = lambda a: jnp.transpose(a.reshape(1, b, a.shape[1] // D_HEAD, D_HEAD, a.shape[2]), (0, 1, 4, 2, 3))
        return (rows(kt_a), rows(vt_a), rows(kt_n[:, :kv]), rows(vt_n[:, :kv]),
                rows(kt_n[:, kv:2 * kv]), rows(vt_n[:, kv:2 * kv]),
                rows(kt_n[:, 2 * kv:, t - wb:]), rows(vt_n[:, 2 * kv:, t - wb:]))

    wb = state_nsa_win_k.shape[2]
    outs_p = new_rows_t(new_p, bp, tp)
    outs_s = new_rows(new_s, bs, ts)
    win_k = jnp.concatenate([state_nsa_win_k, outs_s[6]], axis=2)[:, :, ts:]
    win_v = jnp.concatenate([state_nsa_win_v, outs_s[7]], axis=2)[:, :, ts:]
    outs_s = outs_s[:6] + (win_k, win_v)
    return (y_p.reshape(bp, tp, D_MODEL), y_s.reshape(bs, ts, D_MODEL)) + outs_p + outs_s
```

```python
import functools

import jax
import jax.numpy as jnp
from jax import lax
from jax.experimental import pallas as pl
from jax.experimental.pallas import tpu as pltpu

D_MODEL = 1024
D_HEAD = 64
HALF = D_HEAD // 2
MOBA_HEADS = 8
MOBA_BLOCK = 256
MOBA_TOPK = 3
NSA_HEADS = 8
NSA_KV_HEADS = 2
NSA_GROUP = NSA_HEADS // NSA_KV_HEADS
CMP_LEN = 32
CMP_STRIDE = 16
CMP_HIDDEN = 128
SLC_BLOCK = 64
SLC_TOPN = 16
WINDOW = 512
N_GROUPS = 4
EXPERTS_PER_GROUP = 8
N_EXPERTS = N_GROUPS * EXPERTS_PER_GROUP
D_EXPERT = 256
ROPE_THETA = 10000.0
EPS = 1e-6
NEG_INF = -1e30
BIG = 1e30
TINY = 1e-30
MOBA_WIDTH = MOBA_HEADS * D_HEAD
NSA_WIDTH = NSA_HEADS * D_HEAD
NSA_KV_WIDTH = NSA_KV_HEADS * D_HEAD
SCALE = D_HEAD ** -0.5

LANES = 128
VMEM_LIMIT = 48 * 1024 * 1024

F32 = jnp.float32
BF16 = jnp.bfloat16
HIGHEST = lax.Precision.HIGHEST


def _params(*sem):
    return pltpu.CompilerParams(dimension_semantics=sem, vmem_limit_bytes=VMEM_LIMIT)


def _dot(a, b):
    return jnp.dot(a, b, preferred_element_type=F32)


def _dot_nt(a, b, precision=None):
    return lax.dot_general(a, b, (((1,), (1,)), ((), ())), precision=precision,
                           preferred_element_type=F32)


def _sigmoid(x):
    return 1.0 / (1.0 + jnp.exp(-x))


def _iota(shape, dim):
    return lax.broadcasted_iota(jnp.int32, shape, dim)


def _ada_kernel(c_ref, w_ref, b_ref, o_ref):
    c = c_ref[...]
    s = c * _sigmoid(c)
    o_ref[...] = _dot(s.astype(BF16), w_ref[...].astype(BF16)) + b_ref[...]


def _ada(c, w, b):
    n = c.shape[0]
    tn = 1024
    return pl.pallas_call(
        _ada_kernel,
        grid=(w.shape[1] // tn,),
        in_specs=[pl.BlockSpec((n, D_MODEL), lambda j: (0, 0)),
                  pl.BlockSpec((D_MODEL, tn), lambda j: (0, j)),
                  pl.BlockSpec((1, tn), lambda j: (0, j))],
        out_specs=pl.BlockSpec((n, tn), lambda j: (0, j)),
        out_shape=jax.ShapeDtypeStruct((n, w.shape[1]), F32),
        compiler_params=_params("arbitrary"),
        name="ada",
    )(c, w, b)


_IN_GROUPS = ((2 * MOBA_WIDTH, True),
              (MOBA_WIDTH, False),
              (NSA_WIDTH, True),
              (3 * NSA_KV_WIDTH, True),
              (3 * NSA_KV_WIDTH, False),
              (LANES, False),
              (2 * D_MODEL, False))
_IN_COLS_PAD = sum(w for w, _ in _IN_GROUPS)


def _norm_mod(x, g, sc, sh):
    ms = jnp.mean(x * x, axis=-1, keepdims=True)
    y = x * lax.rsqrt(ms + EPS) * g
    return y * (1.0 + sc) + sh


def _inproj_kernel(x_ref, sc_ref, sh_ref, g_ref, cos_ref, sin_ref, w_ref, *out_refs):
    h = _norm_mod(x_ref[...], g_ref[...], sc_ref[0], sh_ref[0]).astype(BF16)
    cos = cos_ref[...]
    sin = sin_ref[...]
    first_half = (_iota(cos.shape, 1) & (D_HEAD - 1)) < HALF

    def rope(y):
        rot = jnp.where(first_half, pltpu.roll(y, LANES - HALF, 1), pltpu.roll(y, HALF, 1))
        return y * cos + rot * sin

    n_groups = len(_IN_GROUPS)
    t_refs = dict(zip(_IN_T_GROUPS, out_refs[n_groups:]))
    col = 0
    for gi, (out_ref, (width, rotary)) in enumerate(zip(out_refs, _IN_GROUPS)):
        chunk = min(width, 512)
        for c in range(0, width, chunk):
            cw = min(chunk, width - c)
            y = _dot(h, w_ref[:, col + c:col + c + cw])
            for s in range(0, cw, LANES):
                piece = y[:, s:s + LANES]
                piece = rope(piece) if rotary else piece
                out_ref[:, c + s:c + s + LANES] = piece
                if gi in t_refs and c + s >= _IN_T_GROUPS[gi]:
                    t0 = c + s - _IN_T_GROUPS[gi]
                    t_refs[gi][0, t0:t0 + LANES, :] = piece.T
        col += width


_IN_T_GROUPS = {0: MOBA_WIDTH, 1: 0, 3: 0, 4: 0}


def _inproj(x, sc, sh, g, cos, sin, w, tm, transposed=False):
    n = x.shape[0]
    nb, r, _ = sc.shape
    tiles_per_b = (n // nb) // tm
    tab_tiles = cos.shape[0] // tm
    row = lambda i: (i, 0)
    mod = lambda i: (i // tiles_per_b, 0, 0)
    tab = lambda i: (i % tab_tiles, 0)
    out_specs = [pl.BlockSpec((tm, wd), row) for wd, _ in _IN_GROUPS]
    out_shape = [jax.ShapeDtypeStruct((n, wd), F32) for wd, _ in _IN_GROUPS]
    if transposed:
        for gi, first in _IN_T_GROUPS.items():
            cols = _IN_GROUPS[gi][0] - first
            out_specs.append(pl.BlockSpec((1, cols, tm), lambda i: (i // tiles_per_b, 0, i % tiles_per_b)))
            out_shape.append(jax.ShapeDtypeStruct((nb, cols, n // nb), F32))
    return pl.pallas_call(
        _inproj_kernel,
        grid=(n // tm,),
        in_specs=[pl.BlockSpec((tm, D_MODEL), row),
                  pl.BlockSpec((1, r, D_MODEL), mod),
                  pl.BlockSpec((1, r, D_MODEL), mod),
                  pl.BlockSpec((1, D_MODEL), lambda i: (0, 0)),
                  pl.BlockSpec((tm, LANES), tab),
                  pl.BlockSpec((tm, LANES), tab),
                  pl.BlockSpec((D_MODEL, _IN_COLS_PAD), lambda i: (0, 0))],
        out_specs=out_specs,
        out_shape=out_shape,
        compiler_params=_params("arbitrary"),
        name="inproj",
    )(x, sc, sh, g, cos, sin, w)


def _reorder_w_in(w_in):
    kv0 = 3 * MOBA_WIDTH + NSA_WIDTH
    kvs = [w_in[:, kv0 + i * NSA_KV_WIDTH:kv0 + (i + 1) * NSA_KV_WIDTH] for i in range(6)]
    g0 = kv0 + 6 * NSA_KV_WIDTH
    ng = 3 * NSA_HEADS
    gate = jnp.pad(w_in[:, g0:g0 + ng], ((0, 0), (0, LANES - ng)))
    parts = [w_in[:, :kv0], kvs[0], kvs[2], kvs[4], kvs[1], kvs[3], kvs[5], gate, w_in[:, g0 + ng:]]
    return jnp.concatenate(parts, axis=1).astype(BF16)


def _rope_tables(pos):
    inv = ROPE_THETA ** (-jnp.arange(HALF, dtype=F32) / HALF)
    ang = pos.astype(F32)[:, None] * inv[None, :]
    cos = jnp.cos(ang)
    sin = jnp.sin(ang)
    cos = jnp.concatenate([cos, cos, cos, cos], axis=1)
    sin = jnp.concatenate([-sin, sin, -sin, sin], axis=1)
    return cos, sin


def _rank_select(score, n_cols, n_keep):
    lane = _iota(score.shape, 1)
    rank = jnp.zeros(score.shape, jnp.int32)
    for jp in range(n_cols):
        col = score[:, jp:jp + 1]
        beats = (col > score) | ((col == score) & (lane > jp))
        rank = rank + beats.astype(jnp.int32)
    return rank < n_keep


def _rank_select_t(score, n_rows, n_keep):
    row = _iota(score.shape, 0)
    rank = jnp.zeros(score.shape, jnp.int32)
    for jp in range(n_rows):
        r = score[jp:jp + 1, :]
        beats = (r > score) | ((r == score) & (row > jp))
        rank = rank + beats.astype(jnp.int32)
    return rank < n_keep


SUBLANES = 8


def _fold_rows(x, op):
    return op(x.reshape(x.shape[0] // SUBLANES, SUBLANES, x.shape[1]), axis=0)


def _attend_t(make_scores, vt_fn, own_c, lo, hi, width, n_chains):
    score_fn = make_scores(None)

    def max_step(c, own, m):
        return tuple(jnp.maximum(mi, _fold_rows(s, jnp.max)) for mi, s in zip(m, score_fn(c, own)))

    m = max_step(own_c, True, tuple(jnp.full((SUBLANES, width), NEG_INF, F32) for _ in range(n_chains)))
    m = lax.fori_loop(lo, hi, lambda c, mm: max_step(c, False, mm), m)
    shifted_fn = make_scores([jnp.max(mi, axis=0, keepdims=True) for mi in m])

    def acc_step(c, own, carry):
        out = []
        for (l, acc), s, vt in zip(carry, shifted_fn(c, own), vt_fn(c)):
            p = jnp.exp(s)
            out.append((l + _fold_rows(p, jnp.sum), acc + _dot(vt, p.astype(BF16))))
        return tuple(out)

    zero = tuple((jnp.zeros((SUBLANES, width), F32), jnp.zeros((D_HEAD, width), F32)) for _ in range(n_chains))
    carry = acc_step(own_c, True, zero)
    carry = lax.fori_loop(lo, hi, lambda c, cr: acc_step(c, False, cr), carry)
    return [acc / jnp.maximum(jnp.sum(l, axis=0, keepdims=True), TINY) for l, acc in carry]


AUG_ONES = 2
AUG_BLK = SUBLANES


def _key_aug(n_keys, block, n_blocks):
    lane = _iota((n_keys, D_HEAD), 1)
    blk = _iota((n_keys, D_HEAD), 0) // block
    hot = (lane < AUG_ONES) | ((lane >= AUG_BLK) & (lane < AUG_BLK + n_blocks) & (lane - AUG_BLK == blk))
    return jnp.where(hot, 1.0, 0.0).astype(BF16)


def _query_aug(qt, shift, bias):
    width = qt.shape[1]
    top = jnp.zeros((SUBLANES, width), F32)
    if shift is not None:
        hi = shift.astype(BF16).astype(F32)
        row = _iota((SUBLANES, width), 0)
        top = jnp.where(row == 0, -hi, jnp.where(row == 1, hi - shift, 0.0))
    parts = [qt, top]
    used = SUBLANES
    if bias is not None:
        parts.append(bias)
        used += bias.shape[0]
    parts.append(jnp.zeros((D_HEAD - used, width), F32))
    return jnp.concatenate(parts, axis=0).astype(BF16)


MOBA_STEP_WIDTH = 512


def _moba_p_kernel(q_ref, k_ref, v_ref, o_ref, ka_scr, vt_scr, km_scr):
    qi = pl.program_id(2)
    tq = tk = MOBA_BLOCK
    seq = k_ref.shape[1]
    nblk = seq // MOBA_BLOCK
    width = q_ref.shape[2]
    nh = width // D_HEAD

    @pl.when(qi == 0)
    def _():
        kf = k_ref[0]
        km_scr[...] = jnp.mean(kf.reshape(nblk, MOBA_BLOCK, width), axis=1)
        aug = _key_aug(seq, MOBA_BLOCK, nblk)
        for hh in range(nh):
            ka_scr[hh] = jnp.concatenate([kf[:, hh * D_HEAD:(hh + 1) * D_HEAD].astype(BF16), aug], axis=1)
        for j in range(nblk):
            vt_scr[j] = v_ref[0, :, j * tk:(j + 1) * tk].astype(BF16)

    q2t = q_ref[0].T
    km = km_scr[...]
    klane = _iota(km.shape, 1)
    blk = _iota((nblk, tq), 0)
    causal_t = _iota((tk, tq), 0) <= _iota((tk, tq), 1)
    qts, biases = [], []
    for hh in range(nh):
        kmh = jnp.where((klane >= hh * D_HEAD) & (klane < (hh + 1) * D_HEAD), km, 0.0)
        gate = jnp.dot(kmh, q2t, precision=HIGHEST, preferred_element_type=F32)
        gate = jnp.where(blk < qi, gate, NEG_INF)
        keep = _rank_select_t(gate, nblk, MOBA_TOPK) & (blk < qi)
        biases.append(jnp.where(keep, 0.0, NEG_INF))
        qts.append(q2t[hh * D_HEAD:(hh + 1) * D_HEAD, :] * SCALE)

    def make_scores(shift):
        sh = [None] * nh if shift is None else shift
        q_own = [_query_aug(qts[hh], sh[hh], None) for hh in range(nh)]
        q_past = [_query_aug(qts[hh], sh[hh], biases[hh]) for hh in range(nh)]

        def score_fn(c, own):
            out = []
            for hh in range(nh):
                kj = ka_scr[hh, pl.ds(pl.multiple_of(c * tk, tk), tk), :]
                s = _dot(kj, q_own[hh] if own else q_past[hh])
                out.append(jnp.where(causal_t, s, NEG_INF) if own else s)
            return out

        return score_fn

    def values(c):
        return [vt_scr[c, hh * D_HEAD:(hh + 1) * D_HEAD, :] for hh in range(nh)]

    o_ref[0] = jnp.concatenate(_attend_t(make_scores, values, qi, 0, qi, tq, nh), axis=0).T


def _moba_p(qk, vt, batch, seq):
    qk3 = qk.reshape(batch, seq, 2 * MOBA_WIDTH)
    sw = MOBA_STEP_WIDTH
    pairs = MOBA_WIDTH // sw
    nblk = seq // MOBA_BLOCK
    out = pl.pallas_call(
        _moba_p_kernel,
        grid=(batch, pairs, nblk),
        in_specs=[pl.BlockSpec((1, MOBA_BLOCK, sw), lambda b, h, i: (b, i, h)),
                  pl.BlockSpec((1, seq, sw), lambda b, h, i: (b, 0, pairs + h)),
                  pl.BlockSpec((1, sw, seq), lambda b, h, i: (b, h, 0))],
        out_specs=pl.BlockSpec((1, MOBA_BLOCK, sw), lambda b, h, i: (b, i, h)),
        out_shape=jax.ShapeDtypeStruct((batch, seq, MOBA_WIDTH), F32),
        scratch_shapes=[pltpu.VMEM((sw // D_HEAD, seq, 2 * D_HEAD), BF16), pltpu.VMEM((nblk, sw, MOBA_BLOCK), BF16),
                        pltpu.VMEM((nblk, sw), F32)],
        compiler_params=_params("arbitrary", "arbitrary", "arbitrary"),
        name="moba_prompt",
    )(qk3, qk3, vt)
    return out.reshape(batch * seq, MOBA_WIDTH)


def _gelu(x):
    return 0.5 * x * (1.0 + jnp.tanh(0.7978845608028654 * (x + 0.044715 * x * x * x)))


def _compress_p_kernel(seg_ref, pea_ref, peb_ref, wa_ref, wb_ref, w2_ref, o_ref):
    seg = seg_ref[0]
    a = _dot((seg + pea_ref[...]).astype(BF16), wa_ref[...])
    b = _dot((seg + peb_ref[...]).astype(BF16), wb_ref[...])
    nseg = seg.shape[0]
    hid = a + pltpu.roll(b, nseg - 1, 0)
    o_ref[0] = _dot(_gelu(hid).astype(BF16), w2_ref[...])


def _compress_weights(pe, w1, w2):
    g = NSA_KV_HEADS
    eye = jnp.eye(g, dtype=F32)
    w1r = w1.reshape(CMP_LEN, D_HEAD, CMP_HIDDEN)

    def half(lo):
        w = w1r[lo:lo + CMP_STRIDE]
        wbd = jnp.einsum('ldf,gh->lgdhf', w, eye)
        p = jnp.broadcast_to(pe[lo:lo + CMP_STRIDE, None, :], (CMP_STRIDE, g, D_HEAD))
        return wbd.reshape(CMP_STRIDE, g * D_HEAD, g * CMP_HIDDEN).astype(BF16), p.reshape(CMP_STRIDE, g * D_HEAD)

    wa, pea = half(0)
    wb, peb = half(CMP_STRIDE)
    w2bd = jnp.einsum('fd,gh->gfhd', w2, eye).reshape(g * CMP_HIDDEN, g * D_HEAD).astype(BF16)
    return wa, wb, pea, peb, w2bd


def _compress_p(rows, cw, batch, seq):
    wa, wb, pea, peb, w2bd = cw
    nseg = seq // CMP_STRIDE
    width = CMP_STRIDE * NSA_KV_WIDTH
    seg = rows.reshape(batch, nseg, width)
    full = lambda b: (0, 0)
    return pl.pallas_call(
        _compress_p_kernel,
        grid=(batch,),
        in_specs=[pl.BlockSpec((1, nseg, width), lambda b: (b, 0, 0)),
                  pl.BlockSpec((1, width), full), pl.BlockSpec((1, width), full),
                  pl.BlockSpec((width, NSA_KV_HEADS * CMP_HIDDEN), full),
                  pl.BlockSpec((width, NSA_KV_HEADS * CMP_HIDDEN), full),
                  pl.BlockSpec((NSA_KV_HEADS * CMP_HIDDEN, NSA_KV_WIDTH), full)],
        out_specs=pl.BlockSpec((1, nseg, NSA_KV_WIDTH), lambda b: (b, 0, 0)),
        out_shape=jax.ShapeDtypeStruct((batch, nseg, NSA_KV_WIDTH), F32),
        compiler_params=_params("arbitrary"),
        name="compress_prompt",
    )(seg, pea.reshape(1, width), peb.reshape(1, width), wa.reshape(width, -1), wb.reshape(width, -1), w2bd)


NSA_TQ = 256
NSA_TK = 256


def _overlap_matrix(nc_pad, nslc_pad):
    cs = jnp.arange(nc_pad)[:, None] * CMP_STRIDE
    ss = jnp.arange(nslc_pad)[None, :] * SLC_BLOCK
    return ((cs < ss + SLC_BLOCK) & (cs + CMP_LEN > ss)).astype(F32)


def _nsa_p_kernel(q_ref, kc_ref, vc_ref, ks_ref, vs_ref, kw_ref, vw_ref, gate_ref, ovt_ref, o_ref,
                  ksa_scr, kwa_scr, vst_scr, vwt_scr):
    tq, tk = NSA_TQ, NSA_TK
    t = pl.program_id(1)
    q0 = t * tq
    seq = ks_ref.shape[1]
    nc_pad = kc_ref.shape[1]
    nslc = seq // SLC_BLOCK
    width = NSA_GROUP * tq

    @pl.when(t == 0)
    def _():
        aug_s = _key_aug(seq, SLC_BLOCK, nslc)
        aug_w = _key_aug(seq, SLC_BLOCK, 0)
        for g in range(NSA_KV_HEADS):
            rows = slice(g * D_HEAD, (g + 1) * D_HEAD)
            ksa_scr[g] = jnp.concatenate([ks_ref[0, :, rows].astype(BF16), aug_s], axis=1)
            kwa_scr[g] = jnp.concatenate([kw_ref[0, :, rows].astype(BF16), aug_w], axis=1)
        for j in range(seq // tk):
            vst_scr[j] = vs_ref[0, :, j * tk:(j + 1) * tk].astype(BF16)
            vwt_scr[j] = vw_ref[0, :, j * tk:(j + 1) * tk].astype(BF16)

    qt_all = q_ref[0].T
    gates_t = _sigmoid(gate_ref[0]).T
    kct = kc_ref[0].astype(BF16)
    vct = vc_ref[0].T.astype(BF16)
    pos = q0 + _iota((1, tq), 1)
    n_i = _iota((nc_pad, tq), 0)
    valid_c = (n_i < nc_pad - 1) & (n_i * CMP_STRIDE + (CMP_LEN - 1) <= pos)
    valid_c4 = jnp.concatenate([valid_c] * NSA_GROUP, axis=1)
    q_blk = pos >> 6
    jrow = _iota((nslc, tq), 0)
    krow = _iota((tk, tq), 0)
    cd = q0 // tk
    c_win = jnp.maximum(q0 - WINDOW, 0) // tk

    def tile(b):
        return jnp.concatenate([b] * NSA_GROUP, axis=1)

    def win_bias(c):
        dist = pos - (c * tk + krow)
        return jnp.where((dist >= 0) & (dist <= WINDOW), 0.0, NEG_INF)

    qts, o_cs, biases = [], [], []
    for g in range(NSA_KV_HEADS):
        rows = slice(g * D_HEAD, (g + 1) * D_HEAD)
        qt = jnp.concatenate(
            [qt_all[(g * NSA_GROUP + r) * D_HEAD:(g * NSA_GROUP + r + 1) * D_HEAD, :] * SCALE
             for r in range(NSA_GROUP)], axis=1)
        qts.append(qt)
        s = jnp.where(valid_c4, _dot(kct[:, rows], qt.astype(BF16)), NEG_INF)
        m = jnp.max(s, axis=0, keepdims=True)
        p = jnp.where(valid_c4, jnp.exp(s - m), 0.0)
        p = p / jnp.maximum(jnp.sum(p, axis=0, keepdims=True), TINY)
        o_cs.append(_dot(vct[rows, :], p.astype(BF16)))
        psum = p[:, :tq]
        for r in range(1, NSA_GROUP):
            psum = psum + p[:, r * tq:(r + 1) * tq]
        imp = jnp.dot(ovt_ref[...], psum, precision=HIGHEST, preferred_element_type=F32)
        imp = jnp.where(jrow == q_blk, BIG, jnp.where(jrow < q_blk, imp, NEG_INF))
        keep = _rank_select_t(imp, nslc, SLC_TOPN) & (jrow <= q_blk)
        biases.append(tile(jnp.where(keep, 0.0, NEG_INF)))

    causal_own = tile(cd * tk + krow <= pos)

    def make_slc_scores(shift):
        sh = [None] * NSA_KV_HEADS if shift is None else shift
        qa = [_query_aug(qts[g], sh[g], biases[g]) for g in range(NSA_KV_HEADS)]

        def score_fn(c, own):
            out = []
            for g in range(NSA_KV_HEADS):
                s = _dot(ksa_scr[g, pl.ds(pl.multiple_of(c * tk, tk), tk), :], qa[g])
                out.append(jnp.where(causal_own, s, NEG_INF) if own else s)
            return out

        return score_fn

    def make_win_scores(shift):
        sh = [None] * NSA_KV_HEADS if shift is None else shift
        qa = [_query_aug(qts[g], sh[g], None) for g in range(NSA_KV_HEADS)]

        def score_fn(c, own):
            wb = tile(win_bias(c))
            return [_dot(kwa_scr[g, pl.ds(pl.multiple_of(c * tk, tk), tk), :], qa[g]) + wb
                    for g in range(NSA_KV_HEADS)]

        return score_fn

    def values(vt_scr):
        return lambda c: [vt_scr[c, g * D_HEAD:(g + 1) * D_HEAD, :] for g in range(NSA_KV_HEADS)]

    o_ss = _attend_t(make_slc_scores, values(vst_scr), cd, 0, cd, width, NSA_KV_HEADS)
    o_ws = _attend_t(make_win_scores, values(vwt_scr), cd, c_win, cd, width, NSA_KV_HEADS)
    outs = []
    for g in range(NSA_KV_HEADS):
        o_c, o_s, o_w = o_cs[g], o_ss[g], o_ws[g]
        for r in range(NSA_GROUP):
            hd = g * NSA_GROUP + r
            cs = slice(r * tq, (r + 1) * tq)
            outs.append(gates_t[3 * hd:3 * hd + 1, :] * o_c[:, cs] + gates_t[3 * hd + 1:3 * hd + 2, :] * o_s[:, cs]
                        + gates_t[3 * hd + 2:3 * hd + 3, :] * o_w[:, cs])
    o_ref[0] = jnp.concatenate(outs, axis=0).T


def _nsa_p(q, kn, vnt, kcmp, vcmp, gate, batch, seq):
    q3 = q.reshape(batch, seq, NSA_WIDTH)
    kn3 = kn.reshape(batch, seq, 3 * NSA_KV_WIDTH)
    g3 = gate.reshape(batch, seq, LANES)
    nc_pad = kcmp.shape[1]
    nslc = seq // SLC_BLOCK
    nchunk = seq // NSA_TK
    ovt = _overlap_matrix(nc_pad, nslc).T
    tile = lambda b, t: (b, t, 0)
    cmp_spec = pl.BlockSpec((1, nc_pad, NSA_KV_WIDTH), lambda b, t: (b, 0, 0))
    out = pl.pallas_call(
        _nsa_p_kernel,
        grid=(batch, seq // NSA_TQ),
        in_specs=[pl.BlockSpec((1, NSA_TQ, NSA_WIDTH), tile), cmp_spec, cmp_spec,
                  pl.BlockSpec((1, seq, LANES), lambda b, t: (b, 0, 1)),
                  pl.BlockSpec((1, LANES, seq), lambda b, t: (b, 1, 0)),
                  pl.BlockSpec((1, seq, LANES), lambda b, t: (b, 0, 2)),
                  pl.BlockSpec((1, LANES, seq), lambda b, t: (b, 2, 0)),
                  pl.BlockSpec((1, NSA_TQ, LANES), tile),
                  pl.BlockSpec((nslc, nc_pad), lambda b, t: (0, 0))],
        out_specs=pl.BlockSpec((1, NSA_TQ, NSA_WIDTH), tile),
        out_shape=jax.ShapeDtypeStruct((batch, seq, NSA_WIDTH), F32),
        scratch_shapes=[pltpu.VMEM((NSA_KV_HEADS, seq, 2 * D_HEAD), BF16),
                        pltpu.VMEM((NSA_KV_HEADS, seq, 2 * D_HEAD), BF16),
                        pltpu.VMEM((nchunk, LANES, NSA_TK), BF16), pltpu.VMEM((nchunk, LANES, NSA_TK), BF16)],
        compiler_params=_params("arbitrary", "arbitrary"),
        name="nsa_prompt",
    )(q3, kcmp, vcmp, kn3, vnt, kn3, vnt, g3, ovt)
    return out.reshape(batch * seq, NSA_WIDTH)


def _merge_kernel(oa_ref, ob_ref, mg_ref, x_ref, g1_ref, wa_ref, wb_ref, wo_ref, o_ref):
    a = _dot(oa_ref[...].astype(BF16), wa_ref[...])
    b = _dot(ob_ref[...].astype(BF16), wb_ref[...])
    mix = _sigmoid(mg_ref[:, :D_MODEL]) * a + _sigmoid(mg_ref[:, D_MODEL:]) * b
    o_ref[...] = x_ref[...] + g1_ref[0] * _dot(mix.astype(BF16), wo_ref[...])


def _merge(oa, ob, mg, x, g1, wa, wb, wo, tm):
    n = x.shape[0]
    nb, r, _ = g1.shape
    tiles_per_b = (n // nb) // tm
    row = lambda i: (i, 0)
    full = lambda i: (0, 0)
    return pl.pallas_call(
        _merge_kernel,
        grid=(n // tm,),
        in_specs=[pl.BlockSpec((tm, MOBA_WIDTH), row), pl.BlockSpec((tm, NSA_WIDTH), row),
                  pl.BlockSpec((tm, 2 * D_MODEL), row), pl.BlockSpec((tm, D_MODEL), row),
                  pl.BlockSpec((1, r, D_MODEL), lambda i: (i // tiles_per_b, 0, 0)),
                  pl.BlockSpec((MOBA_WIDTH, D_MODEL), full), pl.BlockSpec((NSA_WIDTH, D_MODEL), full),
                  pl.BlockSpec((D_MODEL, D_MODEL), full)],
        out_specs=pl.BlockSpec((tm, D_MODEL), row),
        out_shape=jax.ShapeDtypeStruct((n, D_MODEL), F32),
        compiler_params=_params("arbitrary"),
        name="merge",
    )(oa, ob, mg, x, g1, wa, wb, wo)


def _route(logits):
    lane = _iota(logits.shape, 1)
    is_grp = lane < N_GROUPS
    lg = jnp.where(is_grp, logits, NEG_INF)
    mg = jnp.max(lg, axis=-1, keepdims=True)
    pg = jnp.where(is_grp, jnp.exp(lg - mg), 0.0)
    pg = pg / jnp.sum(pg, axis=-1, keepdims=True)
    g_w = jnp.max(pg, axis=-1, keepdims=True)
    g_sel = jnp.min(jnp.where(is_grp & (pg == g_w), lane, LANES), axis=-1, keepdims=True)
    e_lane = lane - N_GROUPS
    in_grp = (e_lane >= 0) & (e_lane < N_EXPERTS) & ((e_lane >> 3) == g_sel)
    le = jnp.where(in_grp, logits, NEG_INF)
    me = jnp.max(le, axis=-1, keepdims=True)
    pe = jnp.where(in_grp, jnp.exp(le - me), 0.0)
    pe = pe / jnp.sum(pe, axis=-1, keepdims=True)
    v1 = jnp.max(pe, axis=-1, keepdims=True)
    i1 = jnp.min(jnp.where(in_grp & (pe == v1), lane, LANES), axis=-1, keepdims=True)
    rest = in_grp & (lane != i1)
    pr = jnp.where(rest, pe, -1.0)
    v2 = jnp.max(pr, axis=-1, keepdims=True)
    i2 = jnp.min(jnp.where(rest & (pr == v2), lane, LANES), axis=-1, keepdims=True)
    tot = v1 + v2
    comb = jnp.where(lane == i1, v1 / tot, 0.0) + jnp.where(lane == i2, v2 / tot, 0.0)
    comb = comb * g_w
    return pltpu.roll(comb, LANES - N_GROUPS, 1)


MOE_EXPERTS_PER_STEP = 8


def _moe_kernel(x_ref, sc_ref, sh_ref, g2_ref, gn_ref, gf_ref, wr_ref, win_ref, wout_ref, o_ref,
                h_scr, comb_scr, acc_scr):
    e = pl.program_id(1)

    @pl.when(e == 0)
    def _():
        h = _norm_mod(x_ref[...], gn_ref[...], sc_ref[0], sh_ref[0])
        hb = h.astype(BF16)
        h_scr[...] = hb
        h_lo = (h - hb.astype(F32)).astype(BF16)
        logits = _dot(hb, wr_ref[0]) + _dot(hb, wr_ref[1]) + _dot(h_lo, wr_ref[0])
        comb_scr[...] = _route(logits)
        acc_scr[...] = jnp.zeros_like(acc_scr)

    per = win_ref.shape[0]
    hb = h_scr[...]
    comb = comb_scr[...]
    lane = _iota(comb.shape, 1)
    acts = []
    for j in range(per):
        hid = _dot(hb, win_ref[j])
        a = hid[:, :D_EXPERT]
        b = hid[:, D_EXPERT:]
        w = jnp.sum(jnp.where(lane == e * per + j, comb, 0.0), axis=1, keepdims=True)
        acts.append((a * _sigmoid(a) * b * w).astype(BF16))
    act = jnp.concatenate(acts, axis=1)
    acc_scr[...] += _dot(act, wout_ref[...].reshape(per * D_EXPERT, D_MODEL))

    @pl.when(e == pl.num_programs(1) - 1)
    def _():
        y = x_ref[...] + g2_ref[0] * acc_scr[...]
        ms = jnp.mean(y * y, axis=-1, keepdims=True)
        o_ref[...] = y * lax.rsqrt(ms + EPS) * gf_ref[...]


def _moe(x, sc, sh, g2, gn, gf, wr, w_ein, w_eout, tm):
    n = x.shape[0]
    nb, r, _ = sc.shape
    tiles_per_b = (n // nb) // tm
    row = lambda i, e: (i, 0)
    mod = lambda i, e: (i // tiles_per_b, 0, 0)
    full = lambda i, e: (0, 0)
    per = MOE_EXPERTS_PER_STEP
    return pl.pallas_call(
        _moe_kernel,
        grid=(n // tm, N_EXPERTS // per),
        in_specs=[pl.BlockSpec((tm, D_MODEL), row),
                  pl.BlockSpec((1, r, D_MODEL), mod), pl.BlockSpec((1, r, D_MODEL), mod),
                  pl.BlockSpec((1, r, D_MODEL), mod),
                  pl.BlockSpec((1, D_MODEL), full), pl.BlockSpec((1, D_MODEL), full),
                  pl.BlockSpec((2, D_MODEL, LANES), lambda i, e: (0, 0, 0)),
                  pl.BlockSpec((per, D_MODEL, 2 * D_EXPERT), lambda i, e: (e, 0, 0)),
                  pl.BlockSpec((per, D_EXPERT, D_MODEL), lambda i, e: (e, 0, 0))],
        out_specs=pl.BlockSpec((tm, D_MODEL), row),
        out_shape=jax.ShapeDtypeStruct((n, D_MODEL), F32),
        scratch_shapes=[pltpu.VMEM((tm, D_MODEL), BF16), pltpu.VMEM((tm, LANES), F32),
                        pltpu.VMEM((tm, D_MODEL), F32)],
        compiler_params=_params("arbitrary", "arbitrary"),
        name="moe",
    )(x, sc, sh, g2, gn, gf, wr, w_ein, w_eout)


PAGE = 128
MOBA_PAGES_PER_STEP = 16
NSA_PAGES_PER_STEP = 16
COMPRESS_PAGES_PER_STEP = 32


def _page_view(cache):
    n_phys, page, heads, dh = cache.shape
    return jnp.transpose(cache, (0, 2, 3, 1)).reshape(n_phys, heads * dh, page)


def _page_ring_step(pt_ref, streams, per):
    ns = pl.num_programs(1)
    step = pl.program_id(0) * ns + pl.program_id(1)
    last = pl.num_programs(0) * ns - 1
    slot = step % 2

    def copies(step_idx, slot_idx):
        return [pltpu.make_async_copy(hbm.at[pt_ref[step_idx * per + u]], buf.at[slot_idx, u], sem.at[slot_idx])
                for hbm, buf, sem in streams for u in range(per)]

    @pl.when(step == 0)
    def _():
        for c in copies(0, 0):
            c.start()

    for c in copies(step, slot):
        c.wait()
    nxt = jnp.minimum(step + 1, last)
    for c in copies(nxt, 1 - slot):
        c.start()

    def finish():
        @pl.when(step == last)
        def _():
            for c in copies(nxt, 1 - slot):
                c.wait()

    return slot, finish


def _head_diag(full, heads):
    rows = full.shape[0]
    head = _iota((rows, D_HEAD), 0) // (rows // heads)
    out = jnp.zeros((rows, D_HEAD), F32)
    for h in range(heads):
        out = out + jnp.where(head == h, full[:, h * D_HEAD:(h + 1) * D_HEAD], 0.0)
    return out


def _moba_s_kernel(pt_ref, qbd_ref, k_hbm, v_hbm, m_ref, l_ref, ks_ref, o_ref, kbuf, vbuf, ksem, vsem):
    per = MOBA_PAGES_PER_STEP
    s = pl.program_id(1)
    slot, finish = _page_ring_step(pt_ref, [(k_hbm, kbuf, ksem), (v_hbm, vbuf, vsem)], per)

    @pl.when(s == 0)
    def _():
        m_ref[...] = jnp.zeros_like(m_ref)
        l_ref[...] = jnp.zeros_like(l_ref)
        ks_ref[...] = jnp.zeros_like(ks_ref)

    qbd = qbd_ref[0]
    lane_q = _iota(m_ref.shape[1:], 1)
    lane_k = _iota(ks_ref.shape[1:], 1)
    ppb = MOBA_BLOCK // PAGE
    m_all, l_all, ks_all = m_ref[0], l_ref[0], ks_ref[0]
    kts = [kbuf[slot, u] for u in range(per)]
    sc_all = _dot(qbd, jnp.concatenate([kt.astype(BF16) for kt in kts], axis=1))
    for j in range(per // ppb):
        blk = s * (per // ppb) + j
        sc = sc_all[:, j * MOBA_BLOCK:(j + 1) * MOBA_BLOCK]
        m = jnp.max(sc, axis=-1, keepdims=True)
        p = jnp.exp(sc - m)
        l = jnp.sum(p, axis=-1, keepdims=True)
        vt = jnp.concatenate([vbuf[slot, j * ppb + t].astype(BF16) for t in range(ppb)], axis=1)
        o_ref[0, j] = _head_diag(_dot_nt(p.astype(BF16), vt), MOBA_HEADS)
        kb = kts[j * ppb]
        for t in range(1, ppb):
            kb = kb + kts[j * ppb + t]
        ksum = jnp.sum(kb, axis=-1, keepdims=True)
        m_all = jnp.where(lane_q == blk, m, m_all)
        l_all = jnp.where(lane_q == blk, l, l_all)
        ks_all = jnp.where(lane_k == blk, ksum, ks_all)
    m_ref[0] = m_all
    l_ref[0] = l_all
    ks_ref[0] = ks_all
    finish()


def _moba_s_pass(pt_flat, qbd, kt_pages, vt_pages, batch, n_pages):
    per = MOBA_PAGES_PER_STEP
    rows = qbd.shape[1]
    ppb = MOBA_BLOCK // PAGE
    stat = lambda b, s, pt: (b, 0, 0)
    return pl.pallas_call(
        _moba_s_kernel,
        grid_spec=pltpu.PrefetchScalarGridSpec(
            num_scalar_prefetch=1,
            grid=(batch, n_pages // per),
            in_specs=[pl.BlockSpec((1, rows, MOBA_WIDTH), stat),
                      pl.BlockSpec(memory_space=pl.ANY), pl.BlockSpec(memory_space=pl.ANY)],
            out_specs=[pl.BlockSpec((1, rows, LANES), stat), pl.BlockSpec((1, rows, LANES), stat),
                       pl.BlockSpec((1, MOBA_WIDTH, LANES), stat),
                       pl.BlockSpec((1, per // ppb, rows, D_HEAD), lambda b, s, pt: (b, s, 0, 0))],
            scratch_shapes=[pltpu.VMEM((2, per, MOBA_WIDTH, PAGE), F32), pltpu.VMEM((2, per, MOBA_WIDTH, PAGE), F32),
                            pltpu.SemaphoreType.DMA((2,)), pltpu.SemaphoreType.DMA((2,))]),
        out_shape=[jax.ShapeDtypeStruct((batch, rows, LANES), F32), jax.ShapeDtypeStruct((batch, rows, LANES), F32),
                   jax.ShapeDtypeStruct((batch, MOBA_WIDTH, LANES), F32),
                   jax.ShapeDtypeStruct((batch, n_pages // ppb, rows, D_HEAD), F32)],
        compiler_params=_params("arbitrary", "arbitrary"),
        name="moba_decode_pages",
    )(pt_flat, qbd, kt_pages, vt_pages)


def _moba_s_combine_kernel(m_ref, l_ref, ks_ref, o_ref, qf_ref, qbd_ref, kn_ref, vn_ref, out_ref, *, n_pages, ts):
    rows = m_ref.shape[1]
    nblk = n_pages // (MOBA_BLOCK // PAGE)
    lane = _iota((rows, LANES), 1)
    kmean = ks_ref[0] * (1.0 / MOBA_BLOCK)
    gate = jnp.dot(qf_ref[0], kmean, precision=HIGHEST, preferred_element_type=F32)
    gate = jnp.where(lane < nblk, gate, NEG_INF)
    selp = _rank_select(gate, nblk, MOBA_TOPK) & (lane < nblk)
    qbd = qbd_ref[0]
    s_own = _dot_nt(qbd, kn_ref[0].astype(BF16))
    valid_own = lane <= (_iota((rows, LANES), 0) % ts)
    s_own = jnp.where(valid_own, s_own, NEG_INF)
    m_all = jnp.where(selp, m_ref[0], NEG_INF)
    big_m = jnp.maximum(jnp.max(m_all, axis=-1, keepdims=True), jnp.max(s_own, axis=-1, keepdims=True))
    wgt = jnp.where(selp, jnp.exp(m_ref[0] - big_m), 0.0)
    p_own = jnp.where(valid_own, jnp.exp(s_own - big_m), 0.0)
    denom = jnp.sum(wgt * l_ref[0], axis=-1, keepdims=True) + jnp.sum(p_own, axis=-1, keepdims=True)
    num = _head_diag(_dot(p_own.astype(BF16), vn_ref[0].astype(BF16)), MOBA_HEADS)
    for j in range(nblk):
        num = num + wgt[:, j:j + 1] * o_ref[0, j]
    out_ref[0] = num / jnp.maximum(denom, TINY)


def _moba_s_combine(m, l, ks, o, qf, qbd, kn, vn, n_pages, ts):
    batch, rows, _ = m.shape
    b3 = lambda b: (b, 0, 0)
    return pl.pallas_call(
        functools.partial(_moba_s_combine_kernel, n_pages=n_pages, ts=ts),
        grid=(batch,),
        in_specs=[pl.BlockSpec((1, rows, LANES), b3), pl.BlockSpec((1, rows, LANES), b3),
                  pl.BlockSpec((1, MOBA_WIDTH, LANES), b3),
                  pl.BlockSpec((1, o.shape[1], rows, D_HEAD), lambda b: (b, 0, 0, 0)),
                  pl.BlockSpec((1, rows, MOBA_WIDTH), b3), pl.BlockSpec((1, rows, MOBA_WIDTH), b3),
                  pl.BlockSpec((1, LANES, MOBA_WIDTH), b3), pl.BlockSpec((1, LANES, MOBA_WIDTH), b3)],
        out_specs=pl.BlockSpec((1, rows, D_HEAD), b3),
        out_shape=jax.ShapeDtypeStruct((batch, rows, D_HEAD), F32),
        compiler_params=_params("arbitrary"),
        name="moba_decode_combine",
    )(m, l, ks, o, qf, qbd, kn, vn)


def _block_diag_q(q, batch, ts, heads):
    q4 = q.reshape(batch, ts, heads, D_HEAD)
    eye = jnp.eye(heads, dtype=q.dtype)
    return jnp.einsum('bchd,hk->bhckd', q4, eye).reshape(batch, heads * ts, heads * D_HEAD)


def _pad_rows(a, batch, ts):
    a3 = a.reshape(batch, ts, a.shape[-1])
    return jnp.pad(a3, ((0, 0), (0, LANES - ts), (0, 0)))


def _compress_s_kernel(pt_ref, pages_hbm, pea_ref, peb_ref, wa_ref, wb_ref, w2_ref, o_ref,
                       buf, sem, x_scr, a_scr, b_scr):
    per = COMPRESS_PAGES_PER_STEP
    s = pl.program_id(1)
    nseg = per * PAGE // CMP_STRIDE
    slot, finish = _page_ring_step(pt_ref, [(pages_hbm, buf, sem)], per)
    for u in range(per):
        x_scr[u * PAGE:(u + 1) * PAGE, :] = buf[slot, u].T
    finish()
    xs = [x_scr[pl.ds(l, nseg, stride=CMP_STRIDE), :] for l in range(CMP_STRIDE)]
    xa = jnp.concatenate([(xs[l] + pea_ref[l:l + 1, :]).astype(BF16) for l in range(CMP_STRIDE)], axis=1)
    xb = jnp.concatenate([(xs[l] + peb_ref[l:l + 1, :]).astype(BF16) for l in range(CMP_STRIDE)], axis=1)
    a_scr[pl.ds(pl.multiple_of(s * nseg, nseg), nseg), :] = _dot(xa, wa_ref[...])
    b_scr[pl.ds(pl.multiple_of(s * nseg, nseg), nseg), :] = _dot(xb, wb_ref[...])

    @pl.when(s == pl.num_programs(1) - 1)
    def _():
        total = a_scr.shape[0]
        hid = a_scr[...] + pltpu.roll(b_scr[...], total - 1, 0)
        o_ref[0] = _dot(_gelu(hid).astype(BF16), w2_ref[...])


def _compress_s(pt_flat, pages, cw, batch, n_pages):
    wa, wb, pea, peb, w2bd = cw
    per = COMPRESS_PAGES_PER_STEP
    total = n_pages * PAGE // CMP_STRIDE
    hidden = NSA_KV_HEADS * CMP_HIDDEN
    width = CMP_STRIDE * NSA_KV_WIDTH
    full2 = lambda b, s, pt: (0, 0)
    return pl.pallas_call(
        _compress_s_kernel,
        grid_spec=pltpu.PrefetchScalarGridSpec(
            num_scalar_prefetch=1,
            grid=(batch, n_pages // per),
            in_specs=[pl.BlockSpec(memory_space=pl.ANY),
                      pl.BlockSpec((CMP_STRIDE, NSA_KV_WIDTH), full2), pl.BlockSpec((CMP_STRIDE, NSA_KV_WIDTH), full2),
                      pl.BlockSpec((width, hidden), full2), pl.BlockSpec((width, hidden), full2),
                      pl.BlockSpec((hidden, NSA_KV_WIDTH), full2)],
            out_specs=pl.BlockSpec((1, total, NSA_KV_WIDTH), lambda b, s, pt: (b, 0, 0)),
            scratch_shapes=[pltpu.VMEM((2, per, NSA_KV_WIDTH, PAGE), F32), pltpu.SemaphoreType.DMA((2,)),
                            pltpu.VMEM((per * PAGE, NSA_KV_WIDTH), F32), pltpu.VMEM((total, hidden), F32),
                            pltpu.VMEM((total, hidden), F32)]),
        out_shape=jax.ShapeDtypeStruct((batch, total, NSA_KV_WIDTH), F32),
        compiler_params=_params("arbitrary", "arbitrary"),
        name="compress_decode",
    )(pt_flat, pages, pea, peb, wa.reshape(width, hidden), wb.reshape(width, hidden), w2bd)


def _stack_group_q(q_ref, g):
    return jnp.concatenate(
        [q_ref[0, :, (g * NSA_GROUP + r) * D_HEAD:(g * NSA_GROUP + r + 1) * D_HEAD] * SCALE
         for r in range(NSA_GROUP)], axis=0).astype(BF16)


def _nsa_s_kernel(q_ref, kc_ref, vc_ref, wk_ref, wv_ref, kn_ref, vn_ref, gate_ref, ov_ref, ex_ref, part_ref, sel_ref,
                  *, ts):
    rows = NSA_GROUP * ts
    nc_pad = kc_ref.shape[1]
    wlen = wk_ref.shape[3]
    gates = _sigmoid(gate_ref[0])
    n_i = _iota((rows, nc_pad), 1)
    valid_c = n_i < nc_pad - 1
    c_of_row = _iota((rows, 1), 0) % ts
    valid_w = _iota((rows, wlen), 1) >= c_of_row
    valid_n = _iota((rows, LANES), 1) <= c_of_row
    for g in range(NSA_KV_HEADS):
        lane0 = g * D_HEAD
        qs = _stack_group_q(q_ref, g)
        s = jnp.where(valid_c, _dot_nt(qs, kc_ref[0, :, lane0:lane0 + D_HEAD].astype(BF16)), NEG_INF)
        m = jnp.max(s, axis=-1, keepdims=True)
        p = jnp.where(valid_c, jnp.exp(s - m), 0.0)
        p = p / jnp.maximum(jnp.sum(p, axis=-1, keepdims=True), TINY)
        o_c = _dot(p.astype(BF16), vc_ref[0, :, lane0:lane0 + D_HEAD].astype(BF16))
        psum = jnp.sum(p.reshape(NSA_GROUP, ts, nc_pad), axis=0)
        imp = jnp.dot(psum, ov_ref[...], precision=HIGHEST, preferred_element_type=F32)
        keep = _rank_select(imp, LANES, SLC_TOPN - 1).astype(BF16)
        sel_ref[0, g] = jnp.where(_dot(keep, ex_ref[...]) > 0.5, 0.0, NEG_INF)
        s_w = jnp.where(valid_w, _dot(qs, wk_ref[0, g].astype(BF16)), NEG_INF)
        s_n = jnp.where(valid_n, _dot_nt(qs, kn_ref[0, :, lane0:lane0 + D_HEAD].astype(BF16)), NEG_INF)
        m = jnp.maximum(jnp.max(s_w, axis=-1, keepdims=True), jnp.max(s_n, axis=-1, keepdims=True))
        p_w = jnp.where(valid_w, jnp.exp(s_w - m), 0.0)
        p_n = jnp.where(valid_n, jnp.exp(s_n - m), 0.0)
        den = jnp.sum(p_w, axis=-1, keepdims=True) + jnp.sum(p_n, axis=-1, keepdims=True)
        o_w = (_dot_nt(p_w.astype(BF16), wv_ref[0, g].astype(BF16))
               + _dot(p_n.astype(BF16), vn_ref[0, :, lane0:lane0 + D_HEAD].astype(BF16))) / jnp.maximum(den, TINY)
        for r in range(NSA_GROUP):
            hd = g * NSA_GROUP + r
            rs = slice(r * ts, (r + 1) * ts)
            part_ref[0, :, hd * D_HEAD:(hd + 1) * D_HEAD] = (
                gates[:, 3 * hd:3 * hd + 1] * o_c[rs] + gates[:, 3 * hd + 2:3 * hd + 3] * o_w[rs])


def _nsa_s(q3, kcmp, vcmp, wk_t, wv_t, kn_w, vn_w, gate3, ts):
    batch = q3.shape[0]
    nc_pad = kcmp.shape[1]
    wlen = wk_t.shape[3]
    ov = _overlap_matrix(nc_pad, LANES)
    n_keys = LANES * SLC_BLOCK
    expand = (jnp.arange(LANES)[:, None] == jnp.arange(n_keys)[None, :] // SLC_BLOCK).astype(BF16)
    b3 = lambda b: (b, 0, 0)
    b4 = lambda b: (b, 0, 0, 0)
    return pl.pallas_call(
        functools.partial(_nsa_s_kernel, ts=ts),
        grid=(batch,),
        in_specs=[pl.BlockSpec((1, ts, NSA_WIDTH), b3),
                  pl.BlockSpec((1, nc_pad, NSA_KV_WIDTH), b3), pl.BlockSpec((1, nc_pad, NSA_KV_WIDTH), b3),
                  pl.BlockSpec((1, NSA_KV_HEADS, D_HEAD, wlen), b4), pl.BlockSpec((1, NSA_KV_HEADS, D_HEAD, wlen), b4),
                  pl.BlockSpec((1, LANES, NSA_KV_WIDTH), b3), pl.BlockSpec((1, LANES, NSA_KV_WIDTH), b3),
                  pl.BlockSpec((1, ts, LANES), b3),
                  pl.BlockSpec((nc_pad, LANES), lambda b: (0, 0)),
                  pl.BlockSpec((LANES, n_keys), lambda b: (0, 0))],
        out_specs=[pl.BlockSpec((1, ts, NSA_WIDTH), b3), pl.BlockSpec((1, NSA_KV_HEADS, ts, n_keys), b4)],
        out_shape=[jax.ShapeDtypeStruct((batch, ts, NSA_WIDTH), F32),
                   jax.ShapeDtypeStruct((batch, NSA_KV_HEADS, ts, n_keys), F32)],
        compiler_params=_params("arbitrary"),
        name="nsa_decode_cmp_win",
    )(q3, kcmp, vcmp, wk_t, wv_t, kn_w, vn_w, gate3, ov, expand)


def _slc_s_kernel(pt_ref, q_ref, bias_ref, k_hbm, v_hbm, kn_ref, vn_ref, gate_ref, part_ref, o_ref,
                  kbuf, vbuf, ksem, vsem, m_scr, l_scr, acc_scr, *, ts):
    per = NSA_PAGES_PER_STEP
    s = pl.program_id(1)
    slot, finish = _page_ring_step(pt_ref, [(k_hbm, kbuf, ksem), (v_hbm, vbuf, vsem)], per)
    rows = NSA_GROUP * ts
    zero = jnp.zeros((rows, D_HEAD), BF16)
    q2 = jnp.concatenate([jnp.concatenate([_stack_group_q(q_ref, 0), zero], axis=1),
                          jnp.concatenate([zero, _stack_group_q(q_ref, 1)], axis=1)], axis=0)
    in_g0 = _iota((NSA_KV_HEADS * rows, D_HEAD), 0) < rows

    def own_group(full):
        return jnp.where(in_g0, full[:, :D_HEAD], full[:, D_HEAD:])

    @pl.when(s == 0)
    def _():
        valid_n = _iota((NSA_KV_HEADS * rows, LANES), 1) <= (_iota((NSA_KV_HEADS * rows, 1), 0) % ts)
        sc = jnp.where(valid_n, _dot_nt(q2, kn_ref[0].astype(BF16)), NEG_INF)
        m = jnp.max(sc, axis=-1, keepdims=True)
        p = jnp.exp(sc - m)
        m_scr[...] = m
        l_scr[...] = jnp.sum(p, axis=-1, keepdims=True)
        acc_scr[...] = own_group(_dot(p.astype(BF16), vn_ref[0].astype(BF16)))

    kt = jnp.concatenate([kbuf[slot, u].astype(BF16) for u in range(per)], axis=1)
    vt = jnp.concatenate([vbuf[slot, u].astype(BF16) for u in range(per)], axis=1)
    finish()
    bias = jnp.concatenate([bias_ref[0, g] for g in range(NSA_KV_HEADS) for _ in range(NSA_GROUP)], axis=0)
    sc = _dot(q2, kt) + bias
    m_old = m_scr[...]
    m_new = jnp.maximum(m_old, jnp.max(sc, axis=-1, keepdims=True))
    pf = jnp.exp(sc - m_new)
    alpha = jnp.exp(m_old - m_new)
    l_scr[...] = alpha * l_scr[...] + jnp.sum(pf, axis=-1, keepdims=True)
    acc_scr[...] = alpha * acc_scr[...] + own_group(_dot_nt(pf.astype(BF16), vt))
    m_scr[...] = m_new

    @pl.when(s == pl.num_programs(1) - 1)
    def _():
        gates = _sigmoid(gate_ref[0])
        o_s = acc_scr[...] / jnp.maximum(l_scr[...], TINY)
        for hd in range(NSA_HEADS):
            cols = slice(hd * D_HEAD, (hd + 1) * D_HEAD)
            o_ref[0, :, cols] = part_ref[0, :, cols] + gates[:, 3 * hd + 1:3 * hd + 2] * o_s[hd * ts:(hd + 1) * ts]


def _slc_s(pt_flat, q3, sel_bias, k_pages, v_pages, kn_s, vn_s, gate3, part, n_pages, ts):
    batch = q3.shape[0]
    per = NSA_PAGES_PER_STEP
    rows = NSA_GROUP * ts
    b3 = lambda b, s, pt: (b, 0, 0)
    return pl.pallas_call(
        functools.partial(_slc_s_kernel, ts=ts),
        grid_spec=pltpu.PrefetchScalarGridSpec(
            num_scalar_prefetch=1,
            grid=(batch, n_pages // per),
            in_specs=[pl.BlockSpec((1, ts, NSA_WIDTH), b3),
                      pl.BlockSpec((1, NSA_KV_HEADS, ts, per * PAGE), lambda b, s, pt: (b, 0, 0, s)),
                      pl.BlockSpec(memory_space=pl.ANY), pl.BlockSpec(memory_space=pl.ANY),
                      pl.BlockSpec((1, LANES, NSA_KV_WIDTH), b3), pl.BlockSpec((1, LANES, NSA_KV_WIDTH), b3),
                      pl.BlockSpec((1, ts, LANES), b3), pl.BlockSpec((1, ts, NSA_WIDTH), b3)],
            out_specs=pl.BlockSpec((1, ts, NSA_WIDTH), b3),
            scratch_shapes=[pltpu.VMEM((2, per, NSA_KV_WIDTH, PAGE), F32), pltpu.VMEM((2, per, NSA_KV_WIDTH, PAGE), F32),
                            pltpu.SemaphoreType.DMA((2,)), pltpu.SemaphoreType.DMA((2,)),
                            pltpu.VMEM((NSA_KV_HEADS * rows, 1), F32), pltpu.VMEM((NSA_KV_HEADS * rows, 1), F32),
                            pltpu.VMEM((NSA_KV_HEADS * rows, D_HEAD), F32)]),
        out_shape=jax.ShapeDtypeStruct((batch, ts, NSA_WIDTH), F32),
        compiler_params=_params("arbitrary", "arbitrary"),
        name="nsa_decode_slc",
    )(pt_flat, q3, sel_bias, k_pages, v_pages, kn_s, vn_s, gate3, part)


def _prep_weights(w_ada, b_ada, norm_mix_g, w_in, pe_cmp_k, w_cmp_k1, w_cmp_k2, pe_cmp_v, w_cmp_v1, w_cmp_v2,
                  w_br_a, w_br_b, w_out, norm_ffn_g, w_router_grp, w_router_exp, w_expert_in, w_expert_out,
                  norm_final_g):
    wr = jnp.concatenate([w_router_grp, w_router_exp], axis=1)
    wr = jnp.pad(wr, ((0, 0), (0, LANES - wr.shape[1])))
    wr_hi = wr.astype(BF16)
    wr = jnp.stack([wr_hi, (wr - wr_hi.astype(F32)).astype(BF16)])
    return dict(
        w_ada=w_ada, b_ada=b_ada.reshape(1, -1), g_mix=norm_mix_g.reshape(1, -1),
        w_in=_reorder_w_in(w_in),
        cmp_k=_compress_weights(pe_cmp_k, w_cmp_k1, w_cmp_k2),
        cmp_v=_compress_weights(pe_cmp_v, w_cmp_v1, w_cmp_v2),
        w_br_a=w_br_a.astype(BF16), w_br_b=w_br_b.astype(BF16), w_out=w_out.astype(BF16),
        g_ffn=norm_ffn_g.reshape(1, -1), wr=wr,
        w_ein=w_expert_in.astype(BF16), w_eout=w_expert_out.astype(BF16),
        g_final=norm_final_g.reshape(1, -1))


def _prompt_layer(x, mod, w, batch, seq):
    sh1, sc1, g1, sh2, sc2, g2 = mod
    cos, sin = _rope_tables(jnp.arange(seq, dtype=jnp.int32))
    qk_a, _, q_b, k_n, v_n, gate, mg, kt_a, vt_a, kt_n, vt_n = _inproj(
        x, sc1, sh1, w['g_mix'], cos, sin, w['w_in'], 256, transposed=True)
    o_a = _moba_p(qk_a, vt_a, batch, seq)
    kcmp = _compress_p(k_n[:, :NSA_KV_WIDTH], w['cmp_k'], batch, seq)
    vcmp = _compress_p(v_n[:, :NSA_KV_WIDTH], w['cmp_v'], batch, seq)
    o_b = _nsa_p(q_b, k_n, vt_n, kcmp, vcmp, gate, batch, seq)
    x1 = _merge(o_a, o_b, mg, x, g1, w['w_br_a'], w['w_br_b'], w['w_out'], 256)
    y = _moe(x1, sc2, sh2, g2, w['g_ffn'], w['g_final'], w['wr'], w['w_ein'], w['w_eout'], 512)
    return y, (kt_a, vt_a, kt_n, vt_n)


def _sample_layer(x, mod, w, caches, win_state, page_table, batch, ts):
    sh1, sc1, g1, sh2, sc2, g2 = mod
    moba_k, moba_v, cmp_k, cmp_v, slc_k, slc_v = caches
    win_k, win_v = win_state
    n_pages = page_table.shape[1]
    assert moba_k.shape[1] == PAGE and win_k.shape[1] == WINDOW and ts <= LANES
    assert n_pages * PAGE == LANES * SLC_BLOCK and n_pages * PAGE // MOBA_BLOCK <= LANES
    n = batch * ts
    pos = n_pages * PAGE + (jnp.arange(n, dtype=jnp.int32) % ts)
    cos, sin = _rope_tables(pos)
    qk_a, v_a, q_b, k_n, v_n, gate, mg = _inproj(x, sc1, sh1, w['g_mix'], cos, sin, w['w_in'], n)
    pt_flat = page_table.reshape(-1)
    kv = NSA_KV_WIDTH
    qf = _block_diag_q(qk_a[:, :MOBA_WIDTH], batch, ts, MOBA_HEADS)
    qbd = (qf * SCALE).astype(BF16)
    m, l, ks, o = _moba_s_pass(pt_flat, qbd, _page_view(moba_k), _page_view(moba_v), batch, n_pages)
    o_a = _moba_s_combine(m, l, ks, o, qf, qbd, _pad_rows(qk_a[:, MOBA_WIDTH:], batch, ts), _pad_rows(v_a, batch, ts),
                          n_pages, ts)
    o_a = o_a.reshape(batch, MOBA_HEADS, ts, D_HEAD).transpose(0, 2, 1, 3).reshape(n, MOBA_WIDTH)
    kcmp = _compress_s(pt_flat, _page_view(cmp_k), w['cmp_k'], batch, n_pages)
    vcmp = _compress_s(pt_flat, _page_view(cmp_v), w['cmp_v'], batch, n_pages)
    q3 = q_b.reshape(batch, ts, NSA_WIDTH)
    gate3 = gate.reshape(batch, ts, LANES)
    part, sel = _nsa_s(q3, kcmp, vcmp, jnp.transpose(win_k, (0, 2, 3, 1)), jnp.transpose(win_v, (0, 2, 3, 1)),
                       _pad_rows(k_n[:, 2 * kv:], batch, ts), _pad_rows(v_n[:, 2 * kv:], batch, ts), gate3, ts)
    o_b = _slc_s(pt_flat, q3, sel, _page_view(slc_k), _page_view(slc_v),
                 _pad_rows(k_n[:, kv:2 * kv], batch, ts), _pad_rows(v_n[:, kv:2 * kv], batch, ts), gate3, part,
                 n_pages, ts).reshape(n, NSA_WIDTH)
    x1 = _merge(o_a, o_b, mg, x, g1, w['w_br_a'], w['w_br_b'], w['w_out'], n)
    y = _moe(x1, sc2, sh2, g2, w['g_ffn'], w['g_final'], w['wr'], w['w_ein'], w['w_eout'], n)
    return y, (qk_a, v_a, k_n, v_n)


def kernel(x_prompt, x_sample, c_prompt, c_sample, cache_moba_k, cache_moba_v, cache_nsa_cmp_k, cache_nsa_cmp_v,
           cache_nsa_slc_k, cache_nsa_slc_v, state_nsa_win_k, state_nsa_win_v, page_table, w_ada, b_ada, norm_mix_g,
           w_in, pe_cmp_k, w_cmp_k1, w_cmp_k2, pe_cmp_v, w_cmp_v1, w_cmp_v2, w_br_a, w_br_b, w_out, norm_ffn_g,
           w_router_grp, w_router_exp, w_expert_in, w_expert_out, norm_final_g):
    bp, tp, _ = x_prompt.shape
    bs, ts, _ = x_sample.shape
    w = _prep_weights(w_ada[0], b_ada[0], norm_mix_g[0], w_in[0], pe_cmp_k[0], w_cmp_k1[0], w_cmp_k2[0], pe_cmp_v[0],
                      w_cmp_v1[0], w_cmp_v2[0], w_br_a[0], w_br_b[0], w_out[0], norm_ffn_g[0], w_router_grp[0],
                      w_router_exp[0], w_expert_in[0], w_expert_out[0], norm_final_g)
    mod = _ada(jnp.concatenate([c_prompt, c_sample], axis=0), w['w_ada'], w['b_ada'])
    mod_p = [m.reshape(bp, 1, D_MODEL) for m in jnp.split(mod[:bp], 6, axis=-1)]
    y_p, new_p = _prompt_layer(x_prompt.reshape(bp * tp, D_MODEL), mod_p, w, bp, tp)

    mod_s = [jnp.repeat(m, ts, axis=0).reshape(1, bs * ts, D_MODEL) for m in jnp.split(mod[bp:], 6, axis=-1)]
    caches = (cache_moba_k[0], cache_moba_v[0], cache_nsa_cmp_k[0], cache_nsa_cmp_v[0], cache_nsa_slc_k[0],
              cache_nsa_slc_v[0])
    y_s, new_s = _sample_layer(x_sample.reshape(bs * ts, D_MODEL), mod_s, w, caches,
                               (state_nsa_win_k[0], state_nsa_win_v[0]), page_table, bs, ts)

    kv = NSA_KV_WIDTH

    def new_rows(new, b, t):
        qk, v, k_n, v_n = new
        rows = lambda a, heads: a.reshape(1, b, t, heads, D_HEAD)
        return (rows(qk[:, MOBA_WIDTH:], MOBA_HEADS), rows(v, MOBA_HEADS),
                rows(k_n[:, :kv], NSA_KV_HEADS), rows(v_n[:, :kv], NSA_KV_HEADS),
                rows(k_n[:, kv:2 * kv], NSA_KV_HEADS), rows(v_n[:, kv:2 * kv], NSA_KV_HEADS),
                rows(k_n[:, 2 * kv:], NSA_KV_HEADS), rows(v_n[:, 2 * kv:], NSA_KV_HEADS))

    def new_rows_t(new, b, t):
        kt_a, vt_a, kt_n, vt_n = new
        rows = lambda a: jnp.transpose(a.reshape(1, b, a.shape[1] // D_HEAD, D_HEAD, a.shape[2]), (0, 1, 4, 2, 3))
        return (rows(kt_a), rows(vt_a), rows(kt_n[:, :kv]), rows(vt_n[:, :kv]),
                rows(kt_n[:, kv:2 * kv]), rows(vt_n[:, kv:2 * kv]),
                rows(kt_n[:, 2 * kv:, t - wb:]), rows(vt_n[:, 2 * kv:, t - wb:]))

    wb = state_nsa_win_k.shape[2]
    outs_p = new_rows_t(new_p, bp, tp)
    outs_s = new_rows(new_s, bs, ts)
    win_k = jnp.concatenate([state_nsa_win_k, outs_s[6]], axis=2)[:, :, ts:]
    win_v = jnp.concatenate([state_nsa_win_v, outs_s[7]], axis=2)[:, :, ts:]
    outs_s = outs_s[:6] + (win_k, win_v)
    return (y_p.reshape(bp, tp, D_MODEL), y_s.reshape(bs, ts, D_MODEL)) + outs_p + outs_s
```

```python
import functools

import jax
import jax.numpy as jnp
from jax import lax
from jax.experimental import pallas as pl
from jax.experimental.pallas import tpu as pltpu

D_MODEL = 1024
D_HEAD = 64
HALF = D_HEAD // 2
MOBA_HEADS = 8
MOBA_BLOCK = 256
MOBA_TOPK = 3
NSA_HEADS = 8
NSA_KV_HEADS = 2
NSA_GROUP = NSA_HEADS // NSA_KV_HEADS
CMP_LEN = 32
CMP_STRIDE = 16
CMP_HIDDEN = 128
SLC_BLOCK = 64
SLC_TOPN = 16
WINDOW = 512
N_GROUPS = 4
EXPERTS_PER_GROUP = 8
N_EXPERTS = N_GROUPS * EXPERTS_PER_GROUP
D_EXPERT = 256
ROPE_THETA = 10000.0
EPS = 1e-6
NEG_INF = -1e30
BIG = 1e30
TINY = 1e-30
MOBA_WIDTH = MOBA_HEADS * D_HEAD
NSA_WIDTH = NSA_HEADS * D_HEAD
NSA_KV_WIDTH = NSA_KV_HEADS * D_HEAD
SCALE = D_HEAD ** -0.5

LANES = 128
VMEM_LIMIT = 48 * 1024 * 1024

F32 = jnp.float32
BF16 = jnp.bfloat16
HIGHEST = lax.Precision.HIGHEST


def _params(*sem):
    return pltpu.CompilerParams(dimension_semantics=sem, vmem_limit_bytes=VMEM_LIMIT)


def _dot(a, b):
    return jnp.dot(a, b, preferred_element_type=F32)


def _dot_nt(a, b, precision=None):
    return lax.dot_general(a, b, (((1,), (1,)), ((), ())), precision=precision,
                           preferred_element_type=F32)


def _sigmoid(x):
    return 1.0 / (1.0 + jnp.exp(-x))


def _iota(shape, dim):
    return lax.broadcasted_iota(jnp.int32, shape, dim)


def _ada_kernel(c_ref, w_ref, b_ref, o_ref):
    c = c_ref[...]
    s = c * _sigmoid(c)
    o_ref[...] = _dot(s.astype(BF16), w_ref[...].astype(BF16)) + b_ref[...]


def _ada(c, w, b):
    n = c.shape[0]
    tn = 1024
    return pl.pallas_call(
        _ada_kernel,
        grid=(w.shape[1] // tn,),
        in_specs=[pl.BlockSpec((n, D_MODEL), lambda j: (0, 0)),
                  pl.BlockSpec((D_MODEL, tn), lambda j: (0, j)),
                  pl.BlockSpec((1, tn), lambda j: (0, j))],
        out_specs=pl.BlockSpec((n, tn), lambda j: (0, j)),
        out_shape=jax.ShapeDtypeStruct((n, w.shape[1]), F32),
        compiler_params=_params("arbitrary"),
        name="ada",
    )(c, w, b)


_IN_GROUPS = ((2 * MOBA_WIDTH, True),
              (MOBA_WIDTH, False),
              (NSA_WIDTH, True),
              (3 * NSA_KV_WIDTH, True),
              (3 * NSA_KV_WIDTH, False),
              (LANES, False),
              (2 * D_MODEL, False))
_IN_COLS_PAD = sum(w for w, _ in _IN_GROUPS)


def _norm_mod(x, g, sc, sh):
    ms = jnp.mean(x * x, axis=-1, keepdims=True)
    y = x * lax.rsqrt(ms + EPS) * g
    return y * (1.0 + sc) + sh


def _inproj_kernel(x_ref, sc_ref, sh_ref, g_ref, cos_ref, sin_ref, w_ref, *out_refs):
    h = _norm_mod(x_ref[...], g_ref[...], sc_ref[0], sh_ref[0]).astype(BF16)
    cos = cos_ref[...]
    sin = sin_ref[...]
    first_half = (_iota(cos.shape, 1) & (D_HEAD - 1)) < HALF

    def rope(y):
        rot = jnp.where(first_half, pltpu.roll(y, LANES - HALF, 1), pltpu.roll(y, HALF, 1))
        return y * cos + rot * sin

    n_groups = len(_IN_GROUPS)
    t_refs = dict(zip(_IN_T_GROUPS, out_refs[n_groups:]))
    col = 0
    for gi, (out_ref, (width, rotary)) in enumerate(zip(out_refs, _IN_GROUPS)):
        chunk = min(width, 512)
        for c in range(0, width, chunk):
            cw = min(chunk, width - c)
            y = _dot(h, w_ref[:, col + c:col + c + cw])
            for s in range(0, cw, LANES):
                piece = y[:, s:s + LANES]
                piece = rope(piece) if rotary else piece
                out_ref[:, c + s:c + s + LANES] = piece
                if gi in t_refs and c + s >= _IN_T_GROUPS[gi]:
                    t0 = c + s - _IN_T_GROUPS[gi]
                    t_refs[gi][0, t0:t0 + LANES, :] = piece.T
        col += width


_IN_T_GROUPS = {0: MOBA_WIDTH, 1: 0, 3: 0, 4: 0}


def _inproj(x, sc, sh, g, cos, sin, w, tm, transposed=False):
    n = x.shape[0]
    nb, r, _ = sc.shape
    tiles_per_b = (n // nb) // tm
    tab_tiles = cos.shape[0] // tm
    row = lambda i: (i, 0)
    mod = lambda i: (i // tiles_per_b, 0, 0)
    tab = lambda i: (i % tab_tiles, 0)
    out_specs = [pl.BlockSpec((tm, wd), row) for wd, _ in _IN_GROUPS]
    out_shape = [jax.ShapeDtypeStruct((n, wd), F32) for wd, _ in _IN_GROUPS]
    if transposed:
        for gi, first in _IN_T_GROUPS.items():
            cols = _IN_GROUPS[gi][0] - first
            out_specs.append(pl.BlockSpec((1, cols, tm), lambda i: (i // tiles_per_b, 0, i % tiles_per_b)))
            out_shape.append(jax.ShapeDtypeStruct((nb, cols, n // nb), F32))
    return pl.pallas_call(
        _inproj_kernel,
        grid=(n // tm,),
        in_specs=[pl.BlockSpec((tm, D_MODEL), row),
                  pl.BlockSpec((1, r, D_MODEL), mod),
                  pl.BlockSpec((1, r, D_MODEL), mod),
                  pl.BlockSpec((1, D_MODEL), lambda i: (0, 0)),
                  pl.BlockSpec((tm, LANES), tab),
                  pl.BlockSpec((tm, LANES), tab),
                  pl.BlockSpec((D_MODEL, _IN_COLS_PAD), lambda i: (0, 0))],
        out_specs=out_specs,
        out_shape=out_shape,
        compiler_params=_params("arbitrary"),
        name="inproj",
    )(x, sc, sh, g, cos, sin, w)


def _reorder_w_in(w_in):
    kv0 = 3 * MOBA_WIDTH + NSA_WIDTH
    kvs = [w_in[:, kv0 + i * NSA_KV_WIDTH:kv0 + (i + 1) * NSA_KV_WIDTH] for i in range(6)]
    g0 = kv0 + 6 * NSA_KV_WIDTH
    ng = 3 * NSA_HEADS
    gate = jnp.pad(w_in[:, g0:g0 + ng], ((0, 0), (0, LANES - ng)))
    parts = [w_in[:, :kv0], kvs[0], kvs[2], kvs[4], kvs[1], kvs[3], kvs[5], gate, w_in[:, g0 + ng:]]
    return jnp.concatenate(parts, axis=1).astype(BF16)


def _rope_tables(pos):
    inv = ROPE_THETA ** (-jnp.arange(HALF, dtype=F32) / HALF)
    ang = pos.astype(F32)[:, None] * inv[None, :]
    cos = jnp.cos(ang)
    sin = jnp.sin(ang)
    cos = jnp.concatenate([cos, cos, cos, cos], axis=1)
    sin = jnp.concatenate([-sin, sin, -sin, sin], axis=1)
    return cos, sin


def _rank_select(score, n_cols, n_keep):
    lane = _iota(score.shape, 1)
    rank = jnp.zeros(score.shape, jnp.int32)
    for jp in range(n_cols):
        col = score[:, jp:jp + 1]
        beats = (col > score) | ((col == score) & (lane > jp))
        rank = rank + beats.astype(jnp.int32)
    return rank < n_keep


def _rank_select_t(score, n_rows, n_keep):
    row = _iota(score.shape, 0)
    rank = jnp.zeros(score.shape, jnp.int32)
    for jp in range(n_rows):
        r = score[jp:jp + 1, :]
        beats = (r > score) | ((r == score) & (row > jp))
        rank = rank + beats.astype(jnp.int32)
    return rank < n_keep


SUBLANES = 8


def _fold_rows(x, op):
    return op(x.reshape(x.shape[0] // SUBLANES, SUBLANES, x.shape[1]), axis=0)


def _attend_t(make_scores, vt_fn, own_c, lo, hi, width, n_chains):
    score_fn = make_scores(None)

    def max_step(c, own, m):
        return tuple(jnp.maximum(mi, _fold_rows(s, jnp.max)) for mi, s in zip(m, score_fn(c, own)))

    m = max_step(own_c, True, tuple(jnp.full((SUBLANES, width), NEG_INF, F32) for _ in range(n_chains)))
    m = lax.fori_loop(lo, hi, lambda c, mm: max_step(c, False, mm), m)
    shifted_fn = make_scores([jnp.max(mi, axis=0, keepdims=True) for mi in m])

    def acc_step(c, own, carry):
        out = []
        for (l, acc), s, vt in zip(carry, shifted_fn(c, own), vt_fn(c)):
            p = jnp.exp(s)
            out.append((l + _fold_rows(p, jnp.sum), acc + _dot(vt, p.astype(BF16))))
        return tuple(out)

    zero = tuple((jnp.zeros((SUBLANES, width), F32), jnp.zeros((D_HEAD, width), F32)) for _ in range(n_chains))
    carry = acc_step(own_c, True, zero)
    carry = lax.fori_loop(lo, hi, lambda c, cr: acc_step(c, False, cr), carry)
    return [acc / jnp.maximum(jnp.sum(l, axis=0, keepdims=True), TINY) for l, acc in carry]


AUG_ONES = 2
AUG_BLK = SUBLANES


def _key_aug(n_keys, block, n_blocks):
    lane = _iota((n_keys, D_HEAD), 1)
    blk = _iota((n_keys, D_HEAD), 0) // block
    hot = (lane < AUG_ONES) | ((lane >= AUG_BLK) & (lane < AUG_BLK + n_blocks) & (lane - AUG_BLK == blk))
    return jnp.where(hot, 1.0, 0.0).astype(BF16)


def _query_aug(qt, shift, bias):
    width = qt.shape[1]
    top = jnp.zeros((SUBLANES, width), F32)
    if shift is not None:
        hi = shift.astype(BF16).astype(F32)
        row = _iota((SUBLANES, width), 0)
        top = jnp.where(row == 0, -hi, jnp.where(row == 1, hi - shift, 0.0))
    parts = [qt, top]
    used = SUBLANES
    if bias is not None:
        parts.append(bias)
        used += bias.shape[0]
    parts.append(jnp.zeros((D_HEAD - used, width), F32))
    return jnp.concatenate(parts, axis=0).astype(BF16)


MOBA_STEP_WIDTH = 512


def _moba_p_kernel(q_ref, k_ref, v_ref, o_ref, ka_scr, vt_scr, km_scr):
    qi = pl.program_id(2)
    tq = tk = MOBA_BLOCK
    seq = k_ref.shape[1]
    nblk = seq // MOBA_BLOCK
    width = q_ref.shape[2]
    nh = width // D_HEAD

    @pl.when(qi == 0)
    def _():
        kf = k_ref[0]
        km_scr[...] = jnp.mean(kf.reshape(nblk, MOBA_BLOCK, width), axis=1)
        aug = _key_aug(seq, MOBA_BLOCK, nblk)
        for hh in range(nh):
            ka_scr[hh] = jnp.concatenate([kf[:, hh * D_HEAD:(hh + 1) * D_HEAD].astype(BF16), aug], axis=1)
        for j in range(nblk):
            vt_scr[j] = v_ref[0, :, j * tk:(j + 1) * tk].astype(BF16)

    q2t = q_ref[0].T
    km = km_scr[...]
    klane = _iota(km.shape, 1)
    blk = _iota((nblk, tq), 0)
    causal_t = _iota((tk, tq), 0) <= _iota((tk, tq), 1)
    qts, biases = [], []
    for hh in range(nh):
        kmh = jnp.where((klane >= hh * D_HEAD) & (klane < (hh + 1) * D_HEAD), km, 0.0)
        gate = jnp.dot(kmh, q2t, precision=HIGHEST, preferred_element_type=F32)
        gate = jnp.where(blk < qi, gate, NEG_INF)
        keep = _rank_select_t(gate, nblk, MOBA_TOPK) & (blk < qi)
        biases.append(jnp.where(keep, 0.0, NEG_INF))
        qts.append(q2t[hh * D_HEAD:(hh + 1) * D_HEAD, :] * SCALE)

    def make_scores(shift):
        sh = [None] * nh if shift is None else shift
        q_own = [_query_aug(qts[hh], sh[hh], None) for hh in range(nh)]
        q_past = [_query_aug(qts[hh], sh[hh], biases[hh]) for hh in range(nh)]

        def score_fn(c, own):
            out = []
            for hh in range(nh):
                kj = ka_scr[hh, pl.ds(pl.multiple_of(c * tk, tk), tk), :]
                s = _dot(kj, q_own[hh] if own else q_past[hh])
                out.append(jnp.where(causal_t, s, NEG_INF) if own else s)
            return out

        return score_fn

    def values(c):
        return [vt_scr[c, hh * D_HEAD:(hh + 1) * D_HEAD, :] for hh in range(nh)]

    o_ref[0] = jnp.concatenate(_attend_t(make_scores, values, qi, 0, qi, tq, nh), axis=0).T


def _moba_p(qk, vt, batch, seq):
    qk3 = qk.reshape(batch, seq, 2 * MOBA_WIDTH)
    sw = MOBA_STEP_WIDTH
    pairs = MOBA_WIDTH // sw
    nblk = seq // MOBA_BLOCK
    out = pl.pallas_call(
        _moba_p_kernel,
        grid=(batch, pairs, nblk),
        in_specs=[pl.BlockSpec((1, MOBA_BLOCK, sw), lambda b, h, i: (b, i, h)),
                  pl.BlockSpec((1, seq, sw), lambda b, h, i: (b, 0, pairs + h)),
                  pl.BlockSpec((1, sw, seq), lambda b, h, i: (b, h, 0))],
        out_specs=pl.BlockSpec((1, MOBA_BLOCK, sw), lambda b, h, i: (b, i, h)),
        out_shape=jax.ShapeDtypeStruct((batch, seq, MOBA_WIDTH), F32),
        scratch_shapes=[pltpu.VMEM((sw // D_HEAD, seq, 2 * D_HEAD), BF16), pltpu.VMEM((nblk, sw, MOBA_BLOCK), BF16),
                        pltpu.VMEM((nblk, sw), F32)],
        compiler_params=_params("arbitrary", "arbitrary", "arbitrary"),
        name="moba_prompt",
    )(qk3, qk3, vt)
    return out.reshape(batch * seq, MOBA_WIDTH)


def _gelu(x):
    return 0.5 * x * (1.0 + jnp.tanh(0.7978845608028654 * (x + 0.044715 * x * x * x)))


def _compress_p_kernel(seg_ref, pea_ref, peb_ref, wa_ref, wb_ref, w2_ref, o_ref):
    seg = seg_ref[0]
    a = _dot((seg + pea_ref[...]).astype(BF16), wa_ref[...])
    b = _dot((seg + peb_ref[...]).astype(BF16), wb_ref[...])
    nseg = seg.shape[0]
    hid = a + pltpu.roll(b, nseg - 1, 0)
    o_ref[0] = _dot(_gelu(hid).astype(BF16), w2_ref[...])


def _compress_weights(pe, w1, w2):
    g = NSA_KV_HEADS
    eye = jnp.eye(g, dtype=F32)
    w1r = w1.reshape(CMP_LEN, D_HEAD, CMP_HIDDEN)

    def half(lo):
        w = w1r[lo:lo + CMP_STRIDE]
        wbd = jnp.einsum('ldf,gh->lgdhf', w, eye)
        p = jnp.broadcast_to(pe[lo:lo + CMP_STRIDE, None, :], (CMP_STRIDE, g, D_HEAD))
        return wbd.reshape(CMP_STRIDE, g * D_HEAD, g * CMP_HIDDEN).astype(BF16), p.reshape(CMP_STRIDE, g * D_HEAD)

    wa, pea = half(0)
    wb, peb = half(CMP_STRIDE)
    w2bd = jnp.einsum('fd,gh->gfhd', w2, eye).reshape(g * CMP_HIDDEN, g * D_HEAD).astype(BF16)
    return wa, wb, pea, peb, w2bd


def _compress_p(rows, cw, batch, seq):
    wa, wb, pea, peb, w2bd = cw
    nseg = seq // CMP_STRIDE
    width = CMP_STRIDE * NSA_KV_WIDTH
    seg = rows.reshape(batch, nseg, width)
    full = lambda b: (0, 0)
    return pl.pallas_call(
        _compress_p_kernel,
        grid=(batch,),
        in_specs=[pl.BlockSpec((1, nseg, width), lambda b: (b, 0, 0)),
                  pl.BlockSpec((1, width), full), pl.BlockSpec((1, width), full),
                  pl.BlockSpec((width, NSA_KV_HEADS * CMP_HIDDEN), full),
                  pl.BlockSpec((width, NSA_KV_HEADS * CMP_HIDDEN), full),
                  pl.BlockSpec((NSA_KV_HEADS * CMP_HIDDEN, NSA_KV_WIDTH), full)],
        out_specs=pl.BlockSpec((1, nseg, NSA_KV_WIDTH), lambda b: (b, 0, 0)),
        out_shape=jax.ShapeDtypeStruct((batch, nseg, NSA_KV_WIDTH), F32),
        compiler_params=_params("arbitrary"),
        name="compress_prompt",
    )(seg, pea.reshape(1, width), peb.reshape(1, width), wa.reshape(width, -1), wb.reshape(width, -1), w2bd)


NSA_TQ = 256
NSA_TK = 256


def _overlap_matrix(nc_pad, nslc_pad):
    cs = jnp.arange(nc_pad)[:, None] * CMP_STRIDE
    ss = jnp.arange(nslc_pad)[None, :] * SLC_BLOCK
    return ((cs < ss + SLC_BLOCK) & (cs + CMP_LEN > ss)).astype(F32)


def _nsa_p_kernel(q_ref, kc_ref, vc_ref, ks_ref, vs_ref, kw_ref, vw_ref, gate_ref, ovt_ref, o_ref,
                  ksa_scr, kwa_scr, vst_scr, vwt_scr):
    tq, tk = NSA_TQ, NSA_TK
    t = pl.program_id(1)
    q0 = t * tq
    seq = ks_ref.shape[1]
    nc_pad = kc_ref.shape[1]
    nslc = seq // SLC_BLOCK
    width = NSA_GROUP * tq

    @pl.when(t == 0)
    def _():
        aug_s = _key_aug(seq, SLC_BLOCK, nslc)
        aug_w = _key_aug(seq, SLC_BLOCK, 0)
        for g in range(NSA_KV_HEADS):
            rows = slice(g * D_HEAD, (g + 1) * D_HEAD)
            ksa_scr[g] = jnp.concatenate([ks_ref[0, :, rows].astype(BF16), aug_s], axis=1)
            kwa_scr[g] = jnp.concatenate([kw_ref[0, :, rows].astype(BF16), aug_w], axis=1)
        for j in range(seq // tk):
            vst_scr[j] = vs_ref[0, :, j * tk:(j + 1) * tk].astype(BF16)
            vwt_scr[j] = vw_ref[0, :, j * tk:(j + 1) * tk].astype(BF16)

    qt_all = q_ref[0].T
    gates_t = _sigmoid(gate_ref[0]).T
    kct = kc_ref[0].astype(BF16)
    vct = vc_ref[0].T.astype(BF16)
    pos = q0 + _iota((1, tq), 1)
    n_i = _iota((nc_pad, tq), 0)
    valid_c = (n_i < nc_pad - 1) & (n_i * CMP_STRIDE + (CMP_LEN - 1) <= pos)
    valid_c4 = jnp.concatenate([valid_c] * NSA_GROUP, axis=1)
    q_blk = pos >> 6
    jrow = _iota((nslc, tq), 0)
    krow = _iota((tk, tq), 0)
    cd = q0 // tk
    c_win = jnp.maximum(q0 - WINDOW, 0) // tk

    def tile(b):
        return jnp.concatenate([b] * NSA_GROUP, axis=1)

    def win_bias(c):
        dist = pos - (c * tk + krow)
        return jnp.where((dist >= 0) & (dist <= WINDOW), 0.0, NEG_INF)

    qts, o_cs, biases = [], [], []
    for g in range(NSA_KV_HEADS):
        rows = slice(g * D_HEAD, (g + 1) * D_HEAD)
        qt = jnp.concatenate(
            [qt_all[(g * NSA_GROUP + r) * D_HEAD:(g * NSA_GROUP + r + 1) * D_HEAD, :] * SCALE
             for r in range(NSA_GROUP)], axis=1)
        qts.append(qt)
        s = jnp.where(valid_c4, _dot(kct[:, rows], qt.astype(BF16)), NEG_INF)
        m = jnp.max(s, axis=0, keepdims=True)
        p = jnp.where(valid_c4, jnp.exp(s - m), 0.0)
        p = p / jnp.maximum(jnp.sum(p, axis=0, keepdims=True), TINY)
        o_cs.append(_dot(vct[rows, :], p.astype(BF16)))
        psum = p[:, :tq]
        for r in range(1, NSA_GROUP):
            psum = psum + p[:, r * tq:(r + 1) * tq]
        imp = jnp.dot(ovt_ref[...], psum, precision=HIGHEST, preferred_element_type=F32)
        imp = jnp.where(jrow == q_blk, BIG, jnp.where(jrow < q_blk, imp, NEG_INF))
        keep = _rank_select_t(imp, nslc, SLC_TOPN) & (jrow <= q_blk)
        biases.append(tile(jnp.where(keep, 0.0, NEG_INF)))

    causal_own = tile(cd * tk + krow <= pos)

    def make_slc_scores(shift):
        sh = [None] * NSA_KV_HEADS if shift is None else shift
        qa = [_query_aug(qts[g], sh[g], biases[g]) for g in range(NSA_KV_HEADS)]

        def score_fn(c, own):
            out = []
            for g in range(NSA_KV_HEADS):
                s = _dot(ksa_scr[g, pl.ds(pl.multiple_of(c * tk, tk), tk), :], qa[g])
                out.append(jnp.where(causal_own, s, NEG_INF) if own else s)
            return out

        return score_fn

    def make_win_scores(shift):
        sh = [None] * NSA_KV_HEADS if shift is None else shift
        qa = [_query_aug(qts[g], sh[g], None) for g in range(NSA_KV_HEADS)]

        def score_fn(c, own):
            wb = tile(win_bias(c))
            return [_dot(kwa_scr[g, pl.ds(pl.multiple_of(c * tk, tk), tk), :], qa[g]) + wb
                    for g in range(NSA_KV_HEADS)]

        return score_fn

    def values(vt_scr):
        return lambda c: [vt_scr[c, g * D_HEAD:(g + 1) * D_HEAD, :] for g in range(NSA_KV_HEADS)]

    o_ss = _attend_t(make_slc_scores, values(vst_scr), cd, 0, cd, width, NSA_KV_HEADS)
    o_ws = _attend_t(make_win_scores, values(vwt_scr), cd, c_win, cd, width, NSA_KV_HEADS)
    outs = []
    for g in range(NSA_KV_HEADS):
        o_c, o_s, o_w = o_cs[g], o_ss[g], o_ws[g]
        for r in range(NSA_GROUP):
            hd = g * NSA_GROUP + r
            cs = slice(r * tq, (r + 1) * tq)
            outs.append(gates_t[3 * hd:3 * hd + 1, :] * o_c[:, cs] + gates_t[3 * hd + 1:3 * hd + 2, :] * o_s[:, cs]
                        + gates_t[3 * hd + 2:3 * hd + 3, :] * o_w[:, cs])
    o_ref[0] = jnp.concatenate(outs, axis=0).T


def _nsa_p(q, kn, vnt, kcmp, vcmp, gate, batch, seq):
    q3 = q.reshape(batch, seq, NSA_WIDTH)
    kn3 = kn.reshape(batch, seq, 3 * NSA_KV_WIDTH)
    g3 = gate.reshape(batch, seq, LANES)
    nc_pad = kcmp.shape[1]
    nslc = seq // SLC_BLOCK
    nchunk = seq // NSA_TK
    ovt = _overlap_matrix(nc_pad, nslc).T
    tile = lambda b, t: (b, t, 0)
    cmp_spec = pl.BlockSpec((1, nc_pad, NSA_KV_WIDTH), lambda b, t: (b, 0, 0))
    out = pl.pallas_call(
        _nsa_p_kernel,
        grid=(batch, seq // NSA_TQ),
        in_specs=[pl.BlockSpec((1, NSA_TQ, NSA_WIDTH), tile), cmp_spec, cmp_spec,
                  pl.BlockSpec((1, seq, LANES), lambda b, t: (b, 0, 1)),
                  pl.BlockSpec((1, LANES, seq), lambda b, t: (b, 1, 0)),
                  pl.BlockSpec((1, seq, LANES), lambda b, t: (b, 0, 2)),
                  pl.BlockSpec((1, LANES, seq), lambda b, t: (b, 2, 0)),
                  pl.BlockSpec((1, NSA_TQ, LANES), tile),
                  pl.BlockSpec((nslc, nc_pad), lambda b, t: (0, 0))],
        out_specs=pl.BlockSpec((1, NSA_TQ, NSA_WIDTH), tile),
        out_shape=jax.ShapeDtypeStruct((batch, seq, NSA_WIDTH), F32),
        scratch_shapes=[pltpu.VMEM((NSA_KV_HEADS, seq, 2 * D_HEAD), BF16),
                        pltpu.VMEM((NSA_KV_HEADS, seq, 2 * D_HEAD), BF16),
                        pltpu.VMEM((nchunk, LANES, NSA_TK), BF16), pltpu.VMEM((nchunk, LANES, NSA_TK), BF16)],
        compiler_params=_params("arbitrary", "arbitrary"),
        name="nsa_prompt",
    )(q3, kcmp, vcmp, kn3, vnt, kn3, vnt, g3, ovt)
    return out.reshape(batch * seq, NSA_WIDTH)


def _merge_kernel(oa_ref, ob_ref, mg_ref, x_ref, g1_ref, wa_ref, wb_ref, wo_ref, o_ref):
    a = _dot(oa_ref[...].astype(BF16), wa_ref[...])
    b = _dot(ob_ref[...].astype(BF16), wb_ref[...])
    mix = _sigmoid(mg_ref[:, :D_MODEL]) * a + _sigmoid(mg_ref[:, D_MODEL:]) * b
    o_ref[...] = x_ref[...] + g1_ref[0] * _dot(mix.astype(BF16), wo_ref[...])


def _merge(oa, ob, mg, x, g1, wa, wb, wo, tm):
    n = x.shape[0]
    nb, r, _ = g1.shape
    tiles_per_b = (n // nb) // tm
    row = lambda i: (i, 0)
    full = lambda i: (0, 0)
    return pl.pallas_call(
        _merge_kernel,
        grid=(n // tm,),
        in_specs=[pl.BlockSpec((tm, MOBA_WIDTH), row), pl.BlockSpec((tm, NSA_WIDTH), row),
                  pl.BlockSpec((tm, 2 * D_MODEL), row), pl.BlockSpec((tm, D_MODEL), row),
                  pl.BlockSpec((1, r, D_MODEL), lambda i: (i // tiles_per_b, 0, 0)),
                  pl.BlockSpec((MOBA_WIDTH, D_MODEL), full), pl.BlockSpec((NSA_WIDTH, D_MODEL), full),
                  pl.BlockSpec((D_MODEL, D_MODEL), full)],
        out_specs=pl.BlockSpec((tm, D_MODEL), row),
        out_shape=jax.ShapeDtypeStruct((n, D_MODEL), F32),
        compiler_params=_params("arbitrary"),
        name="merge",
    )(oa, ob, mg, x, g1, wa, wb, wo)


def _route(logits):
    lane = _iota(logits.shape, 1)
    is_grp = lane < N_GROUPS
    lg = jnp.where(is_grp, logits, NEG_INF)
    mg = jnp.max(lg, axis=-1, keepdims=True)
    pg = jnp.where(is_grp, jnp.exp(lg - mg), 0.0)
    pg = pg / jnp.sum(pg, axis=-1, keepdims=True)
    g_w = jnp.max(pg, axis=-1, keepdims=True)
    g_sel = jnp.min(jnp.where(is_grp & (pg == g_w), lane, LANES), axis=-1, keepdims=True)
    e_lane = lane - N_GROUPS
    in_grp = (e_lane >= 0) & (e_lane < N_EXPERTS) & ((e_lane >> 3) == g_sel)
    le = jnp.where(in_grp, logits, NEG_INF)
    me = jnp.max(le, axis=-1, keepdims=True)
    pe = jnp.where(in_grp, jnp.exp(le - me), 0.0)
    pe = pe / jnp.sum(pe, axis=-1, keepdims=True)
    v1 = jnp.max(pe, axis=-1, keepdims=True)
    i1 = jnp.min(jnp.where(in_grp & (pe == v1), lane, LANES), axis=-1, keepdims=True)
    rest = in_grp & (lane != i1)
    pr = jnp.where(rest, pe, -1.0)
    v2 = jnp.max(pr, axis=-1, keepdims=True)
    i2 = jnp.min(jnp.where(rest & (pr == v2), lane, LANES), axis=-1, keepdims=True)
    tot = v1 + v2
    comb = jnp.where(lane == i1, v1 / tot, 0.0) + jnp.where(lane == i2, v2 / tot, 0.0)
    comb = comb * g_w
    return pltpu.roll(comb, LANES - N_GROUPS, 1)


MOE_EXPERTS_PER_STEP = 8


def _moe_kernel(x_ref, sc_ref, sh_ref, g2_ref, gn_ref, gf_ref, wr_ref, win_ref, wout_ref, o_ref,
                h_scr, comb_scr, acc_scr):
    e = pl.program_id(1)

    @pl.when(e == 0)
    def _():
        h = _norm_mod(x_ref[...], gn_ref[...], sc_ref[0], sh_ref[0])
        hb = h.astype(BF16)
        h_scr[...] = hb
        h_lo = (h - hb.astype(F32)).astype(BF16)
        logits = _dot(hb, wr_ref[0]) + _dot(hb, wr_ref[1]) + _dot(h_lo, wr_ref[0])
        comb_scr[...] = _route(logits)
        acc_scr[...] = jnp.zeros_like(acc_scr)

    per = win_ref.shape[0]
    hb = h_scr[...]
    comb = comb_scr[...]
    lane = _iota(comb.shape, 1)
    acts = []
    for j in range(per):
        hid = _dot(hb, win_ref[j])
        a = hid[:, :D_EXPERT]
        b = hid[:, D_EXPERT:]
        w = jnp.sum(jnp.where(lane == e * per + j, comb, 0.0), axis=1, keepdims=True)
        acts.append((a * _sigmoid(a) * b * w).astype(BF16))
    act = jnp.concatenate(acts, axis=1)
    acc_scr[...] += _dot(act, wout_ref[...].reshape(per * D_EXPERT, D_MODEL))

    @pl.when(e == pl.num_programs(1) - 1)
    def _():
        y = x_ref[...] + g2_ref[0] * acc_scr[...]
        ms = jnp.mean(y * y, axis=-1, keepdims=True)
        o_ref[...] = y * lax.rsqrt(ms + EPS) * gf_ref[...]


def _moe(x, sc, sh, g2, gn, gf, wr, w_ein, w_eout, tm):
    n = x.shape[0]
    nb, r, _ = sc.shape
    tiles_per_b = (n // nb) // tm
    row = lambda i, e: (i, 0)
    mod = lambda i, e: (i // tiles_per_b, 0, 0)
    full = lambda i, e: (0, 0)
    per = MOE_EXPERTS_PER_STEP
    return pl.pallas_call(
        _moe_kernel,
        grid=(n // tm, N_EXPERTS // per),
        in_specs=[pl.BlockSpec((tm, D_MODEL), row),
                  pl.BlockSpec((1, r, D_MODEL), mod), pl.BlockSpec((1, r, D_MODEL), mod),
                  pl.BlockSpec((1, r, D_MODEL), mod),
                  pl.BlockSpec((1, D_MODEL), full), pl.BlockSpec((1, D_MODEL), full),
                  pl.BlockSpec((2, D_MODEL, LANES), lambda i, e: (0, 0, 0)),
                  pl.BlockSpec((per, D_MODEL, 2 * D_EXPERT), lambda i, e: (e, 0, 0)),
                  pl.BlockSpec((per, D_EXPERT, D_MODEL), lambda i, e: (e, 0, 0))],
        out_specs=pl.BlockSpec((tm, D_MODEL), row),
        out_shape=jax.ShapeDtypeStruct((n, D_MODEL), F32),
        scratch_shapes=[pltpu.VMEM((tm, D_MODEL), BF16), pltpu.VMEM((tm, LANES), F32),
                        pltpu.VMEM((tm, D_MODEL), F32)],
        compiler_params=_params("arbitrary", "arbitrary"),
        name="moe",
    )(x, sc, sh, g2, gn, gf, wr, w_ein, w_eout)


PAGE = 128
MOBA_PAGES_PER_STEP = 16
NSA_PAGES_PER_STEP = 16
COMPRESS_PAGES_PER_STEP = 32


def _page_view(cache):
    n_phys, page, heads, dh = cache.shape
    return jnp.transpose(cache, (0, 2, 3, 1)).reshape(n_phys, heads * dh, page)


PAGE_RING_DEPTH = 3


def _page_ring_step(pt_ref, streams, per):
    depth = PAGE_RING_DEPTH
    ns = pl.num_programs(1)
    step = pl.program_id(0) * ns + pl.program_id(1)
    last = pl.num_programs(0) * ns - 1

    def copies(step_idx):
        slot_idx = step_idx % depth
        return [pltpu.make_async_copy(hbm.at[pt_ref[step_idx * per + u]], buf.at[slot_idx, u], sem.at[slot_idx])
                for hbm, buf, sem in streams for u in range(per)]

    @pl.when(step == 0)
    def _():
        for j in range(depth - 1):
            @pl.when(j <= last)
            def _():
                for c in copies(j):
                    c.start()

    for c in copies(step):
        c.wait()

    @pl.when(step + (depth - 1) <= last)
    def _():
        for c in copies(step + (depth - 1)):
            c.start()

    return step % depth


def _head_diag(full, heads):
    rows = full.shape[0]
    head = _iota((rows, D_HEAD), 0) // (rows // heads)
    out = jnp.zeros((rows, D_HEAD), F32)
    for h in range(heads):
        out = out + jnp.where(head == h, full[:, h * D_HEAD:(h + 1) * D_HEAD], 0.0)
    return out


def _moba_s_kernel(pt_ref, qbd_ref, k_hbm, v_hbm, m_ref, l_ref, ks_ref, o_ref, kbuf, vbuf, ksem, vsem):
    per = MOBA_PAGES_PER_STEP
    s = pl.program_id(1)
    slot = _page_ring_step(pt_ref, [(k_hbm, kbuf, ksem), (v_hbm, vbuf, vsem)], per)

    @pl.when(s == 0)
    def _():
        m_ref[...] = jnp.zeros_like(m_ref)
        l_ref[...] = jnp.zeros_like(l_ref)
        ks_ref[...] = jnp.zeros_like(ks_ref)

    qbd = qbd_ref[0]
    lane_q = _iota(m_ref.shape[1:], 1)
    lane_k = _iota(ks_ref.shape[1:], 1)
    ppb = MOBA_BLOCK // PAGE
    m_all, l_all, ks_all = m_ref[0], l_ref[0], ks_ref[0]
    kts = [kbuf[slot, u] for u in range(per)]
    sc_all = _dot(qbd, jnp.concatenate([kt.astype(BF16) for kt in kts], axis=1))
    for j in range(per // ppb):
        blk = s * (per // ppb) + j
        sc = sc_all[:, j * MOBA_BLOCK:(j + 1) * MOBA_BLOCK]
        m = jnp.max(sc, axis=-1, keepdims=True)
        p = jnp.exp(sc - m)
        l = jnp.sum(p, axis=-1, keepdims=True)
        vt = jnp.concatenate([vbuf[slot, j * ppb + t].astype(BF16) for t in range(ppb)], axis=1)
        o_ref[0, j] = _head_diag(_dot_nt(p.astype(BF16), vt), MOBA_HEADS)
        kb = kts[j * ppb]
        for t in range(1, ppb):
            kb = kb + kts[j * ppb + t]
        ksum = jnp.sum(kb, axis=-1, keepdims=True)
        m_all = jnp.where(lane_q == blk, m, m_all)
        l_all = jnp.where(lane_q == blk, l, l_all)
        ks_all = jnp.where(lane_k == blk, ksum, ks_all)
    m_ref[0] = m_all
    l_ref[0] = l_all
    ks_ref[0] = ks_all


def _moba_s_pass(pt_flat, qbd, kt_pages, vt_pages, batch, n_pages):
    per = MOBA_PAGES_PER_STEP
    rows = qbd.shape[1]
    ppb = MOBA_BLOCK // PAGE
    stat = lambda b, s, pt: (b, 0, 0)
    return pl.pallas_call(
        _moba_s_kernel,
        grid_spec=pltpu.PrefetchScalarGridSpec(
            num_scalar_prefetch=1,
            grid=(batch, n_pages // per),
            in_specs=[pl.BlockSpec((1, rows, MOBA_WIDTH), stat),
                      pl.BlockSpec(memory_space=pl.ANY), pl.BlockSpec(memory_space=pl.ANY)],
            out_specs=[pl.BlockSpec((1, rows, LANES), stat), pl.BlockSpec((1, rows, LANES), stat),
                       pl.BlockSpec((1, MOBA_WIDTH, LANES), stat),
                       pl.BlockSpec((1, per // ppb, rows, D_HEAD), lambda b, s, pt: (b, s, 0, 0))],
            scratch_shapes=[pltpu.VMEM((PAGE_RING_DEPTH, per, MOBA_WIDTH, PAGE), F32),
                            pltpu.VMEM((PAGE_RING_DEPTH, per, MOBA_WIDTH, PAGE), F32),
                            pltpu.SemaphoreType.DMA((PAGE_RING_DEPTH,)), pltpu.SemaphoreType.DMA((PAGE_RING_DEPTH,))]),
        out_shape=[jax.ShapeDtypeStruct((batch, rows, LANES), F32), jax.ShapeDtypeStruct((batch, rows, LANES), F32),
                   jax.ShapeDtypeStruct((batch, MOBA_WIDTH, LANES), F32),
                   jax.ShapeDtypeStruct((batch, n_pages // ppb, rows, D_HEAD), F32)],
        compiler_params=_params("arbitrary", "arbitrary"),
        name="moba_decode_pages",
    )(pt_flat, qbd, kt_pages, vt_pages)


def _moba_s_combine_kernel(m_ref, l_ref, ks_ref, o_ref, qf_ref, qbd_ref, kn_ref, vn_ref, out_ref, *, n_pages, ts):
    rows = m_ref.shape[1]
    nblk = n_pages // (MOBA_BLOCK // PAGE)
    lane = _iota((rows, LANES), 1)
    kmean = ks_ref[0] * (1.0 / MOBA_BLOCK)
    gate = jnp.dot(qf_ref[0], kmean, precision=HIGHEST, preferred_element_type=F32)
    gate = jnp.where(lane < nblk, gate, NEG_INF)
    selp = _rank_select(gate, nblk, MOBA_TOPK) & (lane < nblk)
    qbd = qbd_ref[0]
    s_own = _dot_nt(qbd, kn_ref[0].astype(BF16))
    valid_own = lane <= (_iota((rows, LANES), 0) % ts)
    s_own = jnp.where(valid_own, s_own, NEG_INF)
    m_all = jnp.where(selp, m_ref[0], NEG_INF)
    big_m = jnp.maximum(jnp.max(m_all, axis=-1, keepdims=True), jnp.max(s_own, axis=-1, keepdims=True))
    wgt = jnp.where(selp, jnp.exp(m_ref[0] - big_m), 0.0)
    p_own = jnp.where(valid_own, jnp.exp(s_own - big_m), 0.0)
    denom = jnp.sum(wgt * l_ref[0], axis=-1, keepdims=True) + jnp.sum(p_own, axis=-1, keepdims=True)
    num = _head_diag(_dot(p_own.astype(BF16), vn_ref[0].astype(BF16)), MOBA_HEADS)
    for j in range(nblk):
        num = num + wgt[:, j:j + 1] * o_ref[0, j]
    out_ref[0] = num / jnp.maximum(denom, TINY)


def _moba_s_combine(m, l, ks, o, qf, qbd, kn, vn, n_pages, ts):
    batch, rows, _ = m.shape
    b3 = lambda b: (b, 0, 0)
    return pl.pallas_call(
        functools.partial(_moba_s_combine_kernel, n_pages=n_pages, ts=ts),
        grid=(batch,),
        in_specs=[pl.BlockSpec((1, rows, LANES), b3), pl.BlockSpec((1, rows, LANES), b3),
                  pl.BlockSpec((1, MOBA_WIDTH, LANES), b3),
                  pl.BlockSpec((1, o.shape[1], rows, D_HEAD), lambda b: (b, 0, 0, 0)),
                  pl.BlockSpec((1, rows, MOBA_WIDTH), b3), pl.BlockSpec((1, rows, MOBA_WIDTH), b3),
                  pl.BlockSpec((1, LANES, MOBA_WIDTH), b3), pl.BlockSpec((1, LANES, MOBA_WIDTH), b3)],
        out_specs=pl.BlockSpec((1, rows, D_HEAD), b3),
        out_shape=jax.ShapeDtypeStruct((batch, rows, D_HEAD), F32),
        compiler_params=_params("arbitrary"),
        name="moba_decode_combine",
    )(m, l, ks, o, qf, qbd, kn, vn)


def _block_diag_q(q, batch, ts, heads):
    q4 = q.reshape(batch, ts, heads, D_HEAD)
    eye = jnp.eye(heads, dtype=q.dtype)
    return jnp.einsum('bchd,hk->bhckd', q4, eye).reshape(batch, heads * ts, heads * D_HEAD)


def _pad_rows(a, batch, ts):
    a3 = a.reshape(batch, ts, a.shape[-1])
    return jnp.pad(a3, ((0, 0), (0, LANES - ts), (0, 0)))


def _compress_s_kernel(pt_ref, pages_hbm, pea_ref, peb_ref, wa_ref, wb_ref, w2_ref, o_ref,
                       buf, sem, x_scr, a_scr, b_scr):
    per = COMPRESS_PAGES_PER_STEP
    s = pl.program_id(1)
    nseg = per * PAGE // CMP_STRIDE
    slot = _page_ring_step(pt_ref, [(pages_hbm, buf, sem)], per)
    for u in range(per):
        x_scr[u * PAGE:(u + 1) * PAGE, :] = buf[slot, u].T
    xs = [x_scr[pl.ds(l, nseg, stride=CMP_STRIDE), :] for l in range(CMP_STRIDE)]
    xa = jnp.concatenate([(xs[l] + pea_ref[l:l + 1, :]).astype(BF16) for l in range(CMP_STRIDE)], axis=1)
    xb = jnp.concatenate([(xs[l] + peb_ref[l:l + 1, :]).astype(BF16) for l in range(CMP_STRIDE)], axis=1)
    a_scr[pl.ds(pl.multiple_of(s * nseg, nseg), nseg), :] = _dot(xa, wa_ref[...])
    b_scr[pl.ds(pl.multiple_of(s * nseg, nseg), nseg), :] = _dot(xb, wb_ref[...])

    @pl.when(s == pl.num_programs(1) - 1)
    def _():
        total = a_scr.shape[0]
        hid = a_scr[...] + pltpu.roll(b_scr[...], total - 1, 0)
        o_ref[0] = _dot(_gelu(hid).astype(BF16), w2_ref[...])


def _compress_s(pt_flat, pages, cw, batch, n_pages):
    wa, wb, pea, peb, w2bd = cw
    per = COMPRESS_PAGES_PER_STEP
    total = n_pages * PAGE // CMP_STRIDE
    hidden = NSA_KV_HEADS * CMP_HIDDEN
    width = CMP_STRIDE * NSA_KV_WIDTH
    full2 = lambda b, s, pt: (0, 0)
    return pl.pallas_call(
        _compress_s_kernel,
        grid_spec=pltpu.PrefetchScalarGridSpec(
            num_scalar_prefetch=1,
            grid=(batch, n_pages // per),
            in_specs=[pl.BlockSpec(memory_space=pl.ANY),
                      pl.BlockSpec((CMP_STRIDE, NSA_KV_WIDTH), full2), pl.BlockSpec((CMP_STRIDE, NSA_KV_WIDTH), full2),
                      pl.BlockSpec((width, hidden), full2), pl.BlockSpec((width, hidden), full2),
                      pl.BlockSpec((hidden, NSA_KV_WIDTH), full2)],
            out_specs=pl.BlockSpec((1, total, NSA_KV_WIDTH), lambda b, s, pt: (b, 0, 0)),
            scratch_shapes=[pltpu.VMEM((PAGE_RING_DEPTH, per, NSA_KV_WIDTH, PAGE), F32),
                            pltpu.SemaphoreType.DMA((PAGE_RING_DEPTH,)),
                            pltpu.VMEM((per * PAGE, NSA_KV_WIDTH), F32), pltpu.VMEM((total, hidden), F32),
                            pltpu.VMEM((total, hidden), F32)]),
        out_shape=jax.ShapeDtypeStruct((batch, total, NSA_KV_WIDTH), F32),
        compiler_params=_params("arbitrary", "arbitrary"),
        name="compress_decode",
    )(pt_flat, pages, pea, peb, wa.reshape(width, hidden), wb.reshape(width, hidden), w2bd)


def _stack_group_q(q_ref, g):
    return jnp.concatenate(
        [q_ref[0, :, (g * NSA_GROUP + r) * D_HEAD:(g * NSA_GROUP + r + 1) * D_HEAD] * SCALE
         for r in range(NSA_GROUP)], axis=0).astype(BF16)


def _nsa_s_kernel(q_ref, kc_ref, vc_ref, wk_ref, wv_ref, kn_ref, vn_ref, gate_ref, ov_ref, ex_ref, part_ref, sel_ref,
                  *, ts):
    rows = NSA_GROUP * ts
    nc_pad = kc_ref.shape[1]
    wlen = wk_ref.shape[3]
    gates = _sigmoid(gate_ref[0])
    n_i = _iota((rows, nc_pad), 1)
    valid_c = n_i < nc_pad - 1
    c_of_row = _iota((rows, 1), 0) % ts
    valid_w = _iota((rows, wlen), 1) >= c_of_row
    valid_n = _iota((rows, LANES), 1) <= c_of_row
    for g in range(NSA_KV_HEADS):
        lane0 = g * D_HEAD
        qs = _stack_group_q(q_ref, g)
        s = jnp.where(valid_c, _dot_nt(qs, kc_ref[0, :, lane0:lane0 + D_HEAD].astype(BF16)), NEG_INF)
        m = jnp.max(s, axis=-1, keepdims=True)
        p = jnp.where(valid_c, jnp.exp(s - m), 0.0)
        p = p / jnp.maximum(jnp.sum(p, axis=-1, keepdims=True), TINY)
        o_c = _dot(p.astype(BF16), vc_ref[0, :, lane0:lane0 + D_HEAD].astype(BF16))
        psum = jnp.sum(p.reshape(NSA_GROUP, ts, nc_pad), axis=0)
        imp = jnp.dot(psum, ov_ref[...], precision=HIGHEST, preferred_element_type=F32)
        keep = _rank_select(imp, LANES, SLC_TOPN - 1).astype(BF16)
        sel_ref[0, g] = jnp.where(_dot(keep, ex_ref[...]) > 0.5, 0.0, NEG_INF)
        s_w = jnp.where(valid_w, _dot(qs, wk_ref[0, g].astype(BF16)), NEG_INF)
        s_n = jnp.where(valid_n, _dot_nt(qs, kn_ref[0, :, lane0:lane0 + D_HEAD].astype(BF16)), NEG_INF)
        m = jnp.maximum(jnp.max(s_w, axis=-1, keepdims=True), jnp.max(s_n, axis=-1, keepdims=True))
        p_w = jnp.where(valid_w, jnp.exp(s_w - m), 0.0)
        p_n = jnp.where(valid_n, jnp.exp(s_n - m), 0.0)
        den = jnp.sum(p_w, axis=-1, keepdims=True) + jnp.sum(p_n, axis=-1, keepdims=True)
        o_w = (_dot_nt(p_w.astype(BF16), wv_ref[0, g].astype(BF16))
               + _dot(p_n.astype(BF16), vn_ref[0, :, lane0:lane0 + D_HEAD].astype(BF16))) / jnp.maximum(den, TINY)
        for r in range(NSA_GROUP):
            hd = g * NSA_GROUP + r
            rs = slice(r * ts, (r + 1) * ts)
            part_ref[0, :, hd * D_HEAD:(hd + 1) * D_HEAD] = (
                gates[:, 3 * hd:3 * hd + 1] * o_c[rs] + gates[:, 3 * hd + 2:3 * hd + 3] * o_w[rs])


def _nsa_s(q3, kcmp, vcmp, wk_t, wv_t, kn_w, vn_w, gate3, ts):
    batch = q3.shape[0]
    nc_pad = kcmp.shape[1]
    wlen = wk_t.shape[3]
    ov = _overlap_matrix(nc_pad, LANES)
    n_keys = LANES * SLC_BLOCK
    expand = (jnp.arange(LANES)[:, None] == jnp.arange(n_keys)[None, :] // SLC_BLOCK).astype(BF16)
    b3 = lambda b: (b, 0, 0)
    b4 = lambda b: (b, 0, 0, 0)
    return pl.pallas_call(
        functools.partial(_nsa_s_kernel, ts=ts),
        grid=(batch,),
        in_specs=[pl.BlockSpec((1, ts, NSA_WIDTH), b3),
                  pl.BlockSpec((1, nc_pad, NSA_KV_WIDTH), b3), pl.BlockSpec((1, nc_pad, NSA_KV_WIDTH), b3),
                  pl.BlockSpec((1, NSA_KV_HEADS, D_HEAD, wlen), b4), pl.BlockSpec((1, NSA_KV_HEADS, D_HEAD, wlen), b4),
                  pl.BlockSpec((1, LANES, NSA_KV_WIDTH), b3), pl.BlockSpec((1, LANES, NSA_KV_WIDTH), b3),
                  pl.BlockSpec((1, ts, LANES), b3),
                  pl.BlockSpec((nc_pad, LANES), lambda b: (0, 0)),
                  pl.BlockSpec((LANES, n_keys), lambda b: (0, 0))],
        out_specs=[pl.BlockSpec((1, ts, NSA_WIDTH), b3), pl.BlockSpec((1, NSA_KV_HEADS, ts, n_keys), b4)],
        out_shape=[jax.ShapeDtypeStruct((batch, ts, NSA_WIDTH), F32),
                   jax.ShapeDtypeStruct((batch, NSA_KV_HEADS, ts, n_keys), F32)],
        compiler_params=_params("arbitrary"),
        name="nsa_decode_cmp_win",
    )(q3, kcmp, vcmp, wk_t, wv_t, kn_w, vn_w, gate3, ov, expand)


def _slc_s_kernel(pt_ref, q_ref, bias_ref, k_hbm, v_hbm, kn_ref, vn_ref, gate_ref, part_ref, o_ref,
                  kbuf, vbuf, ksem, vsem, m_scr, l_scr, acc_scr, *, ts):
    per = NSA_PAGES_PER_STEP
    s = pl.program_id(1)
    slot = _page_ring_step(pt_ref, [(k_hbm, kbuf, ksem), (v_hbm, vbuf, vsem)], per)
    rows = NSA_GROUP * ts
    zero = jnp.zeros((rows, D_HEAD), BF16)
    q2 = jnp.concatenate([jnp.concatenate([_stack_group_q(q_ref, 0), zero], axis=1),
                          jnp.concatenate([zero, _stack_group_q(q_ref, 1)], axis=1)], axis=0)
    in_g0 = _iota((NSA_KV_HEADS * rows, D_HEAD), 0) < rows

    def own_group(full):
        return jnp.where(in_g0, full[:, :D_HEAD], full[:, D_HEAD:])

    @pl.when(s == 0)
    def _():
        valid_n = _iota((NSA_KV_HEADS * rows, LANES), 1) <= (_iota((NSA_KV_HEADS * rows, 1), 0) % ts)
        sc = jnp.where(valid_n, _dot_nt(q2, kn_ref[0].astype(BF16)), NEG_INF)
        m = jnp.max(sc, axis=-1, keepdims=True)
        p = jnp.exp(sc - m)
        m_scr[...] = m
        l_scr[...] = jnp.sum(p, axis=-1, keepdims=True)
        acc_scr[...] = own_group(_dot(p.astype(BF16), vn_ref[0].astype(BF16)))

    kt = jnp.concatenate([kbuf[slot, u].astype(BF16) for u in range(per)], axis=1)
    vt = jnp.concatenate([vbuf[slot, u].astype(BF16) for u in range(per)], axis=1)
    bias = jnp.concatenate([bias_ref[0, g] for g in range(NSA_KV_HEADS) for _ in range(NSA_GROUP)], axis=0)
    sc = _dot(q2, kt) + bias
    m_old = m_scr[...]
    m_new = jnp.maximum(m_old, jnp.max(sc, axis=-1, keepdims=True))
    pf = jnp.exp(sc - m_new)
    alpha = jnp.exp(m_old - m_new)
    l_scr[...] = alpha * l_scr[...] + jnp.sum(pf, axis=-1, keepdims=True)
    acc_scr[...] = alpha * acc_scr[...] + own_group(_dot_nt(pf.astype(BF16), vt))
    m_scr[...] = m_new

    @pl.when(s == pl.num_programs(1) - 1)
    def _():
        gates = _sigmoid(gate_ref[0])
        o_s = acc_scr[...] / jnp.maximum(l_scr[...], TINY)
        for hd in range(NSA_HEADS):
            cols = slice(hd * D_HEAD, (hd + 1) * D_HEAD)
            o_ref[0, :, cols] = part_ref[0, :, cols] + gates[:, 3 * hd + 1:3 * hd + 2] * o_s[hd * ts:(hd + 1) * ts]


def _slc_s(pt_flat, q3, sel_bias, k_pages, v_pages, kn_s, vn_s, gate3, part, n_pages, ts):
    batch = q3.shape[0]
    per = NSA_PAGES_PER_STEP
    rows = NSA_GROUP * ts
    b3 = lambda b, s, pt: (b, 0, 0)
    return pl.pallas_call(
        functools.partial(_slc_s_kernel, ts=ts),
        grid_spec=pltpu.PrefetchScalarGridSpec(
            num_scalar_prefetch=1,
            grid=(batch, n_pages // per),
            in_specs=[pl.BlockSpec((1, ts, NSA_WIDTH), b3),
                      pl.BlockSpec((1, NSA_KV_HEADS, ts, per * PAGE), lambda b, s, pt: (b, 0, 0, s)),
                      pl.BlockSpec(memory_space=pl.ANY), pl.BlockSpec(memory_space=pl.ANY),
                      pl.BlockSpec((1, LANES, NSA_KV_WIDTH), b3), pl.BlockSpec((1, LANES, NSA_KV_WIDTH), b3),
                      pl.BlockSpec((1, ts, LANES), b3), pl.BlockSpec((1, ts, NSA_WIDTH), b3)],
            out_specs=pl.BlockSpec((1, ts, NSA_WIDTH), b3),
            scratch_shapes=[pltpu.VMEM((PAGE_RING_DEPTH, per, NSA_KV_WIDTH, PAGE), F32),
                            pltpu.VMEM((PAGE_RING_DEPTH, per, NSA_KV_WIDTH, PAGE), F32),
                            pltpu.SemaphoreType.DMA((PAGE_RING_DEPTH,)), pltpu.SemaphoreType.DMA((PAGE_RING_DEPTH,)),
                            pltpu.VMEM((NSA_KV_HEADS * rows, 1), F32), pltpu.VMEM((NSA_KV_HEADS * rows, 1), F32),
                            pltpu.VMEM((NSA_KV_HEADS * rows, D_HEAD), F32)]),
        out_shape=jax.ShapeDtypeStruct((batch, ts, NSA_WIDTH), F32),
        compiler_params=_params("arbitrary", "arbitrary"),
        name="nsa_decode_slc",
    )(pt_flat, q3, sel_bias, k_pages, v_pages, kn_s, vn_s, gate3, part)


def _prep_weights(w_ada, b_ada, norm_mix_g, w_in, pe_cmp_k, w_cmp_k1, w_cmp_k2, pe_cmp_v, w_cmp_v1, w_cmp_v2,
                  w_br_a, w_br_b, w_out, norm_ffn_g, w_router_grp, w_router_exp, w_expert_in, w_expert_out,
                  norm_final_g):
    wr = jnp.concatenate([w_router_grp, w_router_exp], axis=1)
    wr = jnp.pad(wr, ((0, 0), (0, LANES - wr.shape[1])))
    wr_hi = wr.astype(BF16)
    wr = jnp.stack([wr_hi, (wr - wr_hi.astype(F32)).astype(BF16)])
    return dict(
        w_ada=w_ada, b_ada=b_ada.reshape(1, -1), g_mix=norm_mix_g.reshape(1, -1),
        w_in=_reorder_w_in(w_in),
        cmp_k=_compress_weights(pe_cmp_k, w_cmp_k1, w_cmp_k2),
        cmp_v=_compress_weights(pe_cmp_v, w_cmp_v1, w_cmp_v2),
        w_br_a=w_br_a.astype(BF16), w_br_b=w_br_b.astype(BF16), w_out=w_out.astype(BF16),
        g_ffn=norm_ffn_g.reshape(1, -1), wr=wr,
        w_ein=w_expert_in.astype(BF16), w_eout=w_expert_out.astype(BF16),
        g_final=norm_final_g.reshape(1, -1))


def _prompt_layer(x, mod, w, batch, seq):
    sh1, sc1, g1, sh2, sc2, g2 = mod
    cos, sin = _rope_tables(jnp.arange(seq, dtype=jnp.int32))
    qk_a, _, q_b, k_n, v_n, gate, mg, kt_a, vt_a, kt_n, vt_n = _inproj(
        x, sc1, sh1, w['g_mix'], cos, sin, w['w_in'], 256, transposed=True)
    o_a = _moba_p(qk_a, vt_a, batch, seq)
    kcmp = _compress_p(k_n[:, :NSA_KV_WIDTH], w['cmp_k'], batch, seq)
    vcmp = _compress_p(v_n[:, :NSA_KV_WIDTH], w['cmp_v'], batch, seq)
    o_b = _nsa_p(q_b, k_n, vt_n, kcmp, vcmp, gate, batch, seq)
    x1 = _merge(o_a, o_b, mg, x, g1, w['w_br_a'], w['w_br_b'], w['w_out'], 256)
    y = _moe(x1, sc2, sh2, g2, w['g_ffn'], w['g_final'], w['wr'], w['w_ein'], w['w_eout'], 512)
    return y, (kt_a, vt_a, kt_n, vt_n)


def _sample_layer(x, mod, w, caches, win_state, page_table, batch, ts):
    sh1, sc1, g1, sh2, sc2, g2 = mod
    moba_k, moba_v, cmp_k, cmp_v, slc_k, slc_v = caches
    win_k, win_v = win_state
    n_pages = page_table.shape[1]
    assert moba_k.shape[1] == PAGE and win_k.shape[1] == WINDOW and ts <= LANES
    assert n_pages * PAGE == LANES * SLC_BLOCK and n_pages * PAGE // MOBA_BLOCK <= LANES
    n = batch * ts
    pos = n_pages * PAGE + (jnp.arange(n, dtype=jnp.int32) % ts)
    cos, sin = _rope_tables(pos)
    qk_a, v_a, q_b, k_n, v_n, gate, mg = _inproj(x, sc1, sh1, w['g_mix'], cos, sin, w['w_in'], n)
    pt_flat = page_table.reshape(-1)
    kv = NSA_KV_WIDTH
    qf = _block_diag_q(qk_a[:, :MOBA_WIDTH], batch, ts, MOBA_HEADS)
    qbd = (qf * SCALE).astype(BF16)
    m, l, ks, o = _moba_s_pass(pt_flat, qbd, _page_view(moba_k), _page_view(moba_v), batch, n_pages)
    o_a = _moba_s_combine(m, l, ks, o, qf, qbd, _pad_rows(qk_a[:, MOBA_WIDTH:], batch, ts), _pad_rows(v_a, batch, ts),
                          n_pages, ts)
    o_a = o_a.reshape(batch, MOBA_HEADS, ts, D_HEAD).transpose(0, 2, 1, 3).reshape(n, MOBA_WIDTH)
    kcmp = _compress_s(pt_flat, _page_view(cmp_k), w['cmp_k'], batch, n_pages)
    vcmp = _compress_s(pt_flat, _page_view(cmp_v), w['cmp_v'], batch, n_pages)
    q3 = q_b.reshape(batch, ts, NSA_WIDTH)
    gate3 = gate.reshape(batch, ts, LANES)
    part, sel = _nsa_s(q3, kcmp, vcmp, jnp.transpose(win_k, (0, 2, 3, 1)), jnp.transpose(win_v, (0, 2, 3, 1)),
                       _pad_rows(k_n[:, 2 * kv:], batch, ts), _pad_rows(v_n[:, 2 * kv:], batch, ts), gate3, ts)
    o_b = _slc_s(pt_flat, q3, sel, _page_view(slc_k), _page_view(slc_v),
                 _pad_rows(k_n[:, kv:2 * kv], batch, ts), _pad_rows(v_n[:, kv:2 * kv], batch, ts), gate3, part,
                 n_pages, ts).reshape(n, NSA_WIDTH)
    x1 = _merge(o_a, o_b, mg, x, g1, w['w_br_a'], w['w_br_b'], w['w_out'], n)
    y = _moe(x1, sc2, sh2, g2, w['g_ffn'], w['g_final'], w['wr'], w['w_ein'], w['w_eout'], n)
    return y, (qk_a, v_a, k_n, v_n)


def kernel(x_prompt, x_sample, c_prompt, c_sample, cache_moba_k, cache_moba_v, cache_nsa_cmp_k, cache_nsa_cmp_v,
           cache_nsa_slc_k, cache_nsa_slc_v, state_nsa_win_k, state_nsa_win_v, page_table, w_ada, b_ada, norm_mix_g,
           w_in, pe_cmp_k, w_cmp_k1, w_cmp_k2, pe_cmp_v, w_cmp_v1, w_cmp_v2, w_br_a, w_br_b, w_out, norm_ffn_g,
           w_router_grp, w_router_exp, w_expert_in, w_expert_out, norm_final_g):
    bp, tp, _ = x_prompt.shape
    bs, ts, _ = x_sample.shape
    w = _prep_weights(w_ada[0], b_ada[0], norm_mix_g[0], w_in[0], pe_cmp_k[0], w_cmp_k1[0], w_cmp_k2[0], pe_cmp_v[0],
                      w_cmp_v1[0], w_cmp_v2[0], w_br_a[0], w_br_b[0], w_out[0], norm_ffn_g[0], w_router_grp[0],
                      w_router_exp[0], w_expert_in[0], w_expert_out[0], norm_final_g)
    mod = _ada(jnp.concatenate([c_prompt, c_sample], axis=0), w['w_ada'], w['b_ada'])
    mod_p = [m.reshape(bp, 1, D_MODEL) for m in jnp.split(mod[:bp], 6, axis=-1)]
    y_p, new_p = _prompt_layer(x_prompt.reshape(bp * tp, D_MODEL), mod_p, w, bp, tp)

    mod_s = [jnp.repeat(m, ts, axis=0).reshape(1, bs * ts, D_MODEL) for m in jnp.split(mod[bp:], 6, axis=-1)]
    caches = (cache_moba_k[0], cache_moba_v[0], cache_nsa_cmp_k[0], cache_nsa_cmp_v[0], cache_nsa_slc_k[0],
              cache_nsa_slc_v[0])
    y_s, new_s = _sample_layer(x_sample.reshape(bs * ts, D_MODEL), mod_s, w, caches,
                               (state_nsa_win_k[0], state_nsa_win_v[0]), page_table, bs, ts)

    kv = NSA_KV_WIDTH

    def new_rows(new, b, t):
        qk, v, k_n, v_n = new
        rows = lambda a, heads: a.reshape(1, b, t, heads, D_HEAD)
        return (rows(qk[:, MOBA_WIDTH:], MOBA_HEADS), rows(v, MOBA_HEADS),
                rows(k_n[:, :kv], NSA_KV_HEADS), rows(v_n[:, :kv], NSA_KV_HEADS),
                rows(k_n[:, kv:2 * kv], NSA_KV_HEADS), rows(v_n[:, kv:2 * kv], NSA_KV_HEADS),
                rows(k_n[:, 2 * kv:], NSA_KV_HEADS), rows(v_n[:, 2 * kv:], NSA_KV_HEADS))

    def new_rows_t(new, b, t):
        kt_a, vt_a, kt_n, vt_n = new
        rows = lambda a: jnp.transpose(a.reshape(1, b, a.shape[1] // D_HEAD, D_HEAD, a.shape[2]), (0, 1, 4, 2, 3))
        return (rows(kt_a), rows(vt_a), rows(kt_n[:, :kv]), rows(vt_n[:, :kv]),
                rows(kt_n[:, kv:2 * kv]), rows(vt_n[:, kv:2 * kv]),
                rows(kt_n[:, 2 * kv:, t - wb:]), rows(vt_n[:, 2 * kv:, t - wb:]))

    wb = state_nsa_win_k.shape[2]
    outs_p = new_rows_t(new_p, bp, tp)
    outs_s = new_rows(new_s, bs, ts)
    win_k = jnp.concatenate([state_nsa_win_k, outs_s[6]], axis=2)[:, :, ts:]
    win_v = jnp.concatenate([state_nsa_win_v, outs_s[7]], axis=2)[:, :, ts:]
    outs_s = outs_s[:6] + (win_k, win_v)
    return (y_p.reshape(bp, tp, D_MODEL), y_s.reshape(bs, ts, D_MODEL)) + outs_p + outs_s
```

```python
import functools

import jax
import jax.numpy as jnp
from jax import lax
from jax.experimental import pallas as pl
from jax.experimental.pallas import tpu as pltpu

D_MODEL = 1024
D_HEAD = 64
HALF = D_HEAD // 2
MOBA_HEADS = 8
MOBA_BLOCK = 256
MOBA_TOPK = 3
NSA_HEADS = 8
NSA_KV_HEADS = 2
NSA_GROUP = NSA_HEADS // NSA_KV_HEADS
CMP_LEN = 32
CMP_STRIDE = 16
CMP_HIDDEN = 128
SLC_BLOCK = 64
SLC_TOPN = 16
WINDOW = 512
N_GROUPS = 4
EXPERTS_PER_GROUP = 8
N_EXPERTS = N_GROUPS * EXPERTS_PER_GROUP
D_EXPERT = 256
ROPE_THETA = 10000.0
EPS = 1e-6
NEG_INF = -1e30
BIG = 1e30
TINY = 1e-30
MOBA_WIDTH = MOBA_HEADS * D_HEAD
NSA_WIDTH = NSA_HEADS * D_HEAD
NSA_KV_WIDTH = NSA_KV_HEADS * D_HEAD
SCALE = D_HEAD ** -0.5

LANES = 128
VMEM_LIMIT = 48 * 1024 * 1024

F32 = jnp.float32
BF16 = jnp.bfloat16
HIGHEST = lax.Precision.HIGHEST


def _params(*sem):
    return pltpu.CompilerParams(dimension_semantics=sem, vmem_limit_bytes=VMEM_LIMIT)


def _dot(a, b):
    return jnp.dot(a, b, preferred_element_type=F32)


def _dot_nt(a, b, precision=None):
    return lax.dot_general(a, b, (((1,), (1,)), ((), ())), precision=precision,
                           preferred_element_type=F32)


def _sigmoid(x):
    return 1.0 / (1.0 + jnp.exp(-x))


def _iota(shape, dim):
    return lax.broadcasted_iota(jnp.int32, shape, dim)


def _ada_kernel(c_ref, w_ref, b_ref, o_ref):
    c = c_ref[...]
    s = c * _sigmoid(c)
    o_ref[...] = _dot(s.astype(BF16), w_ref[...].astype(BF16)) + b_ref[...]


def _ada(c, w, b):
    n = c.shape[0]
    tn = 1024
    return pl.pallas_call(
        _ada_kernel,
        grid=(w.shape[1] // tn,),
        in_specs=[pl.BlockSpec((n, D_MODEL), lambda j: (0, 0)),
                  pl.BlockSpec((D_MODEL, tn), lambda j: (0, j)),
                  pl.BlockSpec((1, tn), lambda j: (0, j))],
        out_specs=pl.BlockSpec((n, tn), lambda j: (0, j)),
        out_shape=jax.ShapeDtypeStruct((n, w.shape[1]), F32),
        compiler_params=_params("arbitrary"),
        name="ada",
    )(c, w, b)


_IN_GROUPS = ((2 * MOBA_WIDTH, True),
              (MOBA_WIDTH, False),
              (NSA_WIDTH, True),
              (3 * NSA_KV_WIDTH, True),
              (3 * NSA_KV_WIDTH, False),
              (LANES, False),
              (2 * D_MODEL, False))
_IN_COLS_PAD = sum(w for w, _ in _IN_GROUPS)


def _norm_mod(x, g, sc, sh):
    ms = jnp.mean(x * x, axis=-1, keepdims=True)
    y = x * lax.rsqrt(ms + EPS) * g
    return y * (1.0 + sc) + sh


def _inproj_kernel(x_ref, sc_ref, sh_ref, g_ref, cos_ref, sin_ref, w_ref, *out_refs):
    h = _norm_mod(x_ref[...], g_ref[...], sc_ref[0], sh_ref[0]).astype(BF16)
    cos = cos_ref[...]
    sin = sin_ref[...]
    first_half = (_iota(cos.shape, 1) & (D_HEAD - 1)) < HALF

    def rope(y):
        rot = jnp.where(first_half, pltpu.roll(y, LANES - HALF, 1), pltpu.roll(y, HALF, 1))
        return y * cos + rot * sin

    n_groups = len(_IN_GROUPS)
    t_refs = dict(zip(_IN_T_GROUPS, out_refs[n_groups:]))
    col = 0
    for gi, (out_ref, (width, rotary)) in enumerate(zip(out_refs, _IN_GROUPS)):
        chunk = min(width, 512)
        for c in range(0, width, chunk):
            cw = min(chunk, width - c)
            y = _dot(h, w_ref[:, col + c:col + c + cw])
            for s in range(0, cw, LANES):
                piece = y[:, s:s + LANES]
                piece = rope(piece) if rotary else piece
                out_ref[:, c + s:c + s + LANES] = piece
                if gi in t_refs and c + s >= _IN_T_GROUPS[gi]:
                    t0 = c + s - _IN_T_GROUPS[gi]
                    t_refs[gi][0, t0:t0 + LANES, :] = piece.T
        col += width


_IN_T_GROUPS = {0: MOBA_WIDTH, 1: 0, 3: 0, 4: 0}


def _inproj(x, sc, sh, g, cos, sin, w, tm, transposed=False):
    n = x.shape[0]
    nb, r, _ = sc.shape
    tiles_per_b = (n // nb) // tm
    tab_tiles = cos.shape[0] // tm
    row = lambda i: (i, 0)
    mod = lambda i: (i // tiles_per_b, 0, 0)
    tab = lambda i: (i % tab_tiles, 0)
    out_specs = [pl.BlockSpec((tm, wd), row) for wd, _ in _IN_GROUPS]
    out_shape = [jax.ShapeDtypeStruct((n, wd), F32) for wd, _ in _IN_GROUPS]
    if transposed:
        for gi, first in _IN_T_GROUPS.items():
            cols = _IN_GROUPS[gi][0] - first
            out_specs.append(pl.BlockSpec((1, cols, tm), lambda i: (i // tiles_per_b, 0, i % tiles_per_b)))
            out_shape.append(jax.ShapeDtypeStruct((nb, cols, n // nb), F32))
    return pl.pallas_call(
        _inproj_kernel,
        grid=(n // tm,),
        in_specs=[pl.BlockSpec((tm, D_MODEL), row),
                  pl.BlockSpec((1, r, D_MODEL), mod),
                  pl.BlockSpec((1, r, D_MODEL), mod),
                  pl.BlockSpec((1, D_MODEL), lambda i: (0, 0)),
                  pl.BlockSpec((tm, LANES), tab),
                  pl.BlockSpec((tm, LANES), tab),
                  pl.BlockSpec((D_MODEL, _IN_COLS_PAD), lambda i: (0, 0))],
        out_specs=out_specs,
        out_shape=out_shape,
        compiler_params=_params("arbitrary"),
        name="inproj",
    )(x, sc, sh, g, cos, sin, w)


def _reorder_w_in(w_in):
    kv0 = 3 * MOBA_WIDTH + NSA_WIDTH
    kvs = [w_in[:, kv0 + i * NSA_KV_WIDTH:kv0 + (i + 1) * NSA_KV_WIDTH] for i in range(6)]
    g0 = kv0 + 6 * NSA_KV_WIDTH
    ng = 3 * NSA_HEADS
    gate = jnp.pad(w_in[:, g0:g0 + ng], ((0, 0), (0, LANES - ng)))
    parts = [w_in[:, :kv0], kvs[0], kvs[2], kvs[4], kvs[1], kvs[3], kvs[5], gate, w_in[:, g0 + ng:]]
    return jnp.concatenate(parts, axis=1).astype(BF16)


def _rope_tables(pos):
    inv = ROPE_THETA ** (-jnp.arange(HALF, dtype=F32) / HALF)
    ang = pos.astype(F32)[:, None] * inv[None, :]
    cos = jnp.cos(ang)
    sin = jnp.sin(ang)
    cos = jnp.concatenate([cos, cos, cos, cos], axis=1)
    sin = jnp.concatenate([-sin, sin, -sin, sin], axis=1)
    return cos, sin


def _rank_select(score, n_cols, n_keep):
    lane = _iota(score.shape, 1)
    rank = jnp.zeros(score.shape, jnp.int32)
    for jp in range(n_cols):
        col = score[:, jp:jp + 1]
        beats = (col > score) | ((col == score) & (lane > jp))
        rank = rank + beats.astype(jnp.int32)
    return rank < n_keep


def _rank_select_t(score, n_rows, n_keep):
    row = _iota(score.shape, 0)
    rank = jnp.zeros(score.shape, jnp.int32)
    for jp in range(n_rows):
        r = score[jp:jp + 1, :]
        beats = (r > score) | ((r == score) & (row > jp))
        rank = rank + beats.astype(jnp.int32)
    return rank < n_keep


SUBLANES = 8


def _fold_rows(x, op):
    return op(x.reshape(x.shape[0] // SUBLANES, SUBLANES, x.shape[1]), axis=0)


def _attend_t(score_fn, vt_fn, lo, hi, width, n_chains, cache_ref):
    def chunk_loop(step_fn, carry):
        n = hi - lo
        carry = lax.fori_loop(0, n // 2, lambda i, cr: step_fn(lo + 2 * i + 1, step_fn(lo + 2 * i, cr)), carry)
        return lax.cond(n % 2 == 1, lambda cr: step_fn(hi - 1, cr), lambda cr: cr, carry)

    def max_step(c, own, m):
        out = []
        for ch, (mi, s) in enumerate(zip(m, score_fn(c, own))):
            cache_ref[ch, c - lo] = s
            out.append(jnp.maximum(mi, _fold_rows(s, jnp.max)))
        return tuple(out)

    m = max_step(hi, True, tuple(jnp.full((SUBLANES, width), NEG_INF, F32) for _ in range(n_chains)))
    m = chunk_loop(lambda c, mm: max_step(c, False, mm), m)
    shift = [jnp.max(mi, axis=0, keepdims=True) for mi in m]

    def acc_step(c, carry):
        out = []
        for ch, ((l, acc), vt) in enumerate(zip(carry, vt_fn(c))):
            p = jnp.exp(cache_ref[ch, c - lo] - shift[ch])
            out.append((l + _fold_rows(p, jnp.sum), acc + _dot(vt, p.astype(BF16))))
        return tuple(out)

    zero = tuple((jnp.zeros((SUBLANES, width), F32), jnp.zeros((D_HEAD, width), F32)) for _ in range(n_chains))
    carry = acc_step(hi, zero)
    carry = chunk_loop(acc_step, carry)
    return [acc / jnp.maximum(jnp.sum(l, axis=0, keepdims=True), TINY) for l, acc in carry]


def _key_aug(n_keys, block, n_blocks):
    lane = _iota((n_keys, D_HEAD), 1)
    blk = _iota((n_keys, D_HEAD), 0) // block
    return jnp.where((lane < n_blocks) & (lane == blk), 1.0, 0.0).astype(BF16)


def _query_aug(qt, bias):
    pad = jnp.zeros((D_HEAD - bias.shape[0], qt.shape[1]), F32)
    return jnp.concatenate([qt, bias, pad], axis=0).astype(BF16)


MOBA_STEP_WIDTH = 512


def _moba_p_kernel(q_ref, k_ref, v_ref, o_ref, ka_scr, vt_scr, km_scr, s_scr):
    qi = pl.program_id(2)
    tq = tk = MOBA_BLOCK
    seq = k_ref.shape[1]
    nblk = seq // MOBA_BLOCK
    width = q_ref.shape[2]
    nh = width // D_HEAD

    @pl.when(qi == 0)
    def _():
        kf = k_ref[0]
        km_scr[...] = jnp.mean(kf.reshape(nblk, MOBA_BLOCK, width), axis=1)
        aug = _key_aug(seq, MOBA_BLOCK, nblk)
        for hh in range(nh):
            ka_scr[hh] = jnp.concatenate([kf[:, hh * D_HEAD:(hh + 1) * D_HEAD].astype(BF16), aug], axis=1)
        for j in range(nblk):
            vt_scr[j] = v_ref[0, :, j * tk:(j + 1) * tk].astype(BF16)

    q2t = q_ref[0].T
    km = km_scr[...]
    klane = _iota(km.shape, 1)
    blk = _iota((nblk, tq), 0)
    causal_t = _iota((tk, tq), 0) <= _iota((tk, tq), 1)
    qts, biases = [], []
    for hh in range(nh):
        kmh = jnp.where((klane >= hh * D_HEAD) & (klane < (hh + 1) * D_HEAD), km, 0.0)
        gate = jnp.dot(kmh, q2t, precision=HIGHEST, preferred_element_type=F32)
        gate = jnp.where(blk < qi, gate, NEG_INF)
        keep = _rank_select_t(gate, nblk, MOBA_TOPK) & (blk < qi)
        biases.append(jnp.where(keep, 0.0, NEG_INF))
        qts.append(q2t[hh * D_HEAD:(hh + 1) * D_HEAD, :] * SCALE)

    q_own = [_query_aug(qts[hh], jnp.zeros_like(biases[hh])) for hh in range(nh)]
    q_past = [_query_aug(qts[hh], biases[hh]) for hh in range(nh)]

    def scores(c, own):
        out = []
        for hh in range(nh):
            kj = ka_scr[hh, pl.ds(pl.multiple_of(c * tk, tk), tk), :]
            s = _dot(kj, q_own[hh] if own else q_past[hh])
            out.append(jnp.where(causal_t, s, NEG_INF) if own else s)
        return out

    def values(c):
        return [vt_scr[c, hh * D_HEAD:(hh + 1) * D_HEAD, :] for hh in range(nh)]

    o_ref[0] = jnp.concatenate(_attend_t(scores, values, 0, qi, tq, nh, s_scr), axis=0).T


def _moba_p(qk, vt, batch, seq):
    qk3 = qk.reshape(batch, seq, 2 * MOBA_WIDTH)
    sw = MOBA_STEP_WIDTH
    pairs = MOBA_WIDTH // sw
    nblk = seq // MOBA_BLOCK
    out = pl.pallas_call(
        _moba_p_kernel,
        grid=(batch, pairs, nblk),
        in_specs=[pl.BlockSpec((1, MOBA_BLOCK, sw), lambda b, h, i: (b, i, h)),
                  pl.BlockSpec((1, seq, sw), lambda b, h, i: (b, 0, pairs + h)),
                  pl.BlockSpec((1, sw, seq), lambda b, h, i: (b, h, 0))],
        out_specs=pl.BlockSpec((1, MOBA_BLOCK, sw), lambda b, h, i: (b, i, h)),
        out_shape=jax.ShapeDtypeStruct((batch, seq, MOBA_WIDTH), F32),
        scratch_shapes=[pltpu.VMEM((sw // D_HEAD, seq, 2 * D_HEAD), BF16), pltpu.VMEM((nblk, sw, MOBA_BLOCK), BF16),
                        pltpu.VMEM((nblk, sw), F32),
                        pltpu.VMEM((sw // D_HEAD, nblk, MOBA_BLOCK, MOBA_BLOCK), F32)],
        compiler_params=_params("arbitrary", "arbitrary", "arbitrary"),
        name="moba_prompt",
    )(qk3, qk3, vt)
    return out.reshape(batch * seq, MOBA_WIDTH)


def _gelu(x):
    return 0.5 * x * (1.0 + jnp.tanh(0.7978845608028654 * (x + 0.044715 * x * x * x)))


def _compress_p_kernel(seg_ref, pea_ref, peb_ref, wa_ref, wb_ref, w2_ref, o_ref):
    seg = seg_ref[0]
    a = _dot((seg + pea_ref[...]).astype(BF16), wa_ref[...])
    b = _dot((seg + peb_ref[...]).astype(BF16), wb_ref[...])
    nseg = seg.shape[0]
    hid = a + pltpu.roll(b, nseg - 1, 0)
    o_ref[0] = _dot(_gelu(hid).astype(BF16), w2_ref[...])


def _compress_weights(pe, w1, w2):
    g = NSA_KV_HEADS
    eye = jnp.eye(g, dtype=F32)
    w1r = w1.reshape(CMP_LEN, D_HEAD, CMP_HIDDEN)

    def half(lo):
        w = w1r[lo:lo + CMP_STRIDE]
        wbd = jnp.einsum('ldf,gh->lgdhf', w, eye)
        p = jnp.broadcast_to(pe[lo:lo + CMP_STRIDE, None, :], (CMP_STRIDE, g, D_HEAD))
        return wbd.reshape(CMP_STRIDE, g * D_HEAD, g * CMP_HIDDEN).astype(BF16), p.reshape(CMP_STRIDE, g * D_HEAD)

    wa, pea = half(0)
    wb, peb = half(CMP_STRIDE)
    w2bd = jnp.einsum('fd,gh->gfhd', w2, eye).reshape(g * CMP_HIDDEN, g * D_HEAD).astype(BF16)
    return wa, wb, pea, peb, w2bd


def _compress_p(rows, cw, batch, seq):
    wa, wb, pea, peb, w2bd = cw
    nseg = seq // CMP_STRIDE
    width = CMP_STRIDE * NSA_KV_WIDTH
    seg = rows.reshape(batch, nseg, width)
    full = lambda b: (0, 0)
    return pl.pallas_call(
        _compress_p_kernel,
        grid=(batch,),
        in_specs=[pl.BlockSpec((1, nseg, width), lambda b: (b, 0, 0)),
                  pl.BlockSpec((1, width), full), pl.BlockSpec((1, width), full),
                  pl.BlockSpec((width, NSA_KV_HEADS * CMP_HIDDEN), full),
                  pl.BlockSpec((width, NSA_KV_HEADS * CMP_HIDDEN), full),
                  pl.BlockSpec((NSA_KV_HEADS * CMP_HIDDEN, NSA_KV_WIDTH), full)],
        out_specs=pl.BlockSpec((1, nseg, NSA_KV_WIDTH), lambda b: (b, 0, 0)),
        out_shape=jax.ShapeDtypeStruct((batch, nseg, NSA_KV_WIDTH), F32),
        compiler_params=_params("arbitrary"),
        name="compress_prompt",
    )(seg, pea.reshape(1, width), peb.reshape(1, width), wa.reshape(width, -1), wb.reshape(width, -1), w2bd)


NSA_TQ = 256
NSA_TK = 256


def _overlap_matrix(nc_pad, nslc_pad):
    cs = jnp.arange(nc_pad)[:, None] * CMP_STRIDE
    ss = jnp.arange(nslc_pad)[None, :] * SLC_BLOCK
    return ((cs < ss + SLC_BLOCK) & (cs + CMP_LEN > ss)).astype(F32)


def _nsa_p_kernel(q_ref, kc_ref, vc_ref, ks_ref, vs_ref, kw_ref, vw_ref, gate_ref, ovt_ref, o_ref,
                  ksa_scr, kwb_scr, vst_scr, vwt_scr, ss_scr, sw_scr):
    tq, tk = NSA_TQ, NSA_TK
    t = pl.program_id(1)
    q0 = t * tq
    seq = ks_ref.shape[1]
    nc_pad = kc_ref.shape[1]
    nslc = seq // SLC_BLOCK
    width = NSA_GROUP * tq

    @pl.when(t == 0)
    def _():
        aug_s = _key_aug(seq, SLC_BLOCK, nslc)
        for g in range(NSA_KV_HEADS):
            rows = slice(g * D_HEAD, (g + 1) * D_HEAD)
            ksa_scr[g] = jnp.concatenate([ks_ref[0, :, rows].astype(BF16), aug_s], axis=1)
        kwb_scr[...] = kw_ref[0].astype(BF16)
        for j in range(seq // tk):
            vst_scr[j] = vs_ref[0, :, j * tk:(j + 1) * tk].astype(BF16)
            vwt_scr[j] = vw_ref[0, :, j * tk:(j + 1) * tk].astype(BF16)

    qt_all = q_ref[0].T
    gates_t = _sigmoid(gate_ref[0]).T
    kct = kc_ref[0].astype(BF16)
    vct = vc_ref[0].T.astype(BF16)
    pos = q0 + _iota((1, tq), 1)
    n_i = _iota((nc_pad, tq), 0)
    valid_c = (n_i < nc_pad - 1) & (n_i * CMP_STRIDE + (CMP_LEN - 1) <= pos)
    valid_c4 = jnp.concatenate([valid_c] * NSA_GROUP, axis=1)
    q_blk = pos >> 6
    jrow = _iota((nslc, tq), 0)
    krow = _iota((tk, tq), 0)
    cd = q0 // tk
    c_win = jnp.maximum(q0 - WINDOW, 0) // tk

    def tile(b):
        return jnp.concatenate([b] * NSA_GROUP, axis=1)

    def win_bias(c):
        dist = pos - (c * tk + krow)
        return jnp.where((dist >= 0) & (dist <= WINDOW), 0.0, NEG_INF)

    qts, o_cs, biases = [], [], []
    for g in range(NSA_KV_HEADS):
        rows = slice(g * D_HEAD, (g + 1) * D_HEAD)
        qt = jnp.concatenate(
            [qt_all[(g * NSA_GROUP + r) * D_HEAD:(g * NSA_GROUP + r + 1) * D_HEAD, :] * SCALE
             for r in range(NSA_GROUP)], axis=1)
        qts.append(qt)
        s = jnp.where(valid_c4, _dot(kct[:, rows], qt.astype(BF16)), NEG_INF)
        m = jnp.max(s, axis=0, keepdims=True)
        p = jnp.where(valid_c4, jnp.exp(s - m), 0.0)
        p = p / jnp.maximum(jnp.sum(p, axis=0, keepdims=True), TINY)
        o_cs.append(_dot(vct[rows, :], p.astype(BF16)))
        psum = p[:, :tq]
        for r in range(1, NSA_GROUP):
            psum = psum + p[:, r * tq:(r + 1) * tq]
        imp = jnp.dot(ovt_ref[...], psum, precision=HIGHEST, preferred_element_type=F32)
        imp = jnp.where(jrow == q_blk, BIG, jnp.where(jrow < q_blk, imp, NEG_INF))
        keep = _rank_select_t(imp, nslc, SLC_TOPN) & (jrow <= q_blk)
        biases.append(tile(jnp.where(keep, 0.0, NEG_INF)))

    causal_own = tile(cd * tk + krow <= pos)

    qa_slc = [_query_aug(qts[g], biases[g]) for g in range(NSA_KV_HEADS)]
    qb_win = [qts[g].astype(BF16) for g in range(NSA_KV_HEADS)]

    def slc_scores(c, own):
        out = []
        for g in range(NSA_KV_HEADS):
            s = _dot(ksa_scr[g, pl.ds(pl.multiple_of(c * tk, tk), tk), :], qa_slc[g])
            out.append(jnp.where(causal_own, s, NEG_INF) if own else s)
        return out

    def win_scores(c, own):
        wb = tile(win_bias(c))
        return [_dot(kwb_scr[pl.ds(pl.multiple_of(c * tk, tk), tk), g * D_HEAD:(g + 1) * D_HEAD], qb_win[g]) + wb
                for g in range(NSA_KV_HEADS)]

    def values(vt_scr):
        return lambda c: [vt_scr[c, g * D_HEAD:(g + 1) * D_HEAD, :] for g in range(NSA_KV_HEADS)]

    o_ss = _attend_t(slc_scores, values(vst_scr), 0, cd, width, NSA_KV_HEADS, ss_scr)
    o_ws = _attend_t(win_scores, values(vwt_scr), c_win, cd, width, NSA_KV_HEADS, sw_scr)
    outs = []
    for g in range(NSA_KV_HEADS):
        o_c, o_s, o_w = o_cs[g], o_ss[g], o_ws[g]
        for r in range(NSA_GROUP):
            hd = g * NSA_GROUP + r
            cs = slice(r * tq, (r + 1) * tq)
            outs.append(gates_t[3 * hd:3 * hd + 1, :] * o_c[:, cs] + gates_t[3 * hd + 1:3 * hd + 2, :] * o_s[:, cs]
                        + gates_t[3 * hd + 2:3 * hd + 3, :] * o_w[:, cs])
    o_ref[0] = jnp.concatenate(outs, axis=0).T


def _nsa_p(q, kn, vnt, kcmp, vcmp, gate, batch, seq):
    q3 = q.reshape(batch, seq, NSA_WIDTH)
    kn3 = kn.reshape(batch, seq, 3 * NSA_KV_WIDTH)
    g3 = gate.reshape(batch, seq, LANES)
    nc_pad = kcmp.shape[1]
    nslc = seq // SLC_BLOCK
    nchunk = seq // NSA_TK
    ovt = _overlap_matrix(nc_pad, nslc).T
    tile = lambda b, t: (b, t, 0)
    cmp_spec = pl.BlockSpec((1, nc_pad, NSA_KV_WIDTH), lambda b, t: (b, 0, 0))
    out = pl.pallas_call(
        _nsa_p_kernel,
        grid=(batch, seq // NSA_TQ),
        in_specs=[pl.BlockSpec((1, NSA_TQ, NSA_WIDTH), tile), cmp_spec, cmp_spec,
                  pl.BlockSpec((1, seq, LANES), lambda b, t: (b, 0, 1)),
                  pl.BlockSpec((1, LANES, seq), lambda b, t: (b, 1, 0)),
                  pl.BlockSpec((1, seq, LANES), lambda b, t: (b, 0, 2)),
                  pl.BlockSpec((1, LANES, seq), lambda b, t: (b, 2, 0)),
                  pl.BlockSpec((1, NSA_TQ, LANES), tile),
                  pl.BlockSpec((nslc, nc_pad), lambda b, t: (0, 0))],
        out_specs=pl.BlockSpec((1, NSA_TQ, NSA_WIDTH), tile),
        out_shape=jax.ShapeDtypeStruct((batch, seq, NSA_WIDTH), F32),
        scratch_shapes=[pltpu.VMEM((NSA_KV_HEADS, seq, 2 * D_HEAD), BF16), pltpu.VMEM((seq, LANES), BF16),
                        pltpu.VMEM((nchunk, LANES, NSA_TK), BF16), pltpu.VMEM((nchunk, LANES, NSA_TK), BF16),
                        pltpu.VMEM((NSA_KV_HEADS, nchunk, NSA_TK, NSA_GROUP * NSA_TQ), F32),
                        pltpu.VMEM((NSA_KV_HEADS, WINDOW // NSA_TK + 1, NSA_TK, NSA_GROUP * NSA_TQ), F32)],
        compiler_params=_params("arbitrary", "arbitrary"),
        name="nsa_prompt",
    )(q3, kcmp, vcmp, kn3, vnt, kn3, vnt, g3, ovt)
    return out.reshape(batch * seq, NSA_WIDTH)


def _merge_kernel(oa_ref, ob_ref, mg_ref, x_ref, g1_ref, wa_ref, wb_ref, wo_ref, o_ref):
    a = _dot(oa_ref[...].astype(BF16), wa_ref[...])
    b = _dot(ob_ref[...].astype(BF16), wb_ref[...])
    mix = _sigmoid(mg_ref[:, :D_MODEL]) * a + _sigmoid(mg_ref[:, D_MODEL:]) * b
    o_ref[...] = x_ref[...] + g1_ref[0] * _dot(mix.astype(BF16), wo_ref[...])


def _merge(oa, ob, mg, x, g1, wa, wb, wo, tm):
    n = x.shape[0]
    nb, r, _ = g1.shape
    tiles_per_b = (n // nb) // tm
    row = lambda i: (i, 0)
    full = lambda i: (0, 0)
    return pl.pallas_call(
        _merge_kernel,
        grid=(n // tm,),
        in_specs=[pl.BlockSpec((tm, MOBA_WIDTH), row), pl.BlockSpec((tm, NSA_WIDTH), row),
                  pl.BlockSpec((tm, 2 * D_MODEL), row), pl.BlockSpec((tm, D_MODEL), row),
                  pl.BlockSpec((1, r, D_MODEL), lambda i: (i // tiles_per_b, 0, 0)),
                  pl.BlockSpec((MOBA_WIDTH, D_MODEL), full), pl.BlockSpec((NSA_WIDTH, D_MODEL), full),
                  pl.BlockSpec((D_MODEL, D_MODEL), full)],
        out_specs=pl.BlockSpec((tm, D_MODEL), row),
        out_shape=jax.ShapeDtypeStruct((n, D_MODEL), F32),
        compiler_params=_params("arbitrary"),
        name="merge",
    )(oa, ob, mg, x, g1, wa, wb, wo)


def _route(logits):
    lane = _iota(logits.shape, 1)
    is_grp = lane < N_GROUPS
    lg = jnp.where(is_grp, logits, NEG_INF)
    mg = jnp.max(lg, axis=-1, keepdims=True)
    pg = jnp.where(is_grp, jnp.exp(lg - mg), 0.0)
    pg = pg / jnp.sum(pg, axis=-1, keepdims=True)
    g_w = jnp.max(pg, axis=-1, keepdims=True)
    g_sel = jnp.min(jnp.where(is_grp & (pg == g_w), lane, LANES), axis=-1, keepdims=True)
    e_lane = lane - N_GROUPS
    in_grp = (e_lane >= 0) & (e_lane < N_EXPERTS) & ((e_lane >> 3) == g_sel)
    le = jnp.where(in_grp, logits, NEG_INF)
    me = jnp.max(le, axis=-1, keepdims=True)
    pe = jnp.where(in_grp, jnp.exp(le - me), 0.0)
    pe = pe / jnp.sum(pe, axis=-1, keepdims=True)
    v1 = jnp.max(pe, axis=-1, keepdims=True)
    i1 = jnp.min(jnp.where(in_grp & (pe == v1), lane, LANES), axis=-1, keepdims=True)
    rest = in_grp & (lane != i1)
    pr = jnp.where(rest, pe, -1.0)
    v2 = jnp.max(pr, axis=-1, keepdims=True)
    i2 = jnp.min(jnp.where(rest & (pr == v2), lane, LANES), axis=-1, keepdims=True)
    tot = v1 + v2
    comb = jnp.where(lane == i1, v1 / tot, 0.0) + jnp.where(lane == i2, v2 / tot, 0.0)
    comb = comb * g_w
    return pltpu.roll(comb, LANES - N_GROUPS, 1)


MOE_EXPERTS_PER_STEP = 8


def _moe_kernel(x_ref, sc_ref, sh_ref, g2_ref, gn_ref, gf_ref, wr_ref, win_ref, wout_ref, o_ref,
                h_scr, comb_scr, acc_scr):
    e = pl.program_id(1)

    @pl.when(e == 0)
    def _():
        h = _norm_mod(x_ref[...], gn_ref[...], sc_ref[0], sh_ref[0])
        hb = h.astype(BF16)
        h_scr[...] = hb
        h_lo = (h - hb.astype(F32)).astype(BF16)
        logits = _dot(hb, wr_ref[0]) + _dot(hb, wr_ref[1]) + _dot(h_lo, wr_ref[0])
        comb_scr[...] = _route(logits)
        acc_scr[...] = jnp.zeros_like(acc_scr)

    per = win_ref.shape[0]
    hb = h_scr[...]
    comb = comb_scr[...]
    lane = _iota(comb.shape, 1)
    acts = []
    for j in range(per):
        hid = _dot(hb, win_ref[j])
        a = hid[:, :D_EXPERT]
        b = hid[:, D_EXPERT:]
        w = jnp.sum(jnp.where(lane == e * per + j, comb, 0.0), axis=1, keepdims=True)
        acts.append((a * _sigmoid(a) * b * w).astype(BF16))
    act = jnp.concatenate(acts, axis=1)
    acc_scr[...] += _dot(act, wout_ref[...].reshape(per * D_EXPERT, D_MODEL))

    @pl.when(e == pl.num_programs(1) - 1)
    def _():
        y = x_ref[...] + g2_ref[0] * acc_scr[...]
        ms = jnp.mean(y * y, axis=-1, keepdims=True)
        o_ref[...] = y * lax.rsqrt(ms + EPS) * gf_ref[...]


def _moe(x, sc, sh, g2, gn, gf, wr, w_ein, w_eout, tm):
    n = x.shape[0]
    nb, r, _ = sc.shape
    tiles_per_b = (n // nb) // tm
    row = lambda i, e: (i, 0)
    mod = lambda i, e: (i // tiles_per_b, 0, 0)
    full = lambda i, e: (0, 0)
    per = MOE_EXPERTS_PER_STEP
    return pl.pallas_call(
        _moe_kernel,
        grid=(n // tm, N_EXPERTS // per),
        in_specs=[pl.BlockSpec((tm, D_MODEL), row),
                  pl.BlockSpec((1, r, D_MODEL), mod), pl.BlockSpec((1, r, D_MODEL), mod),
                  pl.BlockSpec((1, r, D_MODEL), mod),
                  pl.BlockSpec((1, D_MODEL), full), pl.BlockSpec((1, D_MODEL), full),
                  pl.BlockSpec((2, D_MODEL, LANES), lambda i, e: (0, 0, 0)),
                  pl.BlockSpec((per, D_MODEL, 2 * D_EXPERT), lambda i, e: (e, 0, 0)),
                  pl.BlockSpec((per, D_EXPERT, D_MODEL), lambda i, e: (e, 0, 0))],
        out_specs=pl.BlockSpec((tm, D_MODEL), row),
        out_shape=jax.ShapeDtypeStruct((n, D_MODEL), F32),
        scratch_shapes=[pltpu.VMEM((tm, D_MODEL), BF16), pltpu.VMEM((tm, LANES), F32),
                        pltpu.VMEM((tm, D_MODEL), F32)],
        compiler_params=_params("arbitrary", "arbitrary"),
        name="moe",
    )(x, sc, sh, g2, gn, gf, wr, w_ein, w_eout)


PAGE = 128
MOBA_PAGES_PER_STEP = 16
NSA_PAGES_PER_STEP = 16
COMPRESS_PAGES_PER_STEP = 32


def _page_view(cache):
    n_phys, page, heads, dh = cache.shape
    return jnp.transpose(cache, (0, 2, 3, 1)).reshape(n_phys, heads * dh, page)


PAGE_RING_DEPTH = 3


def _page_ring_step(pt_ref, streams, per):
    depth = PAGE_RING_DEPTH
    ns = pl.num_programs(1)
    step = pl.program_id(0) * ns + pl.program_id(1)
    last = pl.num_programs(0) * ns - 1

    def copies(step_idx):
        slot_idx = step_idx % depth
        return [pltpu.make_async_copy(hbm.at[pt_ref[step_idx * per + u]], buf.at[slot_idx, u], sem.at[slot_idx])
                for hbm, buf, sem in streams for u in range(per)]

    @pl.when(step == 0)
    def _():
        for j in range(depth - 1):
            @pl.when(j <= last)
            def _():
                for c in copies(j):
                    c.start()

    for c in copies(step):
        c.wait()

    @pl.when(step + (depth - 1) <= last)
    def _():
        for c in copies(step + (depth - 1)):
            c.start()

    return step % depth


def _head_diag(full, heads):
    rows = full.shape[0]
    head = _iota((rows, D_HEAD), 0) // (rows // heads)
    out = jnp.zeros((rows, D_HEAD), F32)
    for h in range(heads):
        out = out + jnp.where(head == h, full[:, h * D_HEAD:(h + 1) * D_HEAD], 0.0)
    return out


def _moba_s_kernel(pt_ref, qbd_ref, k_hbm, v_hbm, m_ref, l_ref, ks_ref, o_ref, kbuf, vbuf, ksem, vsem):
    per = MOBA_PAGES_PER_STEP
    s = pl.program_id(1)
    slot = _page_ring_step(pt_ref, [(k_hbm, kbuf, ksem), (v_hbm, vbuf, vsem)], per)

    @pl.when(s == 0)
    def _():
        m_ref[...] = jnp.zeros_like(m_ref)
        l_ref[...] = jnp.zeros_like(l_ref)
        ks_ref[...] = jnp.zeros_like(ks_ref)

    qbd = qbd_ref[0]
    lane_q = _iota(m_ref.shape[1:], 1)
    lane_k = _iota(ks_ref.shape[1:], 1)
    ppb = MOBA_BLOCK // PAGE
    m_all, l_all, ks_all = m_ref[0], l_ref[0], ks_ref[0]
    kts = [kbuf[slot, u] for u in range(per)]
    sc_all = _dot(qbd, jnp.concatenate([kt.astype(BF16) for kt in kts], axis=1))
    for j in range(per // ppb):
        blk = s * (per // ppb) + j
        sc = sc_all[:, j * MOBA_BLOCK:(j + 1) * MOBA_BLOCK]
        m = jnp.max(sc, axis=-1, keepdims=True)
        p = jnp.exp(sc - m)
        l = jnp.sum(p, axis=-1, keepdims=True)
        vt = jnp.concatenate([vbuf[slot, j * ppb + t].astype(BF16) for t in range(ppb)], axis=1)
        o_ref[0, j] = _head_diag(_dot_nt(p.astype(BF16), vt), MOBA_HEADS)
        kb = kts[j * ppb]
        for t in range(1, ppb):
            kb = kb + kts[j * ppb + t]
        ksum = jnp.sum(kb, axis=-1, keepdims=True)
        m_all = jnp.where(lane_q == blk, m, m_all)
        l_all = jnp.where(lane_q == blk, l, l_all)
        ks_all = jnp.where(lane_k == blk, ksum, ks_all)
    m_ref[0] = m_all
    l_ref[0] = l_all
    ks_ref[0] = ks_all


def _moba_s_pass(pt_flat, qbd, kt_pages, vt_pages, batch, n_pages):
    per = MOBA_PAGES_PER_STEP
    rows = qbd.shape[1]
    ppb = MOBA_BLOCK // PAGE
    stat = lambda b, s, pt: (b, 0, 0)
    return pl.pallas_call(
        _moba_s_kernel,
        grid_spec=pltpu.PrefetchScalarGridSpec(
            num_scalar_prefetch=1,
            grid=(batch, n_pages // per),
            in_specs=[pl.BlockSpec((1, rows, MOBA_WIDTH), stat),
                      pl.BlockSpec(memory_space=pl.ANY), pl.BlockSpec(memory_space=pl.ANY)],
            out_specs=[pl.BlockSpec((1, rows, LANES), stat), pl.BlockSpec((1, rows, LANES), stat),
                       pl.BlockSpec((1, MOBA_WIDTH, LANES), stat),
                       pl.BlockSpec((1, per // ppb, rows, D_HEAD), lambda b, s, pt: (b, s, 0, 0))],
            scratch_shapes=[pltpu.VMEM((PAGE_RING_DEPTH, per, MOBA_WIDTH, PAGE), F32),
                            pltpu.VMEM((PAGE_RING_DEPTH, per, MOBA_WIDTH, PAGE), F32),
                            pltpu.SemaphoreType.DMA((PAGE_RING_DEPTH,)), pltpu.SemaphoreType.DMA((PAGE_RING_DEPTH,))]),
        out_shape=[jax.ShapeDtypeStruct((batch, rows, LANES), F32), jax.ShapeDtypeStruct((batch, rows, LANES), F32),
                   jax.ShapeDtypeStruct((batch, MOBA_WIDTH, LANES), F32),
                   jax.ShapeDtypeStruct((batch, n_pages // ppb, rows, D_HEAD), F32)],
        compiler_params=_params("arbitrary", "arbitrary"),
        name="moba_decode_pages",
    )(pt_flat, qbd, kt_pages, vt_pages)


def _moba_s_combine_kernel(m_ref, l_ref, ks_ref, o_ref, qf_ref, qbd_ref, kn_ref, vn_ref, out_ref, *, n_pages, ts):
    rows = m_ref.shape[1]
    nblk = n_pages // (MOBA_BLOCK // PAGE)
    lane = _iota((rows, LANES), 1)
    kmean = ks_ref[0] * (1.0 / MOBA_BLOCK)
    gate = jnp.dot(qf_ref[0], kmean, precision=HIGHEST, preferred_element_type=F32)
    gate = jnp.where(lane < nblk, gate, NEG_INF)
    selp = _rank_select(gate, nblk, MOBA_TOPK) & (lane < nblk)
    qbd = qbd_ref[0]
    s_own = _dot_nt(qbd, kn_ref[0].astype(BF16))
    valid_own = lane <= (_iota((rows, LANES), 0) % ts)
    s_own = jnp.where(valid_own, s_own, NEG_INF)
    m_all = jnp.where(selp, m_ref[0], NEG_INF)
    big_m = jnp.maximum(jnp.max(m_all, axis=-1, keepdims=True), jnp.max(s_own, axis=-1, keepdims=True))
    wgt = jnp.where(selp, jnp.exp(m_ref[0] - big_m), 0.0)
    p_own = jnp.where(valid_own, jnp.exp(s_own - big_m), 0.0)
    denom = jnp.sum(wgt * l_ref[0], axis=-1, keepdims=True) + jnp.sum(p_own, axis=-1, keepdims=True)
    num = _head_diag(_dot(p_own.astype(BF16), vn_ref[0].astype(BF16)), MOBA_HEADS)
    for j in range(nblk):
        num = num + wgt[:, j:j + 1] * o_ref[0, j]
    out_ref[0] = num / jnp.maximum(denom, TINY)


def _moba_s_combine(m, l, ks, o, qf, qbd, kn, vn, n_pages, ts):
    batch, rows, _ = m.shape
    b3 = lambda b: (b, 0, 0)
    return pl.pallas_call(
        functools.partial(_moba_s_combine_kernel, n_pages=n_pages, ts=ts),
        grid=(batch,),
        in_specs=[pl.BlockSpec((1, rows, LANES), b3), pl.BlockSpec((1, rows, LANES), b3),
                  pl.BlockSpec((1, MOBA_WIDTH, LANES), b3),
                  pl.BlockSpec((1, o.shape[1], rows, D_HEAD), lambda b: (b, 0, 0, 0)),
                  pl.BlockSpec((1, rows, MOBA_WIDTH), b3), pl.BlockSpec((1, rows, MOBA_WIDTH), b3),
                  pl.BlockSpec((1, LANES, MOBA_WIDTH), b3), pl.BlockSpec((1, LANES, MOBA_WIDTH), b3)],
        out_specs=pl.BlockSpec((1, rows, D_HEAD), b3),
        out_shape=jax.ShapeDtypeStruct((batch, rows, D_HEAD), F32),
        compiler_params=_params("arbitrary"),
        name="moba_decode_combine",
    )(m, l, ks, o, qf, qbd, kn, vn)


def _block_diag_q(q, batch, ts, heads):
    q4 = q.reshape(batch, ts, heads, D_HEAD)
    eye = jnp.eye(heads, dtype=q.dtype)
    return jnp.einsum('bchd,hk->bhckd', q4, eye).reshape(batch, heads * ts, heads * D_HEAD)


def _pad_rows(a, batch, ts):
    a3 = a.reshape(batch, ts, a.shape[-1])
    return jnp.pad(a3, ((0, 0), (0, LANES - ts), (0, 0)))


def _compress_s_kernel(pt_ref, pages_hbm, pea_ref, peb_ref, wa_ref, wb_ref, w2_ref, o_ref,
                       buf, sem, x_scr, a_scr, b_scr):
    per = COMPRESS_PAGES_PER_STEP
    s = pl.program_id(1)
    nseg = per * PAGE // CMP_STRIDE
    slot = _page_ring_step(pt_ref, [(pages_hbm, buf, sem)], per)
    for u in range(per):
        x_scr[u * PAGE:(u + 1) * PAGE, :] = buf[slot, u].T
    xs = [x_scr[pl.ds(l, nseg, stride=CMP_STRIDE), :] for l in range(CMP_STRIDE)]
    xa = jnp.concatenate([(xs[l] + pea_ref[l:l + 1, :]).astype(BF16) for l in range(CMP_STRIDE)], axis=1)
    xb = jnp.concatenate([(xs[l] + peb_ref[l:l + 1, :]).astype(BF16) for l in range(CMP_STRIDE)], axis=1)
    a_scr[pl.ds(pl.multiple_of(s * nseg, nseg), nseg), :] = _dot(xa, wa_ref[...])
    b_scr[pl.ds(pl.multiple_of(s * nseg, nseg), nseg), :] = _dot(xb, wb_ref[...])

    @pl.when(s == pl.num_programs(1) - 1)
    def _():
        total = a_scr.shape[0]
        hid = a_scr[...] + pltpu.roll(b_scr[...], total - 1, 0)
        o_ref[0] = _dot(_gelu(hid).astype(BF16), w2_ref[...])


def _compress_s(pt_flat, pages, cw, batch, n_pages):
    wa, wb, pea, peb, w2bd = cw
    per = COMPRESS_PAGES_PER_STEP
    total = n_pages * PAGE // CMP_STRIDE
    hidden = NSA_KV_HEADS * CMP_HIDDEN
    width = CMP_STRIDE * NSA_KV_WIDTH
    full2 = lambda b, s, pt: (0, 0)
    return pl.pallas_call(
        _compress_s_kernel,
        grid_spec=pltpu.PrefetchScalarGridSpec(
            num_scalar_prefetch=1,
            grid=(batch, n_pages // per),
            in_specs=[pl.BlockSpec(memory_space=pl.ANY),
                      pl.BlockSpec((CMP_STRIDE, NSA_KV_WIDTH), full2), pl.BlockSpec((CMP_STRIDE, NSA_KV_WIDTH), full2),
                      pl.BlockSpec((width, hidden), full2), pl.BlockSpec((width, hidden), full2),
                      pl.BlockSpec((hidden, NSA_KV_WIDTH), full2)],
            out_specs=pl.BlockSpec((1, total, NSA_KV_WIDTH), lambda b, s, pt: (b, 0, 0)),
            scratch_shapes=[pltpu.VMEM((PAGE_RING_DEPTH, per, NSA_KV_WIDTH, PAGE), F32),
                            pltpu.SemaphoreType.DMA((PAGE_RING_DEPTH,)),
                            pltpu.VMEM((per * PAGE, NSA_KV_WIDTH), F32), pltpu.VMEM((total, hidden), F32),
                            pltpu.VMEM((total, hidden), F32)]),
        out_shape=jax.ShapeDtypeStruct((batch, total, NSA_KV_WIDTH), F32),
        compiler_params=_params("arbitrary", "arbitrary"),
        name="compress_decode",
    )(pt_flat, pages, pea, peb, wa.reshape(width, hidden), wb.reshape(width, hidden), w2bd)


def _stack_group_q(q_ref, g):
    return jnp.concatenate(
        [q_ref[0, :, (g * NSA_GROUP + r) * D_HEAD:(g * NSA_GROUP + r + 1) * D_HEAD] * SCALE
         for r in range(NSA_GROUP)], axis=0).astype(BF16)


def _nsa_s_kernel(q_ref, kc_ref, vc_ref, wk_ref, wv_ref, kn_ref, vn_ref, gate_ref, ov_ref, ex_ref, part_ref, sel_ref,
                  *, ts):
    rows = NSA_GROUP * ts
    nc_pad = kc_ref.shape[1]
    wlen = wk_ref.shape[3]
    gates = _sigmoid(gate_ref[0])
    n_i = _iota((rows, nc_pad), 1)
    valid_c = n_i < nc_pad - 1
    c_of_row = _iota((rows, 1), 0) % ts
    valid_w = _iota((rows, wlen), 1) >= c_of_row
    valid_n = _iota((rows, LANES), 1) <= c_of_row
    for g in range(NSA_KV_HEADS):
        lane0 = g * D_HEAD
        qs = _stack_group_q(q_ref, g)
        s = jnp.where(valid_c, _dot_nt(qs, kc_ref[0, :, lane0:lane0 + D_HEAD].astype(BF16)), NEG_INF)
        m = jnp.max(s, axis=-1, keepdims=True)
        p = jnp.where(valid_c, jnp.exp(s - m), 0.0)
        p = p / jnp.maximum(jnp.sum(p, axis=-1, keepdims=True), TINY)
        o_c = _dot(p.astype(BF16), vc_ref[0, :, lane0:lane0 + D_HEAD].astype(BF16))
        psum = jnp.sum(p.reshape(NSA_GROUP, ts, nc_pad), axis=0)
        imp = jnp.dot(psum, ov_ref[...], precision=HIGHEST, preferred_element_type=F32)
        keep = _rank_select(imp, LANES, SLC_TOPN - 1).astype(BF16)
        sel_ref[0, g] = jnp.where(_dot(keep, ex_ref[...]) > 0.5, 0.0, NEG_INF)
        s_w = jnp.where(valid_w, _dot(qs, wk_ref[0, g].astype(BF16)), NEG_INF)
        s_n = jnp.where(valid_n, _dot_nt(qs, kn_ref[0, :, lane0:lane0 + D_HEAD].astype(BF16)), NEG_INF)
        m = jnp.maximum(jnp.max(s_w, axis=-1, keepdims=True), jnp.max(s_n, axis=-1, keepdims=True))
        p_w = jnp.where(valid_w, jnp.exp(s_w - m), 0.0)
        p_n = jnp.where(valid_n, jnp.exp(s_n - m), 0.0)
        den = jnp.sum(p_w, axis=-1, keepdims=True) + jnp.sum(p_n, axis=-1, keepdims=True)
        o_w = (_dot_nt(p_w.astype(BF16), wv_ref[0, g].astype(BF16))
               + _dot(p_n.astype(BF16), vn_ref[0, :, lane0:lane0 + D_HEAD].astype(BF16))) / jnp.maximum(den, TINY)
        for r in range(NSA_GROUP):
            hd = g * NSA_GROUP + r
            rs = slice(r * ts, (r + 1) * ts)
            part_ref[0, :, hd * D_HEAD:(hd + 1) * D_HEAD] = (
                gates[:, 3 * hd:3 * hd + 1] * o_c[rs] + gates[:, 3 * hd + 2:3 * hd + 3] * o_w[rs])


def _nsa_s(q3, kcmp, vcmp, wk_t, wv_t, kn_w, vn_w, gate3, ts):
    batch = q3.shape[0]
    nc_pad = kcmp.shape[1]
    wlen = wk_t.shape[3]
    ov = _overlap_matrix(nc_pad, LANES)
    n_keys = LANES * SLC_BLOCK
    expand = (jnp.arange(LANES)[:, None] == jnp.arange(n_keys)[None, :] // SLC_BLOCK).astype(BF16)
    b3 = lambda b: (b, 0, 0)
    b4 = lambda b: (b, 0, 0, 0)
    return pl.pallas_call(
        functools.partial(_nsa_s_kernel, ts=ts),
        grid=(batch,),
        in_specs=[pl.BlockSpec((1, ts, NSA_WIDTH), b3),
                  pl.BlockSpec((1, nc_pad, NSA_KV_WIDTH), b3), pl.BlockSpec((1, nc_pad, NSA_KV_WIDTH), b3),
                  pl.BlockSpec((1, NSA_KV_HEADS, D_HEAD, wlen), b4), pl.BlockSpec((1, NSA_KV_HEADS, D_HEAD, wlen), b4),
                  pl.BlockSpec((1, LANES, NSA_KV_WIDTH), b3), pl.BlockSpec((1, LANES, NSA_KV_WIDTH), b3),
                  pl.BlockSpec((1, ts, LANES), b3),
                  pl.BlockSpec((nc_pad, LANES), lambda b: (0, 0)),
                  pl.BlockSpec((LANES, n_keys), lambda b: (0, 0))],
        out_specs=[pl.BlockSpec((1, ts, NSA_WIDTH), b3), pl.BlockSpec((1, NSA_KV_HEADS, ts, n_keys), b4)],
        out_shape=[jax.ShapeDtypeStruct((batch, ts, NSA_WIDTH), F32),
                   jax.ShapeDtypeStruct((batch, NSA_KV_HEADS, ts, n_keys), F32)],
        compiler_params=_params("arbitrary"),
        name="nsa_decode_cmp_win",
    )(q3, kcmp, vcmp, wk_t, wv_t, kn_w, vn_w, gate3, ov, expand)


def _slc_s_kernel(pt_ref, q_ref, bias_ref, k_hbm, v_hbm, kn_ref, vn_ref, gate_ref, part_ref, o_ref,
                  kbuf, vbuf, ksem, vsem, m_scr, l_scr, acc_scr, *, ts):
    per = NSA_PAGES_PER_STEP
    s = pl.program_id(1)
    slot = _page_ring_step(pt_ref, [(k_hbm, kbuf, ksem), (v_hbm, vbuf, vsem)], per)
    rows = NSA_GROUP * ts
    zero = jnp.zeros((rows, D_HEAD), BF16)
    q2 = jnp.concatenate([jnp.concatenate([_stack_group_q(q_ref, 0), zero], axis=1),
                          jnp.concatenate([zero, _stack_group_q(q_ref, 1)], axis=1)], axis=0)
    in_g0 = _iota((NSA_KV_HEADS * rows, D_HEAD), 0) < rows

    def own_group(full):
        return jnp.where(in_g0, full[:, :D_HEAD], full[:, D_HEAD:])

    @pl.when(s == 0)
    def _():
        valid_n = _iota((NSA_KV_HEADS * rows, LANES), 1) <= (_iota((NSA_KV_HEADS * rows, 1), 0) % ts)
        sc = jnp.where(valid_n, _dot_nt(q2, kn_ref[0].astype(BF16)), NEG_INF)
        m = jnp.max(sc, axis=-1, keepdims=True)
        p = jnp.exp(sc - m)
        m_scr[...] = m
        l_scr[...] = jnp.sum(p, axis=-1, keepdims=True)
        acc_scr[...] = own_group(_dot(p.astype(BF16), vn_ref[0].astype(BF16)))

    kt = jnp.concatenate([kbuf[slot, u].astype(BF16) for u in range(per)], axis=1)
    vt = jnp.concatenate([vbuf[slot, u].astype(BF16) for u in range(per)], axis=1)
    bias = jnp.concatenate([bias_ref[0, g] for g in range(NSA_KV_HEADS) for _ in range(NSA_GROUP)], axis=0)
    sc = _dot(q2, kt) + bias
    m_old = m_scr[...]
    m_new = jnp.maximum(m_old, jnp.max(sc, axis=-1, keepdims=True))
    pf = jnp.exp(sc - m_new)
    alpha = jnp.exp(m_old - m_new)
    l_scr[...] = alpha * l_scr[...] + jnp.sum(pf, axis=-1, keepdims=True)
    acc_scr[...] = alpha * acc_scr[...] + own_group(_dot_nt(pf.astype(BF16), vt))
    m_scr[...] = m_new

    @pl.when(s == pl.num_programs(1) - 1)
    def _():
        gates = _sigmoid(gate_ref[0])
        o_s = acc_scr[...] / jnp.maximum(l_scr[...], TINY)
        for hd in range(NSA_HEADS):
            cols = slice(hd * D_HEAD, (hd + 1) * D_HEAD)
            o_ref[0, :, cols] = part_ref[0, :, cols] + gates[:, 3 * hd + 1:3 * hd + 2] * o_s[hd * ts:(hd + 1) * ts]


def _slc_s(pt_flat, q3, sel_bias, k_pages, v_pages, kn_s, vn_s, gate3, part, n_pages, ts):
    batch = q3.shape[0]
    per = NSA_PAGES_PER_STEP
    rows = NSA_GROUP * ts
    b3 = lambda b, s, pt: (b, 0, 0)
    return pl.pallas_call(
        functools.partial(_slc_s_kernel, ts=ts),
        grid_spec=pltpu.PrefetchScalarGridSpec(
            num_scalar_prefetch=1,
            grid=(batch, n_pages // per),
            in_specs=[pl.BlockSpec((1, ts, NSA_WIDTH), b3),
                      pl.BlockSpec((1, NSA_KV_HEADS, ts, per * PAGE), lambda b, s, pt: (b, 0, 0, s)),
                      pl.BlockSpec(memory_space=pl.ANY), pl.BlockSpec(memory_space=pl.ANY),
                      pl.BlockSpec((1, LANES, NSA_KV_WIDTH), b3), pl.BlockSpec((1, LANES, NSA_KV_WIDTH), b3),
                      pl.BlockSpec((1, ts, LANES), b3), pl.BlockSpec((1, ts, NSA_WIDTH), b3)],
            out_specs=pl.BlockSpec((1, ts, NSA_WIDTH), b3),
            scratch_shapes=[pltpu.VMEM((PAGE_RING_DEPTH, per, NSA_KV_WIDTH, PAGE), F32),
                            pltpu.VMEM((PAGE_RING_DEPTH, per, NSA_KV_WIDTH, PAGE), F32),
                            pltpu.SemaphoreType.DMA((PAGE_RING_DEPTH,)), pltpu.SemaphoreType.DMA((PAGE_RING_DEPTH,)),
                            pltpu.VMEM((NSA_KV_HEADS * rows, 1), F32), pltpu.VMEM((NSA_KV_HEADS * rows, 1), F32),
                            pltpu.VMEM((NSA_KV_HEADS * rows, D_HEAD), F32)]),
        out_shape=jax.ShapeDtypeStruct((batch, ts, NSA_WIDTH), F32),
        compiler_params=_params("arbitrary", "arbitrary"),
        name="nsa_decode_slc",
    )(pt_flat, q3, sel_bias, k_pages, v_pages, kn_s, vn_s, gate3, part)


def _prep_weights(w_ada, b_ada, norm_mix_g, w_in, pe_cmp_k, w_cmp_k1, w_cmp_k2, pe_cmp_v, w_cmp_v1, w_cmp_v2,
                  w_br_a, w_br_b, w_out, norm_ffn_g, w_router_grp, w_router_exp, w_expert_in, w_expert_out,
                  norm_final_g):
    wr = jnp.concatenate([w_router_grp, w_router_exp], axis=1)
    wr = jnp.pad(wr, ((0, 0), (0, LANES - wr.shape[1])))
    wr_hi = wr.astype(BF16)
    wr = jnp.stack([wr_hi, (wr - wr_hi.astype(F32)).astype(BF16)])
    return dict(
        w_ada=w_ada, b_ada=b_ada.reshape(1, -1), g_mix=norm_mix_g.reshape(1, -1),
        w_in=_reorder_w_in(w_in),
        cmp_k=_compress_weights(pe_cmp_k, w_cmp_k1, w_cmp_k2),
        cmp_v=_compress_weights(pe_cmp_v, w_cmp_v1, w_cmp_v2),
        w_br_a=w_br_a.astype(BF16), w_br_b=w_br_b.astype(BF16), w_out=w_out.astype(BF16),
        g_ffn=norm_ffn_g.reshape(1, -1), wr=wr,
        w_ein=w_expert_in.astype(BF16), w_eout=w_expert_out.astype(BF16),
        g_final=norm_final_g.reshape(1, -1))


def _prompt_layer(x, mod, w, batch, seq):
    sh1, sc1, g1, sh2, sc2, g2 = mod
    cos, sin = _rope_tables(jnp.arange(seq, dtype=jnp.int32))
    qk_a, _, q_b, k_n, v_n, gate, mg, kt_a, vt_a, kt_n, vt_n = _inproj(
        x, sc1, sh1, w['g_mix'], cos, sin, w['w_in'], 256, transposed=True)
    o_a = _moba_p(qk_a, vt_a, batch, seq)
    kcmp = _compress_p(k_n[:, :NSA_KV_WIDTH], w['cmp_k'], batch, seq)
    vcmp = _compress_p(v_n[:, :NSA_KV_WIDTH], w['cmp_v'], batch, seq)
    o_b = _nsa_p(q_b, k_n, vt_n, kcmp, vcmp, gate, batch, seq)
    x1 = _merge(o_a, o_b, mg, x, g1, w['w_br_a'], w['w_br_b'], w['w_out'], 256)
    y = _moe(x1, sc2, sh2, g2, w['g_ffn'], w['g_final'], w['wr'], w['w_ein'], w['w_eout'], 512)
    return y, (kt_a, vt_a, kt_n, vt_n)


def _sample_layer(x, mod, w, caches, win_state, page_table, batch, ts):
    sh1, sc1, g1, sh2, sc2, g2 = mod
    moba_k, moba_v, cmp_k, cmp_v, slc_k, slc_v = caches
    win_k, win_v = win_state
    n_pages = page_table.shape[1]
    assert moba_k.shape[1] == PAGE and win_k.shape[1] == WINDOW and ts <= LANES
    assert n_pages * PAGE == LANES * SLC_BLOCK and n_pages * PAGE // MOBA_BLOCK <= LANES
    n = batch * ts
    pos = n_pages * PAGE + (jnp.arange(n, dtype=jnp.int32) % ts)
    cos, sin = _rope_tables(pos)
    qk_a, v_a, q_b, k_n, v_n, gate, mg = _inproj(x, sc1, sh1, w['g_mix'], cos, sin, w['w_in'], n)
    pt_flat = page_table.reshape(-1)
    kv = NSA_KV_WIDTH
    qf = _block_diag_q(qk_a[:, :MOBA_WIDTH], batch, ts, MOBA_HEADS)
    qbd = (qf * SCALE).astype(BF16)
    m, l, ks, o = _moba_s_pass(pt_flat, qbd, _page_view(moba_k), _page_view(moba_v), batch, n_pages)
    o_a = _moba_s_combine(m, l, ks, o, qf, qbd, _pad_rows(qk_a[:, MOBA_WIDTH:], batch, ts), _pad_rows(v_a, batch, ts),
                          n_pages, ts)
    o_a = o_a.reshape(batch, MOBA_HEADS, ts, D_HEAD).transpose(0, 2, 1, 3).reshape(n, MOBA_WIDTH)
    kcmp = _compress_s(pt_flat, _page_view(cmp_k), w['cmp_k'], batch, n_pages)
    vcmp = _compress_s(pt_flat, _page_view(cmp_v), w['cmp_v'], batch, n_pages)
    q3 = q_b.reshape(batch, ts, NSA_WIDTH)
    gate3 = gate.reshape(batch, ts, LANES)
    part, sel = _nsa_s(q3, kcmp, vcmp, jnp.transpose(win_k, (0, 2, 3, 1)), jnp.transpose(win_v, (0, 2, 3, 1)),
                       _pad_rows(k_n[:, 2 * kv:], batch, ts), _pad_rows(v_n[:, 2 * kv:], batch, ts), gate3, ts)
    o_b = _slc_s(pt_flat, q3, sel, _page_view(slc_k), _page_view(slc_v),
                 _pad_rows(k_n[:, kv:2 * kv], batch, ts), _pad_rows(v_n[:, kv:2 * kv], batch, ts), gate3, part,
                 n_pages, ts).reshape(n, NSA_WIDTH)
    x1 = _merge(o_a, o_b, mg, x, g1, w['w_br_a'], w['w_br_b'], w['w_out'], n)
    y = _moe(x1, sc2, sh2, g2, w['g_ffn'], w['g_final'], w['wr'], w['w_ein'], w['w_eout'], n)
    return y, (qk_a, v_a, k_n, v_n)


def kernel(x_prompt, x_sample, c_prompt, c_sample, cache_moba_k, cache_moba_v, cache_nsa_cmp_k, cache_nsa_cmp_v,
           cache_nsa_slc_k, cache_nsa_slc_v, state_nsa_win_k, state_nsa_win_v, page_table, w_ada, b_ada, norm_mix_g,
           w_in, pe_cmp_k, w_cmp_k1, w_cmp_k2, pe_cmp_v, w_cmp_v1, w_cmp_v2, w_br_a, w_br_b, w_out, norm_ffn_g,
           w_router_grp, w_router_exp, w_expert_in, w_expert_out, norm_final_g):
    bp, tp, _ = x_prompt.shape
    bs, ts, _ = x_sample.shape
    w = _prep_weights(w_ada[0], b_ada[0], norm_mix_g[0], w_in[0], pe_cmp_k[0], w_cmp_k1[0], w_cmp_k2[0], pe_cmp_v[0],
                      w_cmp_v1[0], w_cmp_v2[0], w_br_a[0], w_br_b[0], w_out[0], norm_ffn_g[0], w_router_grp[0],
                      w_router_exp[0], w_expert_in[0], w_expert_out[0], norm_final_g)
    mod = _ada(jnp.concatenate([c_prompt, c_sample], axis=0), w['w_ada'], w['b_ada'])
    mod_p = [m.reshape(bp, 1, D_MODEL) for m in jnp.split(mod[:bp], 6, axis=-1)]
    y_p, new_p = _prompt_layer(x_prompt.reshape(bp * tp, D_MODEL), mod_p, w, bp, tp)

    mod_s = [jnp.repeat(m, ts, axis=0).reshape(1, bs * ts, D_MODEL) for m in jnp.split(mod[bp:], 6, axis=-1)]
    caches = (cache_moba_k[0], cache_moba_v[0], cache_nsa_cmp_k[0], cache_nsa_cmp_v[0], cache_nsa_slc_k[0],
              cache_nsa_slc_v[0])
    y_s, new_s = _sample_layer(x_sample.reshape(bs * ts, D_MODEL), mod_s, w, caches,
                               (state_nsa_win_k[0], state_nsa_win_v[0]), page_table, bs, ts)

    kv = NSA_KV_WIDTH

    def new_rows(new, b, t):
        qk, v, k_n, v_n = new
        rows = lambda a, heads: a.reshape(1, b, t, heads, D_HEAD)
        return (rows(qk[:, MOBA_WIDTH:], MOBA_HEADS), rows(v, MOBA_HEADS),
                rows(k_n[:, :kv], NSA_KV_HEADS), rows(v_n[:, :kv], NSA_KV_HEADS),
                rows(k_n[:, kv:2 * kv], NSA_KV_HEADS), rows(v_n[:, kv:2 * kv], NSA_KV_HEADS),
                rows(k_n[:, 2 * kv:], NSA_KV_HEADS), rows(v_n[:, 2 * kv:], NSA_KV_HEADS))

    def new_rows_t(new, b, t):
        kt_a, vt_a, kt_n, vt_n = new
        rows = lambda a: jnp.transpose(a.reshape(1, b, a.shape[1] // D_HEAD, D_HEAD, a.shape[2]), (0, 1, 4, 2, 3))
        return (rows(kt_a), rows(vt_a), rows(kt_n[:, :kv]), rows(vt_n[:, :kv]),
                rows(kt_n[:, kv:2 * kv]), rows(vt_n[:, kv:2 * kv]),
                rows(kt_n[:, 2 * kv:, t - wb:]), rows(vt_n[:, 2 * kv:, t - wb:]))

    wb = state_nsa_win_k.shape[2]
    outs_p = new_rows_t(new_p, bp, tp)
    outs_s = new_rows(new_s, bs, ts)
    win_k = jnp.concatenate([state_nsa_win_k, outs_s[6]], axis=2)[:, :, ts:]
    win_v = jnp.concatenate([state_nsa_win_v, outs_s[7]], axis=2)[:, :, ts:]
    outs_s = outs_s[:6] + (win_k, win_v)
    return (y_p.reshape(bp, tp, D_MODEL), y_s.reshape(bs, ts, D_MODEL)) + outs_p + outs_s
```

```python
import functools

import jax
import jax.numpy as jnp
from jax import lax
from jax.experimental import pallas as pl
from jax.experimental.pallas import tpu as pltpu

D_MODEL = 1024
D_HEAD = 64
HALF = D_HEAD // 2
MOBA_HEADS = 8
MOBA_BLOCK = 256
MOBA_TOPK = 3
NSA_HEADS = 8
NSA_KV_HEADS = 2
NSA_GROUP = NSA_HEADS // NSA_KV_HEADS
CMP_LEN = 32
CMP_STRIDE = 16
CMP_HIDDEN = 128
SLC_BLOCK = 64
SLC_TOPN = 16
WINDOW = 512
N_GROUPS = 4
EXPERTS_PER_GROUP = 8
N_EXPERTS = N_GROUPS * EXPERTS_PER_GROUP
D_EXPERT = 256
ROPE_THETA = 10000.0
EPS = 1e-6
NEG_INF = -1e30
BIG = 1e30
TINY = 1e-30
MOBA_WIDTH = MOBA_HEADS * D_HEAD
NSA_WIDTH = NSA_HEADS * D_HEAD
NSA_KV_WIDTH = NSA_KV_HEADS * D_HEAD
SCALE = D_HEAD ** -0.5

LANES = 128
VMEM_LIMIT = 48 * 1024 * 1024

F32 = jnp.float32
BF16 = jnp.bfloat16
HIGHEST = lax.Precision.HIGHEST


def _params(*sem):
    return pltpu.CompilerParams(dimension_semantics=sem, vmem_limit_bytes=VMEM_LIMIT)


def _dot(a, b):
    return jnp.dot(a, b, preferred_element_type=F32)


def _dot_nt(a, b, precision=None):
    return lax.dot_general(a, b, (((1,), (1,)), ((), ())), precision=precision,
                           preferred_element_type=F32)


def _sigmoid(x):
    return 1.0 / (1.0 + jnp.exp(-x))


def _iota(shape, dim):
    return lax.broadcasted_iota(jnp.int32, shape, dim)


def _ada_kernel(c_ref, w_ref, b_ref, o_ref):
    c = c_ref[...]
    s = c * _sigmoid(c)
    o_ref[...] = _dot(s.astype(BF16), w_ref[...].astype(BF16)) + b_ref[...]


def _ada(c, w, b):
    n = c.shape[0]
    tn = 1024
    return pl.pallas_call(
        _ada_kernel,
        grid=(w.shape[1] // tn,),
        in_specs=[pl.BlockSpec((n, D_MODEL), lambda j: (0, 0)),
                  pl.BlockSpec((D_MODEL, tn), lambda j: (0, j)),
                  pl.BlockSpec((1, tn), lambda j: (0, j))],
        out_specs=pl.BlockSpec((n, tn), lambda j: (0, j)),
        out_shape=jax.ShapeDtypeStruct((n, w.shape[1]), F32),
        compiler_params=_params("arbitrary"),
        name="ada",
    )(c, w, b)


_IN_GROUPS = ((2 * MOBA_WIDTH, True),
              (MOBA_WIDTH, False),
              (NSA_WIDTH, True),
              (3 * NSA_KV_WIDTH, True),
              (3 * NSA_KV_WIDTH, False),
              (LANES, False),
              (2 * D_MODEL, False))
_IN_COLS_PAD = sum(w for w, _ in _IN_GROUPS)


def _norm_mod(x, g, sc, sh):
    ms = jnp.mean(x * x, axis=-1, keepdims=True)
    y = x * lax.rsqrt(ms + EPS) * g
    return y * (1.0 + sc) + sh


def _inproj_kernel(x_ref, sc_ref, sh_ref, g_ref, cos_ref, sin_ref, w_ref, *out_refs):
    h = _norm_mod(x_ref[...], g_ref[...], sc_ref[0], sh_ref[0]).astype(BF16)
    cos = cos_ref[...]
    sin = sin_ref[...]
    first_half = (_iota(cos.shape, 1) & (D_HEAD - 1)) < HALF

    def rope(y):
        rot = jnp.where(first_half, pltpu.roll(y, LANES - HALF, 1), pltpu.roll(y, HALF, 1))
        return y * cos + rot * sin

    n_groups = len(_IN_GROUPS)
    t_refs = dict(zip(_IN_T_GROUPS, out_refs[n_groups:]))
    col = 0
    for gi, (out_ref, (width, rotary)) in enumerate(zip(out_refs, _IN_GROUPS)):
        chunk = min(width, 512)
        for c in range(0, width, chunk):
            cw = min(chunk, width - c)
            y = _dot(h, w_ref[:, col + c:col + c + cw])
            for s in range(0, cw, LANES):
                piece = y[:, s:s + LANES]
                piece = rope(piece) if rotary else piece
                if c + s < out_ref.shape[1]:
                    out_ref[:, c + s:c + s + LANES] = piece
                if gi in t_refs and c + s >= _IN_T_GROUPS[gi]:
                    t0 = c + s - _IN_T_GROUPS[gi]
                    t_refs[gi][0, t0:t0 + LANES, :] = piece.T
        col += width


_IN_T_GROUPS = {0: MOBA_WIDTH, 1: 0, 3: 0, 4: 0}
_IN_T_ROW_COLS = {1: LANES, 4: NSA_KV_WIDTH}


def _inproj(x, sc, sh, g, cos, sin, w, tm, transposed=False):
    n = x.shape[0]
    nb, r, _ = sc.shape
    tiles_per_b = (n // nb) // tm
    tab_tiles = cos.shape[0] // tm
    row = lambda i: (i, 0)
    mod = lambda i: (i // tiles_per_b, 0, 0)
    tab = lambda i: (i % tab_tiles, 0)
    widths = [_IN_T_ROW_COLS.get(gi, wd) if transposed else wd for gi, (wd, _) in enumerate(_IN_GROUPS)]
    out_specs = [pl.BlockSpec((tm, wd), row) for wd in widths]
    out_shape = [jax.ShapeDtypeStruct((n, wd), F32) for wd in widths]
    if transposed:
        for gi, first in _IN_T_GROUPS.items():
            cols = _IN_GROUPS[gi][0] - first
            out_specs.append(pl.BlockSpec((1, cols, tm), lambda i: (i // tiles_per_b, 0, i % tiles_per_b)))
            out_shape.append(jax.ShapeDtypeStruct((nb, cols, n // nb), F32))
    return pl.pallas_call(
        _inproj_kernel,
        grid=(n // tm,),
        in_specs=[pl.BlockSpec((tm, D_MODEL), row),
                  pl.BlockSpec((1, r, D_MODEL), mod),
                  pl.BlockSpec((1, r, D_MODEL), mod),
                  pl.BlockSpec((1, D_MODEL), lambda i: (0, 0)),
                  pl.BlockSpec((tm, LANES), tab),
                  pl.BlockSpec((tm, LANES), tab),
                  pl.BlockSpec((D_MODEL, _IN_COLS_PAD), lambda i: (0, 0))],
        out_specs=out_specs,
        out_shape=out_shape,
        compiler_params=_params("arbitrary"),
        name="inproj",
    )(x, sc, sh, g, cos, sin, w)


def _reorder_w_in(w_in):
    kv0 = 3 * MOBA_WIDTH + NSA_WIDTH
    kvs = [w_in[:, kv0 + i * NSA_KV_WIDTH:kv0 + (i + 1) * NSA_KV_WIDTH] for i in range(6)]
    g0 = kv0 + 6 * NSA_KV_WIDTH
    ng = 3 * NSA_HEADS
    gate = jnp.pad(w_in[:, g0:g0 + ng], ((0, 0), (0, LANES - ng)))
    parts = [w_in[:, :kv0], kvs[0], kvs[2], kvs[4], kvs[1], kvs[3], kvs[5], gate, w_in[:, g0 + ng:]]
    return jnp.concatenate(parts, axis=1).astype(BF16)


def _rope_tables(pos):
    inv = ROPE_THETA ** (-jnp.arange(HALF, dtype=F32) / HALF)
    ang = pos.astype(F32)[:, None] * inv[None, :]
    cos = jnp.cos(ang)
    sin = jnp.sin(ang)
    cos = jnp.concatenate([cos, cos, cos, cos], axis=1)
    sin = jnp.concatenate([-sin, sin, -sin, sin], axis=1)
    return cos, sin


def _rank_select(score, n_cols, n_keep):
    lane = _iota(score.shape, 1)
    rank = jnp.zeros(score.shape, jnp.int32)
    for jp in range(n_cols):
        col = score[:, jp:jp + 1]
        beats = (col > score) | ((col == score) & (lane > jp))
        rank = rank + beats.astype(jnp.int32)
    return rank < n_keep


def _rank_select_t(score, n_rows, n_keep):
    row = _iota(score.shape, 0)
    rank = jnp.zeros(score.shape, jnp.int32)
    for jp in range(n_rows):
        r = score[jp:jp + 1, :]
        beats = (r > score) | ((r == score) & (row > jp))
        rank = rank + beats.astype(jnp.int32)
    return rank < n_keep


SUBLANES = 8


def _fold_rows(x, op):
    return op(x.reshape(x.shape[0] // SUBLANES, SUBLANES, x.shape[1]), axis=0)


def _attend_t(score_fn, vt_fn, lo, hi, width, n_chains, cache_ref):
    def chunk_loop(step_fn, carry):
        n = hi - lo
        carry = lax.fori_loop(0, n // 2, lambda i, cr: step_fn(lo + 2 * i + 1, step_fn(lo + 2 * i, cr)), carry)
        return lax.cond(n % 2 == 1, lambda cr: step_fn(hi - 1, cr), lambda cr: cr, carry)

    def max_step(c, own, m):
        out = []
        for ch, (mi, s) in enumerate(zip(m, score_fn(c, own))):
            cache_ref[ch, c - lo] = s
            out.append(jnp.maximum(mi, _fold_rows(s, jnp.max)))
        return tuple(out)

    m = max_step(hi, True, tuple(jnp.full((SUBLANES, width), NEG_INF, F32) for _ in range(n_chains)))
    m = chunk_loop(lambda c, mm: max_step(c, False, mm), m)
    shift = [jnp.max(mi, axis=0, keepdims=True) for mi in m]

    def acc_step(c, carry):
        out = []
        for ch, ((l, acc), vt) in enumerate(zip(carry, vt_fn(c))):
            p = jnp.exp(cache_ref[ch, c - lo] - shift[ch])
            out.append((l + _fold_rows(p, jnp.sum), acc + _dot(vt, p.astype(BF16))))
        return tuple(out)

    zero = tuple((jnp.zeros((SUBLANES, width), F32), jnp.zeros((D_HEAD, width), F32)) for _ in range(n_chains))
    carry = acc_step(hi, zero)
    carry = chunk_loop(acc_step, carry)
    return [acc / jnp.maximum(jnp.sum(l, axis=0, keepdims=True), TINY) for l, acc in carry]


def _key_aug(n_keys, block, n_blocks):
    lane = _iota((n_keys, D_HEAD), 1)
    blk = _iota((n_keys, D_HEAD), 0) // block
    return jnp.where((lane < n_blocks) & (lane == blk), 1.0, 0.0).astype(BF16)


def _query_aug(qt, bias):
    pad = jnp.zeros((D_HEAD - bias.shape[0], qt.shape[1]), F32)
    return jnp.concatenate([qt, bias, pad], axis=0).astype(BF16)


MOBA_STEP_WIDTH = 512


def _moba_p_kernel(q_ref, k_ref, v_ref, o_ref, ka_scr, vt_scr, km_scr, s_scr):
    qi = pl.program_id(2)
    tq = tk = MOBA_BLOCK
    seq = k_ref.shape[1]
    nblk = seq // MOBA_BLOCK
    width = q_ref.shape[2]
    nh = width // D_HEAD

    @pl.when(qi == 0)
    def _():
        kf = k_ref[0]
        km_scr[...] = jnp.mean(kf.reshape(nblk, MOBA_BLOCK, width), axis=1)
        aug = _key_aug(seq, MOBA_BLOCK, nblk)
        for hh in range(nh):
            ka_scr[hh] = jnp.concatenate([kf[:, hh * D_HEAD:(hh + 1) * D_HEAD].astype(BF16), aug], axis=1)
        for j in range(nblk):
            vt_scr[j] = v_ref[0, :, j * tk:(j + 1) * tk].astype(BF16)

    q2t = q_ref[0].T
    km = km_scr[...]
    klane = _iota(km.shape, 1)
    blk = _iota((nblk, tq), 0)
    causal_t = _iota((tk, tq), 0) <= _iota((tk, tq), 1)
    qts, biases = [], []
    for hh in range(nh):
        kmh = jnp.where((klane >= hh * D_HEAD) & (klane < (hh + 1) * D_HEAD), km, 0.0)
        gate = jnp.dot(kmh, q2t, precision=HIGHEST, preferred_element_type=F32)
        gate = jnp.where(blk < qi, gate, NEG_INF)
        keep = _rank_select_t(gate, nblk, MOBA_TOPK) & (blk < qi)
        biases.append(jnp.where(keep, 0.0, NEG_INF))
        qts.append(q2t[hh * D_HEAD:(hh + 1) * D_HEAD, :] * SCALE)

    q_own = [_query_aug(qts[hh], jnp.zeros_like(biases[hh])) for hh in range(nh)]
    q_past = [_query_aug(qts[hh], biases[hh]) for hh in range(nh)]

    def scores(c, own):
        out = []
        for hh in range(nh):
            kj = ka_scr[hh, pl.ds(pl.multiple_of(c * tk, tk), tk), :]
            s = _dot(kj, q_own[hh] if own else q_past[hh])
            out.append(jnp.where(causal_t, s, NEG_INF) if own else s)
        return out

    def values(c):
        return [vt_scr[c, hh * D_HEAD:(hh + 1) * D_HEAD, :] for hh in range(nh)]

    o_ref[0] = jnp.concatenate(_attend_t(scores, values, 0, qi, tq, nh, s_scr), axis=0).T


def _moba_p(qk, vt, batch, seq):
    qk3 = qk.reshape(batch, seq, 2 * MOBA_WIDTH)
    sw = MOBA_STEP_WIDTH
    pairs = MOBA_WIDTH // sw
    nblk = seq // MOBA_BLOCK
    out = pl.pallas_call(
        _moba_p_kernel,
        grid=(batch, pairs, nblk),
        in_specs=[pl.BlockSpec((1, MOBA_BLOCK, sw), lambda b, h, i: (b, i, h)),
                  pl.BlockSpec((1, seq, sw), lambda b, h, i: (b, 0, pairs + h)),
                  pl.BlockSpec((1, sw, seq), lambda b, h, i: (b, h, 0))],
        out_specs=pl.BlockSpec((1, MOBA_BLOCK, sw), lambda b, h, i: (b, i, h)),
        out_shape=jax.ShapeDtypeStruct((batch, seq, MOBA_WIDTH), F32),
        scratch_shapes=[pltpu.VMEM((sw // D_HEAD, seq, 2 * D_HEAD), BF16), pltpu.VMEM((nblk, sw, MOBA_BLOCK), BF16),
                        pltpu.VMEM((nblk, sw), F32),
                        pltpu.VMEM((sw // D_HEAD, nblk, MOBA_BLOCK, MOBA_BLOCK), F32)],
        compiler_params=_params("arbitrary", "arbitrary", "arbitrary"),
        name="moba_prompt",
    )(qk3, qk3, vt)
    return out.reshape(batch * seq, MOBA_WIDTH)


def _gelu(x):
    return 0.5 * x * (1.0 + jnp.tanh(0.7978845608028654 * (x + 0.044715 * x * x * x)))


def _compress_p_kernel(seg_ref, pea_ref, peb_ref, wa_ref, wb_ref, w2_ref, o_ref):
    seg = seg_ref[0]
    a = _dot((seg + pea_ref[...]).astype(BF16), wa_ref[...])
    b = _dot((seg + peb_ref[...]).astype(BF16), wb_ref[...])
    nseg = seg.shape[0]
    hid = a + pltpu.roll(b, nseg - 1, 0)
    o_ref[0] = _dot(_gelu(hid).astype(BF16), w2_ref[...])


def _compress_weights(pe, w1, w2):
    g = NSA_KV_HEADS
    eye = jnp.eye(g, dtype=F32)
    w1r = w1.reshape(CMP_LEN, D_HEAD, CMP_HIDDEN)

    def half(lo):
        w = w1r[lo:lo + CMP_STRIDE]
        wbd = jnp.einsum('ldf,gh->lgdhf', w, eye)
        p = jnp.broadcast_to(pe[lo:lo + CMP_STRIDE, None, :], (CMP_STRIDE, g, D_HEAD))
        return wbd.reshape(CMP_STRIDE, g * D_HEAD, g * CMP_HIDDEN).astype(BF16), p.reshape(CMP_STRIDE, g * D_HEAD)

    wa, pea = half(0)
    wb, peb = half(CMP_STRIDE)
    w2bd = jnp.einsum('fd,gh->gfhd', w2, eye).reshape(g * CMP_HIDDEN, g * D_HEAD).astype(BF16)
    return wa, wb, pea, peb, w2bd


def _compress_p(rows, cw, batch, seq):
    wa, wb, pea, peb, w2bd = cw
    nseg = seq // CMP_STRIDE
    width = CMP_STRIDE * NSA_KV_WIDTH
    seg = rows.reshape(batch, nseg, width)
    full = lambda b: (0, 0)
    return pl.pallas_call(
        _compress_p_kernel,
        grid=(batch,),
        in_specs=[pl.BlockSpec((1, nseg, width), lambda b: (b, 0, 0)),
                  pl.BlockSpec((1, width), full), pl.BlockSpec((1, width), full),
                  pl.BlockSpec((width, NSA_KV_HEADS * CMP_HIDDEN), full),
                  pl.BlockSpec((width, NSA_KV_HEADS * CMP_HIDDEN), full),
                  pl.BlockSpec((NSA_KV_HEADS * CMP_HIDDEN, NSA_KV_WIDTH), full)],
        out_specs=pl.BlockSpec((1, nseg, NSA_KV_WIDTH), lambda b: (b, 0, 0)),
        out_shape=jax.ShapeDtypeStruct((batch, nseg, NSA_KV_WIDTH), F32),
        compiler_params=_params("arbitrary"),
        name="compress_prompt",
    )(seg, pea.reshape(1, width), peb.reshape(1, width), wa.reshape(width, -1), wb.reshape(width, -1), w2bd)


NSA_TQ = 256
NSA_TK = 256


def _overlap_matrix(nc_pad, nslc_pad):
    cs = jnp.arange(nc_pad)[:, None] * CMP_STRIDE
    ss = jnp.arange(nslc_pad)[None, :] * SLC_BLOCK
    return ((cs < ss + SLC_BLOCK) & (cs + CMP_LEN > ss)).astype(F32)


def _nsa_p_kernel(q_ref, kc_ref, vc_ref, ks_ref, vs_ref, kw_ref, vw_ref, gate_ref, ovt_ref, o_ref,
                  ksa_scr, kwb_scr, vst_scr, vwt_scr, ss_scr, sw_scr):
    tq, tk = NSA_TQ, NSA_TK
    t = pl.program_id(1)
    q0 = t * tq
    seq = ks_ref.shape[1]
    nc_pad = kc_ref.shape[1]
    nslc = seq // SLC_BLOCK
    width = NSA_GROUP * tq

    @pl.when(t == 0)
    def _():
        aug_s = _key_aug(seq, SLC_BLOCK, nslc)
        for g in range(NSA_KV_HEADS):
            rows = slice(g * D_HEAD, (g + 1) * D_HEAD)
            ksa_scr[g] = jnp.concatenate([ks_ref[0, :, rows].astype(BF16), aug_s], axis=1)
        kwb_scr[...] = kw_ref[0].astype(BF16)
        for j in range(seq // tk):
            vst_scr[j] = vs_ref[0, :, j * tk:(j + 1) * tk].astype(BF16)
            vwt_scr[j] = vw_ref[0, :, j * tk:(j + 1) * tk].astype(BF16)

    qt_all = q_ref[0].T
    gates_t = _sigmoid(gate_ref[0]).T
    kct = kc_ref[0].astype(BF16)
    vct = vc_ref[0].T.astype(BF16)
    pos = q0 + _iota((1, tq), 1)
    n_i = _iota((nc_pad, tq), 0)
    valid_c = (n_i < nc_pad - 1) & (n_i * CMP_STRIDE + (CMP_LEN - 1) <= pos)
    valid_c4 = jnp.concatenate([valid_c] * NSA_GROUP, axis=1)
    q_blk = pos >> 6
    jrow = _iota((nslc, tq), 0)
    krow = _iota((tk, tq), 0)
    cd = q0 // tk
    c_win = jnp.maximum(q0 - WINDOW, 0) // tk

    def tile(b):
        return jnp.concatenate([b] * NSA_GROUP, axis=1)

    def win_bias(c):
        dist = pos - (c * tk + krow)
        return jnp.where((dist >= 0) & (dist <= WINDOW), 0.0, NEG_INF)

    qts, o_cs, biases = [], [], []
    for g in range(NSA_KV_HEADS):
        rows = slice(g * D_HEAD, (g + 1) * D_HEAD)
        qt = jnp.concatenate(
            [qt_all[(g * NSA_GROUP + r) * D_HEAD:(g * NSA_GROUP + r + 1) * D_HEAD, :] * SCALE
             for r in range(NSA_GROUP)], axis=1)
        qts.append(qt)
        s = jnp.where(valid_c4, _dot(kct[:, rows], qt.astype(BF16)), NEG_INF)
        m = jnp.max(s, axis=0, keepdims=True)
        p = jnp.where(valid_c4, jnp.exp(s - m), 0.0)
        p = p / jnp.maximum(jnp.sum(p, axis=0, keepdims=True), TINY)
        o_cs.append(_dot(vct[rows, :], p.astype(BF16)))
        psum = p[:, :tq]
        for r in range(1, NSA_GROUP):
            psum = psum + p[:, r * tq:(r + 1) * tq]
        imp = jnp.dot(ovt_ref[...], psum, precision=HIGHEST, preferred_element_type=F32)
        imp = jnp.where(jrow == q_blk, BIG, jnp.where(jrow < q_blk, imp, NEG_INF))
        keep = _rank_select_t(imp, nslc, SLC_TOPN) & (jrow <= q_blk)
        biases.append(tile(jnp.where(keep, 0.0, NEG_INF)))

    causal_own = tile(cd * tk + krow <= pos)

    qa_slc = [_query_aug(qts[g], biases[g]) for g in range(NSA_KV_HEADS)]
    qb_win = [qts[g].astype(BF16) for g in range(NSA_KV_HEADS)]

    def slc_scores(c, own):
        out = []
        for g in range(NSA_KV_HEADS):
            s = _dot(ksa_scr[g, pl.ds(pl.multiple_of(c * tk, tk), tk), :], qa_slc[g])
            out.append(jnp.where(causal_own, s, NEG_INF) if own else s)
        return out

    def win_scores(c, own):
        wb = tile(win_bias(c))
        return [_dot(kwb_scr[pl.ds(pl.multiple_of(c * tk, tk), tk), g * D_HEAD:(g + 1) * D_HEAD], qb_win[g]) + wb
                for g in range(NSA_KV_HEADS)]

    def values(vt_scr):
        return lambda c: [vt_scr[c, g * D_HEAD:(g + 1) * D_HEAD, :] for g in range(NSA_KV_HEADS)]

    o_ss = _attend_t(slc_scores, values(vst_scr), 0, cd, width, NSA_KV_HEADS, ss_scr)
    o_ws = _attend_t(win_scores, values(vwt_scr), c_win, cd, width, NSA_KV_HEADS, sw_scr)
    outs = []
    for g in range(NSA_KV_HEADS):
        o_c, o_s, o_w = o_cs[g], o_ss[g], o_ws[g]
        for r in range(NSA_GROUP):
            hd = g * NSA_GROUP + r
            cs = slice(r * tq, (r + 1) * tq)
            outs.append(gates_t[3 * hd:3 * hd + 1, :] * o_c[:, cs] + gates_t[3 * hd + 1:3 * hd + 2, :] * o_s[:, cs]
                        + gates_t[3 * hd + 2:3 * hd + 3, :] * o_w[:, cs])
    o_ref[0] = jnp.concatenate(outs, axis=0).T


def _nsa_p(q, kn, vnt, kcmp, vcmp, gate, batch, seq):
    q3 = q.reshape(batch, seq, NSA_WIDTH)
    kn3 = kn.reshape(batch, seq, 3 * NSA_KV_WIDTH)
    g3 = gate.reshape(batch, seq, LANES)
    nc_pad = kcmp.shape[1]
    nslc = seq // SLC_BLOCK
    nchunk = seq // NSA_TK
    ovt = _overlap_matrix(nc_pad, nslc).T
    tile = lambda b, t: (b, t, 0)
    cmp_spec = pl.BlockSpec((1, nc_pad, NSA_KV_WIDTH), lambda b, t: (b, 0, 0))
    out = pl.pallas_call(
        _nsa_p_kernel,
        grid=(batch, seq // NSA_TQ),
        in_specs=[pl.BlockSpec((1, NSA_TQ, NSA_WIDTH), tile), cmp_spec, cmp_spec,
                  pl.BlockSpec((1, seq, LANES), lambda b, t: (b, 0, 1)),
                  pl.BlockSpec((1, LANES, seq), lambda b, t: (b, 1, 0)),
                  pl.BlockSpec((1, seq, LANES), lambda b, t: (b, 0, 2)),
                  pl.BlockSpec((1, LANES, seq), lambda b, t: (b, 2, 0)),
                  pl.BlockSpec((1, NSA_TQ, LANES), tile),
                  pl.BlockSpec((nslc, nc_pad), lambda b, t: (0, 0))],
        out_specs=pl.BlockSpec((1, NSA_TQ, NSA_WIDTH), tile),
        out_shape=jax.ShapeDtypeStruct((batch, seq, NSA_WIDTH), F32),
        scratch_shapes=[pltpu.VMEM((NSA_KV_HEADS, seq, 2 * D_HEAD), BF16), pltpu.VMEM((seq, LANES), BF16),
                        pltpu.VMEM((nchunk, LANES, NSA_TK), BF16), pltpu.VMEM((nchunk, LANES, NSA_TK), BF16),
                        pltpu.VMEM((NSA_KV_HEADS, nchunk, NSA_TK, NSA_GROUP * NSA_TQ), F32),
                        pltpu.VMEM((NSA_KV_HEADS, WINDOW // NSA_TK + 1, NSA_TK, NSA_GROUP * NSA_TQ), F32)],
        compiler_params=_params("arbitrary", "arbitrary"),
        name="nsa_prompt",
    )(q3, kcmp, vcmp, kn3, vnt, kn3, vnt, g3, ovt)
    return out.reshape(batch * seq, NSA_WIDTH)


def _merge_kernel(oa_ref, ob_ref, mg_ref, x_ref, g1_ref, wa_ref, wb_ref, wo_ref, o_ref):
    a = _dot(oa_ref[...].astype(BF16), wa_ref[...])
    b = _dot(ob_ref[...].astype(BF16), wb_ref[...])
    mix = _sigmoid(mg_ref[:, :D_MODEL]) * a + _sigmoid(mg_ref[:, D_MODEL:]) * b
    o_ref[...] = x_ref[...] + g1_ref[0] * _dot(mix.astype(BF16), wo_ref[...])


def _merge(oa, ob, mg, x, g1, wa, wb, wo, tm):
    n = x.shape[0]
    nb, r, _ = g1.shape
    tiles_per_b = (n // nb) // tm
    row = lambda i: (i, 0)
    full = lambda i: (0, 0)
    return pl.pallas_call(
        _merge_kernel,
        grid=(n // tm,),
        in_specs=[pl.BlockSpec((tm, MOBA_WIDTH), row), pl.BlockSpec((tm, NSA_WIDTH), row),
                  pl.BlockSpec((tm, 2 * D_MODEL), row), pl.BlockSpec((tm, D_MODEL), row),
                  pl.BlockSpec((1, r, D_MODEL), lambda i: (i // tiles_per_b, 0, 0)),
                  pl.BlockSpec((MOBA_WIDTH, D_MODEL), full), pl.BlockSpec((NSA_WIDTH, D_MODEL), full),
                  pl.BlockSpec((D_MODEL, D_MODEL), full)],
        out_specs=pl.BlockSpec((tm, D_MODEL), row),
        out_shape=jax.ShapeDtypeStruct((n, D_MODEL), F32),
        compiler_params=_params("arbitrary"),
        name="merge",
    )(oa, ob, mg, x, g1, wa, wb, wo)


def _route(logits):
    lane = _iota(logits.shape, 1)
    is_grp = lane < N_GROUPS
    lg = jnp.where(is_grp, logits, NEG_INF)
    mg = jnp.max(lg, axis=-1, keepdims=True)
    pg = jnp.where(is_grp, jnp.exp(lg - mg), 0.0)
    pg = pg / jnp.sum(pg, axis=-1, keepdims=True)
    g_w = jnp.max(pg, axis=-1, keepdims=True)
    g_sel = jnp.min(jnp.where(is_grp & (pg == g_w), lane, LANES), axis=-1, keepdims=True)
    e_lane = lane - N_GROUPS
    in_grp = (e_lane >= 0) & (e_lane < N_EXPERTS) & ((e_lane >> 3) == g_sel)
    le = jnp.where(in_grp, logits, NEG_INF)
    me = jnp.max(le, axis=-1, keepdims=True)
    pe = jnp.where(in_grp, jnp.exp(le - me), 0.0)
    pe = pe / jnp.sum(pe, axis=-1, keepdims=True)
    v1 = jnp.max(pe, axis=-1, keepdims=True)
    i1 = jnp.min(jnp.where(in_grp & (pe == v1), lane, LANES), axis=-1, keepdims=True)
    rest = in_grp & (lane != i1)
    pr = jnp.where(rest, pe, -1.0)
    v2 = jnp.max(pr, axis=-1, keepdims=True)
    i2 = jnp.min(jnp.where(rest & (pr == v2), lane, LANES), axis=-1, keepdims=True)
    tot = v1 + v2
    comb = jnp.where(lane == i1, v1 / tot, 0.0) + jnp.where(lane == i2, v2 / tot, 0.0)
    comb = comb * g_w
    return pltpu.roll(comb, LANES - N_GROUPS, 1)


MOE_EXPERTS_PER_STEP = 8


def _moe_kernel(x_ref, sc_ref, sh_ref, g2_ref, gn_ref, gf_ref, wr_ref, win_ref, wout_ref, o_ref,
                h_scr, comb_scr, acc_scr):
    e = pl.program_id(1)

    @pl.when(e == 0)
    def _():
        h = _norm_mod(x_ref[...], gn_ref[...], sc_ref[0], sh_ref[0])
        hb = h.astype(BF16)
        h_scr[...] = hb
        h_lo = (h - hb.astype(F32)).astype(BF16)
        logits = _dot(hb, wr_ref[0]) + _dot(hb, wr_ref[1]) + _dot(h_lo, wr_ref[0])
        comb_scr[...] = _route(logits)
        acc_scr[...] = jnp.zeros_like(acc_scr)

    per = win_ref.shape[0]
    hb = h_scr[...]
    comb = comb_scr[...]
    lane = _iota(comb.shape, 1)
    acts = []
    for j in range(per):
        hid = _dot(hb, win_ref[j])
        a = hid[:, :D_EXPERT]
        b = hid[:, D_EXPERT:]
        w = jnp.sum(jnp.where(lane == e * per + j, comb, 0.0), axis=1, keepdims=True)
        acts.append((a * _sigmoid(a) * b * w).astype(BF16))
    act = jnp.concatenate(acts, axis=1)
    acc_scr[...] += _dot(act, wout_ref[...].reshape(per * D_EXPERT, D_MODEL))

    @pl.when(e == pl.num_programs(1) - 1)
    def _():
        y = x_ref[...] + g2_ref[0] * acc_scr[...]
        ms = jnp.mean(y * y, axis=-1, keepdims=True)
        o_ref[...] = y * lax.rsqrt(ms + EPS) * gf_ref[...]


def _moe(x, sc, sh, g2, gn, gf, wr, w_ein, w_eout, tm):
    n = x.shape[0]
    nb, r, _ = sc.shape
    tiles_per_b = (n // nb) // tm
    row = lambda i, e: (i, 0)
    mod = lambda i, e: (i // tiles_per_b, 0, 0)
    full = lambda i, e: (0, 0)
    per = MOE_EXPERTS_PER_STEP
    return pl.pallas_call(
        _moe_kernel,
        grid=(n // tm, N_EXPERTS // per),
        in_specs=[pl.BlockSpec((tm, D_MODEL), row),
                  pl.BlockSpec((1, r, D_MODEL), mod), pl.BlockSpec((1, r, D_MODEL), mod),
                  pl.BlockSpec((1, r, D_MODEL), mod),
                  pl.BlockSpec((1, D_MODEL), full), pl.BlockSpec((1, D_MODEL), full),
                  pl.BlockSpec((2, D_MODEL, LANES), lambda i, e: (0, 0, 0)),
                  pl.BlockSpec((per, D_MODEL, 2 * D_EXPERT), lambda i, e: (e, 0, 0)),
                  pl.BlockSpec((per, D_EXPERT, D_MODEL), lambda i, e: (e, 0, 0))],
        out_specs=pl.BlockSpec((tm, D_MODEL), row),
        out_shape=jax.ShapeDtypeStruct((n, D_MODEL), F32),
        scratch_shapes=[pltpu.VMEM((tm, D_MODEL), BF16), pltpu.VMEM((tm, LANES), F32),
                        pltpu.VMEM((tm, D_MODEL), F32)],
        compiler_params=_params("arbitrary", "arbitrary"),
        name="moe",
    )(x, sc, sh, g2, gn, gf, wr, w_ein, w_eout)


PAGE = 128
MOBA_PAGES_PER_STEP = 16
NSA_PAGES_PER_STEP = 32
COMPRESS_PAGES_PER_STEP = 32


def _page_view(cache):
    n_phys, page, heads, dh = cache.shape
    return jnp.transpose(cache, (0, 2, 3, 1)).reshape(n_phys, heads * dh, page)


PAGE_RING_DEPTH = 3


def _page_ring_step(pt_ref, streams, per):
    depth = PAGE_RING_DEPTH
    ns = pl.num_programs(1)
    step = pl.program_id(0) * ns + pl.program_id(1)
    last = pl.num_programs(0) * ns - 1

    def copies(step_idx):
        slot_idx = step_idx % depth
        return [pltpu.make_async_copy(hbm.at[pt_ref[step_idx * per + u]], buf.at[slot_idx, u], sem.at[slot_idx])
                for hbm, buf, sem in streams for u in range(per)]

    @pl.when(step == 0)
    def _():
        for j in range(depth - 1):
            @pl.when(j <= last)
            def _():
                for c in copies(j):
                    c.start()

    for c in copies(step):
        c.wait()

    @pl.when(step + (depth - 1) <= last)
    def _():
        for c in copies(step + (depth - 1)):
            c.start()

    return step % depth


def _head_diag(full, heads):
    rows = full.shape[0]
    head = _iota((rows, D_HEAD), 0) // (rows // heads)
    out = jnp.zeros((rows, D_HEAD), F32)
    for h in range(heads):
        out = out + jnp.where(head == h, full[:, h * D_HEAD:(h + 1) * D_HEAD], 0.0)
    return out


def _moba_s_kernel(pt_ref, qbd_ref, k_hbm, v_hbm, m_ref, l_ref, ks_ref, o_ref, kbuf, vbuf, ksem, vsem):
    per = MOBA_PAGES_PER_STEP
    s = pl.program_id(1)
    slot = _page_ring_step(pt_ref, [(k_hbm, kbuf, ksem), (v_hbm, vbuf, vsem)], per)

    @pl.when(s == 0)
    def _():
        m_ref[...] = jnp.zeros_like(m_ref)
        l_ref[...] = jnp.zeros_like(l_ref)
        ks_ref[...] = jnp.zeros_like(ks_ref)

    qbd = qbd_ref[0]
    lane_q = _iota(m_ref.shape[1:], 1)
    lane_k = _iota(ks_ref.shape[1:], 1)
    ppb = MOBA_BLOCK // PAGE
    m_all, l_all, ks_all = m_ref[0], l_ref[0], ks_ref[0]
    kts = [kbuf[slot, u] for u in range(per)]
    sc_all = _dot(qbd, jnp.concatenate([kt.astype(BF16) for kt in kts], axis=1))
    for j in range(per // ppb):
        blk = s * (per // ppb) + j
        sc = sc_all[:, j * MOBA_BLOCK:(j + 1) * MOBA_BLOCK]
        m = jnp.max(sc, axis=-1, keepdims=True)
        p = jnp.exp(sc - m)
        l = jnp.sum(p, axis=-1, keepdims=True)
        vt = jnp.concatenate([vbuf[slot, j * ppb + t].astype(BF16) for t in range(ppb)], axis=1)
        o_ref[0, j] = _head_diag(_dot_nt(p.astype(BF16), vt), MOBA_HEADS)
        kb = kts[j * ppb]
        for t in range(1, ppb):
            kb = kb + kts[j * ppb + t]
        ksum = jnp.sum(kb, axis=-1, keepdims=True)
        m_all = jnp.where(lane_q == blk, m, m_all)
        l_all = jnp.where(lane_q == blk, l, l_all)
        ks_all = jnp.where(lane_k == blk, ksum, ks_all)
    m_ref[0] = m_all
    l_ref[0] = l_all
    ks_ref[0] = ks_all


def _moba_s_pass(pt_flat, qbd, kt_pages, vt_pages, batch, n_pages):
    per = MOBA_PAGES_PER_STEP
    rows = qbd.shape[1]
    ppb = MOBA_BLOCK // PAGE
    stat = lambda b, s, pt: (b, 0, 0)
    return pl.pallas_call(
        _moba_s_kernel,
        grid_spec=pltpu.PrefetchScalarGridSpec(
            num_scalar_prefetch=1,
            grid=(batch, n_pages // per),
            in_specs=[pl.BlockSpec((1, rows, MOBA_WIDTH), stat),
                      pl.BlockSpec(memory_space=pl.ANY), pl.BlockSpec(memory_space=pl.ANY)],
            out_specs=[pl.BlockSpec((1, rows, LANES), stat), pl.BlockSpec((1, rows, LANES), stat),
                       pl.BlockSpec((1, MOBA_WIDTH, LANES), stat),
                       pl.BlockSpec((1, per // ppb, rows, D_HEAD), lambda b, s, pt: (b, s, 0, 0))],
            scratch_shapes=[pltpu.VMEM((PAGE_RING_DEPTH, per, MOBA_WIDTH, PAGE), F32),
                            pltpu.VMEM((PAGE_RING_DEPTH, per, MOBA_WIDTH, PAGE), F32),
                            pltpu.SemaphoreType.DMA((PAGE_RING_DEPTH,)), pltpu.SemaphoreType.DMA((PAGE_RING_DEPTH,))]),
        out_shape=[jax.ShapeDtypeStruct((batch, rows, LANES), F32), jax.ShapeDtypeStruct((batch, rows, LANES), F32),
                   jax.ShapeDtypeStruct((batch, MOBA_WIDTH, LANES), F32),
                   jax.ShapeDtypeStruct((batch, n_pages // ppb, rows, D_HEAD), F32)],
        compiler_params=_params("arbitrary", "arbitrary"),
        name="moba_decode_pages",
    )(pt_flat, qbd, kt_pages, vt_pages)


def _moba_s_combine_kernel(m_ref, l_ref, ks_ref, o_ref, qf_ref, qbd_ref, kn_ref, vn_ref, out_ref, *, n_pages, ts):
    rows = m_ref.shape[1]
    nblk = n_pages // (MOBA_BLOCK // PAGE)
    lane = _iota((rows, LANES), 1)
    kmean = ks_ref[0] * (1.0 / MOBA_BLOCK)
    gate = jnp.dot(qf_ref[0], kmean, precision=HIGHEST, preferred_element_type=F32)
    gate = jnp.where(lane < nblk, gate, NEG_INF)
    selp = _rank_select(gate, nblk, MOBA_TOPK) & (lane < nblk)
    qbd = qbd_ref[0]
    s_own = _dot_nt(qbd, kn_ref[0].astype(BF16))
    valid_own = lane <= (_iota((rows, LANES), 0) % ts)
    s_own = jnp.where(valid_own, s_own, NEG_INF)
    m_all = jnp.where(selp, m_ref[0], NEG_INF)
    big_m = jnp.maximum(jnp.max(m_all, axis=-1, keepdims=True), jnp.max(s_own, axis=-1, keepdims=True))
    wgt = jnp.where(selp, jnp.exp(m_ref[0] - big_m), 0.0)
    p_own = jnp.where(valid_own, jnp.exp(s_own - big_m), 0.0)
    denom = jnp.sum(wgt * l_ref[0], axis=-1, keepdims=True) + jnp.sum(p_own, axis=-1, keepdims=True)
    num = _head_diag(_dot(p_own.astype(BF16), vn_ref[0].astype(BF16)), MOBA_HEADS)
    for j in range(nblk):
        num = num + wgt[:, j:j + 1] * o_ref[0, j]
    out_ref[0] = num / jnp.maximum(denom, TINY)


def _moba_s_combine(m, l, ks, o, qf, qbd, kn, vn, n_pages, ts):
    batch, rows, _ = m.shape
    b3 = lambda b: (b, 0, 0)
    return pl.pallas_call(
        functools.partial(_moba_s_combine_kernel, n_pages=n_pages, ts=ts),
        grid=(batch,),
        in_specs=[pl.BlockSpec((1, rows, LANES), b3), pl.BlockSpec((1, rows, LANES), b3),
                  pl.BlockSpec((1, MOBA_WIDTH, LANES), b3),
                  pl.BlockSpec((1, o.shape[1], rows, D_HEAD), lambda b: (b, 0, 0, 0)),
                  pl.BlockSpec((1, rows, MOBA_WIDTH), b3), pl.BlockSpec((1, rows, MOBA_WIDTH), b3),
                  pl.BlockSpec((1, LANES, MOBA_WIDTH), b3), pl.BlockSpec((1, LANES, MOBA_WIDTH), b3)],
        out_specs=pl.BlockSpec((1, rows, D_HEAD), b3),
        out_shape=jax.ShapeDtypeStruct((batch, rows, D_HEAD), F32),
        compiler_params=_params("arbitrary"),
        name="moba_decode_combine",
    )(m, l, ks, o, qf, qbd, kn, vn)


def _block_diag_q(q, batch, ts, heads):
    q4 = q.reshape(batch, ts, heads, D_HEAD)
    eye = jnp.eye(heads, dtype=q.dtype)
    return jnp.einsum('bchd,hk->bhckd', q4, eye).reshape(batch, heads * ts, heads * D_HEAD)


def _pad_rows(a, batch, ts):
    a3 = a.reshape(batch, ts, a.shape[-1])
    return jnp.pad(a3, ((0, 0), (0, LANES - ts), (0, 0)))


def _compress_s_kernel(pt_ref, pages_hbm, pea_ref, peb_ref, wa_ref, wb_ref, w2_ref, o_ref,
                       buf, sem, x_scr, a_scr, b_scr):
    per = COMPRESS_PAGES_PER_STEP
    s = pl.program_id(1)
    nseg = per * PAGE // CMP_STRIDE
    slot = _page_ring_step(pt_ref, [(pages_hbm, buf, sem)], per)
    for u in range(per):
        x_scr[u * PAGE:(u + 1) * PAGE, :] = buf[slot, u].T
    xs = [x_scr[pl.ds(l, nseg, stride=CMP_STRIDE), :] for l in range(CMP_STRIDE)]
    xa = jnp.concatenate([(xs[l] + pea_ref[l:l + 1, :]).astype(BF16) for l in range(CMP_STRIDE)], axis=1)
    xb = jnp.concatenate([(xs[l] + peb_ref[l:l + 1, :]).astype(BF16) for l in range(CMP_STRIDE)], axis=1)
    a_scr[pl.ds(pl.multiple_of(s * nseg, nseg), nseg), :] = _dot(xa, wa_ref[...])
    b_scr[pl.ds(pl.multiple_of(s * nseg, nseg), nseg), :] = _dot(xb, wb_ref[...])

    @pl.when(s == pl.num_programs(1) - 1)
    def _():
        total = a_scr.shape[0]
        hid = a_scr[...] + pltpu.roll(b_scr[...], total - 1, 0)
        o_ref[0] = _dot(_gelu(hid).astype(BF16), w2_ref[...])


def _compress_s(pt_flat, pages, cw, batch, n_pages):
    wa, wb, pea, peb, w2bd = cw
    per = COMPRESS_PAGES_PER_STEP
    total = n_pages * PAGE // CMP_STRIDE
    hidden = NSA_KV_HEADS * CMP_HIDDEN
    width = CMP_STRIDE * NSA_KV_WIDTH
    full2 = lambda b, s, pt: (0, 0)
    return pl.pallas_call(
        _compress_s_kernel,
        grid_spec=pltpu.PrefetchScalarGridSpec(
            num_scalar_prefetch=1,
            grid=(batch, n_pages // per),
            in_specs=[pl.BlockSpec(memory_space=pl.ANY),
                      pl.BlockSpec((CMP_STRIDE, NSA_KV_WIDTH), full2), pl.BlockSpec((CMP_STRIDE, NSA_KV_WIDTH), full2),
                      pl.BlockSpec((width, hidden), full2), pl.BlockSpec((width, hidden), full2),
                      pl.BlockSpec((hidden, NSA_KV_WIDTH), full2)],
            out_specs=pl.BlockSpec((1, total, NSA_KV_WIDTH), lambda b, s, pt: (b, 0, 0)),
            scratch_shapes=[pltpu.VMEM((PAGE_RING_DEPTH, per, NSA_KV_WIDTH, PAGE), F32),
                            pltpu.SemaphoreType.DMA((PAGE_RING_DEPTH,)),
                            pltpu.VMEM((per * PAGE, NSA_KV_WIDTH), F32), pltpu.VMEM((total, hidden), F32),
                            pltpu.VMEM((total, hidden), F32)]),
        out_shape=jax.ShapeDtypeStruct((batch, total, NSA_KV_WIDTH), F32),
        compiler_params=_params("arbitrary", "arbitrary"),
        name="compress_decode",
    )(pt_flat, pages, pea, peb, wa.reshape(width, hidden), wb.reshape(width, hidden), w2bd)


def _stack_group_q(q_ref, g):
    return jnp.concatenate(
        [q_ref[0, :, (g * NSA_GROUP + r) * D_HEAD:(g * NSA_GROUP + r + 1) * D_HEAD] * SCALE
         for r in range(NSA_GROUP)], axis=0).astype(BF16)


def _nsa_s_kernel(q_ref, kc_ref, vc_ref, wk_ref, wv_ref, kn_ref, vn_ref, gate_ref, ov_ref, part_ref, sel_ref, *, ts):
    rows = NSA_GROUP * ts
    nc_pad = kc_ref.shape[1]
    wlen = wk_ref.shape[3]
    gates = _sigmoid(gate_ref[0])
    n_i = _iota((rows, nc_pad), 1)
    valid_c = n_i < nc_pad - 1
    c_of_row = _iota((rows, 1), 0) % ts
    valid_w = _iota((rows, wlen), 1) >= c_of_row
    valid_n = _iota((rows, LANES), 1) <= c_of_row
    for g in range(NSA_KV_HEADS):
        lane0 = g * D_HEAD
        qs = _stack_group_q(q_ref, g)
        s = jnp.where(valid_c, _dot_nt(qs, kc_ref[0, :, lane0:lane0 + D_HEAD].astype(BF16)), NEG_INF)
        m = jnp.max(s, axis=-1, keepdims=True)
        p = jnp.where(valid_c, jnp.exp(s - m), 0.0)
        p = p / jnp.maximum(jnp.sum(p, axis=-1, keepdims=True), TINY)
        o_c = _dot(p.astype(BF16), vc_ref[0, :, lane0:lane0 + D_HEAD].astype(BF16))
        psum = jnp.sum(p.reshape(NSA_GROUP, ts, nc_pad), axis=0)
        imp = jnp.dot(psum, ov_ref[...], precision=HIGHEST, preferred_element_type=F32)
        keep = _rank_select(imp, LANES, SLC_TOPN - 1)
        sel_ref[0, g] = jnp.where(keep, 0.0, NEG_INF)
        s_w = jnp.where(valid_w, _dot(qs, wk_ref[0, g].astype(BF16)), NEG_INF)
        s_n = jnp.where(valid_n, _dot_nt(qs, kn_ref[0, :, lane0:lane0 + D_HEAD].astype(BF16)), NEG_INF)
        m = jnp.maximum(jnp.max(s_w, axis=-1, keepdims=True), jnp.max(s_n, axis=-1, keepdims=True))
        p_w = jnp.where(valid_w, jnp.exp(s_w - m), 0.0)
        p_n = jnp.where(valid_n, jnp.exp(s_n - m), 0.0)
        den = jnp.sum(p_w, axis=-1, keepdims=True) + jnp.sum(p_n, axis=-1, keepdims=True)
        o_w = (_dot_nt(p_w.astype(BF16), wv_ref[0, g].astype(BF16))
               + _dot(p_n.astype(BF16), vn_ref[0, :, lane0:lane0 + D_HEAD].astype(BF16))) / jnp.maximum(den, TINY)
        for r in range(NSA_GROUP):
            hd = g * NSA_GROUP + r
            rs = slice(r * ts, (r + 1) * ts)
            part_ref[0, :, hd * D_HEAD:(hd + 1) * D_HEAD] = (
                gates[:, 3 * hd:3 * hd + 1] * o_c[rs] + gates[:, 3 * hd + 2:3 * hd + 3] * o_w[rs])


def _nsa_s(q3, kcmp, vcmp, wk_t, wv_t, kn_w, vn_w, gate3, ts):
    batch = q3.shape[0]
    nc_pad = kcmp.shape[1]
    wlen = wk_t.shape[3]
    ov = _overlap_matrix(nc_pad, LANES)
    b3 = lambda b: (b, 0, 0)
    b4 = lambda b: (b, 0, 0, 0)
    return pl.pallas_call(
        functools.partial(_nsa_s_kernel, ts=ts),
        grid=(batch,),
        in_specs=[pl.BlockSpec((1, ts, NSA_WIDTH), b3),
                  pl.BlockSpec((1, nc_pad, NSA_KV_WIDTH), b3), pl.BlockSpec((1, nc_pad, NSA_KV_WIDTH), b3),
                  pl.BlockSpec((1, NSA_KV_HEADS, D_HEAD, wlen), b4), pl.BlockSpec((1, NSA_KV_HEADS, D_HEAD, wlen), b4),
                  pl.BlockSpec((1, LANES, NSA_KV_WIDTH), b3), pl.BlockSpec((1, LANES, NSA_KV_WIDTH), b3),
                  pl.BlockSpec((1, ts, LANES), b3),
                  pl.BlockSpec((nc_pad, LANES), lambda b: (0, 0))],
        out_specs=[pl.BlockSpec((1, ts, NSA_WIDTH), b3), pl.BlockSpec((1, NSA_KV_HEADS, ts, LANES), b4)],
        out_shape=[jax.ShapeDtypeStruct((batch, ts, NSA_WIDTH), F32),
                   jax.ShapeDtypeStruct((batch, NSA_KV_HEADS, ts, LANES), F32)],
        compiler_params=_params("arbitrary"),
        name="nsa_decode_cmp_win",
    )(q3, kcmp, vcmp, wk_t, wv_t, kn_w, vn_w, gate3, ov)


def _slc_s_kernel(pt_ref, q_ref, bias_ref, hot_ref, k_hbm, v_hbm, kn_ref, vn_ref, gate_ref, part_ref, o_ref,
                  kbuf, vbuf, ksem, vsem, m_scr, l_scr, acc_scr, *, ts):
    per = NSA_PAGES_PER_STEP
    s = pl.program_id(1)
    slot = _page_ring_step(pt_ref, [(k_hbm, kbuf, ksem), (v_hbm, vbuf, vsem)], per)
    rows = NSA_GROUP * ts
    zero = jnp.zeros((rows, D_HEAD), BF16)
    q2 = jnp.concatenate([jnp.concatenate([_stack_group_q(q_ref, 0), zero], axis=1),
                          jnp.concatenate([zero, _stack_group_q(q_ref, 1)], axis=1)], axis=0)
    in_g0 = _iota((NSA_KV_HEADS * rows, D_HEAD), 0) < rows

    def own_group(full):
        return jnp.where(in_g0, full[:, :D_HEAD], full[:, D_HEAD:])

    @pl.when(s == 0)
    def _():
        valid_n = _iota((NSA_KV_HEADS * rows, LANES), 1) <= (_iota((NSA_KV_HEADS * rows, 1), 0) % ts)
        sc = jnp.where(valid_n, _dot_nt(q2, kn_ref[0].astype(BF16)), NEG_INF)
        m = jnp.max(sc, axis=-1, keepdims=True)
        p = jnp.exp(sc - m)
        m_scr[...] = m
        l_scr[...] = jnp.sum(p, axis=-1, keepdims=True)
        acc_scr[...] = own_group(_dot(p.astype(BF16), vn_ref[0].astype(BF16)))

    kt = jnp.concatenate([kbuf[slot, u].astype(BF16) for u in range(per)], axis=1)
    vt = jnp.concatenate([vbuf[slot, u].astype(BF16) for u in range(per)], axis=1)
    nb = per * PAGE // SLC_BLOCK
    bias = jnp.concatenate([bias_ref[0, g] for g in range(NSA_KV_HEADS) for _ in range(NSA_GROUP)], axis=0)
    cols = bias[:, :nb]
    for k in range(1, bias.shape[1] // nb):
        cols = jnp.where(s == k, bias[:, k * nb:(k + 1) * nb], cols)
    sc = _dot(jnp.concatenate([q2, cols.astype(BF16)], axis=1),
              jnp.concatenate([kt, hot_ref[...]], axis=0))
    m_old = m_scr[...]
    m_new = jnp.maximum(m_old, jnp.max(sc, axis=-1, keepdims=True))
    pf = jnp.exp(sc - m_new)
    alpha = jnp.exp(m_old - m_new)
    l_scr[...] = alpha * l_scr[...] + jnp.sum(pf, axis=-1, keepdims=True)
    acc_scr[...] = alpha * acc_scr[...] + own_group(_dot_nt(pf.astype(BF16), vt))
    m_scr[...] = m_new

    @pl.when(s == pl.num_programs(1) - 1)
    def _():
        gates = _sigmoid(gate_ref[0])
        o_s = acc_scr[...] / jnp.maximum(l_scr[...], TINY)
        for hd in range(NSA_HEADS):
            cols = slice(hd * D_HEAD, (hd + 1) * D_HEAD)
            o_ref[0, :, cols] = part_ref[0, :, cols] + gates[:, 3 * hd + 1:3 * hd + 2] * o_s[hd * ts:(hd + 1) * ts]


def _slc_s(pt_flat, q3, sel_bias, k_pages, v_pages, kn_s, vn_s, gate3, part, n_pages, ts):
    batch = q3.shape[0]
    per = NSA_PAGES_PER_STEP
    rows = NSA_GROUP * ts
    nb = per * PAGE // SLC_BLOCK
    hot = (jnp.arange(nb)[:, None] == jnp.arange(per * PAGE)[None, :] // SLC_BLOCK).astype(BF16)
    b3 = lambda b, s, pt: (b, 0, 0)
    return pl.pallas_call(
        functools.partial(_slc_s_kernel, ts=ts),
        grid_spec=pltpu.PrefetchScalarGridSpec(
            num_scalar_prefetch=1,
            grid=(batch, n_pages // per),
            in_specs=[pl.BlockSpec((1, ts, NSA_WIDTH), b3),
                      pl.BlockSpec((1, NSA_KV_HEADS, ts, LANES), lambda b, s, pt: (b, 0, 0, 0)),
                      pl.BlockSpec((nb, per * PAGE), lambda b, s, pt: (0, 0)),
                      pl.BlockSpec(memory_space=pl.ANY), pl.BlockSpec(memory_space=pl.ANY),
                      pl.BlockSpec((1, LANES, NSA_KV_WIDTH), b3), pl.BlockSpec((1, LANES, NSA_KV_WIDTH), b3),
                      pl.BlockSpec((1, ts, LANES), b3), pl.BlockSpec((1, ts, NSA_WIDTH), b3)],
            out_specs=pl.BlockSpec((1, ts, NSA_WIDTH), b3),
            scratch_shapes=[pltpu.VMEM((PAGE_RING_DEPTH, per, NSA_KV_WIDTH, PAGE), F32),
                            pltpu.VMEM((PAGE_RING_DEPTH, per, NSA_KV_WIDTH, PAGE), F32),
                            pltpu.SemaphoreType.DMA((PAGE_RING_DEPTH,)), pltpu.SemaphoreType.DMA((PAGE_RING_DEPTH,)),
                            pltpu.VMEM((NSA_KV_HEADS * rows, 1), F32), pltpu.VMEM((NSA_KV_HEADS * rows, 1), F32),
                            pltpu.VMEM((NSA_KV_HEADS * rows, D_HEAD), F32)]),
        out_shape=jax.ShapeDtypeStruct((batch, ts, NSA_WIDTH), F32),
        compiler_params=_params("arbitrary", "arbitrary"),
        name="nsa_decode_slc",
    )(pt_flat, q3, sel_bias, hot, k_pages, v_pages, kn_s, vn_s, gate3, part)


def _prep_weights(w_ada, b_ada, norm_mix_g, w_in, pe_cmp_k, w_cmp_k1, w_cmp_k2, pe_cmp_v, w_cmp_v1, w_cmp_v2,
                  w_br_a, w_br_b, w_out, norm_ffn_g, w_router_grp, w_router_exp, w_expert_in, w_expert_out,
                  norm_final_g):
    wr = jnp.concatenate([w_router_grp, w_router_exp], axis=1)
    wr = jnp.pad(wr, ((0, 0), (0, LANES - wr.shape[1])))
    wr_hi = wr.astype(BF16)
    wr = jnp.stack([wr_hi, (wr - wr_hi.astype(F32)).astype(BF16)])
    return dict(
        w_ada=w_ada, b_ada=b_ada.reshape(1, -1), g_mix=norm_mix_g.reshape(1, -1),
        w_in=_reorder_w_in(w_in),
        cmp_k=_compress_weights(pe_cmp_k, w_cmp_k1, w_cmp_k2),
        cmp_v=_compress_weights(pe_cmp_v, w_cmp_v1, w_cmp_v2),
        w_br_a=w_br_a.astype(BF16), w_br_b=w_br_b.astype(BF16), w_out=w_out.astype(BF16),
        g_ffn=norm_ffn_g.reshape(1, -1), wr=wr,
        w_ein=w_expert_in.astype(BF16), w_eout=w_expert_out.astype(BF16),
        g_final=norm_final_g.reshape(1, -1))


def _prompt_layer(x, mod, w, batch, seq):
    sh1, sc1, g1, sh2, sc2, g2 = mod
    cos, sin = _rope_tables(jnp.arange(seq, dtype=jnp.int32))
    qk_a, _, q_b, k_n, v_n, gate, mg, kt_a, vt_a, kt_n, vt_n = _inproj(
        x, sc1, sh1, w['g_mix'], cos, sin, w['w_in'], 256, transposed=True)
    o_a = _moba_p(qk_a, vt_a, batch, seq)
    kcmp = _compress_p(k_n[:, :NSA_KV_WIDTH], w['cmp_k'], batch, seq)
    vcmp = _compress_p(v_n[:, :NSA_KV_WIDTH], w['cmp_v'], batch, seq)
    o_b = _nsa_p(q_b, k_n, vt_n, kcmp, vcmp, gate, batch, seq)
    x1 = _merge(o_a, o_b, mg, x, g1, w['w_br_a'], w['w_br_b'], w['w_out'], 256)
    y = _moe(x1, sc2, sh2, g2, w['g_ffn'], w['g_final'], w['wr'], w['w_ein'], w['w_eout'], 512)
    return y, (kt_a, vt_a, kt_n, vt_n)


def _sample_layer(x, mod, w, caches, win_state, page_table, batch, ts):
    sh1, sc1, g1, sh2, sc2, g2 = mod
    moba_k, moba_v, cmp_k, cmp_v, slc_k, slc_v = caches
    win_k, win_v = win_state
    n_pages = page_table.shape[1]
    assert moba_k.shape[1] == PAGE and win_k.shape[1] == WINDOW and ts <= LANES
    assert n_pages * PAGE == LANES * SLC_BLOCK and n_pages * PAGE // MOBA_BLOCK <= LANES
    n = batch * ts
    pos = n_pages * PAGE + (jnp.arange(n, dtype=jnp.int32) % ts)
    cos, sin = _rope_tables(pos)
    qk_a, v_a, q_b, k_n, v_n, gate, mg = _inproj(x, sc1, sh1, w['g_mix'], cos, sin, w['w_in'], n)
    pt_flat = page_table.reshape(-1)
    kv = NSA_KV_WIDTH
    qf = _block_diag_q(qk_a[:, :MOBA_WIDTH], batch, ts, MOBA_HEADS)
    qbd = (qf * SCALE).astype(BF16)
    m, l, ks, o = _moba_s_pass(pt_flat, qbd, _page_view(moba_k), _page_view(moba_v), batch, n_pages)
    o_a = _moba_s_combine(m, l, ks, o, qf, qbd, _pad_rows(qk_a[:, MOBA_WIDTH:], batch, ts), _pad_rows(v_a, batch, ts),
                          n_pages, ts)
    o_a = o_a.reshape(batch, MOBA_HEADS, ts, D_HEAD).transpose(0, 2, 1, 3).reshape(n, MOBA_WIDTH)
    kcmp = _compress_s(pt_flat, _page_view(cmp_k), w['cmp_k'], batch, n_pages)
    vcmp = _compress_s(pt_flat, _page_view(cmp_v), w['cmp_v'], batch, n_pages)
    q3 = q_b.reshape(batch, ts, NSA_WIDTH)
    gate3 = gate.reshape(batch, ts, LANES)
    part, sel = _nsa_s(q3, kcmp, vcmp, jnp.transpose(win_k, (0, 2, 3, 1)), jnp.transpose(win_v, (0, 2, 3, 1)),
                       _pad_rows(k_n[:, 2 * kv:], batch, ts), _pad_rows(v_n[:, 2 * kv:], batch, ts), gate3, ts)
    o_b = _slc_s(pt_flat, q3, sel, _page_view(slc_k), _page_view(slc_v),
                 _pad_rows(k_n[:, kv:2 * kv], batch, ts), _pad_rows(v_n[:, kv:2 * kv], batch, ts), gate3, part,
                 n_pages, ts).reshape(n, NSA_WIDTH)
    x1 = _merge(o_a, o_b, mg, x, g1, w['w_br_a'], w['w_br_b'], w['w_out'], n)
    y = _moe(x1, sc2, sh2, g2, w['g_ffn'], w['g_final'], w['wr'], w['w_ein'], w['w_eout'], n)
    return y, (qk_a, v_a, k_n, v_n)


def kernel(x_prompt, x_sample, c_prompt, c_sample, cache_moba_k, cache_moba_v, cache_nsa_cmp_k, cache_nsa_cmp_v,
           cache_nsa_slc_k, cache_nsa_slc_v, state_nsa_win_k, state_nsa_win_v, page_table, w_ada, b_ada, norm_mix_g,
           w_in, pe_cmp_k, w_cmp_k1, w_cmp_k2, pe_cmp_v, w_cmp_v1, w_cmp_v2, w_br_a, w_br_b, w_out, norm_ffn_g,
           w_router_grp, w_router_exp, w_expert_in, w_expert_out, norm_final_g):
    bp, tp, _ = x_prompt.shape
    bs, ts, _ = x_sample.shape
    w = _prep_weights(w_ada[0], b_ada[0], norm_mix_g[0], w_in[0], pe_cmp_k[0], w_cmp_k1[0], w_cmp_k2[0], pe_cmp_v[0],
                      w_cmp_v1[0], w_cmp_v2[0], w_br_a[0], w_br_b[0], w_out[0], norm_ffn_g[0], w_router_grp[0],
                      w_router_exp[0], w_expert_in[0], w_expert_out[0], norm_final_g)
    mod = _ada(jnp.concatenate([c_prompt, c_sample], axis=0), w['w_ada'], w['b_ada'])
    mod_p = [m.reshape(bp, 1, D_MODEL) for m in jnp.split(mod[:bp], 6, axis=-1)]
    y_p, new_p = _prompt_layer(x_prompt.reshape(bp * tp, D_MODEL), mod_p, w, bp, tp)

    mod_s = [jnp.repeat(m, ts, axis=0).reshape(1, bs * ts, D_MODEL) for m in jnp.split(mod[bp:], 6, axis=-1)]
    caches = (cache_moba_k[0], cache_moba_v[0], cache_nsa_cmp_k[0], cache_nsa_cmp_v[0], cache_nsa_slc_k[0],
              cache_nsa_slc_v[0])
    y_s, new_s = _sample_layer(x_sample.reshape(bs * ts, D_MODEL), mod_s, w, caches,
                               (state_nsa_win_k[0], state_nsa_win_v[0]), page_table, bs, ts)

    kv = NSA_KV_WIDTH

    def new_rows(new, b, t):
        qk, v, k_n, v_n = new
        rows = lambda a, heads: a.reshape(1, b, t, heads, D_HEAD)
        return (rows(qk[:, MOBA_WIDTH:], MOBA_HEADS), rows(v, MOBA_HEADS),
                rows(k_n[:, :kv], NSA_KV_HEADS), rows(v_n[:, :kv], NSA_KV_HEADS),
                rows(k_n[:, kv:2 * kv], NSA_KV_HEADS), rows(v_n[:, kv:2 * kv], NSA_KV_HEADS),
                rows(k_n[:, 2 * kv:], NSA_KV_HEADS), rows(v_n[:, 2 * kv:], NSA_KV_HEADS))

    def new_rows_t(new, b, t):
        kt_a, vt_a, kt_n, vt_n = new
        rows = lambda a: jnp.transpose(a.reshape(1, b, a.shape[1] // D_HEAD, D_HEAD, a.shape[2]), (0, 1, 4, 2, 3))
        return (rows(kt_a), rows(vt_a), rows(kt_n[:, :kv]), rows(vt_n[:, :kv]),
                rows(kt_n[:, kv:2 * kv]), rows(vt_n[:, kv:2 * kv]),
                rows(kt_n[:, 2 * kv:, t - wb:]), rows(vt_n[:, 2 * kv:, t - wb:]))

    wb = state_nsa_win_k.shape[2]
    outs_p = new_rows_t(new_p, bp, tp)
    outs_s = new_rows(new_s, bs, ts)
    win_k = jnp.concatenate([state_nsa_win_k, outs_s[6]], axis=2)[:, :, ts:]
    win_v = jnp.concatenate([state_nsa_win_v, outs_s[7]], axis=2)[:, :, ts:]
    outs_s = outs_s[:6] + (win_k, win_v)
    return (y_p.reshape(bp, tp, D_MODEL), y_s.reshape(bs, ts, D_MODEL)) + outs_p + outs_s
```

```python
import functools

import jax
import jax.numpy as jnp
from jax import lax
from jax.experimental import pallas as pl
from jax.experimental.pallas import tpu as pltpu

D_MODEL = 1024
D_HEAD = 64
HALF = D_HEAD // 2
MOBA_HEADS = 8
MOBA_BLOCK = 256
MOBA_TOPK = 3
NSA_HEADS = 8
NSA_KV_HEADS = 2
NSA_GROUP = NSA_HEADS // NSA_KV_HEADS
CMP_LEN = 32
CMP_STRIDE = 16
CMP_HIDDEN = 128
SLC_BLOCK = 64
SLC_TOPN = 16
WINDOW = 512
N_GROUPS = 4
EXPERTS_PER_GROUP = 8
N_EXPERTS = N_GROUPS * EXPERTS_PER_GROUP
D_EXPERT = 256
ROPE_THETA = 10000.0
EPS = 1e-6
NEG_INF = -1e30
BIG = 1e30
TINY = 1e-30
MOBA_WIDTH = MOBA_HEADS * D_HEAD
NSA_WIDTH = NSA_HEADS * D_HEAD
NSA_KV_WIDTH = NSA_KV_HEADS * D_HEAD
SCALE = D_HEAD ** -0.5

LANES = 128
VMEM_LIMIT = 48 * 1024 * 1024

F32 = jnp.float32
BF16 = jnp.bfloat16
HIGHEST = lax.Precision.HIGHEST


def _params(*sem):
    return pltpu.CompilerParams(dimension_semantics=sem, vmem_limit_bytes=VMEM_LIMIT)


def _dot(a, b):
    return jnp.dot(a, b, preferred_element_type=F32)


def _dot_nt(a, b, precision=None):
    return lax.dot_general(a, b, (((1,), (1,)), ((), ())), precision=precision,
                           preferred_element_type=F32)


def _sigmoid(x):
    return 1.0 / (1.0 + jnp.exp(-x))


def _iota(shape, dim):
    return lax.broadcasted_iota(jnp.int32, shape, dim)


def _ada_kernel(c_ref, w_ref, b_ref, o_ref):
    c = c_ref[...]
    s = c * _sigmoid(c)
    o_ref[...] = _dot(s.astype(BF16), w_ref[...].astype(BF16)) + b_ref[...]


def _ada(c, w, b):
    n = c.shape[0]
    tn = 1024
    return pl.pallas_call(
        _ada_kernel,
        grid=(w.shape[1] // tn,),
        in_specs=[pl.BlockSpec((n, D_MODEL), lambda j: (0, 0)),
                  pl.BlockSpec((D_MODEL, tn), lambda j: (0, j)),
                  pl.BlockSpec((1, tn), lambda j: (0, j))],
        out_specs=pl.BlockSpec((n, tn), lambda j: (0, j)),
        out_shape=jax.ShapeDtypeStruct((n, w.shape[1]), F32),
        compiler_params=_params("arbitrary"),
        name="ada",
    )(c, w, b)


_IN_GROUPS = ((2 * MOBA_WIDTH, True),
              (MOBA_WIDTH, False),
              (NSA_WIDTH, True),
              (3 * NSA_KV_WIDTH, True),
              (3 * NSA_KV_WIDTH, False),
              (LANES, False),
              (2 * D_MODEL, False))
_IN_COLS_PAD = sum(w for w, _ in _IN_GROUPS)


def _norm_mod(x, g, sc, sh):
    ms = jnp.mean(x * x, axis=-1, keepdims=True)
    y = x * lax.rsqrt(ms + EPS) * g
    return y * (1.0 + sc) + sh


def _inproj_kernel(x_ref, sc_ref, sh_ref, g_ref, cos_ref, sin_ref, w_ref, *out_refs):
    h = _norm_mod(x_ref[...], g_ref[...], sc_ref[0], sh_ref[0]).astype(BF16)
    cos = cos_ref[...]
    sin = sin_ref[...]
    first_half = (_iota(cos.shape, 1) & (D_HEAD - 1)) < HALF

    def rope(y):
        rot = jnp.where(first_half, pltpu.roll(y, LANES - HALF, 1), pltpu.roll(y, HALF, 1))
        return y * cos + rot * sin

    n_groups = len(_IN_GROUPS)
    t_refs = dict(zip(_IN_T_GROUPS, out_refs[n_groups:]))
    col = 0
    for gi, (out_ref, (width, rotary)) in enumerate(zip(out_refs, _IN_GROUPS)):
        chunk = min(width, 512)
        for c in range(0, width, chunk):
            cw = min(chunk, width - c)
            y = _dot(h, w_ref[:, col + c:col + c + cw])
            for s in range(0, cw, LANES):
                piece = y[:, s:s + LANES]
                piece = rope(piece) if rotary else piece
                if c + s < out_ref.shape[1]:
                    out_ref[:, c + s:c + s + LANES] = piece
                if gi in t_refs and c + s >= _IN_T_GROUPS[gi]:
                    t0 = c + s - _IN_T_GROUPS[gi]
                    t_refs[gi][0, t0:t0 + LANES, :] = piece.T
        col += width


_IN_T_GROUPS = {0: MOBA_WIDTH, 1: 0, 3: 0, 4: 0}
_IN_T_ROW_COLS = {1: LANES, 4: NSA_KV_WIDTH}


def _inproj(x, sc, sh, g, cos, sin, w, tm, transposed=False):
    n = x.shape[0]
    nb, r, _ = sc.shape
    tiles_per_b = (n // nb) // tm
    tab_tiles = cos.shape[0] // tm
    row = lambda i: (i, 0)
    mod = lambda i: (i // tiles_per_b, 0, 0)
    tab = lambda i: (i % tab_tiles, 0)
    widths = [_IN_T_ROW_COLS.get(gi, wd) if transposed else wd for gi, (wd, _) in enumerate(_IN_GROUPS)]
    out_specs = [pl.BlockSpec((tm, wd), row) for wd in widths]
    out_shape = [jax.ShapeDtypeStruct((n, wd), F32) for wd in widths]
    if transposed:
        for gi, first in _IN_T_GROUPS.items():
            cols = _IN_GROUPS[gi][0] - first
            out_specs.append(pl.BlockSpec((1, cols, tm), lambda i: (i // tiles_per_b, 0, i % tiles_per_b)))
            out_shape.append(jax.ShapeDtypeStruct((nb, cols, n // nb), F32))
    return pl.pallas_call(
        _inproj_kernel,
        grid=(n // tm,),
        in_specs=[pl.BlockSpec((tm, D_MODEL), row),
                  pl.BlockSpec((1, r, D_MODEL), mod),
                  pl.BlockSpec((1, r, D_MODEL), mod),
                  pl.BlockSpec((1, D_MODEL), lambda i: (0, 0)),
                  pl.BlockSpec((tm, LANES), tab),
                  pl.BlockSpec((tm, LANES), tab),
                  pl.BlockSpec((D_MODEL, _IN_COLS_PAD), lambda i: (0, 0))],
        out_specs=out_specs,
        out_shape=out_shape,
        compiler_params=_params("arbitrary"),
        name="inproj",
    )(x, sc, sh, g, cos, sin, w)


def _reorder_w_in(w_in):
    kv0 = 3 * MOBA_WIDTH + NSA_WIDTH
    kvs = [w_in[:, kv0 + i * NSA_KV_WIDTH:kv0 + (i + 1) * NSA_KV_WIDTH] for i in range(6)]
    g0 = kv0 + 6 * NSA_KV_WIDTH
    ng = 3 * NSA_HEADS
    gate = jnp.pad(w_in[:, g0:g0 + ng], ((0, 0), (0, LANES - ng)))
    parts = [w_in[:, :kv0], kvs[0], kvs[2], kvs[4], kvs[1], kvs[3], kvs[5], gate, w_in[:, g0 + ng:]]
    return jnp.concatenate(parts, axis=1).astype(BF16)


def _rope_tables(pos):
    inv = ROPE_THETA ** (-jnp.arange(HALF, dtype=F32) / HALF)
    ang = pos.astype(F32)[:, None] * inv[None, :]
    cos = jnp.cos(ang)
    sin = jnp.sin(ang)
    cos = jnp.concatenate([cos, cos, cos, cos], axis=1)
    sin = jnp.concatenate([-sin, sin, -sin, sin], axis=1)
    return cos, sin


def _rank_select(score, n_cols, n_keep):
    lane = _iota(score.shape, 1)
    rank = jnp.zeros(score.shape, jnp.int32)
    for jp in range(n_cols):
        col = score[:, jp:jp + 1]
        beats = (col > score) | ((col == score) & (lane > jp))
        rank = rank + beats.astype(jnp.int32)
    return rank < n_keep


def _rank_select_t(score, n_rows, n_keep):
    row = _iota(score.shape, 0)
    rank = jnp.zeros(score.shape, jnp.int32)
    for jp in range(n_rows):
        r = score[jp:jp + 1, :]
        beats = (r > score) | ((r == score) & (row > jp))
        rank = rank + beats.astype(jnp.int32)
    return rank < n_keep


SUBLANES = 8


def _fold_rows(x, op):
    return op(x.reshape(x.shape[0] // SUBLANES, SUBLANES, x.shape[1]), axis=0)


def _attend_t(score_fn, vt_fn, lo, hi, width, n_chains, cache_ref):
    def chunk_loop(step_fn, carry):
        n = hi - lo
        carry = lax.fori_loop(0, n // 2, lambda i, cr: step_fn(lo + 2 * i + 1, step_fn(lo + 2 * i, cr)), carry)
        return lax.cond(n % 2 == 1, lambda cr: step_fn(hi - 1, cr), lambda cr: cr, carry)

    def max_step(c, own, m):
        out = []
        for ch, (mi, s) in enumerate(zip(m, score_fn(c, own))):
            cache_ref[ch, c - lo] = s
            out.append(jnp.maximum(mi, _fold_rows(s, jnp.max)))
        return tuple(out)

    m = max_step(hi, True, tuple(jnp.full((SUBLANES, width), NEG_INF, F32) for _ in range(n_chains)))
    m = chunk_loop(lambda c, mm: max_step(c, False, mm), m)
    shift = [jnp.max(mi, axis=0, keepdims=True) for mi in m]

    def acc_step(c, carry):
        out = []
        for ch, ((l, acc), vt) in enumerate(zip(carry, vt_fn(c))):
            p = jnp.exp(cache_ref[ch, c - lo] - shift[ch])
            out.append((l + _fold_rows(p, jnp.sum), acc + _dot(vt, p.astype(BF16))))
        return tuple(out)

    zero = tuple((jnp.zeros((SUBLANES, width), F32), jnp.zeros((D_HEAD, width), F32)) for _ in range(n_chains))
    carry = acc_step(hi, zero)
    carry = chunk_loop(acc_step, carry)
    return [acc / jnp.maximum(jnp.sum(l, axis=0, keepdims=True), TINY) for l, acc in carry]


def _key_aug(n_keys, block, n_blocks):
    lane = _iota((n_keys, D_HEAD), 1)
    blk = _iota((n_keys, D_HEAD), 0) // block
    return jnp.where((lane < n_blocks) & (lane == blk), 1.0, 0.0).astype(BF16)


def _query_aug(qt, bias):
    pad = jnp.zeros((D_HEAD - bias.shape[0], qt.shape[1]), F32)
    return jnp.concatenate([qt, bias, pad], axis=0).astype(BF16)


MOBA_STEP_WIDTH = 512


def _moba_p_kernel(q_ref, k_ref, v_ref, o_ref, ka_scr, vt_scr, km_scr, s_scr):
    qi = pl.program_id(2)
    tq = tk = MOBA_BLOCK
    seq = k_ref.shape[1]
    nblk = seq // MOBA_BLOCK
    width = q_ref.shape[2]
    nh = width // D_HEAD

    @pl.when(qi == 0)
    def _():
        kf = k_ref[0]
        km_scr[...] = jnp.mean(kf.reshape(nblk, MOBA_BLOCK, width), axis=1)
        aug = _key_aug(seq, MOBA_BLOCK, nblk)
        for hh in range(nh):
            ka_scr[hh] = jnp.concatenate([kf[:, hh * D_HEAD:(hh + 1) * D_HEAD].astype(BF16), aug], axis=1)
        for j in range(nblk):
            vt_scr[j] = v_ref[0, :, j * tk:(j + 1) * tk].astype(BF16)

    q2t = q_ref[0].T
    km = km_scr[...]
    klane = _iota(km.shape, 1)
    blk = _iota((nblk, tq), 0)
    causal_t = _iota((tk, tq), 0) <= _iota((tk, tq), 1)
    qts, biases = [], []
    for hh in range(nh):
        kmh = jnp.where((klane >= hh * D_HEAD) & (klane < (hh + 1) * D_HEAD), km, 0.0)
        gate = jnp.dot(kmh, q2t, precision=HIGHEST, preferred_element_type=F32)
        gate = jnp.where(blk < qi, gate, NEG_INF)
        keep = _rank_select_t(gate, nblk, MOBA_TOPK) & (blk < qi)
        biases.append(jnp.where(keep, 0.0, NEG_INF))
        qts.append(q2t[hh * D_HEAD:(hh + 1) * D_HEAD, :] * SCALE)

    q_own = [_query_aug(qts[hh], jnp.zeros_like(biases[hh])) for hh in range(nh)]
    q_past = [_query_aug(qts[hh], biases[hh]) for hh in range(nh)]

    def scores(c, own):
        out = []
        for hh in range(nh):
            kj = ka_scr[hh, pl.ds(pl.multiple_of(c * tk, tk), tk), :]
            s = _dot(kj, q_own[hh] if own else q_past[hh])
            out.append(jnp.where(causal_t, s, NEG_INF) if own else s)
        return out

    def values(c):
        return [vt_scr[c, hh * D_HEAD:(hh + 1) * D_HEAD, :] for hh in range(nh)]

    o_ref[0] = jnp.concatenate(_attend_t(scores, values, 0, qi, tq, nh, s_scr), axis=0).T


def _moba_p(qk, vt, batch, seq):
    qk3 = qk.reshape(batch, seq, 2 * MOBA_WIDTH)
    sw = MOBA_STEP_WIDTH
    pairs = MOBA_WIDTH // sw
    nblk = seq // MOBA_BLOCK
    out = pl.pallas_call(
        _moba_p_kernel,
        grid=(batch, pairs, nblk),
        in_specs=[pl.BlockSpec((1, MOBA_BLOCK, sw), lambda b, h, i: (b, i, h)),
                  pl.BlockSpec((1, seq, sw), lambda b, h, i: (b, 0, pairs + h)),
                  pl.BlockSpec((1, sw, seq), lambda b, h, i: (b, h, 0))],
        out_specs=pl.BlockSpec((1, MOBA_BLOCK, sw), lambda b, h, i: (b, i, h)),
        out_shape=jax.ShapeDtypeStruct((batch, seq, MOBA_WIDTH), F32),
        scratch_shapes=[pltpu.VMEM((sw // D_HEAD, seq, 2 * D_HEAD), BF16), pltpu.VMEM((nblk, sw, MOBA_BLOCK), BF16),
                        pltpu.VMEM((nblk, sw), F32),
                        pltpu.VMEM((sw // D_HEAD, nblk, MOBA_BLOCK, MOBA_BLOCK), F32)],
        compiler_params=_params("arbitrary", "arbitrary", "arbitrary"),
        name="moba_prompt",
    )(qk3, qk3, vt)
    return out.reshape(batch * seq, MOBA_WIDTH)


def _gelu(x):
    return 0.5 * x * (1.0 + jnp.tanh(0.7978845608028654 * (x + 0.044715 * x * x * x)))


def _compress_p_kernel(seg_ref, pea_ref, peb_ref, wa_ref, wb_ref, w2_ref, o_ref):
    seg = seg_ref[0]
    a = _dot((seg + pea_ref[...]).astype(BF16), wa_ref[...])
    b = _dot((seg + peb_ref[...]).astype(BF16), wb_ref[...])
    nseg = seg.shape[0]
    hid = a + pltpu.roll(b, nseg - 1, 0)
    o_ref[0] = _dot(_gelu(hid).astype(BF16), w2_ref[...])


def _compress_weights(pe, w1, w2):
    g = NSA_KV_HEADS
    eye = jnp.eye(g, dtype=F32)
    w1r = w1.reshape(CMP_LEN, D_HEAD, CMP_HIDDEN)

    def half(lo):
        w = w1r[lo:lo + CMP_STRIDE]
        wbd = jnp.einsum('ldf,gh->lgdhf', w, eye)
        p = jnp.broadcast_to(pe[lo:lo + CMP_STRIDE, None, :], (CMP_STRIDE, g, D_HEAD))
        return wbd.reshape(CMP_STRIDE, g * D_HEAD, g * CMP_HIDDEN).astype(BF16), p.reshape(CMP_STRIDE, g * D_HEAD)

    wa, pea = half(0)
    wb, peb = half(CMP_STRIDE)
    w2bd = jnp.einsum('fd,gh->gfhd', w2, eye).reshape(g * CMP_HIDDEN, g * D_HEAD).astype(BF16)
    return wa, wb, pea, peb, w2bd


def _compress_p(rows, cw, batch, seq):
    wa, wb, pea, peb, w2bd = cw
    nseg = seq // CMP_STRIDE
    width = CMP_STRIDE * NSA_KV_WIDTH
    seg = rows.reshape(batch, nseg, width)
    full = lambda b: (0, 0)
    return pl.pallas_call(
        _compress_p_kernel,
        grid=(batch,),
        in_specs=[pl.BlockSpec((1, nseg, width), lambda b: (b, 0, 0)),
                  pl.BlockSpec((1, width), full), pl.BlockSpec((1, width), full),
                  pl.BlockSpec((width, NSA_KV_HEADS * CMP_HIDDEN), full),
                  pl.BlockSpec((width, NSA_KV_HEADS * CMP_HIDDEN), full),
                  pl.BlockSpec((NSA_KV_HEADS * CMP_HIDDEN, NSA_KV_WIDTH), full)],
        out_specs=pl.BlockSpec((1, nseg, NSA_KV_WIDTH), lambda b: (b, 0, 0)),
        out_shape=jax.ShapeDtypeStruct((batch, nseg, NSA_KV_WIDTH), F32),
        compiler_params=_params("arbitrary"),
        name="compress_prompt",
    )(seg, pea.reshape(1, width), peb.reshape(1, width), wa.reshape(width, -1), wb.reshape(width, -1), w2bd)


NSA_TQ = 256
NSA_TK = 256


def _overlap_matrix(nc_pad, nslc_pad):
    cs = jnp.arange(nc_pad)[:, None] * CMP_STRIDE
    ss = jnp.arange(nslc_pad)[None, :] * SLC_BLOCK
    return ((cs < ss + SLC_BLOCK) & (cs + CMP_LEN > ss)).astype(F32)


def _nsa_p_kernel(q_ref, kc_ref, vc_ref, ks_ref, vs_ref, kw_ref, vw_ref, gate_ref, ovt_ref, o_ref,
                  ksa_scr, kwb_scr, vst_scr, vwt_scr, ss_scr, sw_scr):
    tq, tk = NSA_TQ, NSA_TK
    t = pl.program_id(1)
    q0 = t * tq
    seq = ks_ref.shape[1]
    nc_pad = kc_ref.shape[1]
    nslc = seq // SLC_BLOCK
    width = NSA_GROUP * tq

    @pl.when(t == 0)
    def _():
        aug_s = _key_aug(seq, SLC_BLOCK, nslc)
        for g in range(NSA_KV_HEADS):
            rows = slice(g * D_HEAD, (g + 1) * D_HEAD)
            ksa_scr[g] = jnp.concatenate([ks_ref[0, :, rows].astype(BF16), aug_s], axis=1)
        kwb_scr[...] = kw_ref[0].astype(BF16)
        for j in range(seq // tk):
            vst_scr[j] = vs_ref[0, :, j * tk:(j + 1) * tk].astype(BF16)
            vwt_scr[j] = vw_ref[0, :, j * tk:(j + 1) * tk].astype(BF16)

    qt_all = q_ref[0].T
    gates_t = _sigmoid(gate_ref[0]).T
    kct = kc_ref[0].astype(BF16)
    vct = vc_ref[0].T.astype(BF16)
    pos = q0 + _iota((1, tq), 1)
    n_i = _iota((nc_pad, tq), 0)
    valid_c = (n_i < nc_pad - 1) & (n_i * CMP_STRIDE + (CMP_LEN - 1) <= pos)
    valid_c4 = jnp.concatenate([valid_c] * NSA_GROUP, axis=1)
    q_blk = pos >> 6
    jrow = _iota((nslc, tq), 0)
    krow = _iota((tk, tq), 0)
    cd = q0 // tk
    c_win = jnp.maximum(q0 - WINDOW, 0) // tk

    def tile(b):
        return jnp.concatenate([b] * NSA_GROUP, axis=1)

    def win_bias(c):
        dist = pos - (c * tk + krow)
        return jnp.where((dist >= 0) & (dist <= WINDOW), 0.0, NEG_INF)

    qts, o_cs, biases = [], [], []
    for g in range(NSA_KV_HEADS):
        rows = slice(g * D_HEAD, (g + 1) * D_HEAD)
        qt = jnp.concatenate(
            [qt_all[(g * NSA_GROUP + r) * D_HEAD:(g * NSA_GROUP + r + 1) * D_HEAD, :] * SCALE
             for r in range(NSA_GROUP)], axis=1)
        qts.append(qt)
        s = jnp.where(valid_c4, _dot(kct[:, rows], qt.astype(BF16)), NEG_INF)
        m = jnp.max(s, axis=0, keepdims=True)
        p = jnp.where(valid_c4, jnp.exp(s - m), 0.0)
        p = p / jnp.maximum(jnp.sum(p, axis=0, keepdims=True), TINY)
        o_cs.append(_dot(vct[rows, :], p.astype(BF16)))
        psum = p[:, :tq]
        for r in range(1, NSA_GROUP):
            psum = psum + p[:, r * tq:(r + 1) * tq]
        imp = jnp.dot(ovt_ref[...], psum, precision=HIGHEST, preferred_element_type=F32)
        imp = jnp.where(jrow == q_blk, BIG, jnp.where(jrow < q_blk, imp, NEG_INF))
        keep = _rank_select_t(imp, nslc, SLC_TOPN) & (jrow <= q_blk)
        biases.append(tile(jnp.where(keep, 0.0, NEG_INF)))

    causal_own = tile(cd * tk + krow <= pos)

    qa_slc = [_query_aug(qts[g], biases[g]) for g in range(NSA_KV_HEADS)]
    qb_win = [qts[g].astype(BF16) for g in range(NSA_KV_HEADS)]

    def slc_scores(c, own):
        out = []
        for g in range(NSA_KV_HEADS):
            s = _dot(ksa_scr[g, pl.ds(pl.multiple_of(c * tk, tk), tk), :], qa_slc[g])
            out.append(jnp.where(causal_own, s, NEG_INF) if own else s)
        return out

    def win_scores(c, own):
        wb = tile(win_bias(c))
        return [_dot(kwb_scr[pl.ds(pl.multiple_of(c * tk, tk), tk), g * D_HEAD:(g + 1) * D_HEAD], qb_win[g]) + wb
                for g in range(NSA_KV_HEADS)]

    def values(vt_scr):
        return lambda c: [vt_scr[c, g * D_HEAD:(g + 1) * D_HEAD, :] for g in range(NSA_KV_HEADS)]

    o_ss = _attend_t(slc_scores, values(vst_scr), 0, cd, width, NSA_KV_HEADS, ss_scr)
    o_ws = _attend_t(win_scores, values(vwt_scr), c_win, cd, width, NSA_KV_HEADS, sw_scr)
    outs = []
    for g in range(NSA_KV_HEADS):
        o_c, o_s, o_w = o_cs[g], o_ss[g], o_ws[g]
        for r in range(NSA_GROUP):
            hd = g * NSA_GROUP + r
            cs = slice(r * tq, (r + 1) * tq)
            outs.append(gates_t[3 * hd:3 * hd + 1, :] * o_c[:, cs] + gates_t[3 * hd + 1:3 * hd + 2, :] * o_s[:, cs]
                        + gates_t[3 * hd + 2:3 * hd + 3, :] * o_w[:, cs])
    o_ref[0] = jnp.concatenate(outs, axis=0).T


def _nsa_p(q, kn, vnt, kcmp, vcmp, gate, batch, seq):
    q3 = q.reshape(batch, seq, NSA_WIDTH)
    kn3 = kn.reshape(batch, seq, 3 * NSA_KV_WIDTH)
    g3 = gate.reshape(batch, seq, LANES)
    nc_pad = kcmp.shape[1]
    nslc = seq // SLC_BLOCK
    nchunk = seq // NSA_TK
    ovt = _overlap_matrix(nc_pad, nslc).T
    tile = lambda b, t: (b, t, 0)
    cmp_spec = pl.BlockSpec((1, nc_pad, NSA_KV_WIDTH), lambda b, t: (b, 0, 0))
    out = pl.pallas_call(
        _nsa_p_kernel,
        grid=(batch, seq // NSA_TQ),
        in_specs=[pl.BlockSpec((1, NSA_TQ, NSA_WIDTH), tile), cmp_spec, cmp_spec,
                  pl.BlockSpec((1, seq, LANES), lambda b, t: (b, 0, 1)),
                  pl.BlockSpec((1, LANES, seq), lambda b, t: (b, 1, 0)),
                  pl.BlockSpec((1, seq, LANES), lambda b, t: (b, 0, 2)),
                  pl.BlockSpec((1, LANES, seq), lambda b, t: (b, 2, 0)),
                  pl.BlockSpec((1, NSA_TQ, LANES), tile),
                  pl.BlockSpec((nslc, nc_pad), lambda b, t: (0, 0))],
        out_specs=pl.BlockSpec((1, NSA_TQ, NSA_WIDTH), tile),
        out_shape=jax.ShapeDtypeStruct((batch, seq, NSA_WIDTH), F32),
        scratch_shapes=[pltpu.VMEM((NSA_KV_HEADS, seq, 2 * D_HEAD), BF16), pltpu.VMEM((seq, LANES), BF16),
                        pltpu.VMEM((nchunk, LANES, NSA_TK), BF16), pltpu.VMEM((nchunk, LANES, NSA_TK), BF16),
                        pltpu.VMEM((NSA_KV_HEADS, nchunk, NSA_TK, NSA_GROUP * NSA_TQ), F32),
                        pltpu.VMEM((NSA_KV_HEADS, WINDOW // NSA_TK + 1, NSA_TK, NSA_GROUP * NSA_TQ), F32)],
        compiler_params=_params("arbitrary", "arbitrary"),
        name="nsa_prompt",
    )(q3, kcmp, vcmp, kn3, vnt, kn3, vnt, g3, ovt)
    return out.reshape(batch * seq, NSA_WIDTH)


def _merge_kernel(oa_ref, ob_ref, mg_ref, x_ref, g1_ref, wa_ref, wb_ref, wo_ref, o_ref):
    a = _dot(oa_ref[...].astype(BF16), wa_ref[...])
    b = _dot(ob_ref[...].astype(BF16), wb_ref[...])
    mix = _sigmoid(mg_ref[:, :D_MODEL]) * a + _sigmoid(mg_ref[:, D_MODEL:]) * b
    o_ref[...] = x_ref[...] + g1_ref[0] * _dot(mix.astype(BF16), wo_ref[...])


def _merge(oa, ob, mg, x, g1, wa, wb, wo, tm):
    n = x.shape[0]
    nb, r, _ = g1.shape
    tiles_per_b = (n // nb) // tm
    row = lambda i: (i, 0)
    full = lambda i: (0, 0)
    return pl.pallas_call(
        _merge_kernel,
        grid=(n // tm,),
        in_specs=[pl.BlockSpec((tm, MOBA_WIDTH), row), pl.BlockSpec((tm, NSA_WIDTH), row),
                  pl.BlockSpec((tm, 2 * D_MODEL), row), pl.BlockSpec((tm, D_MODEL), row),
                  pl.BlockSpec((1, r, D_MODEL), lambda i: (i // tiles_per_b, 0, 0)),
                  pl.BlockSpec((MOBA_WIDTH, D_MODEL), full), pl.BlockSpec((NSA_WIDTH, D_MODEL), full),
                  pl.BlockSpec((D_MODEL, D_MODEL), full)],
        out_specs=pl.BlockSpec((tm, D_MODEL), row),
        out_shape=jax.ShapeDtypeStruct((n, D_MODEL), F32),
        compiler_params=_params("arbitrary"),
        name="merge",
    )(oa, ob, mg, x, g1, wa, wb, wo)


def _route(logits):
    lane = _iota(logits.shape, 1)
    is_grp = lane < N_GROUPS
    lg = jnp.where(is_grp, logits, NEG_INF)
    mg = jnp.max(lg, axis=-1, keepdims=True)
    pg = jnp.where(is_grp, jnp.exp(lg - mg), 0.0)
    pg = pg / jnp.sum(pg, axis=-1, keepdims=True)
    g_w = jnp.max(pg, axis=-1, keepdims=True)
    g_sel = jnp.min(jnp.where(is_grp & (pg == g_w), lane, LANES), axis=-1, keepdims=True)
    e_lane = lane - N_GROUPS
    in_grp = (e_lane >= 0) & (e_lane < N_EXPERTS) & ((e_lane >> 3) == g_sel)
    le = jnp.where(in_grp, logits, NEG_INF)
    me = jnp.max(le, axis=-1, keepdims=True)
    pe = jnp.where(in_grp, jnp.exp(le - me), 0.0)
    pe = pe / jnp.sum(pe, axis=-1, keepdims=True)
    v1 = jnp.max(pe, axis=-1, keepdims=True)
    i1 = jnp.min(jnp.where(in_grp & (pe == v1), lane, LANES), axis=-1, keepdims=True)
    rest = in_grp & (lane != i1)
    pr = jnp.where(rest, pe, -1.0)
    v2 = jnp.max(pr, axis=-1, keepdims=True)
    i2 = jnp.min(jnp.where(rest & (pr == v2), lane, LANES), axis=-1, keepdims=True)
    tot = v1 + v2
    comb = jnp.where(lane == i1, v1 / tot, 0.0) + jnp.where(lane == i2, v2 / tot, 0.0)
    comb = comb * g_w
    return pltpu.roll(comb, LANES - N_GROUPS, 1)


MOE_EXPERTS_PER_STEP = 8


def _moe_kernel(x_ref, sc_ref, sh_ref, g2_ref, gn_ref, gf_ref, wr_ref, win_ref, wout_ref, o_ref,
                h_scr, comb_scr, acc_scr):
    e = pl.program_id(1)

    @pl.when(e == 0)
    def _():
        h = _norm_mod(x_ref[...], gn_ref[...], sc_ref[0], sh_ref[0])
        hb = h.astype(BF16)
        h_scr[...] = hb
        h_lo = (h - hb.astype(F32)).astype(BF16)
        logits = _dot(hb, wr_ref[0]) + _dot(hb, wr_ref[1]) + _dot(h_lo, wr_ref[0])
        comb_scr[...] = _route(logits)
        acc_scr[...] = jnp.zeros_like(acc_scr)

    per = win_ref.shape[0]
    hb = h_scr[...]
    comb = comb_scr[...]
    lane = _iota(comb.shape, 1)
    acts = []
    for j in range(per):
        hid = _dot(hb, win_ref[j])
        a = hid[:, :D_EXPERT]
        b = hid[:, D_EXPERT:]
        w = jnp.sum(jnp.where(lane == e * per + j, comb, 0.0), axis=1, keepdims=True)
        acts.append((a * _sigmoid(a) * b * w).astype(BF16))
    act = jnp.concatenate(acts, axis=1)
    acc_scr[...] += _dot(act, wout_ref[...].reshape(per * D_EXPERT, D_MODEL))

    @pl.when(e == pl.num_programs(1) - 1)
    def _():
        y = x_ref[...] + g2_ref[0] * acc_scr[...]
        ms = jnp.mean(y * y, axis=-1, keepdims=True)
        o_ref[...] = y * lax.rsqrt(ms + EPS) * gf_ref[...]


def _moe(x, sc, sh, g2, gn, gf, wr, w_ein, w_eout, tm):
    n = x.shape[0]
    nb, r, _ = sc.shape
    tiles_per_b = (n // nb) // tm
    row = lambda i, e: (i, 0)
    mod = lambda i, e: (i // tiles_per_b, 0, 0)
    full = lambda i, e: (0, 0)
    per = MOE_EXPERTS_PER_STEP
    return pl.pallas_call(
        _moe_kernel,
        grid=(n // tm, N_EXPERTS // per),
        in_specs=[pl.BlockSpec((tm, D_MODEL), row),
                  pl.BlockSpec((1, r, D_MODEL), mod), pl.BlockSpec((1, r, D_MODEL), mod),
                  pl.BlockSpec((1, r, D_MODEL), mod),
                  pl.BlockSpec((1, D_MODEL), full), pl.BlockSpec((1, D_MODEL), full),
                  pl.BlockSpec((2, D_MODEL, LANES), lambda i, e: (0, 0, 0)),
                  pl.BlockSpec((per, D_MODEL, 2 * D_EXPERT), lambda i, e: (e, 0, 0)),
                  pl.BlockSpec((per, D_EXPERT, D_MODEL), lambda i, e: (e, 0, 0))],
        out_specs=pl.BlockSpec((tm, D_MODEL), row),
        out_shape=jax.ShapeDtypeStruct((n, D_MODEL), F32),
        scratch_shapes=[pltpu.VMEM((tm, D_MODEL), BF16), pltpu.VMEM((tm, LANES), F32),
                        pltpu.VMEM((tm, D_MODEL), F32)],
        compiler_params=_params("arbitrary", "arbitrary"),
        name="moe",
    )(x, sc, sh, g2, gn, gf, wr, w_ein, w_eout)


PAGE = 128
MOBA_PAGES_PER_STEP = 16
NSA_PAGES_PER_STEP = 32
COMPRESS_PAGES_PER_STEP = 32


def _page_view(cache):
    n_phys, page, heads, dh = cache.shape
    return jnp.transpose(cache, (0, 2, 3, 1)).reshape(n_phys, heads * dh, page)


PAGE_RING_DEPTH = 3


def _page_ring_step(pt_ref, streams, per):
    depth = PAGE_RING_DEPTH
    ns = pl.num_programs(1)
    step = pl.program_id(0) * ns + pl.program_id(1)
    last = pl.num_programs(0) * ns - 1

    def copies(step_idx):
        slot_idx = step_idx % depth
        return [pltpu.make_async_copy(hbm.at[pt_ref[step_idx * per + u]], buf.at[slot_idx, u], sem.at[slot_idx])
                for hbm, buf, sem in streams for u in range(per)]

    @pl.when(step == 0)
    def _():
        for j in range(depth - 1):
            @pl.when(j <= last)
            def _():
                for c in copies(j):
                    c.start()

    for c in copies(step):
        c.wait()

    @pl.when(step + (depth - 1) <= last)
    def _():
        for c in copies(step + (depth - 1)):
            c.start()

    return step % depth


def _head_diag(full, heads):
    rows = full.shape[0]
    head = _iota((rows, D_HEAD), 0) // (rows // heads)
    out = jnp.zeros((rows, D_HEAD), F32)
    for h in range(heads):
        out = out + jnp.where(head == h, full[:, h * D_HEAD:(h + 1) * D_HEAD], 0.0)
    return out


def _moba_s_kernel(pt_ref, qbd_ref, k_hbm, v_hbm, m_ref, l_ref, ks_ref, o_ref, kbuf, vbuf, ksem, vsem):
    per = MOBA_PAGES_PER_STEP
    s = pl.program_id(1)
    slot = _page_ring_step(pt_ref, [(k_hbm, kbuf, ksem), (v_hbm, vbuf, vsem)], per)

    @pl.when(s == 0)
    def _():
        m_ref[...] = jnp.zeros_like(m_ref)
        l_ref[...] = jnp.zeros_like(l_ref)
        ks_ref[...] = jnp.zeros_like(ks_ref)

    qbd = qbd_ref[0]
    lane_q = _iota(m_ref.shape[1:], 1)
    lane_k = _iota(ks_ref.shape[1:], 1)
    ppb = MOBA_BLOCK // PAGE
    m_all, l_all, ks_all = m_ref[0], l_ref[0], ks_ref[0]
    kts = [kbuf[slot, u] for u in range(per)]
    sc_all = _dot(qbd, jnp.concatenate([kt.astype(BF16) for kt in kts], axis=1))
    for j in range(per // ppb):
        blk = s * (per // ppb) + j
        sc = sc_all[:, j * MOBA_BLOCK:(j + 1) * MOBA_BLOCK]
        m = jnp.max(sc, axis=-1, keepdims=True)
        p = jnp.exp(sc - m)
        l = jnp.sum(p, axis=-1, keepdims=True)
        vt = jnp.concatenate([vbuf[slot, j * ppb + t].astype(BF16) for t in range(ppb)], axis=1)
        o_ref[0, j] = _head_diag(_dot_nt(p.astype(BF16), vt), MOBA_HEADS)
        kb = kts[j * ppb]
        for t in range(1, ppb):
            kb = kb + kts[j * ppb + t]
        ksum = jnp.sum(kb, axis=-1, keepdims=True)
        m_all = jnp.where(lane_q == blk, m, m_all)
        l_all = jnp.where(lane_q == blk, l, l_all)
        ks_all = jnp.where(lane_k == blk, ksum, ks_all)
    m_ref[0] = m_all
    l_ref[0] = l_all
    ks_ref[0] = ks_all


def _moba_s_pass(pt_flat, qbd, kt_pages, vt_pages, batch, n_pages):
    per = MOBA_PAGES_PER_STEP
    rows = qbd.shape[1]
    ppb = MOBA_BLOCK // PAGE
    stat = lambda b, s, pt: (b, 0, 0)
    return pl.pallas_call(
        _moba_s_kernel,
        grid_spec=pltpu.PrefetchScalarGridSpec(
            num_scalar_prefetch=1,
            grid=(batch, n_pages // per),
            in_specs=[pl.BlockSpec((1, rows, MOBA_WIDTH), stat),
                      pl.BlockSpec(memory_space=pl.ANY), pl.BlockSpec(memory_space=pl.ANY)],
            out_specs=[pl.BlockSpec((1, rows, LANES), stat), pl.BlockSpec((1, rows, LANES), stat),
                       pl.BlockSpec((1, MOBA_WIDTH, LANES), stat),
                       pl.BlockSpec((1, per // ppb, rows, D_HEAD), lambda b, s, pt: (b, s, 0, 0))],
            scratch_shapes=[pltpu.VMEM((PAGE_RING_DEPTH, per, MOBA_WIDTH, PAGE), F32),
                            pltpu.VMEM((PAGE_RING_DEPTH, per, MOBA_WIDTH, PAGE), F32),
                            pltpu.SemaphoreType.DMA((PAGE_RING_DEPTH,)), pltpu.SemaphoreType.DMA((PAGE_RING_DEPTH,))]),
        out_shape=[jax.ShapeDtypeStruct((batch, rows, LANES), F32), jax.ShapeDtypeStruct((batch, rows, LANES), F32),
                   jax.ShapeDtypeStruct((batch, MOBA_WIDTH, LANES), F32),
                   jax.ShapeDtypeStruct((batch, n_pages // ppb, rows, D_HEAD), F32)],
        compiler_params=_params("arbitrary", "arbitrary"),
        name="moba_decode_pages",
    )(pt_flat, qbd, kt_pages, vt_pages)


def _moba_s_combine_kernel(m_ref, l_ref, ks_ref, o_ref, qf_ref, qbd_ref, kn_ref, vn_ref, out_ref, *, n_pages, ts):
    rows = m_ref.shape[1]
    nblk = n_pages // (MOBA_BLOCK // PAGE)
    lane = _iota((rows, LANES), 1)
    kmean = ks_ref[0] * (1.0 / MOBA_BLOCK)
    gate = jnp.dot(qf_ref[0], kmean, precision=HIGHEST, preferred_element_type=F32)
    gate = jnp.where(lane < nblk, gate, NEG_INF)
    selp = _rank_select(gate, nblk, MOBA_TOPK) & (lane < nblk)
    qbd = qbd_ref[0]
    s_own = _dot_nt(qbd, _new_rows(kn_ref).astype(BF16))
    valid_own = lane <= (_iota((rows, LANES), 0) % ts)
    s_own = jnp.where(valid_own, s_own, NEG_INF)
    m_all = jnp.where(selp, m_ref[0], NEG_INF)
    big_m = jnp.maximum(jnp.max(m_all, axis=-1, keepdims=True), jnp.max(s_own, axis=-1, keepdims=True))
    wgt = jnp.where(selp, jnp.exp(m_ref[0] - big_m), 0.0)
    p_own = jnp.where(valid_own, jnp.exp(s_own - big_m), 0.0)
    denom = jnp.sum(wgt * l_ref[0], axis=-1, keepdims=True) + jnp.sum(p_own, axis=-1, keepdims=True)
    num = _head_diag(_dot(p_own.astype(BF16), _new_rows(vn_ref).astype(BF16)), MOBA_HEADS)
    for j in range(nblk):
        num = num + wgt[:, j:j + 1] * o_ref[0, j]
    out_ref[0] = num / jnp.maximum(denom, TINY)


def _moba_s_combine(m, l, ks, o, qf, qbd, kn, vn, n_pages, ts):
    batch, rows, _ = m.shape
    b3 = lambda b: (b, 0, 0)
    return pl.pallas_call(
        functools.partial(_moba_s_combine_kernel, n_pages=n_pages, ts=ts),
        grid=(batch,),
        in_specs=[pl.BlockSpec((1, rows, LANES), b3), pl.BlockSpec((1, rows, LANES), b3),
                  pl.BlockSpec((1, MOBA_WIDTH, LANES), b3),
                  pl.BlockSpec((1, o.shape[1], rows, D_HEAD), lambda b: (b, 0, 0, 0)),
                  pl.BlockSpec((1, rows, MOBA_WIDTH), b3), pl.BlockSpec((1, rows, MOBA_WIDTH), b3),
                  pl.BlockSpec((1, ts, MOBA_WIDTH), b3), pl.BlockSpec((1, ts, MOBA_WIDTH), b3)],
        out_specs=pl.BlockSpec((1, rows, D_HEAD), b3),
        out_shape=jax.ShapeDtypeStruct((batch, rows, D_HEAD), F32),
        compiler_params=_params("arbitrary"),
        name="moba_decode_combine",
    )(m, l, ks, o, qf, qbd, kn, vn)


def _block_diag_q(q, batch, ts, heads):
    q4 = q.reshape(batch, ts, heads, D_HEAD)
    eye = jnp.eye(heads, dtype=q.dtype)
    return jnp.einsum('bchd,hk->bhckd', q4, eye).reshape(batch, heads * ts, heads * D_HEAD)


def _new_rows(ref, lane0=0, width=None):
    rows = ref[0] if width is None else ref[0, :, lane0:lane0 + width]
    return jnp.concatenate([rows, jnp.zeros((LANES - rows.shape[0], rows.shape[1]), rows.dtype)], axis=0)


def _compress_s_kernel(pt_ref, k_hbm, v_hbm, pea_ref, peb_ref, wa_ref, wb_ref, w2_ref, ok_ref, ov_ref,
                       kbuf, vbuf, ksem, vsem, x_scr, a_scr, b_scr):
    per = COMPRESS_PAGES_PER_STEP
    s = pl.program_id(1)
    nseg = per * PAGE // CMP_STRIDE
    slot = _page_ring_step(pt_ref, [(k_hbm, kbuf, ksem), (v_hbm, vbuf, vsem)], per)
    for i, (buf, o_ref) in enumerate(((kbuf, ok_ref), (vbuf, ov_ref))):
        x_i, a_i, b_i, pea_i, peb_i = x_scr.at[i], a_scr.at[i], b_scr.at[i], pea_ref.at[i], peb_ref.at[i]
        for u in range(per):
            x_i[u * PAGE:(u + 1) * PAGE, :] = buf[slot, u].T
        xs = [x_i[pl.ds(l, nseg, stride=CMP_STRIDE), :] for l in range(CMP_STRIDE)]
        xa = jnp.concatenate([(xs[l] + pea_i[l:l + 1, :]).astype(BF16) for l in range(CMP_STRIDE)], axis=1)
        xb = jnp.concatenate([(xs[l] + peb_i[l:l + 1, :]).astype(BF16) for l in range(CMP_STRIDE)], axis=1)
        a_i[pl.ds(pl.multiple_of(s * nseg, nseg), nseg), :] = _dot(xa, wa_ref[i])
        b_i[pl.ds(pl.multiple_of(s * nseg, nseg), nseg), :] = _dot(xb, wb_ref[i])

        @pl.when(s == pl.num_programs(1) - 1)
        def _(i=i, o_ref=o_ref, a_i=a_i, b_i=b_i):
            total = a_i.shape[0]
            hid = a_i[...] + pltpu.roll(b_i[...], total - 1, 0)
            o_ref[0] = _dot(_gelu(hid).astype(BF16), w2_ref[i])


def _compress_s(pt_flat, k_pages, v_pages, cw_k, cw_v, batch, n_pages):
    per = COMPRESS_PAGES_PER_STEP
    total = n_pages * PAGE // CMP_STRIDE
    hidden = NSA_KV_HEADS * CMP_HIDDEN
    width = CMP_STRIDE * NSA_KV_WIDTH
    wa, wb, pea, peb, w2bd = [jnp.stack([a, b]) for a, b in zip(cw_k, cw_v)]
    full3 = lambda b, s, pt: (0, 0, 0)
    out_spec = pl.BlockSpec((1, total, NSA_KV_WIDTH), lambda b, s, pt: (b, 0, 0))
    out = jax.ShapeDtypeStruct((batch, total, NSA_KV_WIDTH), F32)
    page_buf = pltpu.VMEM((PAGE_RING_DEPTH, per, NSA_KV_WIDTH, PAGE), F32)
    return pl.pallas_call(
        _compress_s_kernel,
        grid_spec=pltpu.PrefetchScalarGridSpec(
            num_scalar_prefetch=1,
            grid=(batch, n_pages // per),
            in_specs=[pl.BlockSpec(memory_space=pl.ANY), pl.BlockSpec(memory_space=pl.ANY),
                      pl.BlockSpec((2, CMP_STRIDE, NSA_KV_WIDTH), full3),
                      pl.BlockSpec((2, CMP_STRIDE, NSA_KV_WIDTH), full3),
                      pl.BlockSpec((2, width, hidden), full3), pl.BlockSpec((2, width, hidden), full3),
                      pl.BlockSpec((2, hidden, NSA_KV_WIDTH), full3)],
            out_specs=[out_spec, out_spec],
            scratch_shapes=[page_buf, page_buf,
                            pltpu.SemaphoreType.DMA((PAGE_RING_DEPTH,)), pltpu.SemaphoreType.DMA((PAGE_RING_DEPTH,)),
                            pltpu.VMEM((2, per * PAGE, NSA_KV_WIDTH), F32), pltpu.VMEM((2, total, hidden), F32),
                            pltpu.VMEM((2, total, hidden), F32)]),
        out_shape=[out, out],
        compiler_params=_params("arbitrary", "arbitrary"),
        name="compress_decode",
    )(pt_flat, k_pages, v_pages, pea, peb, wa.reshape(2, width, hidden), wb.reshape(2, width, hidden), w2bd)


def _stack_group_q(q_ref, g):
    return jnp.concatenate(
        [q_ref[0, :, (g * NSA_GROUP + r) * D_HEAD:(g * NSA_GROUP + r + 1) * D_HEAD] * SCALE
         for r in range(NSA_GROUP)], axis=0).astype(BF16)


def _nsa_s_kernel(q_ref, kc_ref, vc_ref, wk_ref, wv_ref, kn_ref, vn_ref, gate_ref, ov_ref, part_ref, sel_ref, *, ts):
    rows = NSA_GROUP * ts
    nc_pad = kc_ref.shape[1]
    wlen = wk_ref.shape[3]
    gates = _sigmoid(gate_ref[0])
    n_i = _iota((rows, nc_pad), 1)
    valid_c = n_i < nc_pad - 1
    c_of_row = _iota((rows, 1), 0) % ts
    valid_w = _iota((rows, wlen), 1) >= c_of_row
    valid_n = _iota((rows, LANES), 1) <= c_of_row
    for g in range(NSA_KV_HEADS):
        lane0 = g * D_HEAD
        qs = _stack_group_q(q_ref, g)
        s = jnp.where(valid_c, _dot_nt(qs, kc_ref[0, :, lane0:lane0 + D_HEAD].astype(BF16)), NEG_INF)
        m = jnp.max(s, axis=-1, keepdims=True)
        p = jnp.where(valid_c, jnp.exp(s - m), 0.0)
        p = p / jnp.maximum(jnp.sum(p, axis=-1, keepdims=True), TINY)
        o_c = _dot(p.astype(BF16), vc_ref[0, :, lane0:lane0 + D_HEAD].astype(BF16))
        psum = jnp.sum(p.reshape(NSA_GROUP, ts, nc_pad), axis=0)
        imp = jnp.dot(psum, ov_ref[...], precision=HIGHEST, preferred_element_type=F32)
        keep = _rank_select(imp, LANES, SLC_TOPN - 1)
        sel_ref[0, g] = jnp.where(keep, 0.0, NEG_INF)
        s_w = jnp.where(valid_w, _dot(qs, wk_ref[0, g].astype(BF16)), NEG_INF)
        s_n = jnp.where(valid_n, _dot_nt(qs, _new_rows(kn_ref, lane0, D_HEAD).astype(BF16)), NEG_INF)
        m = jnp.maximum(jnp.max(s_w, axis=-1, keepdims=True), jnp.max(s_n, axis=-1, keepdims=True))
        p_w = jnp.where(valid_w, jnp.exp(s_w - m), 0.0)
        p_n = jnp.where(valid_n, jnp.exp(s_n - m), 0.0)
        den = jnp.sum(p_w, axis=-1, keepdims=True) + jnp.sum(p_n, axis=-1, keepdims=True)
        o_w = (_dot_nt(p_w.astype(BF16), wv_ref[0, g].astype(BF16))
               + _dot(p_n.astype(BF16), _new_rows(vn_ref, lane0, D_HEAD).astype(BF16))) / jnp.maximum(den, TINY)
        for r in range(NSA_GROUP):
            hd = g * NSA_GROUP + r
            rs = slice(r * ts, (r + 1) * ts)
            part_ref[0, :, hd * D_HEAD:(hd + 1) * D_HEAD] = (
                gates[:, 3 * hd:3 * hd + 1] * o_c[rs] + gates[:, 3 * hd + 2:3 * hd + 3] * o_w[rs])


def _nsa_s(q3, kcmp, vcmp, wk_t, wv_t, kn_w, vn_w, gate3, ts):
    batch = q3.shape[0]
    nc_pad = kcmp.shape[1]
    wlen = wk_t.shape[3]
    ov = _overlap_matrix(nc_pad, LANES)
    b3 = lambda b: (b, 0, 0)
    b4 = lambda b: (b, 0, 0, 0)
    return pl.pallas_call(
        functools.partial(_nsa_s_kernel, ts=ts),
        grid=(batch,),
        in_specs=[pl.BlockSpec((1, ts, NSA_WIDTH), b3),
                  pl.BlockSpec((1, nc_pad, NSA_KV_WIDTH), b3), pl.BlockSpec((1, nc_pad, NSA_KV_WIDTH), b3),
                  pl.BlockSpec((1, NSA_KV_HEADS, D_HEAD, wlen), b4), pl.BlockSpec((1, NSA_KV_HEADS, D_HEAD, wlen), b4),
                  pl.BlockSpec((1, ts, NSA_KV_WIDTH), lambda b: (b, 0, 2)),
                  pl.BlockSpec((1, ts, NSA_KV_WIDTH), lambda b: (b, 0, 2)),
                  pl.BlockSpec((1, ts, LANES), b3),
                  pl.BlockSpec((nc_pad, LANES), lambda b: (0, 0))],
        out_specs=[pl.BlockSpec((1, ts, NSA_WIDTH), b3), pl.BlockSpec((1, NSA_KV_HEADS, ts, LANES), b4)],
        out_shape=[jax.ShapeDtypeStruct((batch, ts, NSA_WIDTH), F32),
                   jax.ShapeDtypeStruct((batch, NSA_KV_HEADS, ts, LANES), F32)],
        compiler_params=_params("arbitrary"),
        name="nsa_decode_cmp_win",
    )(q3, kcmp, vcmp, wk_t, wv_t, kn_w, vn_w, gate3, ov)


def _slc_s_kernel(pt_ref, q_ref, bias_ref, hot_ref, k_hbm, v_hbm, kn_ref, vn_ref, gate_ref, part_ref, o_ref,
                  kbuf, vbuf, ksem, vsem, m_scr, l_scr, acc_scr, *, ts):
    per = NSA_PAGES_PER_STEP
    s = pl.program_id(1)
    slot = _page_ring_step(pt_ref, [(k_hbm, kbuf, ksem), (v_hbm, vbuf, vsem)], per)
    rows = NSA_GROUP * ts
    zero = jnp.zeros((rows, D_HEAD), BF16)
    q2 = jnp.concatenate([jnp.concatenate([_stack_group_q(q_ref, 0), zero], axis=1),
                          jnp.concatenate([zero, _stack_group_q(q_ref, 1)], axis=1)], axis=0)
    in_g0 = _iota((NSA_KV_HEADS * rows, D_HEAD), 0) < rows

    def own_group(full):
        return jnp.where(in_g0, full[:, :D_HEAD], full[:, D_HEAD:])

    @pl.when(s == 0)
    def _():
        valid_n = _iota((NSA_KV_HEADS * rows, LANES), 1) <= (_iota((NSA_KV_HEADS * rows, 1), 0) % ts)
        sc = jnp.where(valid_n, _dot_nt(q2, _new_rows(kn_ref).astype(BF16)), NEG_INF)
        m = jnp.max(sc, axis=-1, keepdims=True)
        p = jnp.exp(sc - m)
        m_scr[...] = m
        l_scr[...] = jnp.sum(p, axis=-1, keepdims=True)
        acc_scr[...] = own_group(_dot(p.astype(BF16), _new_rows(vn_ref).astype(BF16)))

    kt = jnp.concatenate([kbuf[slot, u].astype(BF16) for u in range(per)], axis=1)
    vt = jnp.concatenate([vbuf[slot, u].astype(BF16) for u in range(per)], axis=1)
    nb = per * PAGE // SLC_BLOCK
    bias = jnp.concatenate([bias_ref[0, g] for g in range(NSA_KV_HEADS) for _ in range(NSA_GROUP)], axis=0)
    cols = bias[:, :nb]
    for k in range(1, bias.shape[1] // nb):
        cols = jnp.where(s == k, bias[:, k * nb:(k + 1) * nb], cols)
    sc = _dot(jnp.concatenate([q2, cols.astype(BF16)], axis=1),
              jnp.concatenate([kt, hot_ref[...]], axis=0))
    m_old = m_scr[...]
    m_new = jnp.maximum(m_old, jnp.max(sc, axis=-1, keepdims=True))
    pf = jnp.exp(sc - m_new)
    alpha = jnp.exp(m_old - m_new)
    l_scr[...] = alpha * l_scr[...] + jnp.sum(pf, axis=-1, keepdims=True)
    acc_scr[...] = alpha * acc_scr[...] + own_group(_dot_nt(pf.astype(BF16), vt))
    m_scr[...] = m_new

    @pl.when(s == pl.num_programs(1) - 1)
    def _():
        gates = _sigmoid(gate_ref[0])
        o_s = acc_scr[...] / jnp.maximum(l_scr[...], TINY)
        for hd in range(NSA_HEADS):
            cols = slice(hd * D_HEAD, (hd + 1) * D_HEAD)
            o_ref[0, :, cols] = part_ref[0, :, cols] + gates[:, 3 * hd + 1:3 * hd + 2] * o_s[hd * ts:(hd + 1) * ts]


def _slc_s(pt_flat, q3, sel_bias, k_pages, v_pages, kn_s, vn_s, gate3, part, n_pages, ts):
    batch = q3.shape[0]
    per = NSA_PAGES_PER_STEP
    rows = NSA_GROUP * ts
    nb = per * PAGE // SLC_BLOCK
    hot = (jnp.arange(nb)[:, None] == jnp.arange(per * PAGE)[None, :] // SLC_BLOCK).astype(BF16)
    b3 = lambda b, s, pt: (b, 0, 0)
    return pl.pallas_call(
        functools.partial(_slc_s_kernel, ts=ts),
        grid_spec=pltpu.PrefetchScalarGridSpec(
            num_scalar_prefetch=1,
            grid=(batch, n_pages // per),
            in_specs=[pl.BlockSpec((1, ts, NSA_WIDTH), b3),
                      pl.BlockSpec((1, NSA_KV_HEADS, ts, LANES), lambda b, s, pt: (b, 0, 0, 0)),
                      pl.BlockSpec((nb, per * PAGE), lambda b, s, pt: (0, 0)),
                      pl.BlockSpec(memory_space=pl.ANY), pl.BlockSpec(memory_space=pl.ANY),
                      pl.BlockSpec((1, ts, NSA_KV_WIDTH), lambda b, s, pt: (b, 0, 1)),
                      pl.BlockSpec((1, ts, NSA_KV_WIDTH), lambda b, s, pt: (b, 0, 1)),
                      pl.BlockSpec((1, ts, LANES), b3), pl.BlockSpec((1, ts, NSA_WIDTH), b3)],
            out_specs=pl.BlockSpec((1, ts, NSA_WIDTH), b3),
            scratch_shapes=[pltpu.VMEM((PAGE_RING_DEPTH, per, NSA_KV_WIDTH, PAGE), F32),
                            pltpu.VMEM((PAGE_RING_DEPTH, per, NSA_KV_WIDTH, PAGE), F32),
                            pltpu.SemaphoreType.DMA((PAGE_RING_DEPTH,)), pltpu.SemaphoreType.DMA((PAGE_RING_DEPTH,)),
                            pltpu.VMEM((NSA_KV_HEADS * rows, 1), F32), pltpu.VMEM((NSA_KV_HEADS * rows, 1), F32),
                            pltpu.VMEM((NSA_KV_HEADS * rows, D_HEAD), F32)]),
        out_shape=jax.ShapeDtypeStruct((batch, ts, NSA_WIDTH), F32),
        compiler_params=_params("arbitrary", "arbitrary"),
        name="nsa_decode_slc",
    )(pt_flat, q3, sel_bias, hot, k_pages, v_pages, kn_s, vn_s, gate3, part)


def _prep_weights(w_ada, b_ada, norm_mix_g, w_in, pe_cmp_k, w_cmp_k1, w_cmp_k2, pe_cmp_v, w_cmp_v1, w_cmp_v2,
                  w_br_a, w_br_b, w_out, norm_ffn_g, w_router_grp, w_router_exp, w_expert_in, w_expert_out,
                  norm_final_g):
    wr = jnp.concatenate([w_router_grp, w_router_exp], axis=1)
    wr = jnp.pad(wr, ((0, 0), (0, LANES - wr.shape[1])))
    wr_hi = wr.astype(BF16)
    wr = jnp.stack([wr_hi, (wr - wr_hi.astype(F32)).astype(BF16)])
    return dict(
        w_ada=w_ada, b_ada=b_ada.reshape(1, -1), g_mix=norm_mix_g.reshape(1, -1),
        w_in=_reorder_w_in(w_in),
        cmp_k=_compress_weights(pe_cmp_k, w_cmp_k1, w_cmp_k2),
        cmp_v=_compress_weights(pe_cmp_v, w_cmp_v1, w_cmp_v2),
        w_br_a=w_br_a.astype(BF16), w_br_b=w_br_b.astype(BF16), w_out=w_out.astype(BF16),
        g_ffn=norm_ffn_g.reshape(1, -1), wr=wr,
        w_ein=w_expert_in.astype(BF16), w_eout=w_expert_out.astype(BF16),
        g_final=norm_final_g.reshape(1, -1))


def _prompt_layer(x, mod, w, batch, seq):
    sh1, sc1, g1, sh2, sc2, g2 = mod
    cos, sin = _rope_tables(jnp.arange(seq, dtype=jnp.int32))
    qk_a, _, q_b, k_n, v_n, gate, mg, kt_a, vt_a, kt_n, vt_n = _inproj(
        x, sc1, sh1, w['g_mix'], cos, sin, w['w_in'], 256, transposed=True)
    o_a = _moba_p(qk_a, vt_a, batch, seq)
    kcmp = _compress_p(k_n[:, :NSA_KV_WIDTH], w['cmp_k'], batch, seq)
    vcmp = _compress_p(v_n[:, :NSA_KV_WIDTH], w['cmp_v'], batch, seq)
    o_b = _nsa_p(q_b, k_n, vt_n, kcmp, vcmp, gate, batch, seq)
    x1 = _merge(o_a, o_b, mg, x, g1, w['w_br_a'], w['w_br_b'], w['w_out'], 256)
    y = _moe(x1, sc2, sh2, g2, w['g_ffn'], w['g_final'], w['wr'], w['w_ein'], w['w_eout'], 512)
    return y, (kt_a, vt_a, kt_n, vt_n)


def _sample_layer(x, mod, w, caches, win_state, page_table, batch, ts):
    sh1, sc1, g1, sh2, sc2, g2 = mod
    moba_k, moba_v, cmp_k, cmp_v, slc_k, slc_v = caches
    win_k, win_v = win_state
    n_pages = page_table.shape[1]
    assert moba_k.shape[1] == PAGE and win_k.shape[1] == WINDOW and ts <= LANES
    assert n_pages * PAGE == LANES * SLC_BLOCK and n_pages * PAGE // MOBA_BLOCK <= LANES
    n = batch * ts
    pos = n_pages * PAGE + (jnp.arange(n, dtype=jnp.int32) % ts)
    cos, sin = _rope_tables(pos)
    qk_a, v_a, q_b, k_n, v_n, gate, mg = _inproj(x, sc1, sh1, w['g_mix'], cos, sin, w['w_in'], n)
    pt_flat = page_table.reshape(-1)
    kv = NSA_KV_WIDTH
    qf = _block_diag_q(qk_a[:, :MOBA_WIDTH], batch, ts, MOBA_HEADS)
    qbd = (qf * SCALE).astype(BF16)
    m, l, ks, o = _moba_s_pass(pt_flat, qbd, _page_view(moba_k), _page_view(moba_v), batch, n_pages)
    o_a = _moba_s_combine(m, l, ks, o, qf, qbd, qk_a[:, MOBA_WIDTH:].reshape(batch, ts, MOBA_WIDTH),
                          v_a.reshape(batch, ts, MOBA_WIDTH), n_pages, ts)
    o_a = o_a.reshape(batch, MOBA_HEADS, ts, D_HEAD).transpose(0, 2, 1, 3).reshape(n, MOBA_WIDTH)
    kcmp, vcmp = _compress_s(pt_flat, _page_view(cmp_k), _page_view(cmp_v), w['cmp_k'], w['cmp_v'], batch, n_pages)
    q3 = q_b.reshape(batch, ts, NSA_WIDTH)
    gate3 = gate.reshape(batch, ts, LANES)
    kn3 = k_n.reshape(batch, ts, 3 * kv)
    vn3 = v_n.reshape(batch, ts, 3 * kv)
    part, sel = _nsa_s(q3, kcmp, vcmp, jnp.transpose(win_k, (0, 2, 3, 1)), jnp.transpose(win_v, (0, 2, 3, 1)),
                       kn3, vn3, gate3, ts)
    o_b = _slc_s(pt_flat, q3, sel, _page_view(slc_k), _page_view(slc_v), kn3, vn3, gate3, part,
                 n_pages, ts).reshape(n, NSA_WIDTH)
    x1 = _merge(o_a, o_b, mg, x, g1, w['w_br_a'], w['w_br_b'], w['w_out'], n)
    y = _moe(x1, sc2, sh2, g2, w['g_ffn'], w['g_final'], w['wr'], w['w_ein'], w['w_eout'], n)
    return y, (qk_a, v_a, k_n, v_n)


def kernel(x_prompt, x_sample, c_prompt, c_sample, cache_moba_k, cache_moba_v, cache_nsa_cmp_k, cache_nsa_cmp_v,
           cache_nsa_slc_k, cache_nsa_slc_v, state_nsa_win_k, state_nsa_win_v, page_table, w_ada, b_ada, norm_mix_g,
           w_in, pe_cmp_k, w_cmp_k1, w_cmp_k2, pe_cmp_v, w_cmp_v1, w_cmp_v2, w_br_a, w_br_b, w_out, norm_ffn_g,
           w_router_grp, w_router_exp, w_expert_in, w_expert_out, norm_final_g):
    bp, tp, _ = x_prompt.shape
    bs, ts, _ = x_sample.shape
    w = _prep_weights(w_ada[0], b_ada[0], norm_mix_g[0], w_in[0], pe_cmp_k[0], w_cmp_k1[0], w_cmp_k2[0], pe_cmp_v[0],
                      w_cmp_v1[0], w_cmp_v2[0], w_br_a[0], w_br_b[0], w_out[0], norm_ffn_g[0], w_router_grp[0],
                      w_router_exp[0], w_expert_in[0], w_expert_out[0], norm_final_g)
    mod = _ada(jnp.concatenate([c_prompt, c_sample], axis=0), w['w_ada'], w['b_ada'])
    mod_p = [m.reshape(bp, 1, D_MODEL) for m in jnp.split(mod[:bp], 6, axis=-1)]
    y_p, new_p = _prompt_layer(x_prompt.reshape(bp * tp, D_MODEL), mod_p, w, bp, tp)

    mod_s = [jnp.repeat(m, ts, axis=0).reshape(1, bs * ts, D_MODEL) for m in jnp.split(mod[bp:], 6, axis=-1)]
    caches = (cache_moba_k[0], cache_moba_v[0], cache_nsa_cmp_k[0], cache_nsa_cmp_v[0], cache_nsa_slc_k[0],
              cache_nsa_slc_v[0])
    y_s, new_s = _sample_layer(x_sample.reshape(bs * ts, D_MODEL), mod_s, w, caches,
                               (state_nsa_win_k[0], state_nsa_win_v[0]), page_table, bs, ts)

    kv = NSA_KV_WIDTH

    def new_rows(new, b, t):
        qk, v, k_n, v_n = new
        rows = lambda a, heads: a.reshape(1, b, t, heads, D_HEAD)
        return (rows(qk[:, MOBA_WIDTH:], MOBA_HEADS), rows(v, MOBA_HEADS),
                rows(k_n[:, :kv], NSA_KV_HEADS), rows(v_n[:, :kv], NSA_KV_HEADS),
                rows(k_n[:, kv:2 * kv], NSA_KV_HEADS), rows(v_n[:, kv:2 * kv], NSA_KV_HEADS),
                rows(k_n[:, 2 * kv:], NSA_KV_HEADS), rows(v_n[:, 2 * kv:], NSA_KV_HEADS))

    def new_rows_t(new, b, t):
        kt_a, vt_a, kt_n, vt_n = new
        rows = lambda a: jnp.transpose(a.reshape(1, b, a.shape[1] // D_HEAD, D_HEAD, a.shape[2]), (0, 1, 4, 2, 3))
        return (rows(kt_a), rows(vt_a), rows(kt_n[:, :kv]), rows(vt_n[:, :kv]),
                rows(kt_n[:, kv:2 * kv]), rows(vt_n[:, kv:2 * kv]),
                rows(kt_n[:, 2 * kv:, t - wb:]), rows(vt_n[:, 2 * kv:, t - wb:]))

    wb = state_nsa_win_k.shape[2]
    outs_p = new_rows_t(new_p, bp, tp)
    outs_s = new_rows(new_s, bs, ts)
    win_k = jnp.concatenate([state_nsa_win_k, outs_s[6]], axis=2)[:, :, ts:]
    win_v = jnp.concatenate([state_nsa_win_v, outs_s[7]], axis=2)[:, :, ts:]
    outs_s = outs_s[:6] + (win_k, win_v)
    return (y_p.reshape(bp, tp, D_MODEL), y_s.reshape(bs, ts, D_MODEL)) + outs_p + outs_s
```

```python
import functools

import jax
import jax.numpy as jnp
from jax import lax
from jax.experimental import pallas as pl
from jax.experimental.pallas import tpu as pltpu

D_MODEL = 1024
D_HEAD = 64
HALF = D_HEAD // 2
MOBA_HEADS = 8
MOBA_BLOCK = 256
MOBA_TOPK = 3
NSA_HEADS = 8
NSA_KV_HEADS = 2
NSA_GROUP = NSA_HEADS // NSA_KV_HEADS
CMP_LEN = 32
CMP_STRIDE = 16
CMP_HIDDEN = 128
SLC_BLOCK = 64
SLC_TOPN = 16
WINDOW = 512
N_GROUPS = 4
EXPERTS_PER_GROUP = 8
N_EXPERTS = N_GROUPS * EXPERTS_PER_GROUP
D_EXPERT = 256
ROPE_THETA = 10000.0
EPS = 1e-6
NEG_INF = -1e30
BIG = 1e30
TINY = 1e-30
MOBA_WIDTH = MOBA_HEADS * D_HEAD
NSA_WIDTH = NSA_HEADS * D_HEAD
NSA_KV_WIDTH = NSA_KV_HEADS * D_HEAD
SCALE = D_HEAD ** -0.5

LANES = 128
VMEM_LIMIT = 48 * 1024 * 1024

F32 = jnp.float32
BF16 = jnp.bfloat16
HIGHEST = lax.Precision.HIGHEST


def _params(*sem):
    return pltpu.CompilerParams(dimension_semantics=sem, vmem_limit_bytes=VMEM_LIMIT)


def _dot(a, b):
    return jnp.dot(a, b, preferred_element_type=F32)


def _dot_nt(a, b, precision=None):
    return lax.dot_general(a, b, (((1,), (1,)), ((), ())), precision=precision,
                           preferred_element_type=F32)


def _sigmoid(x):
    return 1.0 / (1.0 + jnp.exp(-x))


def _iota(shape, dim):
    return lax.broadcasted_iota(jnp.int32, shape, dim)


def _ada_kernel(c_ref, w_ref, b_ref, o_ref):
    c = c_ref[...]
    s = c * _sigmoid(c)
    o_ref[...] = _dot(s.astype(BF16), w_ref[...].astype(BF16)) + b_ref[...]


def _ada(c, w, b):
    n = c.shape[0]
    tn = 1024
    return pl.pallas_call(
        _ada_kernel,
        grid=(w.shape[1] // tn,),
        in_specs=[pl.BlockSpec((n, D_MODEL), lambda j: (0, 0)),
                  pl.BlockSpec((D_MODEL, tn), lambda j: (0, j)),
                  pl.BlockSpec((1, tn), lambda j: (0, j))],
        out_specs=pl.BlockSpec((n, tn), lambda j: (0, j)),
        out_shape=jax.ShapeDtypeStruct((n, w.shape[1]), F32),
        compiler_params=_params("arbitrary"),
        name="ada",
    )(c, w, b)


_IN_GROUPS = ((2 * MOBA_WIDTH, True),
              (MOBA_WIDTH, False),
              (NSA_WIDTH, True),
              (3 * NSA_KV_WIDTH, True),
              (3 * NSA_KV_WIDTH, False),
              (LANES, False))
_IN_COLS_PAD = sum(w for w, _ in _IN_GROUPS)


def _norm_mod(x, g, sc, sh):
    ms = jnp.mean(x * x, axis=-1, keepdims=True)
    y = x * lax.rsqrt(ms + EPS) * g
    return y * (1.0 + sc) + sh


def _inproj_kernel(x_ref, sc_ref, sh_ref, g_ref, cos_ref, sin_ref, w_ref, *out_refs):
    h = _norm_mod(x_ref[...], g_ref[...], sc_ref[0], sh_ref[0]).astype(BF16)
    cos = cos_ref[...]
    sin = sin_ref[...]
    first_half = (_iota(cos.shape, 1) & (D_HEAD - 1)) < HALF

    def rope(y):
        rot = jnp.where(first_half, pltpu.roll(y, LANES - HALF, 1), pltpu.roll(y, HALF, 1))
        return y * cos + rot * sin

    n_groups = len(_IN_GROUPS)
    t_refs = dict(zip(_IN_T_GROUPS, out_refs[n_groups:]))
    col = 0
    for gi, (out_ref, (width, rotary)) in enumerate(zip(out_refs, _IN_GROUPS)):
        chunk = min(width, 512)
        for c in range(0, width, chunk):
            cw = min(chunk, width - c)
            y = _dot(h, w_ref[:, col + c:col + c + cw])
            for s in range(0, cw, LANES):
                piece = y[:, s:s + LANES]
                piece = rope(piece) if rotary else piece
                if c + s < out_ref.shape[1]:
                    out_ref[:, c + s:c + s + LANES] = piece
                if gi in t_refs and c + s >= _IN_T_GROUPS[gi]:
                    t0 = c + s - _IN_T_GROUPS[gi]
                    t_refs[gi][0, t0:t0 + LANES, :] = piece.T
        col += width


_IN_T_GROUPS = {0: MOBA_WIDTH, 1: 0, 3: 0, 4: 0}
_IN_T_ROW_COLS = {1: LANES, 4: NSA_KV_WIDTH}


def _inproj(x, sc, sh, g, cos, sin, w, tm, transposed=False):
    n = x.shape[0]
    nb, r, _ = sc.shape
    tiles_per_b = (n // nb) // tm
    tab_tiles = cos.shape[0] // tm
    row = lambda i: (i, 0)
    mod = lambda i: (i // tiles_per_b, 0, 0)
    tab = lambda i: (i % tab_tiles, 0)
    widths = [_IN_T_ROW_COLS.get(gi, wd) if transposed else wd for gi, (wd, _) in enumerate(_IN_GROUPS)]
    out_specs = [pl.BlockSpec((tm, wd), row) for wd in widths]
    out_shape = [jax.ShapeDtypeStruct((n, wd), F32) for wd in widths]
    if transposed:
        for gi, first in _IN_T_GROUPS.items():
            cols = _IN_GROUPS[gi][0] - first
            out_specs.append(pl.BlockSpec((1, cols, tm), lambda i: (i // tiles_per_b, 0, i % tiles_per_b)))
            out_shape.append(jax.ShapeDtypeStruct((nb, cols, n // nb), F32))
    return pl.pallas_call(
        _inproj_kernel,
        grid=(n // tm,),
        in_specs=[pl.BlockSpec((tm, D_MODEL), row),
                  pl.BlockSpec((1, r, D_MODEL), mod),
                  pl.BlockSpec((1, r, D_MODEL), mod),
                  pl.BlockSpec((1, D_MODEL), lambda i: (0, 0)),
                  pl.BlockSpec((tm, LANES), tab),
                  pl.BlockSpec((tm, LANES), tab),
                  pl.BlockSpec((D_MODEL, _IN_COLS_PAD), lambda i: (0, 0))],
        out_specs=out_specs,
        out_shape=out_shape,
        compiler_params=_params("arbitrary"),
        name="inproj",
    )(x, sc, sh, g, cos, sin, w)


def _reorder_w_in(w_in):
    kv0 = 3 * MOBA_WIDTH + NSA_WIDTH
    kvs = [w_in[:, kv0 + i * NSA_KV_WIDTH:kv0 + (i + 1) * NSA_KV_WIDTH] for i in range(6)]
    g0 = kv0 + 6 * NSA_KV_WIDTH
    ng = 3 * NSA_HEADS
    gate = jnp.pad(w_in[:, g0:g0 + ng], ((0, 0), (0, LANES - ng)))
    parts = [w_in[:, :kv0], kvs[0], kvs[2], kvs[4], kvs[1], kvs[3], kvs[5], gate]
    return jnp.concatenate(parts, axis=1).astype(BF16), w_in[:, g0 + ng:].astype(BF16)


def _rope_tables(pos):
    inv = ROPE_THETA ** (-jnp.arange(HALF, dtype=F32) / HALF)
    ang = pos.astype(F32)[:, None] * inv[None, :]
    cos = jnp.cos(ang)
    sin = jnp.sin(ang)
    cos = jnp.concatenate([cos, cos, cos, cos], axis=1)
    sin = jnp.concatenate([-sin, sin, -sin, sin], axis=1)
    return cos, sin


def _rank_select(score, n_cols, n_keep):
    lane = _iota(score.shape, 1)
    rank = jnp.zeros(score.shape, jnp.int32)
    for jp in range(n_cols):
        col = score[:, jp:jp + 1]
        beats = (col > score) | ((col == score) & (lane > jp))
        rank = rank + beats.astype(jnp.int32)
    return rank < n_keep


def _rank_select_t(score, n_rows, n_keep):
    row = _iota(score.shape, 0)
    rank = jnp.zeros(score.shape, jnp.int32)
    for jp in range(n_rows):
        r = score[jp:jp + 1, :]
        beats = (r > score) | ((r == score) & (row > jp))
        rank = rank + beats.astype(jnp.int32)
    return rank < n_keep


SUBLANES = 8


def _fold_rows(x, op):
    return op(x.reshape(x.shape[0] // SUBLANES, SUBLANES, x.shape[1]), axis=0)


def _attend_t(score_fn, vt_fn, lo, hi, width, n_chains, cache_ref):
    def chunk_loop(step_fn, carry):
        n = hi - lo
        carry = lax.fori_loop(0, n // 2, lambda i, cr: step_fn(lo + 2 * i + 1, step_fn(lo + 2 * i, cr)), carry)
        return lax.cond(n % 2 == 1, lambda cr: step_fn(hi - 1, cr), lambda cr: cr, carry)

    def max_step(c, own, m):
        out = []
        for ch, (mi, s) in enumerate(zip(m, score_fn(c, own))):
            cache_ref[ch, c - lo] = s
            out.append(jnp.maximum(mi, _fold_rows(s, jnp.max)))
        return tuple(out)

    m = max_step(hi, True, tuple(jnp.full((SUBLANES, width), NEG_INF, F32) for _ in range(n_chains)))
    m = chunk_loop(lambda c, mm: max_step(c, False, mm), m)
    shift = [jnp.max(mi, axis=0, keepdims=True) for mi in m]

    def acc_step(c, carry):
        out = []
        for ch, ((l, acc), vt) in enumerate(zip(carry, vt_fn(c))):
            p = jnp.exp(cache_ref[ch, c - lo] - shift[ch])
            out.append((l + _fold_rows(p, jnp.sum), acc + _dot(vt, p.astype(BF16))))
        return tuple(out)

    zero = tuple((jnp.zeros((SUBLANES, width), F32), jnp.zeros((D_HEAD, width), F32)) for _ in range(n_chains))
    carry = acc_step(hi, zero)
    carry = chunk_loop(acc_step, carry)
    return [acc / jnp.maximum(jnp.sum(l, axis=0, keepdims=True), TINY) for l, acc in carry]


def _key_aug(n_keys, block, n_blocks):
    lane = _iota((n_keys, D_HEAD), 1)
    blk = _iota((n_keys, D_HEAD), 0) // block
    return jnp.where((lane < n_blocks) & (lane == blk), 1.0, 0.0).astype(BF16)


def _query_aug(qt, bias):
    pad = jnp.zeros((D_HEAD - bias.shape[0], qt.shape[1]), F32)
    return jnp.concatenate([qt, bias, pad], axis=0).astype(BF16)


MOBA_STEP_WIDTH = 512


def _moba_p_kernel(q_ref, k_ref, v_ref, o_ref, ka_scr, vt_scr, km_scr, s_scr):
    qi = pl.program_id(2)
    tq = tk = MOBA_BLOCK
    seq = k_ref.shape[1]
    nblk = seq // MOBA_BLOCK
    width = q_ref.shape[2]
    nh = width // D_HEAD

    @pl.when(qi == 0)
    def _():
        kf = k_ref[0]
        km_scr[...] = jnp.mean(kf.reshape(nblk, MOBA_BLOCK, width), axis=1)
        aug = _key_aug(seq, MOBA_BLOCK, nblk)
        for hh in range(nh):
            ka_scr[hh] = jnp.concatenate([kf[:, hh * D_HEAD:(hh + 1) * D_HEAD].astype(BF16), aug], axis=1)
        for j in range(nblk):
            vt_scr[j] = v_ref[0, :, j * tk:(j + 1) * tk].astype(BF16)

    q2t = q_ref[0].T
    km = km_scr[...]
    klane = _iota(km.shape, 1)
    blk = _iota((nblk, tq), 0)
    causal_t = _iota((tk, tq), 0) <= _iota((tk, tq), 1)
    qts, biases = [], []
    for hh in range(nh):
        kmh = jnp.where((klane >= hh * D_HEAD) & (klane < (hh + 1) * D_HEAD), km, 0.0)
        gate = jnp.dot(kmh, q2t, precision=HIGHEST, preferred_element_type=F32)
        gate = jnp.where(blk < qi, gate, NEG_INF)
        keep = _rank_select_t(gate, nblk, MOBA_TOPK) & (blk < qi)
        biases.append(jnp.where(keep, 0.0, NEG_INF))
        qts.append(q2t[hh * D_HEAD:(hh + 1) * D_HEAD, :] * SCALE)

    q_own = [_query_aug(qts[hh], jnp.zeros_like(biases[hh])) for hh in range(nh)]
    q_past = [_query_aug(qts[hh], biases[hh]) for hh in range(nh)]

    def scores(c, own):
        out = []
        for hh in range(nh):
            kj = ka_scr[hh, pl.ds(pl.multiple_of(c * tk, tk), tk), :]
            s = _dot(kj, q_own[hh] if own else q_past[hh])
            out.append(jnp.where(causal_t, s, NEG_INF) if own else s)
        return out

    def values(c):
        return [vt_scr[c, hh * D_HEAD:(hh + 1) * D_HEAD, :] for hh in range(nh)]

    o_ref[0] = jnp.concatenate(_attend_t(scores, values, 0, qi, tq, nh, s_scr), axis=0).T


def _moba_p(qk, vt, batch, seq):
    qk3 = qk.reshape(batch, seq, 2 * MOBA_WIDTH)
    sw = MOBA_STEP_WIDTH
    pairs = MOBA_WIDTH // sw
    nblk = seq // MOBA_BLOCK
    out = pl.pallas_call(
        _moba_p_kernel,
        grid=(batch, pairs, nblk),
        in_specs=[pl.BlockSpec((1, MOBA_BLOCK, sw), lambda b, h, i: (b, i, h)),
                  pl.BlockSpec((1, seq, sw), lambda b, h, i: (b, 0, pairs + h)),
                  pl.BlockSpec((1, sw, seq), lambda b, h, i: (b, h, 0))],
        out_specs=pl.BlockSpec((1, MOBA_BLOCK, sw), lambda b, h, i: (b, i, h)),
        out_shape=jax.ShapeDtypeStruct((batch, seq, MOBA_WIDTH), F32),
        scratch_shapes=[pltpu.VMEM((sw // D_HEAD, seq, 2 * D_HEAD), BF16), pltpu.VMEM((nblk, sw, MOBA_BLOCK), BF16),
                        pltpu.VMEM((nblk, sw), F32),
                        pltpu.VMEM((sw // D_HEAD, nblk, MOBA_BLOCK, MOBA_BLOCK), F32)],
        compiler_params=_params("arbitrary", "arbitrary", "arbitrary"),
        name="moba_prompt",
    )(qk3, qk3, vt)
    return out.reshape(batch * seq, MOBA_WIDTH)


def _gelu(x):
    return 0.5 * x * (1.0 + jnp.tanh(0.7978845608028654 * (x + 0.044715 * x * x * x)))


def _compress_p_kernel(seg_ref, pea_ref, peb_ref, wa_ref, wb_ref, w2_ref, o_ref):
    seg = seg_ref[0]
    a = _dot((seg + pea_ref[...]).astype(BF16), wa_ref[...])
    b = _dot((seg + peb_ref[...]).astype(BF16), wb_ref[...])
    nseg = seg.shape[0]
    hid = a + pltpu.roll(b, nseg - 1, 0)
    o_ref[0] = _dot(_gelu(hid).astype(BF16), w2_ref[...])


def _compress_weights(pe, w1, w2):
    g = NSA_KV_HEADS
    eye = jnp.eye(g, dtype=F32)
    w1r = w1.reshape(CMP_LEN, D_HEAD, CMP_HIDDEN)

    def half(lo):
        w = w1r[lo:lo + CMP_STRIDE]
        wbd = jnp.einsum('ldf,gh->lgdhf', w, eye)
        p = jnp.broadcast_to(pe[lo:lo + CMP_STRIDE, None, :], (CMP_STRIDE, g, D_HEAD))
        return wbd.reshape(CMP_STRIDE, g * D_HEAD, g * CMP_HIDDEN).astype(BF16), p.reshape(CMP_STRIDE, g * D_HEAD)

    wa, pea = half(0)
    wb, peb = half(CMP_STRIDE)
    w2bd = jnp.einsum('fd,gh->gfhd', w2, eye).reshape(g * CMP_HIDDEN, g * D_HEAD).astype(BF16)
    return wa, wb, pea, peb, w2bd


def _compress_p(rows, cw, batch, seq):
    wa, wb, pea, peb, w2bd = cw
    nseg = seq // CMP_STRIDE
    width = CMP_STRIDE * NSA_KV_WIDTH
    seg = rows.reshape(batch, nseg, width)
    full = lambda b: (0, 0)
    return pl.pallas_call(
        _compress_p_kernel,
        grid=(batch,),
        in_specs=[pl.BlockSpec((1, nseg, width), lambda b: (b, 0, 0)),
                  pl.BlockSpec((1, width), full), pl.BlockSpec((1, width), full),
                  pl.BlockSpec((width, NSA_KV_HEADS * CMP_HIDDEN), full),
                  pl.BlockSpec((width, NSA_KV_HEADS * CMP_HIDDEN), full),
                  pl.BlockSpec((NSA_KV_HEADS * CMP_HIDDEN, NSA_KV_WIDTH), full)],
        out_specs=pl.BlockSpec((1, nseg, NSA_KV_WIDTH), lambda b: (b, 0, 0)),
        out_shape=jax.ShapeDtypeStruct((batch, nseg, NSA_KV_WIDTH), F32),
        compiler_params=_params("arbitrary"),
        name="compress_prompt",
    )(seg, pea.reshape(1, width), peb.reshape(1, width), wa.reshape(width, -1), wb.reshape(width, -1), w2bd)


NSA_TQ = 256
NSA_TK = 256


def _overlap_matrix(nc_pad, nslc_pad):
    cs = jnp.arange(nc_pad)[:, None] * CMP_STRIDE
    ss = jnp.arange(nslc_pad)[None, :] * SLC_BLOCK
    return ((cs < ss + SLC_BLOCK) & (cs + CMP_LEN > ss)).astype(F32)


def _nsa_p_kernel(q_ref, kc_ref, vc_ref, ks_ref, vs_ref, kw_ref, vw_ref, gate_ref, ovt_ref, o_ref,
                  ksa_scr, kwb_scr, vst_scr, vwt_scr, ss_scr, sw_scr):
    tq, tk = NSA_TQ, NSA_TK
    t = pl.program_id(1)
    q0 = t * tq
    seq = ks_ref.shape[1]
    nc_pad = kc_ref.shape[1]
    nslc = seq // SLC_BLOCK
    width = NSA_GROUP * tq

    @pl.when(t == 0)
    def _():
        aug_s = _key_aug(seq, SLC_BLOCK, nslc)
        for g in range(NSA_KV_HEADS):
            rows = slice(g * D_HEAD, (g + 1) * D_HEAD)
            ksa_scr[g] = jnp.concatenate([ks_ref[0, :, rows].astype(BF16), aug_s], axis=1)
        kwb_scr[...] = kw_ref[0].astype(BF16)
        for j in range(seq // tk):
            vst_scr[j] = vs_ref[0, :, j * tk:(j + 1) * tk].astype(BF16)
            vwt_scr[j] = vw_ref[0, :, j * tk:(j + 1) * tk].astype(BF16)

    qt_all = q_ref[0].T
    gates_t = _sigmoid(gate_ref[0]).T
    kct = kc_ref[0].astype(BF16)
    vct = vc_ref[0].T.astype(BF16)
    pos = q0 + _iota((1, tq), 1)
    n_i = _iota((nc_pad, tq), 0)
    valid_c = (n_i < nc_pad - 1) & (n_i * CMP_STRIDE + (CMP_LEN - 1) <= pos)
    valid_c4 = jnp.concatenate([valid_c] * NSA_GROUP, axis=1)
    q_blk = pos >> 6
    jrow = _iota((nslc, tq), 0)
    krow = _iota((tk, tq), 0)
    cd = q0 // tk
    c_win = jnp.maximum(q0 - WINDOW, 0) // tk

    def tile(b):
        return jnp.concatenate([b] * NSA_GROUP, axis=1)

    def win_bias(c):
        dist = pos - (c * tk + krow)
        return jnp.where((dist >= 0) & (dist <= WINDOW), 0.0, NEG_INF)

    qts, o_cs, biases = [], [], []
    for g in range(NSA_KV_HEADS):
        rows = slice(g * D_HEAD, (g + 1) * D_HEAD)
        qt = jnp.concatenate(
            [qt_all[(g * NSA_GROUP + r) * D_HEAD:(g * NSA_GROUP + r + 1) * D_HEAD, :] * SCALE
             for r in range(NSA_GROUP)], axis=1)
        qts.append(qt)
        s = jnp.where(valid_c4, _dot(kct[:, rows], qt.astype(BF16)), NEG_INF)
        m = jnp.max(s, axis=0, keepdims=True)
        p = jnp.where(valid_c4, jnp.exp(s - m), 0.0)
        p = p / jnp.maximum(jnp.sum(p, axis=0, keepdims=True), TINY)
        o_cs.append(_dot(vct[rows, :], p.astype(BF16)))
        psum = p[:, :tq]
        for r in range(1, NSA_GROUP):
            psum = psum + p[:, r * tq:(r + 1) * tq]
        imp = jnp.dot(ovt_ref[...], psum, precision=HIGHEST, preferred_element_type=F32)
        imp = jnp.where(jrow == q_blk, BIG, jnp.where(jrow < q_blk, imp, NEG_INF))
        keep = _rank_select_t(imp, nslc, SLC_TOPN) & (jrow <= q_blk)
        biases.append(tile(jnp.where(keep, 0.0, NEG_INF)))

    causal_own = tile(cd * tk + krow <= pos)

    qa_slc = [_query_aug(qts[g], biases[g]) for g in range(NSA_KV_HEADS)]
    qb_win = [qts[g].astype(BF16) for g in range(NSA_KV_HEADS)]

    def slc_scores(c, own):
        out = []
        for g in range(NSA_KV_HEADS):
            s = _dot(ksa_scr[g, pl.ds(pl.multiple_of(c * tk, tk), tk), :], qa_slc[g])
            out.append(jnp.where(causal_own, s, NEG_INF) if own else s)
        return out

    def win_scores(c, own):
        wb = tile(win_bias(c))
        return [_dot(kwb_scr[pl.ds(pl.multiple_of(c * tk, tk), tk), g * D_HEAD:(g + 1) * D_HEAD], qb_win[g]) + wb
                for g in range(NSA_KV_HEADS)]

    def values(vt_scr):
        return lambda c: [vt_scr[c, g * D_HEAD:(g + 1) * D_HEAD, :] for g in range(NSA_KV_HEADS)]

    o_ss = _attend_t(slc_scores, values(vst_scr), 0, cd, width, NSA_KV_HEADS, ss_scr)
    o_ws = _attend_t(win_scores, values(vwt_scr), c_win, cd, width, NSA_KV_HEADS, sw_scr)
    outs = []
    for g in range(NSA_KV_HEADS):
        o_c, o_s, o_w = o_cs[g], o_ss[g], o_ws[g]
        for r in range(NSA_GROUP):
            hd = g * NSA_GROUP + r
            cs = slice(r * tq, (r + 1) * tq)
            outs.append(gates_t[3 * hd:3 * hd + 1, :] * o_c[:, cs] + gates_t[3 * hd + 1:3 * hd + 2, :] * o_s[:, cs]
                        + gates_t[3 * hd + 2:3 * hd + 3, :] * o_w[:, cs])
    o_ref[0] = jnp.concatenate(outs, axis=0).T


def _nsa_p(q, kn, vnt, kcmp, vcmp, gate, batch, seq):
    q3 = q.reshape(batch, seq, NSA_WIDTH)
    kn3 = kn.reshape(batch, seq, 3 * NSA_KV_WIDTH)
    g3 = gate.reshape(batch, seq, LANES)
    nc_pad = kcmp.shape[1]
    nslc = seq // SLC_BLOCK
    nchunk = seq // NSA_TK
    ovt = _overlap_matrix(nc_pad, nslc).T
    tile = lambda b, t: (b, t, 0)
    cmp_spec = pl.BlockSpec((1, nc_pad, NSA_KV_WIDTH), lambda b, t: (b, 0, 0))
    out = pl.pallas_call(
        _nsa_p_kernel,
        grid=(batch, seq // NSA_TQ),
        in_specs=[pl.BlockSpec((1, NSA_TQ, NSA_WIDTH), tile), cmp_spec, cmp_spec,
                  pl.BlockSpec((1, seq, LANES), lambda b, t: (b, 0, 1)),
                  pl.BlockSpec((1, LANES, seq), lambda b, t: (b, 1, 0)),
                  pl.BlockSpec((1, seq, LANES), lambda b, t: (b, 0, 2)),
                  pl.BlockSpec((1, LANES, seq), lambda b, t: (b, 2, 0)),
                  pl.BlockSpec((1, NSA_TQ, LANES), tile),
                  pl.BlockSpec((nslc, nc_pad), lambda b, t: (0, 0))],
        out_specs=pl.BlockSpec((1, NSA_TQ, NSA_WIDTH), tile),
        out_shape=jax.ShapeDtypeStruct((batch, seq, NSA_WIDTH), F32),
        scratch_shapes=[pltpu.VMEM((NSA_KV_HEADS, seq, 2 * D_HEAD), BF16), pltpu.VMEM((seq, LANES), BF16),
                        pltpu.VMEM((nchunk, LANES, NSA_TK), BF16), pltpu.VMEM((nchunk, LANES, NSA_TK), BF16),
                        pltpu.VMEM((NSA_KV_HEADS, nchunk, NSA_TK, NSA_GROUP * NSA_TQ), F32),
                        pltpu.VMEM((NSA_KV_HEADS, WINDOW // NSA_TK + 1, NSA_TK, NSA_GROUP * NSA_TQ), F32)],
        compiler_params=_params("arbitrary", "arbitrary"),
        name="nsa_prompt",
    )(q3, kcmp, vcmp, kn3, vnt, kn3, vnt, g3, ovt)
    return out.reshape(batch * seq, NSA_WIDTH)


def _merge_kernel(oa_ref, ob_ref, x_ref, sc_ref, sh_ref, g1_ref, gn_ref, wm_ref, wa_ref, wb_ref, wo_ref, o_ref):
    x = x_ref[...]
    h = _norm_mod(x, gn_ref[...], sc_ref[0], sh_ref[0]).astype(BF16)
    mg = _dot(h, wm_ref[...])
    a = _dot(oa_ref[...].astype(BF16), wa_ref[...])
    b = _dot(ob_ref[...].astype(BF16), wb_ref[...])
    mix = _sigmoid(mg[:, :D_MODEL]) * a + _sigmoid(mg[:, D_MODEL:]) * b
    o_ref[...] = x + g1_ref[0] * _dot(mix.astype(BF16), wo_ref[...])


def _merge(oa, ob, x, sc, sh, g1, gn, wm, wa, wb, wo, tm):
    n = x.shape[0]
    nb, r, _ = g1.shape
    tiles_per_b = (n // nb) // tm
    row = lambda i: (i, 0)
    full = lambda i: (0, 0)
    mod = pl.BlockSpec((1, r, D_MODEL), lambda i: (i // tiles_per_b, 0, 0))
    return pl.pallas_call(
        _merge_kernel,
        grid=(n // tm,),
        in_specs=[pl.BlockSpec((tm, MOBA_WIDTH), row), pl.BlockSpec((tm, NSA_WIDTH), row),
                  pl.BlockSpec((tm, D_MODEL), row), mod, mod, mod,
                  pl.BlockSpec((1, D_MODEL), full), pl.BlockSpec((D_MODEL, 2 * D_MODEL), full),
                  pl.BlockSpec((MOBA_WIDTH, D_MODEL), full), pl.BlockSpec((NSA_WIDTH, D_MODEL), full),
                  pl.BlockSpec((D_MODEL, D_MODEL), full)],
        out_specs=pl.BlockSpec((tm, D_MODEL), row),
        out_shape=jax.ShapeDtypeStruct((n, D_MODEL), F32),
        compiler_params=_params("arbitrary"),
        name="merge",
    )(oa, ob, x, sc, sh, g1, gn, wm, wa, wb, wo)


def _route(logits):
    lane = _iota(logits.shape, 1)
    is_grp = lane < N_GROUPS
    lg = jnp.where(is_grp, logits, NEG_INF)
    mg = jnp.max(lg, axis=-1, keepdims=True)
    pg = jnp.where(is_grp, jnp.exp(lg - mg), 0.0)
    pg = pg / jnp.sum(pg, axis=-1, keepdims=True)
    g_w = jnp.max(pg, axis=-1, keepdims=True)
    g_sel = jnp.min(jnp.where(is_grp & (pg == g_w), lane, LANES), axis=-1, keepdims=True)
    e_lane = lane - N_GROUPS
    in_grp = (e_lane >= 0) & (e_lane < N_EXPERTS) & ((e_lane >> 3) == g_sel)
    le = jnp.where(in_grp, logits, NEG_INF)
    me = jnp.max(le, axis=-1, keepdims=True)
    pe = jnp.where(in_grp, jnp.exp(le - me), 0.0)
    pe = pe / jnp.sum(pe, axis=-1, keepdims=True)
    v1 = jnp.max(pe, axis=-1, keepdims=True)
    i1 = jnp.min(jnp.where(in_grp & (pe == v1), lane, LANES), axis=-1, keepdims=True)
    rest = in_grp & (lane != i1)
    pr = jnp.where(rest, pe, -1.0)
    v2 = jnp.max(pr, axis=-1, keepdims=True)
    i2 = jnp.min(jnp.where(rest & (pr == v2), lane, LANES), axis=-1, keepdims=True)
    tot = v1 + v2
    comb = jnp.where(lane == i1, v1 / tot, 0.0) + jnp.where(lane == i2, v2 / tot, 0.0)
    comb = comb * g_w
    return pltpu.roll(comb, LANES - N_GROUPS, 1)


MOE_EXPERTS_PER_STEP = 8


def _moe_kernel(x_ref, sc_ref, sh_ref, g2_ref, gn_ref, gf_ref, wr_ref, win_ref, wout_ref, o_ref,
                h_scr, comb_scr, acc_scr):
    e = pl.program_id(1)

    @pl.when(e == 0)
    def _():
        h = _norm_mod(x_ref[...], gn_ref[...], sc_ref[0], sh_ref[0])
        hb = h.astype(BF16)
        h_scr[...] = hb
        h_lo = (h - hb.astype(F32)).astype(BF16)
        logits = _dot(hb, wr_ref[0]) + _dot(hb, wr_ref[1]) + _dot(h_lo, wr_ref[0])
        comb_scr[...] = _route(logits)
        acc_scr[...] = jnp.zeros_like(acc_scr)

    per = win_ref.shape[0]
    hb = h_scr[...]
    comb = comb_scr[...]
    lane = _iota(comb.shape, 1)
    acts = []
    for j in range(per):
        hid = _dot(hb, win_ref[j])
        a = hid[:, :D_EXPERT]
        b = hid[:, D_EXPERT:]
        w = jnp.sum(jnp.where(lane == e * per + j, comb, 0.0), axis=1, keepdims=True)
        acts.append((a * _sigmoid(a) * b * w).astype(BF16))
    act = jnp.concatenate(acts, axis=1)
    acc_scr[...] += _dot(act, wout_ref[...].reshape(per * D_EXPERT, D_MODEL))

    @pl.when(e == pl.num_programs(1) - 1)
    def _():
        y = x_ref[...] + g2_ref[0] * acc_scr[...]
        ms = jnp.mean(y * y, axis=-1, keepdims=True)
        o_ref[...] = y * lax.rsqrt(ms + EPS) * gf_ref[...]


def _moe(x, sc, sh, g2, gn, gf, wr, w_ein, w_eout, tm):
    n = x.shape[0]
    nb, r, _ = sc.shape
    tiles_per_b = (n // nb) // tm
    row = lambda i, e: (i, 0)
    mod = lambda i, e: (i // tiles_per_b, 0, 0)
    full = lambda i, e: (0, 0)
    per = MOE_EXPERTS_PER_STEP
    return pl.pallas_call(
        _moe_kernel,
        grid=(n // tm, N_EXPERTS // per),
        in_specs=[pl.BlockSpec((tm, D_MODEL), row),
                  pl.BlockSpec((1, r, D_MODEL), mod), pl.BlockSpec((1, r, D_MODEL), mod),
                  pl.BlockSpec((1, r, D_MODEL), mod),
                  pl.BlockSpec((1, D_MODEL), full), pl.BlockSpec((1, D_MODEL), full),
                  pl.BlockSpec((2, D_MODEL, LANES), lambda i, e: (0, 0, 0)),
                  pl.BlockSpec((per, D_MODEL, 2 * D_EXPERT), lambda i, e: (e, 0, 0)),
                  pl.BlockSpec((per, D_EXPERT, D_MODEL), lambda i, e: (e, 0, 0))],
        out_specs=pl.BlockSpec((tm, D_MODEL), row),
        out_shape=jax.ShapeDtypeStruct((n, D_MODEL), F32),
        scratch_shapes=[pltpu.VMEM((tm, D_MODEL), BF16), pltpu.VMEM((tm, LANES), F32),
                        pltpu.VMEM((tm, D_MODEL), F32)],
        compiler_params=_params("arbitrary", "arbitrary"),
        name="moe",
    )(x, sc, sh, g2, gn, gf, wr, w_ein, w_eout)


PAGE = 128
MOBA_PAGES_PER_STEP = 16
NSA_PAGES_PER_STEP = 32
COMPRESS_PAGES_PER_STEP = 32


def _page_view(cache):
    n_phys, page, heads, dh = cache.shape
    return jnp.transpose(cache, (0, 2, 3, 1)).reshape(n_phys, heads * dh, page)


PAGE_RING_DEPTH = 3


def _page_ring_step(pt_ref, streams, per):
    depth = PAGE_RING_DEPTH
    ns = pl.num_programs(1)
    step = pl.program_id(0) * ns + pl.program_id(1)
    last = pl.num_programs(0) * ns - 1

    def copies(step_idx):
        slot_idx = step_idx % depth
        return [pltpu.make_async_copy(hbm.at[pt_ref[step_idx * per + u]], buf.at[slot_idx, u], sem.at[slot_idx])
                for hbm, buf, sem in streams for u in range(per)]

    @pl.when(step == 0)
    def _():
        for j in range(depth - 1):
            @pl.when(j <= last)
            def _():
                for c in copies(j):
                    c.start()

    for c in copies(step):
        c.wait()

    @pl.when(step + (depth - 1) <= last)
    def _():
        for c in copies(step + (depth - 1)):
            c.start()

    return step % depth


def _head_diag(full, heads):
    rows = full.shape[0]
    head = _iota((rows, D_HEAD), 0) // (rows // heads)
    out = jnp.zeros((rows, D_HEAD), F32)
    for h in range(heads):
        out = out + jnp.where(head == h, full[:, h * D_HEAD:(h + 1) * D_HEAD], 0.0)
    return out


def _moba_s_kernel(pt_ref, qbd_ref, k_hbm, v_hbm, m_ref, l_ref, ks_ref, o_ref, kbuf, vbuf, ksem, vsem):
    per = MOBA_PAGES_PER_STEP
    s = pl.program_id(1)
    slot = _page_ring_step(pt_ref, [(k_hbm, kbuf, ksem), (v_hbm, vbuf, vsem)], per)

    @pl.when(s == 0)
    def _():
        m_ref[...] = jnp.zeros_like(m_ref)
        l_ref[...] = jnp.zeros_like(l_ref)
        ks_ref[...] = jnp.zeros_like(ks_ref)

    qbd = qbd_ref[0]
    lane_q = _iota(m_ref.shape[1:], 1)
    lane_k = _iota(ks_ref.shape[1:], 1)
    ppb = MOBA_BLOCK // PAGE
    m_all, l_all, ks_all = m_ref[0], l_ref[0], ks_ref[0]
    kts = [kbuf[slot, u] for u in range(per)]
    sc_all = _dot(qbd, jnp.concatenate([kt.astype(BF16) for kt in kts], axis=1))
    for j in range(per // ppb):
        blk = s * (per // ppb) + j
        sc = sc_all[:, j * MOBA_BLOCK:(j + 1) * MOBA_BLOCK]
        m = jnp.max(sc, axis=-1, keepdims=True)
        p = jnp.exp(sc - m)
        l = jnp.sum(p, axis=-1, keepdims=True)
        vt = jnp.concatenate([vbuf[slot, j * ppb + t].astype(BF16) for t in range(ppb)], axis=1)
        o_ref[0, j] = _head_diag(_dot_nt(p.astype(BF16), vt), MOBA_HEADS)
        kb = kts[j * ppb]
        for t in range(1, ppb):
            kb = kb + kts[j * ppb + t]
        ksum = jnp.sum(kb, axis=-1, keepdims=True)
        m_all = jnp.where(lane_q == blk, m, m_all)
        l_all = jnp.where(lane_q == blk, l, l_all)
        ks_all = jnp.where(lane_k == blk, ksum, ks_all)
    m_ref[0] = m_all
    l_ref[0] = l_all
    ks_ref[0] = ks_all


def _moba_s_pass(pt_flat, qbd, kt_pages, vt_pages, batch, n_pages):
    per = MOBA_PAGES_PER_STEP
    rows = qbd.shape[1]
    ppb = MOBA_BLOCK // PAGE
    stat = lambda b, s, pt: (b, 0, 0)
    return pl.pallas_call(
        _moba_s_kernel,
        grid_spec=pltpu.PrefetchScalarGridSpec(
            num_scalar_prefetch=1,
            grid=(batch, n_pages // per),
            in_specs=[pl.BlockSpec((1, rows, MOBA_WIDTH), stat),
                      pl.BlockSpec(memory_space=pl.ANY), pl.BlockSpec(memory_space=pl.ANY)],
            out_specs=[pl.BlockSpec((1, rows, LANES), stat), pl.BlockSpec((1, rows, LANES), stat),
                       pl.BlockSpec((1, MOBA_WIDTH, LANES), stat),
                       pl.BlockSpec((1, per // ppb, rows, D_HEAD), lambda b, s, pt: (b, s, 0, 0))],
            scratch_shapes=[pltpu.VMEM((PAGE_RING_DEPTH, per, MOBA_WIDTH, PAGE), F32),
                            pltpu.VMEM((PAGE_RING_DEPTH, per, MOBA_WIDTH, PAGE), F32),
                            pltpu.SemaphoreType.DMA((PAGE_RING_DEPTH,)), pltpu.SemaphoreType.DMA((PAGE_RING_DEPTH,))]),
        out_shape=[jax.ShapeDtypeStruct((batch, rows, LANES), F32), jax.ShapeDtypeStruct((batch, rows, LANES), F32),
                   jax.ShapeDtypeStruct((batch, MOBA_WIDTH, LANES), F32),
                   jax.ShapeDtypeStruct((batch, n_pages // ppb, rows, D_HEAD), F32)],
        compiler_params=_params("arbitrary", "arbitrary"),
        name="moba_decode_pages",
    )(pt_flat, qbd, kt_pages, vt_pages)


def _moba_s_combine_kernel(m_ref, l_ref, ks_ref, o_ref, qf_ref, qbd_ref, kn_ref, vn_ref, out_ref, *, n_pages, ts):
    rows = m_ref.shape[1]
    nblk = n_pages // (MOBA_BLOCK // PAGE)
    lane = _iota((rows, LANES), 1)
    kmean = ks_ref[0] * (1.0 / MOBA_BLOCK)
    gate = jnp.dot(qf_ref[0], kmean, precision=HIGHEST, preferred_element_type=F32)
    gate = jnp.where(lane < nblk, gate, NEG_INF)
    selp = _rank_select(gate, nblk, MOBA_TOPK) & (lane < nblk)
    qbd = qbd_ref[0]
    s_own = _dot_nt(qbd, _new_rows(kn_ref).astype(BF16))
    valid_own = lane <= (_iota((rows, LANES), 0) % ts)
    s_own = jnp.where(valid_own, s_own, NEG_INF)
    m_all = jnp.where(selp, m_ref[0], NEG_INF)
    big_m = jnp.maximum(jnp.max(m_all, axis=-1, keepdims=True), jnp.max(s_own, axis=-1, keepdims=True))
    wgt = jnp.where(selp, jnp.exp(m_ref[0] - big_m), 0.0)
    p_own = jnp.where(valid_own, jnp.exp(s_own - big_m), 0.0)
    denom = jnp.sum(wgt * l_ref[0], axis=-1, keepdims=True) + jnp.sum(p_own, axis=-1, keepdims=True)
    num = _head_diag(_dot(p_own.astype(BF16), _new_rows(vn_ref).astype(BF16)), MOBA_HEADS)
    for j in range(nblk):
        num = num + wgt[:, j:j + 1] * o_ref[0, j]
    out_ref[0] = num / jnp.maximum(denom, TINY)


def _moba_s_combine(m, l, ks, o, qf, qbd, kn, vn, n_pages, ts):
    batch, rows, _ = m.shape
    b3 = lambda b: (b, 0, 0)
    return pl.pallas_call(
        functools.partial(_moba_s_combine_kernel, n_pages=n_pages, ts=ts),
        grid=(batch,),
        in_specs=[pl.BlockSpec((1, rows, LANES), b3), pl.BlockSpec((1, rows, LANES), b3),
                  pl.BlockSpec((1, MOBA_WIDTH, LANES), b3),
                  pl.BlockSpec((1, o.shape[1], rows, D_HEAD), lambda b: (b, 0, 0, 0)),
                  pl.BlockSpec((1, rows, MOBA_WIDTH), b3), pl.BlockSpec((1, rows, MOBA_WIDTH), b3),
                  pl.BlockSpec((1, ts, MOBA_WIDTH), b3), pl.BlockSpec((1, ts, MOBA_WIDTH), b3)],
        out_specs=pl.BlockSpec((1, rows, D_HEAD), b3),
        out_shape=jax.ShapeDtypeStruct((batch, rows, D_HEAD), F32),
        compiler_params=_params("arbitrary"),
        name="moba_decode_combine",
    )(m, l, ks, o, qf, qbd, kn, vn)


def _block_diag_q(q, batch, ts, heads):
    q4 = q.reshape(batch, ts, heads, D_HEAD)
    eye = jnp.eye(heads, dtype=q.dtype)
    return jnp.einsum('bchd,hk->bhckd', q4, eye).reshape(batch, heads * ts, heads * D_HEAD)


def _new_rows(ref, lane0=0, width=None):
    rows = ref[0] if width is None else ref[0, :, lane0:lane0 + width]
    return jnp.concatenate([rows, jnp.zeros((LANES - rows.shape[0], rows.shape[1]), rows.dtype)], axis=0)


def _compress_s_kernel(pt_ref, k_hbm, v_hbm, pea_ref, peb_ref, wa_ref, wb_ref, w2_ref, ok_ref, ov_ref,
                       kbuf, vbuf, ksem, vsem, x_scr, a_scr, b_scr):
    per = COMPRESS_PAGES_PER_STEP
    s = pl.program_id(1)
    nseg = per * PAGE // CMP_STRIDE
    slot = _page_ring_step(pt_ref, [(k_hbm, kbuf, ksem), (v_hbm, vbuf, vsem)], per)
    for i, (buf, o_ref) in enumerate(((kbuf, ok_ref), (vbuf, ov_ref))):
        x_i, a_i, b_i, pea_i, peb_i = x_scr.at[i], a_scr.at[i], b_scr.at[i], pea_ref.at[i], peb_ref.at[i]
        for u in range(per):
            x_i[u * PAGE:(u + 1) * PAGE, :] = buf[slot, u].T
        xs = [x_i[pl.ds(l, nseg, stride=CMP_STRIDE), :] for l in range(CMP_STRIDE)]
        xa = jnp.concatenate([(xs[l] + pea_i[l:l + 1, :]).astype(BF16) for l in range(CMP_STRIDE)], axis=1)
        xb = jnp.concatenate([(xs[l] + peb_i[l:l + 1, :]).astype(BF16) for l in range(CMP_STRIDE)], axis=1)
        a_i[pl.ds(pl.multiple_of(s * nseg, nseg), nseg), :] = _dot(xa, wa_ref[i])
        b_i[pl.ds(pl.multiple_of(s * nseg, nseg), nseg), :] = _dot(xb, wb_ref[i])

        @pl.when(s == pl.num_programs(1) - 1)
        def _(i=i, o_ref=o_ref, a_i=a_i, b_i=b_i):
            total = a_i.shape[0]
            hid = a_i[...] + pltpu.roll(b_i[...], total - 1, 0)
            o_ref[0] = _dot(_gelu(hid).astype(BF16), w2_ref[i])


def _compress_s(pt_flat, k_pages, v_pages, cw_k, cw_v, batch, n_pages):
    per = COMPRESS_PAGES_PER_STEP
    total = n_pages * PAGE // CMP_STRIDE
    hidden = NSA_KV_HEADS * CMP_HIDDEN
    width = CMP_STRIDE * NSA_KV_WIDTH
    wa, wb, pea, peb, w2bd = [jnp.stack([a, b]) for a, b in zip(cw_k, cw_v)]
    full3 = lambda b, s, pt: (0, 0, 0)
    out_spec = pl.BlockSpec((1, total, NSA_KV_WIDTH), lambda b, s, pt: (b, 0, 0))
    out = jax.ShapeDtypeStruct((batch, total, NSA_KV_WIDTH), F32)
    page_buf = pltpu.VMEM((PAGE_RING_DEPTH, per, NSA_KV_WIDTH, PAGE), F32)
    return pl.pallas_call(
        _compress_s_kernel,
        grid_spec=pltpu.PrefetchScalarGridSpec(
            num_scalar_prefetch=1,
            grid=(batch, n_pages // per),
            in_specs=[pl.BlockSpec(memory_space=pl.ANY), pl.BlockSpec(memory_space=pl.ANY),
                      pl.BlockSpec((2, CMP_STRIDE, NSA_KV_WIDTH), full3),
                      pl.BlockSpec((2, CMP_STRIDE, NSA_KV_WIDTH), full3),
                      pl.BlockSpec((2, width, hidden), full3), pl.BlockSpec((2, width, hidden), full3),
                      pl.BlockSpec((2, hidden, NSA_KV_WIDTH), full3)],
            out_specs=[out_spec, out_spec],
            scratch_shapes=[page_buf, page_buf,
                            pltpu.SemaphoreType.DMA((PAGE_RING_DEPTH,)), pltpu.SemaphoreType.DMA((PAGE_RING_DEPTH,)),
                            pltpu.VMEM((2, per * PAGE, NSA_KV_WIDTH), F32), pltpu.VMEM((2, total, hidden), F32),
                            pltpu.VMEM((2, total, hidden), F32)]),
        out_shape=[out, out],
        compiler_params=_params("arbitrary", "arbitrary"),
        name="compress_decode",
    )(pt_flat, k_pages, v_pages, pea, peb, wa.reshape(2, width, hidden), wb.reshape(2, width, hidden), w2bd)


def _stack_group_q(q_ref, g):
    return jnp.concatenate(
        [q_ref[0, :, (g * NSA_GROUP + r) * D_HEAD:(g * NSA_GROUP + r + 1) * D_HEAD] * SCALE
         for r in range(NSA_GROUP)], axis=0).astype(BF16)


def _nsa_s_kernel(q_ref, kc_ref, vc_ref, wk_ref, wv_ref, kn_ref, vn_ref, gate_ref, ov_ref, part_ref, sel_ref, *, ts):
    rows = NSA_GROUP * ts
    nc_pad = kc_ref.shape[1]
    wlen = wk_ref.shape[3]
    gates = _sigmoid(gate_ref[0])
    n_i = _iota((rows, nc_pad), 1)
    valid_c = n_i < nc_pad - 1
    c_of_row = _iota((rows, 1), 0) % ts
    valid_w = _iota((rows, wlen), 1) >= c_of_row
    valid_n = _iota((rows, LANES), 1) <= c_of_row
    for g in range(NSA_KV_HEADS):
        lane0 = g * D_HEAD
        qs = _stack_group_q(q_ref, g)
        s = jnp.where(valid_c, _dot_nt(qs, kc_ref[0, :, lane0:lane0 + D_HEAD].astype(BF16)), NEG_INF)
        m = jnp.max(s, axis=-1, keepdims=True)
        p = jnp.where(valid_c, jnp.exp(s - m), 0.0)
        p = p / jnp.maximum(jnp.sum(p, axis=-1, keepdims=True), TINY)
        o_c = _dot(p.astype(BF16), vc_ref[0, :, lane0:lane0 + D_HEAD].astype(BF16))
        psum = jnp.sum(p.reshape(NSA_GROUP, ts, nc_pad), axis=0)
        imp = jnp.dot(psum, ov_ref[...], precision=HIGHEST, preferred_element_type=F32)
        keep = _rank_select(imp, LANES, SLC_TOPN - 1)
        sel_ref[0, g] = jnp.where(keep, 0.0, NEG_INF)
        s_w = jnp.where(valid_w, _dot(qs, wk_ref[0, g].astype(BF16)), NEG_INF)
        s_n = jnp.where(valid_n, _dot_nt(qs, _new_rows(kn_ref, lane0, D_HEAD).astype(BF16)), NEG_INF)
        m = jnp.maximum(jnp.max(s_w, axis=-1, keepdims=True), jnp.max(s_n, axis=-1, keepdims=True))
        p_w = jnp.where(valid_w, jnp.exp(s_w - m), 0.0)
        p_n = jnp.where(valid_n, jnp.exp(s_n - m), 0.0)
        den = jnp.sum(p_w, axis=-1, keepdims=True) + jnp.sum(p_n, axis=-1, keepdims=True)
        o_w = (_dot_nt(p_w.astype(BF16), wv_ref[0, g].astype(BF16))
               + _dot(p_n.astype(BF16), _new_rows(vn_ref, lane0, D_HEAD).astype(BF16))) / jnp.maximum(den, TINY)
        for r in range(NSA_GROUP):
            hd = g * NSA_GROUP + r
            rs = slice(r * ts, (r + 1) * ts)
            part_ref[0, :, hd * D_HEAD:(hd + 1) * D_HEAD] = (
                gates[:, 3 * hd:3 * hd + 1] * o_c[rs] + gates[:, 3 * hd + 2:3 * hd + 3] * o_w[rs])


def _nsa_s(q3, kcmp, vcmp, wk_t, wv_t, kn_w, vn_w, gate3, ts):
    batch = q3.shape[0]
    nc_pad = kcmp.shape[1]
    wlen = wk_t.shape[3]
    ov = _overlap_matrix(nc_pad, LANES)
    b3 = lambda b: (b, 0, 0)
    b4 = lambda b: (b, 0, 0, 0)
    return pl.pallas_call(
        functools.partial(_nsa_s_kernel, ts=ts),
        grid=(batch,),
        in_specs=[pl.BlockSpec((1, ts, NSA_WIDTH), b3),
                  pl.BlockSpec((1, nc_pad, NSA_KV_WIDTH), b3), pl.BlockSpec((1, nc_pad, NSA_KV_WIDTH), b3),
                  pl.BlockSpec((1, NSA_KV_HEADS, D_HEAD, wlen), b4), pl.BlockSpec((1, NSA_KV_HEADS, D_HEAD, wlen), b4),
                  pl.BlockSpec((1, ts, NSA_KV_WIDTH), lambda b: (b, 0, 2)),
                  pl.BlockSpec((1, ts, NSA_KV_WIDTH), lambda b: (b, 0, 2)),
                  pl.BlockSpec((1, ts, LANES), b3),
                  pl.BlockSpec((nc_pad, LANES), lambda b: (0, 0))],
        out_specs=[pl.BlockSpec((1, ts, NSA_WIDTH), b3), pl.BlockSpec((1, NSA_KV_HEADS, ts, LANES), b4)],
        out_shape=[jax.ShapeDtypeStruct((batch, ts, NSA_WIDTH), F32),
                   jax.ShapeDtypeStruct((batch, NSA_KV_HEADS, ts, LANES), F32)],
        compiler_params=_params("arbitrary"),
        name="nsa_decode_cmp_win",
    )(q3, kcmp, vcmp, wk_t, wv_t, kn_w, vn_w, gate3, ov)


def _slc_s_kernel(pt_ref, q_ref, bias_ref, hot_ref, k_hbm, v_hbm, kn_ref, vn_ref, gate_ref, part_ref, o_ref,
                  kbuf, vbuf, ksem, vsem, m_scr, l_scr, acc_scr, *, ts):
    per = NSA_PAGES_PER_STEP
    s = pl.program_id(1)
    slot = _page_ring_step(pt_ref, [(k_hbm, kbuf, ksem), (v_hbm, vbuf, vsem)], per)
    rows = NSA_GROUP * ts
    zero = jnp.zeros((rows, D_HEAD), BF16)
    q2 = jnp.concatenate([jnp.concatenate([_stack_group_q(q_ref, 0), zero], axis=1),
                          jnp.concatenate([zero, _stack_group_q(q_ref, 1)], axis=1)], axis=0)
    in_g0 = _iota((NSA_KV_HEADS * rows, D_HEAD), 0) < rows

    def own_group(full):
        return jnp.where(in_g0, full[:, :D_HEAD], full[:, D_HEAD:])

    @pl.when(s == 0)
    def _():
        valid_n = _iota((NSA_KV_HEADS * rows, LANES), 1) <= (_iota((NSA_KV_HEADS * rows, 1), 0) % ts)
        sc = jnp.where(valid_n, _dot_nt(q2, _new_rows(kn_ref).astype(BF16)), NEG_INF)
        m = jnp.max(sc, axis=-1, keepdims=True)
        p = jnp.exp(sc - m)
        m_scr[...] = m
        l_scr[...] = jnp.sum(p, axis=-1, keepdims=True)
        acc_scr[...] = own_group(_dot(p.astype(BF16), _new_rows(vn_ref).astype(BF16)))

    kt = jnp.concatenate([kbuf[slot, u].astype(BF16) for u in range(per)], axis=1)
    vt = jnp.concatenate([vbuf[slot, u].astype(BF16) for u in range(per)], axis=1)
    nb = per * PAGE // SLC_BLOCK
    bias = jnp.concatenate([bias_ref[0, g] for g in range(NSA_KV_HEADS) for _ in range(NSA_GROUP)], axis=0)
    cols = bias[:, :nb]
    for k in range(1, bias.shape[1] // nb):
        cols = jnp.where(s == k, bias[:, k * nb:(k + 1) * nb], cols)
    sc = _dot(jnp.concatenate([q2, cols.astype(BF16)], axis=1),
              jnp.concatenate([kt, hot_ref[...]], axis=0))
    m_old = m_scr[...]
    m_new = jnp.maximum(m_old, jnp.max(sc, axis=-1, keepdims=True))
    pf = jnp.exp(sc - m_new)
    alpha = jnp.exp(m_old - m_new)
    l_scr[...] = alpha * l_scr[...] + jnp.sum(pf, axis=-1, keepdims=True)
    acc_scr[...] = alpha * acc_scr[...] + own_group(_dot_nt(pf.astype(BF16), vt))
    m_scr[...] = m_new

    @pl.when(s == pl.num_programs(1) - 1)
    def _():
        gates = _sigmoid(gate_ref[0])
        o_s = acc_scr[...] / jnp.maximum(l_scr[...], TINY)
        for hd in range(NSA_HEADS):
            cols = slice(hd * D_HEAD, (hd + 1) * D_HEAD)
            o_ref[0, :, cols] = part_ref[0, :, cols] + gates[:, 3 * hd + 1:3 * hd + 2] * o_s[hd * ts:(hd + 1) * ts]


def _slc_s(pt_flat, q3, sel_bias, k_pages, v_pages, kn_s, vn_s, gate3, part, n_pages, ts):
    batch = q3.shape[0]
    per = NSA_PAGES_PER_STEP
    rows = NSA_GROUP * ts
    nb = per * PAGE // SLC_BLOCK
    hot = (jnp.arange(nb)[:, None] == jnp.arange(per * PAGE)[None, :] // SLC_BLOCK).astype(BF16)
    b3 = lambda b, s, pt: (b, 0, 0)
    return pl.pallas_call(
        functools.partial(_slc_s_kernel, ts=ts),
        grid_spec=pltpu.PrefetchScalarGridSpec(
            num_scalar_prefetch=1,
            grid=(batch, n_pages // per),
            in_specs=[pl.BlockSpec((1, ts, NSA_WIDTH), b3),
                      pl.BlockSpec((1, NSA_KV_HEADS, ts, LANES), lambda b, s, pt: (b, 0, 0, 0)),
                      pl.BlockSpec((nb, per * PAGE), lambda b, s, pt: (0, 0)),
                      pl.BlockSpec(memory_space=pl.ANY), pl.BlockSpec(memory_space=pl.ANY),
                      pl.BlockSpec((1, ts, NSA_KV_WIDTH), lambda b, s, pt: (b, 0, 1)),
                      pl.BlockSpec((1, ts, NSA_KV_WIDTH), lambda b, s, pt: (b, 0, 1)),
                      pl.BlockSpec((1, ts, LANES), b3), pl.BlockSpec((1, ts, NSA_WIDTH), b3)],
            out_specs=pl.BlockSpec((1, ts, NSA_WIDTH), b3),
            scratch_shapes=[pltpu.VMEM((PAGE_RING_DEPTH, per, NSA_KV_WIDTH, PAGE), F32),
                            pltpu.VMEM((PAGE_RING_DEPTH, per, NSA_KV_WIDTH, PAGE), F32),
                            pltpu.SemaphoreType.DMA((PAGE_RING_DEPTH,)), pltpu.SemaphoreType.DMA((PAGE_RING_DEPTH,)),
                            pltpu.VMEM((NSA_KV_HEADS * rows, 1), F32), pltpu.VMEM((NSA_KV_HEADS * rows, 1), F32),
                            pltpu.VMEM((NSA_KV_HEADS * rows, D_HEAD), F32)]),
        out_shape=jax.ShapeDtypeStruct((batch, ts, NSA_WIDTH), F32),
        compiler_params=_params("arbitrary", "arbitrary"),
        name="nsa_decode_slc",
    )(pt_flat, q3, sel_bias, hot, k_pages, v_pages, kn_s, vn_s, gate3, part)


def _prep_weights(w_ada, b_ada, norm_mix_g, w_in, pe_cmp_k, w_cmp_k1, w_cmp_k2, pe_cmp_v, w_cmp_v1, w_cmp_v2,
                  w_br_a, w_br_b, w_out, norm_ffn_g, w_router_grp, w_router_exp, w_expert_in, w_expert_out,
                  norm_final_g):
    wr = jnp.concatenate([w_router_grp, w_router_exp], axis=1)
    wr = jnp.pad(wr, ((0, 0), (0, LANES - wr.shape[1])))
    wr_hi = wr.astype(BF16)
    wr = jnp.stack([wr_hi, (wr - wr_hi.astype(F32)).astype(BF16)])
    w_in_main, w_in_merge = _reorder_w_in(w_in)
    return dict(
        w_ada=w_ada, b_ada=b_ada.reshape(1, -1), g_mix=norm_mix_g.reshape(1, -1),
        w_in=w_in_main, w_merge=w_in_merge,
        cmp_k=_compress_weights(pe_cmp_k, w_cmp_k1, w_cmp_k2),
        cmp_v=_compress_weights(pe_cmp_v, w_cmp_v1, w_cmp_v2),
        w_br_a=w_br_a.astype(BF16), w_br_b=w_br_b.astype(BF16), w_out=w_out.astype(BF16),
        g_ffn=norm_ffn_g.reshape(1, -1), wr=wr,
        w_ein=w_expert_in.astype(BF16), w_eout=w_expert_out.astype(BF16),
        g_final=norm_final_g.reshape(1, -1))


def _prompt_layer(x, mod, w, batch, seq):
    sh1, sc1, g1, sh2, sc2, g2 = mod
    cos, sin = _rope_tables(jnp.arange(seq, dtype=jnp.int32))
    qk_a, _, q_b, k_n, v_n, gate, kt_a, vt_a, kt_n, vt_n = _inproj(
        x, sc1, sh1, w['g_mix'], cos, sin, w['w_in'], 256, transposed=True)
    o_a = _moba_p(qk_a, vt_a, batch, seq)
    kcmp = _compress_p(k_n[:, :NSA_KV_WIDTH], w['cmp_k'], batch, seq)
    vcmp = _compress_p(v_n[:, :NSA_KV_WIDTH], w['cmp_v'], batch, seq)
    o_b = _nsa_p(q_b, k_n, vt_n, kcmp, vcmp, gate, batch, seq)
    x1 = _merge(o_a, o_b, x, sc1, sh1, g1, w['g_mix'], w['w_merge'], w['w_br_a'], w['w_br_b'], w['w_out'], 256)
    y = _moe(x1, sc2, sh2, g2, w['g_ffn'], w['g_final'], w['wr'], w['w_ein'], w['w_eout'], 512)
    return y, (kt_a, vt_a, kt_n, vt_n)


def _sample_layer(x, mod, w, caches, win_state, page_table, batch, ts):
    sh1, sc1, g1, sh2, sc2, g2 = mod
    moba_k, moba_v, cmp_k, cmp_v, slc_k, slc_v = caches
    win_k, win_v = win_state
    n_pages = page_table.shape[1]
    assert moba_k.shape[1] == PAGE and win_k.shape[1] == WINDOW and ts <= LANES
    assert n_pages * PAGE == LANES * SLC_BLOCK and n_pages * PAGE // MOBA_BLOCK <= LANES
    n = batch * ts
    pos = n_pages * PAGE + (jnp.arange(n, dtype=jnp.int32) % ts)
    cos, sin = _rope_tables(pos)
    qk_a, v_a, q_b, k_n, v_n, gate = _inproj(x, sc1, sh1, w['g_mix'], cos, sin, w['w_in'], n)
    pt_flat = page_table.reshape(-1)
    kv = NSA_KV_WIDTH
    qf = _block_diag_q(qk_a[:, :MOBA_WIDTH], batch, ts, MOBA_HEADS)
    qbd = (qf * SCALE).astype(BF16)
    m, l, ks, o = _moba_s_pass(pt_flat, qbd, _page_view(moba_k), _page_view(moba_v), batch, n_pages)
    o_a = _moba_s_combine(m, l, ks, o, qf, qbd, qk_a[:, MOBA_WIDTH:].reshape(batch, ts, MOBA_WIDTH),
                          v_a.reshape(batch, ts, MOBA_WIDTH), n_pages, ts)
    o_a = o_a.reshape(batch, MOBA_HEADS, ts, D_HEAD).transpose(0, 2, 1, 3).reshape(n, MOBA_WIDTH)
    kcmp, vcmp = _compress_s(pt_flat, _page_view(cmp_k), _page_view(cmp_v), w['cmp_k'], w['cmp_v'], batch, n_pages)
    q3 = q_b.reshape(batch, ts, NSA_WIDTH)
    gate3 = gate.reshape(batch, ts, LANES)
    kn3 = k_n.reshape(batch, ts, 3 * kv)
    vn3 = v_n.reshape(batch, ts, 3 * kv)
    part, sel = _nsa_s(q3, kcmp, vcmp, jnp.transpose(win_k, (0, 2, 3, 1)), jnp.transpose(win_v, (0, 2, 3, 1)),
                       kn3, vn3, gate3, ts)
    o_b = _slc_s(pt_flat, q3, sel, _page_view(slc_k), _page_view(slc_v), kn3, vn3, gate3, part,
                 n_pages, ts).reshape(n, NSA_WIDTH)
    x1 = _merge(o_a, o_b, x, sc1, sh1, g1, w['g_mix'], w['w_merge'], w['w_br_a'], w['w_br_b'], w['w_out'], n)
    y = _moe(x1, sc2, sh2, g2, w['g_ffn'], w['g_final'], w['wr'], w['w_ein'], w['w_eout'], n)
    return y, (qk_a, v_a, k_n, v_n)


def kernel(x_prompt, x_sample, c_prompt, c_sample, cache_moba_k, cache_moba_v, cache_nsa_cmp_k, cache_nsa_cmp_v,
           cache_nsa_slc_k, cache_nsa_slc_v, state_nsa_win_k, state_nsa_win_v, page_table, w_ada, b_ada, norm_mix_g,
           w_in, pe_cmp_k, w_cmp_k1, w_cmp_k2, pe_cmp_v, w_cmp_v1, w_cmp_v2, w_br_a, w_br_b, w_out, norm_ffn_g,
           w_router_grp, w_router_exp, w_expert_in, w_expert_out, norm_final_g):
    bp, tp, _ = x_prompt.shape
    bs, ts, _ = x_sample.shape
    w = _prep_weights(w_ada[0], b_ada[0], norm_mix_g[0], w_in[0], pe_cmp_k[0], w_cmp_k1[0], w_cmp_k2[0], pe_cmp_v[0],
                      w_cmp_v1[0], w_cmp_v2[0], w_br_a[0], w_br_b[0], w_out[0], norm_ffn_g[0], w_router_grp[0],
                      w_router_exp[0], w_expert_in[0], w_expert_out[0], norm_final_g)
    mod = _ada(jnp.concatenate([c_prompt, c_sample], axis=0), w['w_ada'], w['b_ada'])
    mod_p = [m.reshape(bp, 1, D_MODEL) for m in jnp.split(mod[:bp], 6, axis=-1)]
    y_p, new_p = _prompt_layer(x_prompt.reshape(bp * tp, D_MODEL), mod_p, w, bp, tp)

    mod_s = [jnp.repeat(m, ts, axis=0).reshape(1, bs * ts, D_MODEL) for m in jnp.split(mod[bp:], 6, axis=-1)]
    caches = (cache_moba_k[0], cache_moba_v[0], cache_nsa_cmp_k[0], cache_nsa_cmp_v[0], cache_nsa_slc_k[0],
              cache_nsa_slc_v[0])
    y_s, new_s = _sample_layer(x_sample.reshape(bs * ts, D_MODEL), mod_s, w, caches,
                               (state_nsa_win_k[0], state_nsa_win_v[0]), page_table, bs, ts)

    kv = NSA_KV_WIDTH

    def new_rows(new, b, t):
        qk, v, k_n, v_n = new
        rows = lambda a, heads: a.reshape(1, b, t, heads, D_HEAD)
        return (rows(qk[:, MOBA_WIDTH:], MOBA_HEADS), rows(v, MOBA_HEADS),
                rows(k_n[:, :kv], NSA_KV_HEADS), rows(v_n[:, :kv], NSA_KV_HEADS),
                rows(k_n[:, kv:2 * kv], NSA_KV_HEADS), rows(v_n[:, kv:2 * kv], NSA_KV_HEADS),
                rows(k_n[:, 2 * kv:], NSA_KV_HEADS), rows(v_n[:, 2 * kv:], NSA_KV_HEADS))

    def new_rows_t(new, b, t):
        kt_a, vt_a, kt_n, vt_n = new
        rows = lambda a: jnp.transpose(a.reshape(1, b, a.shape[1] // D_HEAD, D_HEAD, a.shape[2]), (0, 1, 4, 2, 3))
        return (rows(kt_a), rows(vt_a), rows(kt_n[:, :kv]), rows(vt_n[:, :kv]),
                rows(kt_n[:, kv:2 * kv]), rows(vt_n[:, kv:2 * kv]),
                rows(kt_n[:, 2 * kv:, t - wb:]), rows(vt_n[:, 2 * kv:, t - wb:]))

    wb = state_nsa_win_k.shape[2]
    outs_p = new_rows_t(new_p, bp, tp)
    outs_s = new_rows(new_s, bs, ts)
    win_k = jnp.concatenate([state_nsa_win_k, outs_s[6]], axis=2)[:, :, ts:]
    win_v = jnp.concatenate([state_nsa_win_v, outs_s[7]], axis=2)[:, :, ts:]
    outs_s = outs_s[:6] + (win_k, win_v)
    return (y_p.reshape(bp, tp, D_MODEL), y_s.reshape(bs, ts, D_MODEL)) + outs_p + outs_s
```

```python
import functools

import jax
import jax.numpy as jnp
from jax import lax
from jax.experimental import pallas as pl
from jax.experimental.pallas import tpu as pltpu

D_MODEL = 1024
D_HEAD = 64
HALF = D_HEAD // 2
MOBA_HEADS = 8
MOBA_BLOCK = 256
MOBA_TOPK = 3
NSA_HEADS = 8
NSA_KV_HEADS = 2
NSA_GROUP = NSA_HEADS // NSA_KV_HEADS
CMP_LEN = 32
CMP_STRIDE = 16
CMP_HIDDEN = 128
SLC_BLOCK = 64
SLC_TOPN = 16
WINDOW = 512
N_GROUPS = 4
EXPERTS_PER_GROUP = 8
N_EXPERTS = N_GROUPS * EXPERTS_PER_GROUP
D_EXPERT = 256
ROPE_THETA = 10000.0
EPS = 1e-6
NEG_INF = -1e30
BIG = 1e30
TINY = 1e-30
MOBA_WIDTH = MOBA_HEADS * D_HEAD
NSA_WIDTH = NSA_HEADS * D_HEAD
NSA_KV_WIDTH = NSA_KV_HEADS * D_HEAD
SCALE = D_HEAD ** -0.5

LANES = 128
VMEM_LIMIT = 48 * 1024 * 1024

F32 = jnp.float32
BF16 = jnp.bfloat16
HIGHEST = lax.Precision.HIGHEST


def _params(*sem):
    return pltpu.CompilerParams(dimension_semantics=sem, vmem_limit_bytes=VMEM_LIMIT)


def _dot(a, b):
    return jnp.dot(a, b, preferred_element_type=F32)


def _dot_nt(a, b, precision=None):
    return lax.dot_general(a, b, (((1,), (1,)), ((), ())), precision=precision,
                           preferred_element_type=F32)


def _sigmoid(x):
    return 1.0 / (1.0 + jnp.exp(-x))


def _iota(shape, dim):
    return lax.broadcasted_iota(jnp.int32, shape, dim)


def _ada_kernel(c_ref, w_ref, b_ref, o_ref):
    c = c_ref[...]
    s = c * _sigmoid(c)
    o_ref[...] = _dot(s.astype(BF16), w_ref[...].astype(BF16)) + b_ref[...]


def _ada(c, w, b):
    n = c.shape[0]
    tn = 1024
    return pl.pallas_call(
        _ada_kernel,
        grid=(w.shape[1] // tn,),
        in_specs=[pl.BlockSpec((n, D_MODEL), lambda j: (0, 0)),
                  pl.BlockSpec((D_MODEL, tn), lambda j: (0, j)),
                  pl.BlockSpec((1, tn), lambda j: (0, j))],
        out_specs=pl.BlockSpec((n, tn), lambda j: (0, j)),
        out_shape=jax.ShapeDtypeStruct((n, w.shape[1]), F32),
        compiler_params=_params("arbitrary"),
        name="ada",
    )(c, w, b)


_IN_GROUPS = ((2 * MOBA_WIDTH, True),
              (MOBA_WIDTH, False),
              (NSA_WIDTH, True),
              (3 * NSA_KV_WIDTH, True),
              (3 * NSA_KV_WIDTH, False),
              (LANES, False))
_IN_COLS_PAD = sum(w for w, _ in _IN_GROUPS)


def _norm_mod(x, g, sc, sh):
    ms = jnp.mean(x * x, axis=-1, keepdims=True)
    y = x * lax.rsqrt(ms + EPS) * g
    return y * (1.0 + sc) + sh


def _inproj_kernel(x_ref, sc_ref, sh_ref, g_ref, cos_ref, sin_ref, w_ref, *out_refs):
    h = _norm_mod(x_ref[...], g_ref[...], sc_ref[0], sh_ref[0]).astype(BF16)
    cos = cos_ref[...]
    sin = sin_ref[...]
    first_half = (_iota(cos.shape, 1) & (D_HEAD - 1)) < HALF

    def rope(y):
        rot = jnp.where(first_half, pltpu.roll(y, LANES - HALF, 1), pltpu.roll(y, HALF, 1))
        return y * cos + rot * sin

    n_groups = len(_IN_GROUPS)
    t_refs = dict(zip(_IN_T_GROUPS, out_refs[n_groups:]))
    col = 0
    for gi, (out_ref, (width, rotary)) in enumerate(zip(out_refs, _IN_GROUPS)):
        chunk = min(width, 512)
        for c in range(0, width, chunk):
            cw = min(chunk, width - c)
            y = _dot(h, w_ref[:, col + c:col + c + cw])
            for s in range(0, cw, LANES):
                piece = y[:, s:s + LANES]
                piece = rope(piece) if rotary else piece
                if c + s < out_ref.shape[1]:
                    out_ref[:, c + s:c + s + LANES] = piece
                if gi in t_refs and c + s >= _IN_T_GROUPS[gi]:
                    t0 = c + s - _IN_T_GROUPS[gi]
                    t_refs[gi][0, t0:t0 + LANES, :] = piece.T
        col += width


_IN_T_GROUPS = {0: MOBA_WIDTH, 1: 0, 3: 0, 4: 0}
_IN_T_ROW_COLS = {1: LANES, 4: NSA_KV_WIDTH}


def _inproj(x, sc, sh, g, cos, sin, w, tm, transposed=False):
    n = x.shape[0]
    nb, r, _ = sc.shape
    tiles_per_b = (n // nb) // tm
    tab_tiles = cos.shape[0] // tm
    row = lambda i: (i, 0)
    mod = lambda i: (i // tiles_per_b, 0, 0)
    tab = lambda i: (i % tab_tiles, 0)
    widths = [_IN_T_ROW_COLS.get(gi, wd) if transposed else wd for gi, (wd, _) in enumerate(_IN_GROUPS)]
    out_specs = [pl.BlockSpec((tm, wd), row) for wd in widths]
    out_shape = [jax.ShapeDtypeStruct((n, wd), F32) for wd in widths]
    if transposed:
        for gi, first in _IN_T_GROUPS.items():
            cols = _IN_GROUPS[gi][0] - first
            out_specs.append(pl.BlockSpec((1, cols, tm), lambda i: (i // tiles_per_b, 0, i % tiles_per_b)))
            out_shape.append(jax.ShapeDtypeStruct((nb, cols, n // nb), F32))
    return pl.pallas_call(
        _inproj_kernel,
        grid=(n // tm,),
        in_specs=[pl.BlockSpec((tm, D_MODEL), row),
                  pl.BlockSpec((1, r, D_MODEL), mod),
                  pl.BlockSpec((1, r, D_MODEL), mod),
                  pl.BlockSpec((1, D_MODEL), lambda i: (0, 0)),
                  pl.BlockSpec((tm, LANES), tab),
                  pl.BlockSpec((tm, LANES), tab),
                  pl.BlockSpec((D_MODEL, _IN_COLS_PAD), lambda i: (0, 0))],
        out_specs=out_specs,
        out_shape=out_shape,
        compiler_params=_params("arbitrary"),
        name="inproj",
    )(x, sc, sh, g, cos, sin, w)


def _reorder_w_in(w_in):
    kv0 = 3 * MOBA_WIDTH + NSA_WIDTH
    kvs = [w_in[:, kv0 + i * NSA_KV_WIDTH:kv0 + (i + 1) * NSA_KV_WIDTH] for i in range(6)]
    g0 = kv0 + 6 * NSA_KV_WIDTH
    ng = 3 * NSA_HEADS
    gate = jnp.pad(w_in[:, g0:g0 + ng], ((0, 0), (0, LANES - ng)))
    parts = [w_in[:, :kv0], kvs[0], kvs[2], kvs[4], kvs[1], kvs[3], kvs[5], gate]
    return jnp.concatenate(parts, axis=1).astype(BF16), w_in[:, g0 + ng:].astype(BF16)


def _rope_tables(pos):
    inv = ROPE_THETA ** (-jnp.arange(HALF, dtype=F32) / HALF)
    ang = pos.astype(F32)[:, None] * inv[None, :]
    cos = jnp.cos(ang)
    sin = jnp.sin(ang)
    cos = jnp.concatenate([cos, cos, cos, cos], axis=1)
    sin = jnp.concatenate([-sin, sin, -sin, sin], axis=1)
    return cos, sin


def _rank_select(score, n_cols, n_keep):
    lane = _iota(score.shape, 1)
    rank = jnp.zeros(score.shape, jnp.int32)
    for jp in range(n_cols):
        col = score[:, jp:jp + 1]
        beats = (col > score) | ((col == score) & (lane > jp))
        rank = rank + beats.astype(jnp.int32)
    return rank < n_keep


def _rank_select_t(score, n_rows, n_keep):
    row = _iota(score.shape, 0)
    rank = jnp.zeros(score.shape, jnp.int32)
    for jp in range(n_rows):
        r = score[jp:jp + 1, :]
        beats = (r > score) | ((r == score) & (row > jp))
        rank = rank + beats.astype(jnp.int32)
    return rank < n_keep


SUBLANES = 8


def _fold_rows(x, op):
    return op(x.reshape(x.shape[0] // SUBLANES, SUBLANES, x.shape[1]), axis=0)


def _attend_t(score_fn, vt_fn, lo, hi, width, n_chains, cache_ref):
    def chunk_loop(step_fn, carry):
        n = hi - lo
        carry = lax.fori_loop(0, n // 2, lambda i, cr: step_fn(lo + 2 * i + 1, step_fn(lo + 2 * i, cr)), carry)
        return lax.cond(n % 2 == 1, lambda cr: step_fn(hi - 1, cr), lambda cr: cr, carry)

    def max_step(c, own, m):
        out = []
        for ch, (mi, s) in enumerate(zip(m, score_fn(c, own))):
            cache_ref[ch, c - lo] = s
            out.append(jnp.maximum(mi, _fold_rows(s, jnp.max)))
        return tuple(out)

    m = max_step(hi, True, tuple(jnp.full((SUBLANES, width), NEG_INF, F32) for _ in range(n_chains)))
    m = chunk_loop(lambda c, mm: max_step(c, False, mm), m)
    shift = [jnp.max(mi, axis=0, keepdims=True) for mi in m]

    def acc_step(c, carry):
        out = []
        for ch, ((l, acc), vt) in enumerate(zip(carry, vt_fn(c))):
            p = jnp.exp(cache_ref[ch, c - lo] - shift[ch])
            out.append((l + _fold_rows(p, jnp.sum), acc + _dot(vt, p.astype(BF16))))
        return tuple(out)

    zero = tuple((jnp.zeros((SUBLANES, width), F32), jnp.zeros((D_HEAD, width), F32)) for _ in range(n_chains))
    carry = acc_step(hi, zero)
    carry = chunk_loop(acc_step, carry)
    return [acc / jnp.maximum(jnp.sum(l, axis=0, keepdims=True), TINY) for l, acc in carry]


def _key_aug(n_keys, block, n_blocks):
    lane = _iota((n_keys, D_HEAD), 1)
    blk = _iota((n_keys, D_HEAD), 0) // block
    return jnp.where((lane < n_blocks) & (lane == blk), 1.0, 0.0).astype(BF16)


def _query_aug(qt, bias):
    pad = jnp.zeros((D_HEAD - bias.shape[0], qt.shape[1]), F32)
    return jnp.concatenate([qt, bias, pad], axis=0).astype(BF16)


MOBA_STEP_WIDTH = 512


def _moba_p_kernel(q_ref, k_ref, v_ref, o_ref, ka_scr, vt_scr, km_scr, s_scr):
    qi = pl.program_id(2)
    tq = tk = MOBA_BLOCK
    seq = k_ref.shape[1]
    nblk = seq // MOBA_BLOCK
    width = q_ref.shape[2]
    nh = width // D_HEAD

    @pl.when(qi == 0)
    def _():
        kf = k_ref[0]
        km_scr[...] = jnp.mean(kf.reshape(nblk, MOBA_BLOCK, width), axis=1)
        aug = _key_aug(seq, MOBA_BLOCK, nblk)
        for hh in range(nh):
            ka_scr[hh] = jnp.concatenate([kf[:, hh * D_HEAD:(hh + 1) * D_HEAD].astype(BF16), aug], axis=1)
        for j in range(nblk):
            vt_scr[j] = v_ref[0, :, j * tk:(j + 1) * tk].astype(BF16)

    q2t = q_ref[0].T
    km = km_scr[...]
    klane = _iota(km.shape, 1)
    blk = _iota((nblk, tq), 0)
    causal_t = _iota((tk, tq), 0) <= _iota((tk, tq), 1)
    qts, biases = [], []
    for hh in range(nh):
        kmh = jnp.where((klane >= hh * D_HEAD) & (klane < (hh + 1) * D_HEAD), km, 0.0)
        gate = jnp.dot(kmh, q2t, precision=HIGHEST, preferred_element_type=F32)
        gate = jnp.where(blk < qi, gate, NEG_INF)
        keep = _rank_select_t(gate, nblk, MOBA_TOPK) & (blk < qi)
        biases.append(jnp.where(keep, 0.0, NEG_INF))
        qts.append(q2t[hh * D_HEAD:(hh + 1) * D_HEAD, :] * SCALE)

    q_own = [_query_aug(qts[hh], jnp.zeros_like(biases[hh])) for hh in range(nh)]
    q_past = [_query_aug(qts[hh], biases[hh]) for hh in range(nh)]

    def scores(c, own):
        out = []
        for hh in range(nh):
            kj = ka_scr[hh, pl.ds(pl.multiple_of(c * tk, tk), tk), :]
            s = _dot(kj, q_own[hh] if own else q_past[hh])
            out.append(jnp.where(causal_t, s, NEG_INF) if own else s)
        return out

    def values(c):
        return [vt_scr[c, hh * D_HEAD:(hh + 1) * D_HEAD, :] for hh in range(nh)]

    o_ref[0] = jnp.concatenate(_attend_t(scores, values, 0, qi, tq, nh, s_scr), axis=0).T


def _moba_p(qk, vt, batch, seq):
    qk3 = qk.reshape(batch, seq, 2 * MOBA_WIDTH)
    sw = MOBA_STEP_WIDTH
    pairs = MOBA_WIDTH // sw
    nblk = seq // MOBA_BLOCK
    out = pl.pallas_call(
        _moba_p_kernel,
        grid=(batch, pairs, nblk),
        in_specs=[pl.BlockSpec((1, MOBA_BLOCK, sw), lambda b, h, i: (b, i, h)),
                  pl.BlockSpec((1, seq, sw), lambda b, h, i: (b, 0, pairs + h)),
                  pl.BlockSpec((1, sw, seq), lambda b, h, i: (b, h, 0))],
        out_specs=pl.BlockSpec((1, MOBA_BLOCK, sw), lambda b, h, i: (b, i, h)),
        out_shape=jax.ShapeDtypeStruct((batch, seq, MOBA_WIDTH), F32),
        scratch_shapes=[pltpu.VMEM((sw // D_HEAD, seq, 2 * D_HEAD), BF16), pltpu.VMEM((nblk, sw, MOBA_BLOCK), BF16),
                        pltpu.VMEM((nblk, sw), F32),
                        pltpu.VMEM((sw // D_HEAD, nblk, MOBA_BLOCK, MOBA_BLOCK), F32)],
        compiler_params=_params("arbitrary", "arbitrary", "arbitrary"),
        name="moba_prompt",
    )(qk3, qk3, vt)
    return out.reshape(batch * seq, MOBA_WIDTH)


def _gelu(x):
    return 0.5 * x * (1.0 + jnp.tanh(0.7978845608028654 * (x + 0.044715 * x * x * x)))


def _compress_p_kernel(seg_ref, pea_ref, peb_ref, wa_ref, wb_ref, w2_ref, o_ref):
    seg = seg_ref[0]
    a = _dot((seg + pea_ref[...]).astype(BF16), wa_ref[...])
    b = _dot((seg + peb_ref[...]).astype(BF16), wb_ref[...])
    nseg = seg.shape[0]
    hid = a + pltpu.roll(b, nseg - 1, 0)
    o_ref[0] = _dot(_gelu(hid).astype(BF16), w2_ref[...])


def _compress_weights(pe, w1, w2):
    g = NSA_KV_HEADS
    eye = jnp.eye(g, dtype=F32)
    w1r = w1.reshape(CMP_LEN, D_HEAD, CMP_HIDDEN)

    def half(lo):
        w = w1r[lo:lo + CMP_STRIDE]
        wbd = jnp.einsum('ldf,gh->lgdhf', w, eye)
        p = jnp.broadcast_to(pe[lo:lo + CMP_STRIDE, None, :], (CMP_STRIDE, g, D_HEAD))
        return wbd.reshape(CMP_STRIDE, g * D_HEAD, g * CMP_HIDDEN).astype(BF16), p.reshape(CMP_STRIDE, g * D_HEAD)

    wa, pea = half(0)
    wb, peb = half(CMP_STRIDE)
    w2bd = jnp.einsum('fd,gh->gfhd', w2, eye).reshape(g * CMP_HIDDEN, g * D_HEAD).astype(BF16)
    return wa, wb, pea, peb, w2bd


def _compress_p(rows, cw, batch, seq):
    wa, wb, pea, peb, w2bd = cw
    nseg = seq // CMP_STRIDE
    width = CMP_STRIDE * NSA_KV_WIDTH
    seg = rows.reshape(batch, nseg, width)
    full = lambda b: (0, 0)
    return pl.pallas_call(
        _compress_p_kernel,
        grid=(batch,),
        in_specs=[pl.BlockSpec((1, nseg, width), lambda b: (b, 0, 0)),
                  pl.BlockSpec((1, width), full), pl.BlockSpec((1, width), full),
                  pl.BlockSpec((width, NSA_KV_HEADS * CMP_HIDDEN), full),
                  pl.BlockSpec((width, NSA_KV_HEADS * CMP_HIDDEN), full),
                  pl.BlockSpec((NSA_KV_HEADS * CMP_HIDDEN, NSA_KV_WIDTH), full)],
        out_specs=pl.BlockSpec((1, nseg, NSA_KV_WIDTH), lambda b: (b, 0, 0)),
        out_shape=jax.ShapeDtypeStruct((batch, nseg, NSA_KV_WIDTH), F32),
        compiler_params=_params("arbitrary"),
        name="compress_prompt",
    )(seg, pea.reshape(1, width), peb.reshape(1, width), wa.reshape(width, -1), wb.reshape(width, -1), w2bd)


NSA_TQ = 256
NSA_TK = 256


def _overlap_matrix(nc_pad, nslc_pad):
    cs = jnp.arange(nc_pad)[:, None] * CMP_STRIDE
    ss = jnp.arange(nslc_pad)[None, :] * SLC_BLOCK
    return ((cs < ss + SLC_BLOCK) & (cs + CMP_LEN > ss)).astype(F32)


def _nsa_p_kernel(q_ref, kc_ref, vc_ref, ks_ref, vs_ref, kw_ref, vw_ref, gate_ref, ovt_ref, o_ref,
                  ksa_scr, kwb_scr, vst_scr, vwt_scr, ss_scr, sw_scr):
    tq, tk = NSA_TQ, NSA_TK
    t = pl.program_id(1)
    q0 = t * tq
    seq = ks_ref.shape[1]
    nc_pad = kc_ref.shape[1]
    nslc = seq // SLC_BLOCK
    width = NSA_GROUP * tq

    @pl.when(t == 0)
    def _():
        aug_s = _key_aug(seq, SLC_BLOCK, nslc)
        for g in range(NSA_KV_HEADS):
            rows = slice(g * D_HEAD, (g + 1) * D_HEAD)
            ksa_scr[g] = jnp.concatenate([ks_ref[0, :, rows].astype(BF16), aug_s], axis=1)
        kwb_scr[...] = kw_ref[0].astype(BF16)
        for j in range(seq // tk):
            vst_scr[j] = vs_ref[0, :, j * tk:(j + 1) * tk].astype(BF16)
            vwt_scr[j] = vw_ref[0, :, j * tk:(j + 1) * tk].astype(BF16)

    qt_all = q_ref[0].T
    gates_t = _sigmoid(gate_ref[0]).T
    kct = kc_ref[0].astype(BF16)
    vct = vc_ref[0].T.astype(BF16)
    pos = q0 + _iota((1, tq), 1)
    n_i = _iota((nc_pad, tq), 0)
    valid_c = (n_i < nc_pad - 1) & (n_i * CMP_STRIDE + (CMP_LEN - 1) <= pos)
    valid_c4 = jnp.concatenate([valid_c] * NSA_GROUP, axis=1)
    q_blk = pos >> 6
    jrow = _iota((nslc, tq), 0)
    krow = _iota((tk, tq), 0)
    cd = q0 // tk
    c_win = jnp.maximum(q0 - WINDOW, 0) // tk

    def tile(b):
        return jnp.concatenate([b] * NSA_GROUP, axis=1)

    def win_bias(c):
        dist = pos - (c * tk + krow)
        return jnp.where((dist >= 0) & (dist <= WINDOW), 0.0, NEG_INF)

    qts, o_cs, biases = [], [], []
    for g in range(NSA_KV_HEADS):
        rows = slice(g * D_HEAD, (g + 1) * D_HEAD)
        qt = jnp.concatenate(
            [qt_all[(g * NSA_GROUP + r) * D_HEAD:(g * NSA_GROUP + r + 1) * D_HEAD, :] * SCALE
             for r in range(NSA_GROUP)], axis=1)
        qts.append(qt)
        s = jnp.where(valid_c4, _dot(kct[:, rows], qt.astype(BF16)), NEG_INF)
        m = jnp.max(s, axis=0, keepdims=True)
        p = jnp.where(valid_c4, jnp.exp(s - m), 0.0)
        p = p / jnp.maximum(jnp.sum(p, axis=0, keepdims=True), TINY)
        o_cs.append(_dot(vct[rows, :], p.astype(BF16)))
        psum = p[:, :tq]
        for r in range(1, NSA_GROUP):
            psum = psum + p[:, r * tq:(r + 1) * tq]
        imp = jnp.dot(ovt_ref[...], psum, precision=HIGHEST, preferred_element_type=F32)
        imp = jnp.where(jrow == q_blk, BIG, jnp.where(jrow < q_blk, imp, NEG_INF))
        keep = _rank_select_t(imp, nslc, SLC_TOPN) & (jrow <= q_blk)
        biases.append(tile(jnp.where(keep, 0.0, NEG_INF)))

    causal_own = tile(cd * tk + krow <= pos)

    qa_slc = [_query_aug(qts[g], biases[g]) for g in range(NSA_KV_HEADS)]
    qb_win = [qts[g].astype(BF16) for g in range(NSA_KV_HEADS)]

    def slc_scores(c, own):
        out = []
        for g in range(NSA_KV_HEADS):
            s = _dot(ksa_scr[g, pl.ds(pl.multiple_of(c * tk, tk), tk), :], qa_slc[g])
            out.append(jnp.where(causal_own, s, NEG_INF) if own else s)
        return out

    def win_scores(c, own):
        wb = tile(win_bias(c))
        return [_dot(kwb_scr[pl.ds(pl.multiple_of(c * tk, tk), tk), g * D_HEAD:(g + 1) * D_HEAD], qb_win[g]) + wb
                for g in range(NSA_KV_HEADS)]

    def values(vt_scr):
        return lambda c: [vt_scr[c, g * D_HEAD:(g + 1) * D_HEAD, :] for g in range(NSA_KV_HEADS)]

    o_ss = _attend_t(slc_scores, values(vst_scr), 0, cd, width, NSA_KV_HEADS, ss_scr)
    o_ws = _attend_t(win_scores, values(vwt_scr), c_win, cd, width, NSA_KV_HEADS, sw_scr)
    outs = []
    for g in range(NSA_KV_HEADS):
        o_c, o_s, o_w = o_cs[g], o_ss[g], o_ws[g]
        for r in range(NSA_GROUP):
            hd = g * NSA_GROUP + r
            cs = slice(r * tq, (r + 1) * tq)
            outs.append(gates_t[3 * hd:3 * hd + 1, :] * o_c[:, cs] + gates_t[3 * hd + 1:3 * hd + 2, :] * o_s[:, cs]
                        + gates_t[3 * hd + 2:3 * hd + 3, :] * o_w[:, cs])
    o_ref[0] = jnp.concatenate(outs, axis=0).T


def _nsa_p(q, kn, vnt, kcmp, vcmp, gate, batch, seq):
    q3 = q.reshape(batch, seq, NSA_WIDTH)
    kn3 = kn.reshape(batch, seq, 3 * NSA_KV_WIDTH)
    g3 = gate.reshape(batch, seq, LANES)
    nc_pad = kcmp.shape[1]
    nslc = seq // SLC_BLOCK
    nchunk = seq // NSA_TK
    ovt = _overlap_matrix(nc_pad, nslc).T
    tile = lambda b, t: (b, t, 0)
    cmp_spec = pl.BlockSpec((1, nc_pad, NSA_KV_WIDTH), lambda b, t: (b, 0, 0))
    out = pl.pallas_call(
        _nsa_p_kernel,
        grid=(batch, seq // NSA_TQ),
        in_specs=[pl.BlockSpec((1, NSA_TQ, NSA_WIDTH), tile), cmp_spec, cmp_spec,
                  pl.BlockSpec((1, seq, LANES), lambda b, t: (b, 0, 1)),
                  pl.BlockSpec((1, LANES, seq), lambda b, t: (b, 1, 0)),
                  pl.BlockSpec((1, seq, LANES), lambda b, t: (b, 0, 2)),
                  pl.BlockSpec((1, LANES, seq), lambda b, t: (b, 2, 0)),
                  pl.BlockSpec((1, NSA_TQ, LANES), tile),
                  pl.BlockSpec((nslc, nc_pad), lambda b, t: (0, 0))],
        out_specs=pl.BlockSpec((1, NSA_TQ, NSA_WIDTH), tile),
        out_shape=jax.ShapeDtypeStruct((batch, seq, NSA_WIDTH), F32),
        scratch_shapes=[pltpu.VMEM((NSA_KV_HEADS, seq, 2 * D_HEAD), BF16), pltpu.VMEM((seq, LANES), BF16),
                        pltpu.VMEM((nchunk, LANES, NSA_TK), BF16), pltpu.VMEM((nchunk, LANES, NSA_TK), BF16),
                        pltpu.VMEM((NSA_KV_HEADS, nchunk, NSA_TK, NSA_GROUP * NSA_TQ), F32),
                        pltpu.VMEM((NSA_KV_HEADS, WINDOW // NSA_TK + 1, NSA_TK, NSA_GROUP * NSA_TQ), F32)],
        compiler_params=_params("arbitrary", "arbitrary"),
        name="nsa_prompt",
    )(q3, kcmp, vcmp, kn3, vnt, kn3, vnt, g3, ovt)
    return out.reshape(batch * seq, NSA_WIDTH)


def _merge_kernel(oa_ref, ob_ref, x_ref, sc_ref, sh_ref, g1_ref, gn_ref, wm_ref, wa_ref, wb_ref, wo_ref, o_ref):
    x = x_ref[...]
    h = _norm_mod(x, gn_ref[...], sc_ref[0], sh_ref[0]).astype(BF16)
    mg = _dot(h, wm_ref[...])
    a = _dot(oa_ref[...].astype(BF16), wa_ref[...])
    b = _dot(ob_ref[...].astype(BF16), wb_ref[...])
    mix = _sigmoid(mg[:, :D_MODEL]) * a + _sigmoid(mg[:, D_MODEL:]) * b
    o_ref[...] = x + g1_ref[0] * _dot(mix.astype(BF16), wo_ref[...])


def _merge(oa, ob, x, sc, sh, g1, gn, wm, wa, wb, wo, tm):
    n = x.shape[0]
    nb, r, _ = g1.shape
    tiles_per_b = (n // nb) // tm
    row = lambda i: (i, 0)
    full = lambda i: (0, 0)
    mod = pl.BlockSpec((1, r, D_MODEL), lambda i: (i // tiles_per_b, 0, 0))
    return pl.pallas_call(
        _merge_kernel,
        grid=(n // tm,),
        in_specs=[pl.BlockSpec((tm, MOBA_WIDTH), row), pl.BlockSpec((tm, NSA_WIDTH), row),
                  pl.BlockSpec((tm, D_MODEL), row), mod, mod, mod,
                  pl.BlockSpec((1, D_MODEL), full), pl.BlockSpec((D_MODEL, 2 * D_MODEL), full),
                  pl.BlockSpec((MOBA_WIDTH, D_MODEL), full), pl.BlockSpec((NSA_WIDTH, D_MODEL), full),
                  pl.BlockSpec((D_MODEL, D_MODEL), full)],
        out_specs=pl.BlockSpec((tm, D_MODEL), row),
        out_shape=jax.ShapeDtypeStruct((n, D_MODEL), F32),
        compiler_params=_params("arbitrary"),
        name="merge",
    )(oa, ob, x, sc, sh, g1, gn, wm, wa, wb, wo)


def _route(logits):
    lane = _iota(logits.shape, 1)
    is_grp = lane < N_GROUPS
    lg = jnp.where(is_grp, logits, NEG_INF)
    mg = jnp.max(lg, axis=-1, keepdims=True)
    pg = jnp.where(is_grp, jnp.exp(lg - mg), 0.0)
    pg = pg / jnp.sum(pg, axis=-1, keepdims=True)
    g_w = jnp.max(pg, axis=-1, keepdims=True)
    g_sel = jnp.min(jnp.where(is_grp & (pg == g_w), lane, LANES), axis=-1, keepdims=True)
    e_lane = lane - N_GROUPS
    in_grp = (e_lane >= 0) & (e_lane < N_EXPERTS) & ((e_lane >> 3) == g_sel)
    le = jnp.where(in_grp, logits, NEG_INF)
    me = jnp.max(le, axis=-1, keepdims=True)
    pe = jnp.where(in_grp, jnp.exp(le - me), 0.0)
    pe = pe / jnp.sum(pe, axis=-1, keepdims=True)
    v1 = jnp.max(pe, axis=-1, keepdims=True)
    i1 = jnp.min(jnp.where(in_grp & (pe == v1), lane, LANES), axis=-1, keepdims=True)
    rest = in_grp & (lane != i1)
    pr = jnp.where(rest, pe, -1.0)
    v2 = jnp.max(pr, axis=-1, keepdims=True)
    i2 = jnp.min(jnp.where(rest & (pr == v2), lane, LANES), axis=-1, keepdims=True)
    tot = v1 + v2
    comb = jnp.where(lane == i1, v1 / tot, 0.0) + jnp.where(lane == i2, v2 / tot, 0.0)
    comb = comb * g_w
    return pltpu.roll(comb, LANES - N_GROUPS, 1)


MOE_EXPERTS_PER_STEP = 8


def _moe_kernel(x_ref, sc_ref, sh_ref, g2_ref, gn_ref, gf_ref, wr_ref, win_ref, wout_ref, o_ref,
                h_scr, comb_scr, acc_scr):
    e = pl.program_id(1)

    @pl.when(e == 0)
    def _():
        h = _norm_mod(x_ref[...], gn_ref[...], sc_ref[0], sh_ref[0])
        hb = h.astype(BF16)
        h_scr[...] = hb
        h_lo = (h - hb.astype(F32)).astype(BF16)
        logits = _dot(hb, wr_ref[0]) + _dot(hb, wr_ref[1]) + _dot(h_lo, wr_ref[0])
        comb_scr[...] = _route(logits)
        acc_scr[...] = jnp.zeros_like(acc_scr)

    per = win_ref.shape[0]
    hb = h_scr[...]
    comb = comb_scr[...]
    lane = _iota(comb.shape, 1)
    acts = []
    for j in range(per):
        hid = _dot(hb, win_ref[j])
        a = hid[:, :D_EXPERT]
        b = hid[:, D_EXPERT:]
        w = jnp.sum(jnp.where(lane == e * per + j, comb, 0.0), axis=1, keepdims=True)
        acts.append((a * _sigmoid(a) * b * w).astype(BF16))
    act = jnp.concatenate(acts, axis=1)
    acc_scr[...] += _dot(act, wout_ref[...].reshape(per * D_EXPERT, D_MODEL))

    @pl.when(e == pl.num_programs(1) - 1)
    def _():
        y = x_ref[...] + g2_ref[0] * acc_scr[...]
        ms = jnp.mean(y * y, axis=-1, keepdims=True)
        o_ref[...] = y * lax.rsqrt(ms + EPS) * gf_ref[...]


def _moe(x, sc, sh, g2, gn, gf, wr, w_ein, w_eout, tm):
    n = x.shape[0]
    nb, r, _ = sc.shape
    tiles_per_b = (n // nb) // tm
    row = lambda i, e: (i, 0)
    mod = lambda i, e: (i // tiles_per_b, 0, 0)
    full = lambda i, e: (0, 0)
    per = MOE_EXPERTS_PER_STEP
    return pl.pallas_call(
        _moe_kernel,
        grid=(n // tm, N_EXPERTS // per),
        in_specs=[pl.BlockSpec((tm, D_MODEL), row),
                  pl.BlockSpec((1, r, D_MODEL), mod), pl.BlockSpec((1, r, D_MODEL), mod),
                  pl.BlockSpec((1, r, D_MODEL), mod),
                  pl.BlockSpec((1, D_MODEL), full), pl.BlockSpec((1, D_MODEL), full),
                  pl.BlockSpec((2, D_MODEL, LANES), lambda i, e: (0, 0, 0)),
                  pl.BlockSpec((per, D_MODEL, 2 * D_EXPERT), lambda i, e: (e, 0, 0)),
                  pl.BlockSpec((per, D_EXPERT, D_MODEL), lambda i, e: (e, 0, 0))],
        out_specs=pl.BlockSpec((tm, D_MODEL), row),
        out_shape=jax.ShapeDtypeStruct((n, D_MODEL), F32),
        scratch_shapes=[pltpu.VMEM((tm, D_MODEL), BF16), pltpu.VMEM((tm, LANES), F32),
                        pltpu.VMEM((tm, D_MODEL), F32)],
        compiler_params=_params("arbitrary", "arbitrary"),
        name="moe",
    )(x, sc, sh, g2, gn, gf, wr, w_ein, w_eout)


PAGE = 128
MOBA_PAGES_PER_STEP = 16
NSA_PAGES_PER_STEP = 32
COMPRESS_PAGES_PER_STEP = 32


def _page_view(cache):
    n_phys, page, heads, dh = cache.shape
    return jnp.transpose(cache, (0, 2, 3, 1)).reshape(n_phys, heads * dh, page)


PAGE_RING_DEPTH = 3
SLC_RING_DEPTH = 4


def _page_ring_step(pt_ref, streams, per):
    depth = streams[0][1].shape[0]
    ns = pl.num_programs(1)
    step = pl.program_id(0) * ns + pl.program_id(1)
    last = pl.num_programs(0) * ns - 1

    def copies(step_idx):
        slot_idx = step_idx % depth
        return [pltpu.make_async_copy(hbm.at[pt_ref[step_idx * per + u]], buf.at[slot_idx, u], sem.at[slot_idx])
                for hbm, buf, sem in streams for u in range(per)]

    @pl.when(step == 0)
    def _():
        for j in range(depth - 1):
            @pl.when(j <= last)
            def _():
                for c in copies(j):
                    c.start()

    for c in copies(step):
        c.wait()

    @pl.when(step + (depth - 1) <= last)
    def _():
        for c in copies(step + (depth - 1)):
            c.start()

    return step % depth


def _head_diag(full, heads):
    rows = full.shape[0]
    head = _iota((rows, D_HEAD), 0) // (rows // heads)
    out = jnp.zeros((rows, D_HEAD), F32)
    for h in range(heads):
        out = out + jnp.where(head == h, full[:, h * D_HEAD:(h + 1) * D_HEAD], 0.0)
    return out


def _moba_s_kernel(pt_ref, qbd_ref, k_hbm, v_hbm, m_ref, l_ref, ks_ref, o_ref, kbuf, vbuf, ksem, vsem):
    per = MOBA_PAGES_PER_STEP
    s = pl.program_id(1)
    slot = _page_ring_step(pt_ref, [(k_hbm, kbuf, ksem), (v_hbm, vbuf, vsem)], per)

    @pl.when(s == 0)
    def _():
        m_ref[...] = jnp.zeros_like(m_ref)
        l_ref[...] = jnp.zeros_like(l_ref)
        ks_ref[...] = jnp.zeros_like(ks_ref)

    qbd = qbd_ref[0]
    lane_q = _iota(m_ref.shape[1:], 1)
    lane_k = _iota(ks_ref.shape[1:], 1)
    ppb = MOBA_BLOCK // PAGE
    m_all, l_all, ks_all = m_ref[0], l_ref[0], ks_ref[0]
    kts = [kbuf[slot, u] for u in range(per)]
    sc_all = _dot(qbd, jnp.concatenate([kt.astype(BF16) for kt in kts], axis=1))
    for j in range(per // ppb):
        blk = s * (per // ppb) + j
        sc = sc_all[:, j * MOBA_BLOCK:(j + 1) * MOBA_BLOCK]
        m = jnp.max(sc, axis=-1, keepdims=True)
        p = jnp.exp(sc - m)
        l = jnp.sum(p, axis=-1, keepdims=True)
        vt = jnp.concatenate([vbuf[slot, j * ppb + t].astype(BF16) for t in range(ppb)], axis=1)
        o_ref[0, j] = _head_diag(_dot_nt(p.astype(BF16), vt), MOBA_HEADS)
        kb = kts[j * ppb]
        for t in range(1, ppb):
            kb = kb + kts[j * ppb + t]
        ksum = jnp.sum(kb, axis=-1, keepdims=True)
        m_all = jnp.where(lane_q == blk, m, m_all)
        l_all = jnp.where(lane_q == blk, l, l_all)
        ks_all = jnp.where(lane_k == blk, ksum, ks_all)
    m_ref[0] = m_all
    l_ref[0] = l_all
    ks_ref[0] = ks_all


def _moba_s_pass(pt_flat, qbd, kt_pages, vt_pages, batch, n_pages):
    per = MOBA_PAGES_PER_STEP
    rows = qbd.shape[1]
    ppb = MOBA_BLOCK // PAGE
    stat = lambda b, s, pt: (b, 0, 0)
    return pl.pallas_call(
        _moba_s_kernel,
        grid_spec=pltpu.PrefetchScalarGridSpec(
            num_scalar_prefetch=1,
            grid=(batch, n_pages // per),
            in_specs=[pl.BlockSpec((1, rows, MOBA_WIDTH), stat),
                      pl.BlockSpec(memory_space=pl.ANY), pl.BlockSpec(memory_space=pl.ANY)],
            out_specs=[pl.BlockSpec((1, rows, LANES), stat), pl.BlockSpec((1, rows, LANES), stat),
                       pl.BlockSpec((1, MOBA_WIDTH, LANES), stat),
                       pl.BlockSpec((1, per // ppb, rows, D_HEAD), lambda b, s, pt: (b, s, 0, 0))],
            scratch_shapes=[pltpu.VMEM((PAGE_RING_DEPTH, per, MOBA_WIDTH, PAGE), F32),
                            pltpu.VMEM((PAGE_RING_DEPTH, per, MOBA_WIDTH, PAGE), F32),
                            pltpu.SemaphoreType.DMA((PAGE_RING_DEPTH,)), pltpu.SemaphoreType.DMA((PAGE_RING_DEPTH,))]),
        out_shape=[jax.ShapeDtypeStruct((batch, rows, LANES), F32), jax.ShapeDtypeStruct((batch, rows, LANES), F32),
                   jax.ShapeDtypeStruct((batch, MOBA_WIDTH, LANES), F32),
                   jax.ShapeDtypeStruct((batch, n_pages // ppb, rows, D_HEAD), F32)],
        compiler_params=_params("arbitrary", "arbitrary"),
        name="moba_decode_pages",
    )(pt_flat, qbd, kt_pages, vt_pages)


def _moba_s_combine_kernel(m_ref, l_ref, ks_ref, o_ref, qf_ref, qbd_ref, kn_ref, vn_ref, out_ref, *, n_pages, ts):
    rows = m_ref.shape[1]
    nblk = n_pages // (MOBA_BLOCK // PAGE)
    lane = _iota((rows, LANES), 1)
    kmean = ks_ref[0] * (1.0 / MOBA_BLOCK)
    gate = jnp.dot(qf_ref[0], kmean, precision=HIGHEST, preferred_element_type=F32)
    gate = jnp.where(lane < nblk, gate, NEG_INF)
    selp = _rank_select(gate, nblk, MOBA_TOPK) & (lane < nblk)
    qbd = qbd_ref[0]
    s_own = _dot_nt(qbd, _new_rows(kn_ref).astype(BF16))
    valid_own = lane <= (_iota((rows, LANES), 0) % ts)
    s_own = jnp.where(valid_own, s_own, NEG_INF)
    m_all = jnp.where(selp, m_ref[0], NEG_INF)
    big_m = jnp.maximum(jnp.max(m_all, axis=-1, keepdims=True), jnp.max(s_own, axis=-1, keepdims=True))
    wgt = jnp.where(selp, jnp.exp(m_ref[0] - big_m), 0.0)
    p_own = jnp.where(valid_own, jnp.exp(s_own - big_m), 0.0)
    denom = jnp.sum(wgt * l_ref[0], axis=-1, keepdims=True) + jnp.sum(p_own, axis=-1, keepdims=True)
    num = _head_diag(_dot(p_own.astype(BF16), _new_rows(vn_ref).astype(BF16)), MOBA_HEADS)
    for j in range(nblk):
        num = num + wgt[:, j:j + 1] * o_ref[0, j]
    out_ref[0] = num / jnp.maximum(denom, TINY)


def _moba_s_combine(m, l, ks, o, qf, qbd, kn, vn, n_pages, ts):
    batch, rows, _ = m.shape
    b3 = lambda b: (b, 0, 0)
    return pl.pallas_call(
        functools.partial(_moba_s_combine_kernel, n_pages=n_pages, ts=ts),
        grid=(batch,),
        in_specs=[pl.BlockSpec((1, rows, LANES), b3), pl.BlockSpec((1, rows, LANES), b3),
                  pl.BlockSpec((1, MOBA_WIDTH, LANES), b3),
                  pl.BlockSpec((1, o.shape[1], rows, D_HEAD), lambda b: (b, 0, 0, 0)),
                  pl.BlockSpec((1, rows, MOBA_WIDTH), b3), pl.BlockSpec((1, rows, MOBA_WIDTH), b3),
                  pl.BlockSpec((1, ts, MOBA_WIDTH), b3), pl.BlockSpec((1, ts, MOBA_WIDTH), b3)],
        out_specs=pl.BlockSpec((1, rows, D_HEAD), b3),
        out_shape=jax.ShapeDtypeStruct((batch, rows, D_HEAD), F32),
        compiler_params=_params("arbitrary"),
        name="moba_decode_combine",
    )(m, l, ks, o, qf, qbd, kn, vn)


def _block_diag_q(q, batch, ts, heads):
    q4 = q.reshape(batch, ts, heads, D_HEAD)
    eye = jnp.eye(heads, dtype=q.dtype)
    return jnp.einsum('bchd,hk->bhckd', q4, eye).reshape(batch, heads * ts, heads * D_HEAD)


def _new_rows(ref, lane0=0, width=None):
    rows = ref[0] if width is None else ref[0, :, lane0:lane0 + width]
    return jnp.concatenate([rows, jnp.zeros((LANES - rows.shape[0], rows.shape[1]), rows.dtype)], axis=0)


def _compress_s_kernel(pt_ref, k_hbm, v_hbm, pea_ref, peb_ref, wa_ref, wb_ref, w2_ref, ok_ref, ov_ref,
                       kbuf, vbuf, ksem, vsem, x_scr, a_scr, b_scr):
    per = COMPRESS_PAGES_PER_STEP
    s = pl.program_id(1)
    nseg = per * PAGE // CMP_STRIDE
    slot = _page_ring_step(pt_ref, [(k_hbm, kbuf, ksem), (v_hbm, vbuf, vsem)], per)
    for i, (buf, o_ref) in enumerate(((kbuf, ok_ref), (vbuf, ov_ref))):
        x_i, a_i, b_i, pea_i, peb_i = x_scr.at[i], a_scr.at[i], b_scr.at[i], pea_ref.at[i], peb_ref.at[i]
        for u in range(per):
            x_i[u * PAGE:(u + 1) * PAGE, :] = buf[slot, u].T
        xs = [x_i[pl.ds(l, nseg, stride=CMP_STRIDE), :] for l in range(CMP_STRIDE)]
        xa = jnp.concatenate([(xs[l] + pea_i[l:l + 1, :]).astype(BF16) for l in range(CMP_STRIDE)], axis=1)
        xb = jnp.concatenate([(xs[l] + peb_i[l:l + 1, :]).astype(BF16) for l in range(CMP_STRIDE)], axis=1)
        a_i[pl.ds(pl.multiple_of(s * nseg, nseg), nseg), :] = _dot(xa, wa_ref[i])
        b_i[pl.ds(pl.multiple_of(s * nseg, nseg), nseg), :] = _dot(xb, wb_ref[i])

        @pl.when(s == pl.num_programs(1) - 1)
        def _(i=i, o_ref=o_ref, a_i=a_i, b_i=b_i):
            total = a_i.shape[0]
            hid = a_i[...] + pltpu.roll(b_i[...], total - 1, 0)
            o_ref[0] = _dot(_gelu(hid).astype(BF16), w2_ref[i])


def _compress_s(pt_flat, k_pages, v_pages, cw_k, cw_v, batch, n_pages):
    per = COMPRESS_PAGES_PER_STEP
    total = n_pages * PAGE // CMP_STRIDE
    hidden = NSA_KV_HEADS * CMP_HIDDEN
    width = CMP_STRIDE * NSA_KV_WIDTH
    wa, wb, pea, peb, w2bd = [jnp.stack([a, b]) for a, b in zip(cw_k, cw_v)]
    full3 = lambda b, s, pt: (0, 0, 0)
    out_spec = pl.BlockSpec((1, total, NSA_KV_WIDTH), lambda b, s, pt: (b, 0, 0))
    out = jax.ShapeDtypeStruct((batch, total, NSA_KV_WIDTH), F32)
    page_buf = pltpu.VMEM((PAGE_RING_DEPTH, per, NSA_KV_WIDTH, PAGE), F32)
    return pl.pallas_call(
        _compress_s_kernel,
        grid_spec=pltpu.PrefetchScalarGridSpec(
            num_scalar_prefetch=1,
            grid=(batch, n_pages // per),
            in_specs=[pl.BlockSpec(memory_space=pl.ANY), pl.BlockSpec(memory_space=pl.ANY),
                      pl.BlockSpec((2, CMP_STRIDE, NSA_KV_WIDTH), full3),
                      pl.BlockSpec((2, CMP_STRIDE, NSA_KV_WIDTH), full3),
                      pl.BlockSpec((2, width, hidden), full3), pl.BlockSpec((2, width, hidden), full3),
                      pl.BlockSpec((2, hidden, NSA_KV_WIDTH), full3)],
            out_specs=[out_spec, out_spec],
            scratch_shapes=[page_buf, page_buf,
                            pltpu.SemaphoreType.DMA((PAGE_RING_DEPTH,)), pltpu.SemaphoreType.DMA((PAGE_RING_DEPTH,)),
                            pltpu.VMEM((2, per * PAGE, NSA_KV_WIDTH), F32), pltpu.VMEM((2, total, hidden), F32),
                            pltpu.VMEM((2, total, hidden), F32)]),
        out_shape=[out, out],
        compiler_params=_params("arbitrary", "arbitrary"),
        name="compress_decode",
    )(pt_flat, k_pages, v_pages, pea, peb, wa.reshape(2, width, hidden), wb.reshape(2, width, hidden), w2bd)


def _stack_group_q(q_ref, g):
    return jnp.concatenate(
        [q_ref[0, :, (g * NSA_GROUP + r) * D_HEAD:(g * NSA_GROUP + r + 1) * D_HEAD] * SCALE
         for r in range(NSA_GROUP)], axis=0).astype(BF16)


def _nsa_s_kernel(q_ref, kc_ref, vc_ref, wk_ref, wv_ref, kn_ref, vn_ref, gate_ref, ov_ref, part_ref, sel_ref, *, ts):
    rows = NSA_GROUP * ts
    nc_pad = kc_ref.shape[1]
    wlen = wk_ref.shape[3]
    gates = _sigmoid(gate_ref[0])
    n_i = _iota((rows, nc_pad), 1)
    valid_c = n_i < nc_pad - 1
    c_of_row = _iota((rows, 1), 0) % ts
    valid_w = _iota((rows, wlen), 1) >= c_of_row
    valid_n = _iota((rows, LANES), 1) <= c_of_row
    for g in range(NSA_KV_HEADS):
        lane0 = g * D_HEAD
        qs = _stack_group_q(q_ref, g)
        s = jnp.where(valid_c, _dot_nt(qs, kc_ref[0, :, lane0:lane0 + D_HEAD].astype(BF16)), NEG_INF)
        m = jnp.max(s, axis=-1, keepdims=True)
        p = jnp.where(valid_c, jnp.exp(s - m), 0.0)
        p = p / jnp.maximum(jnp.sum(p, axis=-1, keepdims=True), TINY)
        o_c = _dot(p.astype(BF16), vc_ref[0, :, lane0:lane0 + D_HEAD].astype(BF16))
        psum = jnp.sum(p.reshape(NSA_GROUP, ts, nc_pad), axis=0)
        imp = jnp.dot(psum, ov_ref[...], precision=HIGHEST, preferred_element_type=F32)
        keep = _rank_select(imp, LANES, SLC_TOPN - 1)
        sel_ref[0, g] = jnp.where(keep, 0.0, NEG_INF)
        s_w = jnp.where(valid_w, _dot(qs, wk_ref[0, g].astype(BF16)), NEG_INF)
        s_n = jnp.where(valid_n, _dot_nt(qs, _new_rows(kn_ref, lane0, D_HEAD).astype(BF16)), NEG_INF)
        m = jnp.maximum(jnp.max(s_w, axis=-1, keepdims=True), jnp.max(s_n, axis=-1, keepdims=True))
        p_w = jnp.where(valid_w, jnp.exp(s_w - m), 0.0)
        p_n = jnp.where(valid_n, jnp.exp(s_n - m), 0.0)
        den = jnp.sum(p_w, axis=-1, keepdims=True) + jnp.sum(p_n, axis=-1, keepdims=True)
        o_w = (_dot_nt(p_w.astype(BF16), wv_ref[0, g].astype(BF16))
               + _dot(p_n.astype(BF16), _new_rows(vn_ref, lane0, D_HEAD).astype(BF16))) / jnp.maximum(den, TINY)
        for r in range(NSA_GROUP):
            hd = g * NSA_GROUP + r
            rs = slice(r * ts, (r + 1) * ts)
            part_ref[0, :, hd * D_HEAD:(hd + 1) * D_HEAD] = (
                gates[:, 3 * hd:3 * hd + 1] * o_c[rs] + gates[:, 3 * hd + 2:3 * hd + 3] * o_w[rs])


def _nsa_s(q3, kcmp, vcmp, wk_t, wv_t, kn_w, vn_w, gate3, ts):
    batch = q3.shape[0]
    nc_pad = kcmp.shape[1]
    wlen = wk_t.shape[3]
    ov = _overlap_matrix(nc_pad, LANES)
    b3 = lambda b: (b, 0, 0)
    b4 = lambda b: (b, 0, 0, 0)
    return pl.pallas_call(
        functools.partial(_nsa_s_kernel, ts=ts),
        grid=(batch,),
        in_specs=[pl.BlockSpec((1, ts, NSA_WIDTH), b3),
                  pl.BlockSpec((1, nc_pad, NSA_KV_WIDTH), b3), pl.BlockSpec((1, nc_pad, NSA_KV_WIDTH), b3),
                  pl.BlockSpec((1, NSA_KV_HEADS, D_HEAD, wlen), b4), pl.BlockSpec((1, NSA_KV_HEADS, D_HEAD, wlen), b4),
                  pl.BlockSpec((1, ts, NSA_KV_WIDTH), lambda b: (b, 0, 2)),
                  pl.BlockSpec((1, ts, NSA_KV_WIDTH), lambda b: (b, 0, 2)),
                  pl.BlockSpec((1, ts, LANES), b3),
                  pl.BlockSpec((nc_pad, LANES), lambda b: (0, 0))],
        out_specs=[pl.BlockSpec((1, ts, NSA_WIDTH), b3), pl.BlockSpec((1, NSA_KV_HEADS, ts, LANES), b4)],
        out_shape=[jax.ShapeDtypeStruct((batch, ts, NSA_WIDTH), F32),
                   jax.ShapeDtypeStruct((batch, NSA_KV_HEADS, ts, LANES), F32)],
        compiler_params=_params("arbitrary"),
        name="nsa_decode_cmp_win",
    )(q3, kcmp, vcmp, wk_t, wv_t, kn_w, vn_w, gate3, ov)


def _slc_s_kernel(pt_ref, q_ref, bias_ref, hot_ref, k_hbm, v_hbm, kn_ref, vn_ref, gate_ref, part_ref, o_ref,
                  kbuf, vbuf, ksem, vsem, m_scr, l_scr, acc_scr, *, ts):
    per = NSA_PAGES_PER_STEP
    s = pl.program_id(1)
    slot = _page_ring_step(pt_ref, [(k_hbm, kbuf, ksem), (v_hbm, vbuf, vsem)], per)
    rows = NSA_GROUP * ts
    zero = jnp.zeros((rows, D_HEAD), BF16)
    q2 = jnp.concatenate([jnp.concatenate([_stack_group_q(q_ref, 0), zero], axis=1),
                          jnp.concatenate([zero, _stack_group_q(q_ref, 1)], axis=1)], axis=0)
    in_g0 = _iota((NSA_KV_HEADS * rows, D_HEAD), 0) < rows

    def own_group(full):
        return jnp.where(in_g0, full[:, :D_HEAD], full[:, D_HEAD:])

    @pl.when(s == 0)
    def _():
        valid_n = _iota((NSA_KV_HEADS * rows, LANES), 1) <= (_iota((NSA_KV_HEADS * rows, 1), 0) % ts)
        sc = jnp.where(valid_n, _dot_nt(q2, _new_rows(kn_ref).astype(BF16)), NEG_INF)
        m = jnp.max(sc, axis=-1, keepdims=True)
        p = jnp.exp(sc - m)
        m_scr[...] = m
        l_scr[...] = jnp.sum(p, axis=-1, keepdims=True)
        acc_scr[...] = own_group(_dot(p.astype(BF16), _new_rows(vn_ref).astype(BF16)))

    kt = jnp.concatenate([kbuf[slot, u].astype(BF16) for u in range(per)], axis=1)
    vt = jnp.concatenate([vbuf[slot, u].astype(BF16) for u in range(per)], axis=1)
    nb = per * PAGE // SLC_BLOCK
    bias = jnp.concatenate([bias_ref[0, g] for g in range(NSA_KV_HEADS) for _ in range(NSA_GROUP)], axis=0)
    cols = bias[:, :nb]
    for k in range(1, bias.shape[1] // nb):
        cols = jnp.where(s == k, bias[:, k * nb:(k + 1) * nb], cols)
    sc = _dot(jnp.concatenate([q2, cols.astype(BF16)], axis=1),
              jnp.concatenate([kt, hot_ref[...]], axis=0))
    m_old = m_scr[...]
    m_new = jnp.maximum(m_old, jnp.max(sc, axis=-1, keepdims=True))
    pf = jnp.exp(sc - m_new)
    alpha = jnp.exp(m_old - m_new)
    l_scr[...] = alpha * l_scr[...] + jnp.sum(pf, axis=-1, keepdims=True)
    acc_scr[...] = alpha * acc_scr[...] + own_group(_dot_nt(pf.astype(BF16), vt))
    m_scr[...] = m_new

    @pl.when(s == pl.num_programs(1) - 1)
    def _():
        gates = _sigmoid(gate_ref[0])
        o_s = acc_scr[...] / jnp.maximum(l_scr[...], TINY)
        for hd in range(NSA_HEADS):
            cols = slice(hd * D_HEAD, (hd + 1) * D_HEAD)
            o_ref[0, :, cols] = part_ref[0, :, cols] + gates[:, 3 * hd + 1:3 * hd + 2] * o_s[hd * ts:(hd + 1) * ts]


def _slc_s(pt_flat, q3, sel_bias, k_pages, v_pages, kn_s, vn_s, gate3, part, n_pages, ts):
    batch = q3.shape[0]
    per = NSA_PAGES_PER_STEP
    rows = NSA_GROUP * ts
    nb = per * PAGE // SLC_BLOCK
    hot = (jnp.arange(nb)[:, None] == jnp.arange(per * PAGE)[None, :] // SLC_BLOCK).astype(BF16)
    b3 = lambda b, s, pt: (b, 0, 0)
    return pl.pallas_call(
        functools.partial(_slc_s_kernel, ts=ts),
        grid_spec=pltpu.PrefetchScalarGridSpec(
            num_scalar_prefetch=1,
            grid=(batch, n_pages // per),
            in_specs=[pl.BlockSpec((1, ts, NSA_WIDTH), b3),
                      pl.BlockSpec((1, NSA_KV_HEADS, ts, LANES), lambda b, s, pt: (b, 0, 0, 0)),
                      pl.BlockSpec((nb, per * PAGE), lambda b, s, pt: (0, 0)),
                      pl.BlockSpec(memory_space=pl.ANY), pl.BlockSpec(memory_space=pl.ANY),
                      pl.BlockSpec((1, ts, NSA_KV_WIDTH), lambda b, s, pt: (b, 0, 1)),
                      pl.BlockSpec((1, ts, NSA_KV_WIDTH), lambda b, s, pt: (b, 0, 1)),
                      pl.BlockSpec((1, ts, LANES), b3), pl.BlockSpec((1, ts, NSA_WIDTH), b3)],
            out_specs=pl.BlockSpec((1, ts, NSA_WIDTH), b3),
            scratch_shapes=[pltpu.VMEM((SLC_RING_DEPTH, per, NSA_KV_WIDTH, PAGE), F32),
                            pltpu.VMEM((SLC_RING_DEPTH, per, NSA_KV_WIDTH, PAGE), F32),
                            pltpu.SemaphoreType.DMA((SLC_RING_DEPTH,)), pltpu.SemaphoreType.DMA((SLC_RING_DEPTH,)),
                            pltpu.VMEM((NSA_KV_HEADS * rows, 1), F32), pltpu.VMEM((NSA_KV_HEADS * rows, 1), F32),
                            pltpu.VMEM((NSA_KV_HEADS * rows, D_HEAD), F32)]),
        out_shape=jax.ShapeDtypeStruct((batch, ts, NSA_WIDTH), F32),
        compiler_params=_params("arbitrary", "arbitrary"),
        name="nsa_decode_slc",
    )(pt_flat, q3, sel_bias, hot, k_pages, v_pages, kn_s, vn_s, gate3, part)


def _prep_weights(w_ada, b_ada, norm_mix_g, w_in, pe_cmp_k, w_cmp_k1, w_cmp_k2, pe_cmp_v, w_cmp_v1, w_cmp_v2,
                  w_br_a, w_br_b, w_out, norm_ffn_g, w_router_grp, w_router_exp, w_expert_in, w_expert_out,
                  norm_final_g):
    wr = jnp.concatenate([w_router_grp, w_router_exp], axis=1)
    wr = jnp.pad(wr, ((0, 0), (0, LANES - wr.shape[1])))
    wr_hi = wr.astype(BF16)
    wr = jnp.stack([wr_hi, (wr - wr_hi.astype(F32)).astype(BF16)])
    w_in_main, w_in_merge = _reorder_w_in(w_in)
    return dict(
        w_ada=w_ada, b_ada=b_ada.reshape(1, -1), g_mix=norm_mix_g.reshape(1, -1),
        w_in=w_in_main, w_merge=w_in_merge,
        cmp_k=_compress_weights(pe_cmp_k, w_cmp_k1, w_cmp_k2),
        cmp_v=_compress_weights(pe_cmp_v, w_cmp_v1, w_cmp_v2),
        w_br_a=w_br_a.astype(BF16), w_br_b=w_br_b.astype(BF16), w_out=w_out.astype(BF16),
        g_ffn=norm_ffn_g.reshape(1, -1), wr=wr,
        w_ein=w_expert_in.astype(BF16), w_eout=w_expert_out.astype(BF16),
        g_final=norm_final_g.reshape(1, -1))


def _prompt_layer(x, mod, w, batch, seq):
    sh1, sc1, g1, sh2, sc2, g2 = mod
    cos, sin = _rope_tables(jnp.arange(seq, dtype=jnp.int32))
    qk_a, _, q_b, k_n, v_n, gate, kt_a, vt_a, kt_n, vt_n = _inproj(
        x, sc1, sh1, w['g_mix'], cos, sin, w['w_in'], 256, transposed=True)
    o_a = _moba_p(qk_a, vt_a, batch, seq)
    kcmp = _compress_p(k_n[:, :NSA_KV_WIDTH], w['cmp_k'], batch, seq)
    vcmp = _compress_p(v_n[:, :NSA_KV_WIDTH], w['cmp_v'], batch, seq)
    o_b = _nsa_p(q_b, k_n, vt_n, kcmp, vcmp, gate, batch, seq)
    x1 = _merge(o_a, o_b, x, sc1, sh1, g1, w['g_mix'], w['w_merge'], w['w_br_a'], w['w_br_b'], w['w_out'], 256)
    y = _moe(x1, sc2, sh2, g2, w['g_ffn'], w['g_final'], w['wr'], w['w_ein'], w['w_eout'], 512)
    return y, (kt_a, vt_a, kt_n, vt_n)


def _sample_layer(x, mod, w, caches, win_state, page_table, batch, ts):
    sh1, sc1, g1, sh2, sc2, g2 = mod
    moba_k, moba_v, cmp_k, cmp_v, slc_k, slc_v = caches
    win_k, win_v = win_state
    n_pages = page_table.shape[1]
    assert moba_k.shape[1] == PAGE and win_k.shape[1] == WINDOW and ts <= LANES
    assert n_pages * PAGE == LANES * SLC_BLOCK and n_pages * PAGE // MOBA_BLOCK <= LANES
    n = batch * ts
    pos = n_pages * PAGE + (jnp.arange(n, dtype=jnp.int32) % ts)
    cos, sin = _rope_tables(pos)
    qk_a, v_a, q_b, k_n, v_n, gate = _inproj(x, sc1, sh1, w['g_mix'], cos, sin, w['w_in'], n)
    pt_flat = page_table.reshape(-1)
    kv = NSA_KV_WIDTH
    qf = _block_diag_q(qk_a[:, :MOBA_WIDTH], batch, ts, MOBA_HEADS)
    qbd = (qf * SCALE).astype(BF16)
    m, l, ks, o = _moba_s_pass(pt_flat, qbd, _page_view(moba_k), _page_view(moba_v), batch, n_pages)
    o_a = _moba_s_combine(m, l, ks, o, qf, qbd, qk_a[:, MOBA_WIDTH:].reshape(batch, ts, MOBA_WIDTH),
                          v_a.reshape(batch, ts, MOBA_WIDTH), n_pages, ts)
    o_a = o_a.reshape(batch, MOBA_HEADS, ts, D_HEAD).transpose(0, 2, 1, 3).reshape(n, MOBA_WIDTH)
    kcmp, vcmp = _compress_s(pt_flat, _page_view(cmp_k), _page_view(cmp_v), w['cmp_k'], w['cmp_v'], batch, n_pages)
    q3 = q_b.reshape(batch, ts, NSA_WIDTH)
    gate3 = gate.reshape(batch, ts, LANES)
    kn3 = k_n.reshape(batch, ts, 3 * kv)
    vn3 = v_n.reshape(batch, ts, 3 * kv)
    part, sel = _nsa_s(q3, kcmp, vcmp, jnp.transpose(win_k, (0, 2, 3, 1)), jnp.transpose(win_v, (0, 2, 3, 1)),
                       kn3, vn3, gate3, ts)
    o_b = _slc_s(pt_flat, q3, sel, _page_view(slc_k), _page_view(slc_v), kn3, vn3, gate3, part,
                 n_pages, ts).reshape(n, NSA_WIDTH)
    x1 = _merge(o_a, o_b, x, sc1, sh1, g1, w['g_mix'], w['w_merge'], w['w_br_a'], w['w_br_b'], w['w_out'], n)
    y = _moe(x1, sc2, sh2, g2, w['g_ffn'], w['g_final'], w['wr'], w['w_ein'], w['w_eout'], n)
    return y, (qk_a, v_a, k_n, v_n)


def kernel(x_prompt, x_sample, c_prompt, c_sample, cache_moba_k, cache_moba_v, cache_nsa_cmp_k, cache_nsa_cmp_v,
           cache_nsa_slc_k, cache_nsa_slc_v, state_nsa_win_k, state_nsa_win_v, page_table, w_ada, b_ada, norm_mix_g,
           w_in, pe_cmp_k, w_cmp_k1, w_cmp_k2, pe_cmp_v, w_cmp_v1, w_cmp_v2, w_br_a, w_br_b, w_out, norm_ffn_g,
           w_router_grp, w_router_exp, w_expert_in, w_expert_out, norm_final_g):
    bp, tp, _ = x_prompt.shape
    bs, ts, _ = x_sample.shape
    w = _prep_weights(w_ada[0], b_ada[0], norm_mix_g[0], w_in[0], pe_cmp_k[0], w_cmp_k1[0], w_cmp_k2[0], pe_cmp_v[0],
                      w_cmp_v1[0], w_cmp_v2[0], w_br_a[0], w_br_b[0], w_out[0], norm_ffn_g[0], w_router_grp[0],
                      w_router_exp[0], w_expert_in[0], w_expert_out[0], norm_final_g)
    mod = _ada(jnp.concatenate([c_prompt, c_sample], axis=0), w['w_ada'], w['b_ada'])
    mod_p = [m.reshape(bp, 1, D_MODEL) for m in jnp.split(mod[:bp], 6, axis=-1)]
    y_p, new_p = _prompt_layer(x_prompt.reshape(bp * tp, D_MODEL), mod_p, w, bp, tp)

    mod_s = [jnp.repeat(m, ts, axis=0).reshape(1, bs * ts, D_MODEL) for m in jnp.split(mod[bp:], 6, axis=-1)]
    caches = (cache_moba_k[0], cache_moba_v[0], cache_nsa_cmp_k[0], cache_nsa_cmp_v[0], cache_nsa_slc_k[0],
              cache_nsa_slc_v[0])
    y_s, new_s = _sample_layer(x_sample.reshape(bs * ts, D_MODEL), mod_s, w, caches,
                               (state_nsa_win_k[0], state_nsa_win_v[0]), page_table, bs, ts)

    kv = NSA_KV_WIDTH

    def new_rows(new, b, t):
        qk, v, k_n, v_n = new
        rows = lambda a, heads: a.reshape(1, b, t, heads, D_HEAD)
        return (rows(qk[:, MOBA_WIDTH:], MOBA_HEADS), rows(v, MOBA_HEADS),
                rows(k_n[:, :kv], NSA_KV_HEADS), rows(v_n[:, :kv], NSA_KV_HEADS),
                rows(k_n[:, kv:2 * kv], NSA_KV_HEADS), rows(v_n[:, kv:2 * kv], NSA_KV_HEADS),
                rows(k_n[:, 2 * kv:], NSA_KV_HEADS), rows(v_n[:, 2 * kv:], NSA_KV_HEADS))

    def new_rows_t(new, b, t):
        kt_a, vt_a, kt_n, vt_n = new
        rows = lambda a: jnp.transpose(a.reshape(1, b, a.shape[1] // D_HEAD, D_HEAD, a.shape[2]), (0, 1, 4, 2, 3))
        return (rows(kt_a), rows(vt_a), rows(kt_n[:, :kv]), rows(vt_n[:, :kv]),
                rows(kt_n[:, kv:2 * kv]), rows(vt_n[:, kv:2 * kv]),
                rows(kt_n[:, 2 * kv:, t - wb:]), rows(vt_n[:, 2 * kv:, t - wb:]))

    wb = state_nsa_win_k.shape[2]
    outs_p = new_rows_t(new_p, bp, tp)
    outs_s = new_rows(new_s, bs, ts)
    win_k = jnp.concatenate([state_nsa_win_k, outs_s[6]], axis=2)[:, :, ts:]
    win_v = jnp.concatenate([state_nsa_win_v, outs_s[7]], axis=2)[:, :, ts:]
    outs_s = outs_s[:6] + (win_k, win_v)
    return (y_p.reshape(bp, tp, D_MODEL), y_s.reshape(bs, ts, D_MODEL)) + outs_p + outs_s
```

```python
import functools

import jax
import jax.numpy as jnp
from jax import lax
from jax.experimental import pallas as pl
from jax.experimental.pallas import tpu as pltpu

D_MODEL = 1024
D_HEAD = 64
HALF = D_HEAD // 2
MOBA_HEADS = 8
MOBA_BLOCK = 256
MOBA_TOPK = 3
NSA_HEADS = 8
NSA_KV_HEADS = 2
NSA_GROUP = NSA_HEADS // NSA_KV_HEADS
CMP_LEN = 32
CMP_STRIDE = 16
CMP_HIDDEN = 128
SLC_BLOCK = 64
SLC_TOPN = 16
WINDOW = 512
N_GROUPS = 4
EXPERTS_PER_GROUP = 8
N_EXPERTS = N_GROUPS * EXPERTS_PER_GROUP
D_EXPERT = 256
ROPE_THETA = 10000.0
EPS = 1e-6
NEG_INF = -1e30
BIG = 1e30
TINY = 1e-30
MOBA_WIDTH = MOBA_HEADS * D_HEAD
NSA_WIDTH = NSA_HEADS * D_HEAD
NSA_KV_WIDTH = NSA_KV_HEADS * D_HEAD
SCALE = D_HEAD ** -0.5

LANES = 128
VMEM_LIMIT = 48 * 1024 * 1024

F32 = jnp.float32
BF16 = jnp.bfloat16
HIGHEST = lax.Precision.HIGHEST


def _params(*sem):
    return pltpu.CompilerParams(dimension_semantics=sem, vmem_limit_bytes=VMEM_LIMIT)


def _dot(a, b):
    return jnp.dot(a, b, preferred_element_type=F32)


def _dot_nt(a, b, precision=None):
    return lax.dot_general(a, b, (((1,), (1,)), ((), ())), precision=precision,
                           preferred_element_type=F32)


def _sigmoid(x):
    return 1.0 / (1.0 + jnp.exp(-x))


def _iota(shape, dim):
    return lax.broadcasted_iota(jnp.int32, shape, dim)


def _ada_kernel(c_ref, w_ref, b_ref, o_ref):
    c = c_ref[...]
    s = c * _sigmoid(c)
    o_ref[...] = _dot(s.astype(BF16), w_ref[...].astype(BF16)) + b_ref[...]


def _ada(c, w, b):
    n = c.shape[0]
    tn = 1024
    return pl.pallas_call(
        _ada_kernel,
        grid=(w.shape[1] // tn,),
        in_specs=[pl.BlockSpec((n, D_MODEL), lambda j: (0, 0)),
                  pl.BlockSpec((D_MODEL, tn), lambda j: (0, j)),
                  pl.BlockSpec((1, tn), lambda j: (0, j))],
        out_specs=pl.BlockSpec((n, tn), lambda j: (0, j)),
        out_shape=jax.ShapeDtypeStruct((n, w.shape[1]), F32),
        compiler_params=_params("arbitrary"),
        name="ada",
    )(c, w, b)


_IN_GROUPS = ((2 * MOBA_WIDTH, True),
              (MOBA_WIDTH, False),
              (NSA_WIDTH, True),
              (3 * NSA_KV_WIDTH, True),
              (3 * NSA_KV_WIDTH, False),
              (LANES, False))
_IN_COLS_PAD = sum(w for w, _ in _IN_GROUPS)


def _norm_mod(x, g, sc, sh):
    ms = jnp.mean(x * x, axis=-1, keepdims=True)
    y = x * lax.rsqrt(ms + EPS) * g
    return y * (1.0 + sc) + sh


def _inproj_kernel(x_ref, sc_ref, sh_ref, g_ref, cos_ref, sin_ref, w_ref, *out_refs):
    h = _norm_mod(x_ref[...], g_ref[...], sc_ref[0], sh_ref[0]).astype(BF16)
    cos = cos_ref[...]
    sin = sin_ref[...]
    first_half = (_iota(cos.shape, 1) & (D_HEAD - 1)) < HALF

    def rope(y):
        rot = jnp.where(first_half, pltpu.roll(y, LANES - HALF, 1), pltpu.roll(y, HALF, 1))
        return y * cos + rot * sin

    n_groups = len(_IN_GROUPS)
    t_refs = dict(zip(_IN_T_GROUPS, out_refs[n_groups:]))
    col = 0
    for gi, (out_ref, (width, rotary)) in enumerate(zip(out_refs, _IN_GROUPS)):
        chunk = min(width, 512)
        for c in range(0, width, chunk):
            cw = min(chunk, width - c)
            y = _dot(h, w_ref[:, col + c:col + c + cw])
            for s in range(0, cw, LANES):
                piece = y[:, s:s + LANES]
                piece = rope(piece) if rotary else piece
                if c + s < out_ref.shape[1]:
                    out_ref[:, c + s:c + s + LANES] = piece
                if gi in t_refs and c + s >= _IN_T_GROUPS[gi]:
                    t0 = c + s - _IN_T_GROUPS[gi]
                    t_refs[gi][0, t0:t0 + LANES, :] = piece.T
        col += width


_IN_T_GROUPS = {0: MOBA_WIDTH, 1: 0, 3: 0, 4: 0}
_IN_T_ROW_COLS = {1: LANES, 4: NSA_KV_WIDTH}


def _inproj(x, sc, sh, g, cos, sin, w, tm, transposed=False):
    n = x.shape[0]
    nb, r, _ = sc.shape
    tiles_per_b = (n // nb) // tm
    tab_tiles = cos.shape[0] // tm
    row = lambda i: (i, 0)
    mod = lambda i: (i // tiles_per_b, 0, 0)
    tab = lambda i: (i % tab_tiles, 0)
    widths = [_IN_T_ROW_COLS.get(gi, wd) if transposed else wd for gi, (wd, _) in enumerate(_IN_GROUPS)]
    out_specs = [pl.BlockSpec((tm, wd), row) for wd in widths]
    out_shape = [jax.ShapeDtypeStruct((n, wd), F32) for wd in widths]
    if transposed:
        for gi, first in _IN_T_GROUPS.items():
            cols = _IN_GROUPS[gi][0] - first
            out_specs.append(pl.BlockSpec((1, cols, tm), lambda i: (i // tiles_per_b, 0, i % tiles_per_b)))
            out_shape.append(jax.ShapeDtypeStruct((nb, cols, n // nb), F32))
    return pl.pallas_call(
        _inproj_kernel,
        grid=(n // tm,),
        in_specs=[pl.BlockSpec((tm, D_MODEL), row),
                  pl.BlockSpec((1, r, D_MODEL), mod),
                  pl.BlockSpec((1, r, D_MODEL), mod),
                  pl.BlockSpec((1, D_MODEL), lambda i: (0, 0)),
                  pl.BlockSpec((tm, LANES), tab),
                  pl.BlockSpec((tm, LANES), tab),
                  pl.BlockSpec((D_MODEL, _IN_COLS_PAD), lambda i: (0, 0))],
        out_specs=out_specs,
        out_shape=out_shape,
        compiler_params=_params("arbitrary"),
        name="inproj",
    )(x, sc, sh, g, cos, sin, w)


def _reorder_w_in(w_in):
    kv0 = 3 * MOBA_WIDTH + NSA_WIDTH
    kvs = [w_in[:, kv0 + i * NSA_KV_WIDTH:kv0 + (i + 1) * NSA_KV_WIDTH] for i in range(6)]
    g0 = kv0 + 6 * NSA_KV_WIDTH
    ng = 3 * NSA_HEADS
    gate = jnp.pad(w_in[:, g0:g0 + ng], ((0, 0), (0, LANES - ng)))
    parts = [w_in[:, :kv0], kvs[0], kvs[2], kvs[4], kvs[1], kvs[3], kvs[5], gate]
    return jnp.concatenate(parts, axis=1).astype(BF16), w_in[:, g0 + ng:].astype(BF16)


def _rope_tables(pos):
    inv = ROPE_THETA ** (-jnp.arange(HALF, dtype=F32) / HALF)
    ang = pos.astype(F32)[:, None] * inv[None, :]
    cos = jnp.cos(ang)
    sin = jnp.sin(ang)
    cos = jnp.concatenate([cos, cos, cos, cos], axis=1)
    sin = jnp.concatenate([-sin, sin, -sin, sin], axis=1)
    return cos, sin


def _rank_select(score, n_cols, n_keep):
    lane = _iota(score.shape, 1)
    rank = jnp.zeros(score.shape, jnp.int32)
    for jp in range(n_cols):
        col = score[:, jp:jp + 1]
        beats = (col > score) | ((col == score) & (lane > jp))
        rank = rank + beats.astype(jnp.int32)
    return rank < n_keep


def _rank_select_t(score, n_rows, n_keep):
    row = _iota(score.shape, 0)
    rank = jnp.zeros(score.shape, jnp.int32)
    for jp in range(n_rows):
        r = score[jp:jp + 1, :]
        beats = (r > score) | ((r == score) & (row > jp))
        rank = rank + beats.astype(jnp.int32)
    return rank < n_keep


SUBLANES = 8


def _fold_rows(x, op):
    return op(x.reshape(x.shape[0] // SUBLANES, SUBLANES, x.shape[1]), axis=0)


def _attend_t(score_fn, vt_fn, lo, hi, width, n_chains, cache_ref):
    def chunk_loop(step_fn, carry):
        n = hi - lo
        carry = lax.fori_loop(0, n // 2, lambda i, cr: step_fn(lo + 2 * i + 1, step_fn(lo + 2 * i, cr)), carry)
        return lax.cond(n % 2 == 1, lambda cr: step_fn(hi - 1, cr), lambda cr: cr, carry)

    def max_step(c, own, m):
        out = []
        for ch, (mi, s) in enumerate(zip(m, score_fn(c, own))):
            cache_ref[ch, c - lo] = s
            out.append(jnp.maximum(mi, _fold_rows(s, jnp.max)))
        return tuple(out)

    m = max_step(hi, True, tuple(jnp.full((SUBLANES, width), NEG_INF, F32) for _ in range(n_chains)))
    m = chunk_loop(lambda c, mm: max_step(c, False, mm), m)
    shift = [jnp.max(mi, axis=0, keepdims=True) for mi in m]

    def acc_step(c, carry):
        out = []
        for ch, ((l, acc), vt) in enumerate(zip(carry, vt_fn(c))):
            p = jnp.exp(cache_ref[ch, c - lo] - shift[ch])
            out.append((l + _fold_rows(p, jnp.sum), acc + _dot(vt, p.astype(BF16))))
        return tuple(out)

    zero = tuple((jnp.zeros((SUBLANES, width), F32), jnp.zeros((D_HEAD, width), F32)) for _ in range(n_chains))
    carry = acc_step(hi, zero)
    carry = chunk_loop(acc_step, carry)
    return [acc / jnp.maximum(jnp.sum(l, axis=0, keepdims=True), TINY) for l, acc in carry]


def _key_aug(n_keys, block, n_blocks):
    lane = _iota((n_keys, D_HEAD), 1)
    blk = _iota((n_keys, D_HEAD), 0) // block
    return jnp.where((lane < n_blocks) & (lane == blk), 1.0, 0.0).astype(BF16)


def _query_aug(qt, bias):
    pad = jnp.zeros((D_HEAD - bias.shape[0], qt.shape[1]), F32)
    return jnp.concatenate([qt, bias, pad], axis=0).astype(BF16)


MOBA_STEP_WIDTH = 512


def _moba_p_kernel(q_ref, k_ref, v_ref, o_ref, ka_scr, vt_scr, km_scr, s_scr):
    qi = pl.program_id(2)
    tq = tk = MOBA_BLOCK
    seq = k_ref.shape[1]
    nblk = seq // MOBA_BLOCK
    width = q_ref.shape[2]
    nh = width // D_HEAD

    @pl.when(qi == 0)
    def _():
        kf = k_ref[0]
        km_scr[...] = jnp.mean(kf.reshape(nblk, MOBA_BLOCK, width), axis=1)
        aug = _key_aug(seq, MOBA_BLOCK, nblk)
        for hh in range(nh):
            ka_scr[hh] = jnp.concatenate([kf[:, hh * D_HEAD:(hh + 1) * D_HEAD].astype(BF16), aug], axis=1)
        for j in range(nblk):
            vt_scr[j] = v_ref[0, :, j * tk:(j + 1) * tk].astype(BF16)

    q2t = q_ref[0].T
    km = km_scr[...]
    klane = _iota(km.shape, 1)
    blk = _iota((nblk, tq), 0)
    causal_t = _iota((tk, tq), 0) <= _iota((tk, tq), 1)
    qts, biases = [], []
    for hh in range(nh):
        kmh = jnp.where((klane >= hh * D_HEAD) & (klane < (hh + 1) * D_HEAD), km, 0.0)
        gate = jnp.dot(kmh, q2t, precision=HIGHEST, preferred_element_type=F32)
        gate = jnp.where(blk < qi, gate, NEG_INF)
        keep = _rank_select_t(gate, nblk, MOBA_TOPK) & (blk < qi)
        biases.append(jnp.where(keep, 0.0, NEG_INF))
        qts.append(q2t[hh * D_HEAD:(hh + 1) * D_HEAD, :] * SCALE)

    q_own = [_query_aug(qts[hh], jnp.zeros_like(biases[hh])) for hh in range(nh)]
    q_past = [_query_aug(qts[hh], biases[hh]) for hh in range(nh)]

    def scores(c, own):
        out = []
        for hh in range(nh):
            kj = ka_scr[hh, pl.ds(pl.multiple_of(c * tk, tk), tk), :]
            s = _dot(kj, q_own[hh] if own else q_past[hh])
            out.append(jnp.where(causal_t, s, NEG_INF) if own else s)
        return out

    def values(c):
        return [vt_scr[c, hh * D_HEAD:(hh + 1) * D_HEAD, :] for hh in range(nh)]

    o_ref[0] = jnp.concatenate(_attend_t(scores, values, 0, qi, tq, nh, s_scr), axis=0).T


def _moba_p(qk, vt, batch, seq):
    qk3 = qk.reshape(batch, seq, 2 * MOBA_WIDTH)
    sw = MOBA_STEP_WIDTH
    pairs = MOBA_WIDTH // sw
    nblk = seq // MOBA_BLOCK
    out = pl.pallas_call(
        _moba_p_kernel,
        grid=(batch, pairs, nblk),
        in_specs=[pl.BlockSpec((1, MOBA_BLOCK, sw), lambda b, h, i: (b, i, h)),
                  pl.BlockSpec((1, seq, sw), lambda b, h, i: (b, 0, pairs + h)),
                  pl.BlockSpec((1, sw, seq), lambda b, h, i: (b, h, 0))],
        out_specs=pl.BlockSpec((1, MOBA_BLOCK, sw), lambda b, h, i: (b, i, h)),
        out_shape=jax.ShapeDtypeStruct((batch, seq, MOBA_WIDTH), F32),
        scratch_shapes=[pltpu.VMEM((sw // D_HEAD, seq, 2 * D_HEAD), BF16), pltpu.VMEM((nblk, sw, MOBA_BLOCK), BF16),
                        pltpu.VMEM((nblk, sw), F32),
                        pltpu.VMEM((sw // D_HEAD, nblk, MOBA_BLOCK, MOBA_BLOCK), F32)],
        compiler_params=_params("arbitrary", "arbitrary", "arbitrary"),
        name="moba_prompt",
    )(qk3, qk3, vt)
    return out.reshape(batch * seq, MOBA_WIDTH)


def _gelu(x):
    return 0.5 * x * (1.0 + jnp.tanh(0.7978845608028654 * (x + 0.044715 * x * x * x)))


def _compress_p_kernel(seg_ref, pea_ref, peb_ref, wa_ref, wb_ref, w2_ref, o_ref):
    seg = seg_ref[0]
    a = _dot((seg + pea_ref[...]).astype(BF16), wa_ref[...])
    b = _dot((seg + peb_ref[...]).astype(BF16), wb_ref[...])
    nseg = seg.shape[0]
    hid = a + pltpu.roll(b, nseg - 1, 0)
    o_ref[0] = _dot(_gelu(hid).astype(BF16), w2_ref[...])


def _compress_weights(pe, w1, w2):
    g = NSA_KV_HEADS
    eye = jnp.eye(g, dtype=F32)
    w1r = w1.reshape(CMP_LEN, D_HEAD, CMP_HIDDEN)

    def half(lo):
        w = w1r[lo:lo + CMP_STRIDE]
        wbd = jnp.einsum('ldf,gh->lgdhf', w, eye)
        p = jnp.broadcast_to(pe[lo:lo + CMP_STRIDE, None, :], (CMP_STRIDE, g, D_HEAD))
        return wbd.reshape(CMP_STRIDE, g * D_HEAD, g * CMP_HIDDEN).astype(BF16), p.reshape(CMP_STRIDE, g * D_HEAD)

    wa, pea = half(0)
    wb, peb = half(CMP_STRIDE)
    w2bd = jnp.einsum('fd,gh->gfhd', w2, eye).reshape(g * CMP_HIDDEN, g * D_HEAD).astype(BF16)
    return wa, wb, pea, peb, w2bd


def _compress_p(rows, cw, batch, seq):
    wa, wb, pea, peb, w2bd = cw
    nseg = seq // CMP_STRIDE
    width = CMP_STRIDE * NSA_KV_WIDTH
    seg = rows.reshape(batch, nseg, width)
    full = lambda b: (0, 0)
    return pl.pallas_call(
        _compress_p_kernel,
        grid=(batch,),
        in_specs=[pl.BlockSpec((1, nseg, width), lambda b: (b, 0, 0)),
                  pl.BlockSpec((1, width), full), pl.BlockSpec((1, width), full),
                  pl.BlockSpec((width, NSA_KV_HEADS * CMP_HIDDEN), full),
                  pl.BlockSpec((width, NSA_KV_HEADS * CMP_HIDDEN), full),
                  pl.BlockSpec((NSA_KV_HEADS * CMP_HIDDEN, NSA_KV_WIDTH), full)],
        out_specs=pl.BlockSpec((1, nseg, NSA_KV_WIDTH), lambda b: (b, 0, 0)),
        out_shape=jax.ShapeDtypeStruct((batch, nseg, NSA_KV_WIDTH), F32),
        compiler_params=_params("arbitrary"),
        name="compress_prompt",
    )(seg, pea.reshape(1, width), peb.reshape(1, width), wa.reshape(width, -1), wb.reshape(width, -1), w2bd)


NSA_TQ = 256
NSA_TK = 256


def _overlap_matrix(nc_pad, nslc_pad):
    cs = jnp.arange(nc_pad)[:, None] * CMP_STRIDE
    ss = jnp.arange(nslc_pad)[None, :] * SLC_BLOCK
    return ((cs < ss + SLC_BLOCK) & (cs + CMP_LEN > ss)).astype(F32)


def _nsa_p_kernel(q_ref, kc_ref, vc_ref, ks_ref, vs_ref, kw_ref, vw_ref, gate_ref, ovt_ref, o_ref,
                  ksa_scr, kwb_scr, vst_scr, vwt_scr, ss_scr, sw_scr):
    tq, tk = NSA_TQ, NSA_TK
    t = pl.program_id(1)
    q0 = t * tq
    seq = ks_ref.shape[1]
    nc_pad = kc_ref.shape[1]
    nslc = seq // SLC_BLOCK
    width = NSA_GROUP * tq

    @pl.when(t == 0)
    def _():
        aug_s = _key_aug(seq, SLC_BLOCK, nslc)
        for g in range(NSA_KV_HEADS):
            rows = slice(g * D_HEAD, (g + 1) * D_HEAD)
            ksa_scr[g] = jnp.concatenate([ks_ref[0, :, rows].astype(BF16), aug_s], axis=1)
        kwb_scr[...] = kw_ref[0].astype(BF16)
        for j in range(seq // tk):
            vst_scr[j] = vs_ref[0, :, j * tk:(j + 1) * tk].astype(BF16)
            vwt_scr[j] = vw_ref[0, :, j * tk:(j + 1) * tk].astype(BF16)

    qt_all = q_ref[0].T
    gates_t = _sigmoid(gate_ref[0]).T
    kct = kc_ref[0].astype(BF16)
    vct = vc_ref[0].T.astype(BF16)
    pos = q0 + _iota((1, tq), 1)
    n_i = _iota((nc_pad, tq), 0)
    valid_c = (n_i < nc_pad - 1) & (n_i * CMP_STRIDE + (CMP_LEN - 1) <= pos)
    valid_c4 = jnp.concatenate([valid_c] * NSA_GROUP, axis=1)
    q_blk = pos >> 6
    jrow = _iota((nslc, tq), 0)
    krow = _iota((tk, tq), 0)
    cd = q0 // tk
    c_win = jnp.maximum(q0 - WINDOW, 0) // tk

    def tile(b):
        return jnp.concatenate([b] * NSA_GROUP, axis=1)

    def win_bias(c):
        dist = pos - (c * tk + krow)
        return jnp.where((dist >= 0) & (dist <= WINDOW), 0.0, NEG_INF)

    qts, o_cs, biases = [], [], []
    for g in range(NSA_KV_HEADS):
        rows = slice(g * D_HEAD, (g + 1) * D_HEAD)
        qt = jnp.concatenate(
            [qt_all[(g * NSA_GROUP + r) * D_HEAD:(g * NSA_GROUP + r + 1) * D_HEAD, :] * SCALE
             for r in range(NSA_GROUP)], axis=1)
        qts.append(qt)
        s = jnp.where(valid_c4, _dot(kct[:, rows], qt.astype(BF16)), NEG_INF)
        m = jnp.max(s, axis=0, keepdims=True)
        p = jnp.where(valid_c4, jnp.exp(s - m), 0.0)
        p = p / jnp.maximum(jnp.sum(p, axis=0, keepdims=True), TINY)
        o_cs.append(_dot(vct[rows, :], p.astype(BF16)))
        psum = p[:, :tq]
        for r in range(1, NSA_GROUP):
            psum = psum + p[:, r * tq:(r + 1) * tq]
        imp = jnp.dot(ovt_ref[...], psum, precision=HIGHEST, preferred_element_type=F32)
        imp = jnp.where(jrow == q_blk, BIG, jnp.where(jrow < q_blk, imp, NEG_INF))
        keep = _rank_select_t(imp, nslc, SLC_TOPN) & (jrow <= q_blk)
        biases.append(tile(jnp.where(keep, 0.0, NEG_INF)))

    causal_own = tile(cd * tk + krow <= pos)

    qa_slc = [_query_aug(qts[g], biases[g]) for g in range(NSA_KV_HEADS)]
    qb_win = [qts[g].astype(BF16) for g in range(NSA_KV_HEADS)]

    def slc_scores(c, own):
        out = []
        for g in range(NSA_KV_HEADS):
            s = _dot(ksa_scr[g, pl.ds(pl.multiple_of(c * tk, tk), tk), :], qa_slc[g])
            out.append(jnp.where(causal_own, s, NEG_INF) if own else s)
        return out

    def win_scores(c, own):
        wb = tile(win_bias(c))
        return [_dot(kwb_scr[pl.ds(pl.multiple_of(c * tk, tk), tk), g * D_HEAD:(g + 1) * D_HEAD], qb_win[g]) + wb
                for g in range(NSA_KV_HEADS)]

    def values(vt_scr):
        return lambda c: [vt_scr[c, g * D_HEAD:(g + 1) * D_HEAD, :] for g in range(NSA_KV_HEADS)]

    o_ss = _attend_t(slc_scores, values(vst_scr), 0, cd, width, NSA_KV_HEADS, ss_scr)
    o_ws = _attend_t(win_scores, values(vwt_scr), c_win, cd, width, NSA_KV_HEADS, sw_scr)
    outs = []
    for g in range(NSA_KV_HEADS):
        o_c, o_s, o_w = o_cs[g], o_ss[g], o_ws[g]
        for r in range(NSA_GROUP):
            hd = g * NSA_GROUP + r
            cs = slice(r * tq, (r + 1) * tq)
            outs.append(gates_t[3 * hd:3 * hd + 1, :] * o_c[:, cs] + gates_t[3 * hd + 1:3 * hd + 2, :] * o_s[:, cs]
                        + gates_t[3 * hd + 2:3 * hd + 3, :] * o_w[:, cs])
    o_ref[0] = jnp.concatenate(outs, axis=0).T


def _nsa_p(q, kn, vnt, kcmp, vcmp, gate, batch, seq):
    q3 = q.reshape(batch, seq, NSA_WIDTH)
    kn3 = kn.reshape(batch, seq, 3 * NSA_KV_WIDTH)
    g3 = gate.reshape(batch, seq, LANES)
    nc_pad = kcmp.shape[1]
    nslc = seq // SLC_BLOCK
    nchunk = seq // NSA_TK
    ovt = _overlap_matrix(nc_pad, nslc).T
    tile = lambda b, t: (b, t, 0)
    cmp_spec = pl.BlockSpec((1, nc_pad, NSA_KV_WIDTH), lambda b, t: (b, 0, 0))
    out = pl.pallas_call(
        _nsa_p_kernel,
        grid=(batch, seq // NSA_TQ),
        in_specs=[pl.BlockSpec((1, NSA_TQ, NSA_WIDTH), tile), cmp_spec, cmp_spec,
                  pl.BlockSpec((1, seq, LANES), lambda b, t: (b, 0, 1)),
                  pl.BlockSpec((1, LANES, seq), lambda b, t: (b, 1, 0)),
                  pl.BlockSpec((1, seq, LANES), lambda b, t: (b, 0, 2)),
                  pl.BlockSpec((1, LANES, seq), lambda b, t: (b, 2, 0)),
                  pl.BlockSpec((1, NSA_TQ, LANES), tile),
                  pl.BlockSpec((nslc, nc_pad), lambda b, t: (0, 0))],
        out_specs=pl.BlockSpec((1, NSA_TQ, NSA_WIDTH), tile),
        out_shape=jax.ShapeDtypeStruct((batch, seq, NSA_WIDTH), F32),
        scratch_shapes=[pltpu.VMEM((NSA_KV_HEADS, seq, 2 * D_HEAD), BF16), pltpu.VMEM((seq, LANES), BF16),
                        pltpu.VMEM((nchunk, LANES, NSA_TK), BF16), pltpu.VMEM((nchunk, LANES, NSA_TK), BF16),
                        pltpu.VMEM((NSA_KV_HEADS, nchunk, NSA_TK, NSA_GROUP * NSA_TQ), F32),
                        pltpu.VMEM((NSA_KV_HEADS, WINDOW // NSA_TK + 1, NSA_TK, NSA_GROUP * NSA_TQ), F32)],
        compiler_params=_params("arbitrary", "arbitrary"),
        name="nsa_prompt",
    )(q3, kcmp, vcmp, kn3, vnt, kn3, vnt, g3, ovt)
    return out.reshape(batch * seq, NSA_WIDTH)


def _merge_kernel(oa_ref, ob_ref, x_ref, sc_ref, sh_ref, g1_ref, gn_ref, wm_ref, wa_ref, wb_ref, wo_ref, o_ref):
    x = x_ref[...]
    h = _norm_mod(x, gn_ref[...], sc_ref[0], sh_ref[0]).astype(BF16)
    mg = _dot(h, wm_ref[...])
    a = _dot(oa_ref[...].astype(BF16), wa_ref[...])
    b = _dot(ob_ref[...].astype(BF16), wb_ref[...])
    mix = _sigmoid(mg[:, :D_MODEL]) * a + _sigmoid(mg[:, D_MODEL:]) * b
    o_ref[...] = x + g1_ref[0] * _dot(mix.astype(BF16), wo_ref[...])


def _merge(oa, ob, x, sc, sh, g1, gn, wm, wa, wb, wo, tm):
    n = x.shape[0]
    nb, r, _ = g1.shape
    tiles_per_b = (n // nb) // tm
    row = lambda i: (i, 0)
    full = lambda i: (0, 0)
    mod = pl.BlockSpec((1, r, D_MODEL), lambda i: (i // tiles_per_b, 0, 0))
    return pl.pallas_call(
        _merge_kernel,
        grid=(n // tm,),
        in_specs=[pl.BlockSpec((tm, MOBA_WIDTH), row), pl.BlockSpec((tm, NSA_WIDTH), row),
                  pl.BlockSpec((tm, D_MODEL), row), mod, mod, mod,
                  pl.BlockSpec((1, D_MODEL), full), pl.BlockSpec((D_MODEL, 2 * D_MODEL), full),
                  pl.BlockSpec((MOBA_WIDTH, D_MODEL), full), pl.BlockSpec((NSA_WIDTH, D_MODEL), full),
                  pl.BlockSpec((D_MODEL, D_MODEL), full)],
        out_specs=pl.BlockSpec((tm, D_MODEL), row),
        out_shape=jax.ShapeDtypeStruct((n, D_MODEL), F32),
        compiler_params=_params("arbitrary"),
        name="merge",
    )(oa, ob, x, sc, sh, g1, gn, wm, wa, wb, wo)


def _route(logits):
    lane = _iota(logits.shape, 1)
    is_grp = lane < N_GROUPS
    lg = jnp.where(is_grp, logits, NEG_INF)
    mg = jnp.max(lg, axis=-1, keepdims=True)
    pg = jnp.where(is_grp, jnp.exp(lg - mg), 0.0)
    pg = pg / jnp.sum(pg, axis=-1, keepdims=True)
    g_w = jnp.max(pg, axis=-1, keepdims=True)
    g_sel = jnp.min(jnp.where(is_grp & (pg == g_w), lane, LANES), axis=-1, keepdims=True)
    e_lane = lane - N_GROUPS
    in_grp = (e_lane >= 0) & (e_lane < N_EXPERTS) & ((e_lane >> 3) == g_sel)
    le = jnp.where(in_grp, logits, NEG_INF)
    me = jnp.max(le, axis=-1, keepdims=True)
    pe = jnp.where(in_grp, jnp.exp(le - me), 0.0)
    pe = pe / jnp.sum(pe, axis=-1, keepdims=True)
    v1 = jnp.max(pe, axis=-1, keepdims=True)
    i1 = jnp.min(jnp.where(in_grp & (pe == v1), lane, LANES), axis=-1, keepdims=True)
    rest = in_grp & (lane != i1)
    pr = jnp.where(rest, pe, -1.0)
    v2 = jnp.max(pr, axis=-1, keepdims=True)
    i2 = jnp.min(jnp.where(rest & (pr == v2), lane, LANES), axis=-1, keepdims=True)
    tot = v1 + v2
    comb = jnp.where(lane == i1, v1 / tot, 0.0) + jnp.where(lane == i2, v2 / tot, 0.0)
    comb = comb * g_w
    return pltpu.roll(comb, LANES - N_GROUPS, 1)


MOE_EXPERTS_PER_STEP = 8


def _moe_kernel(x_ref, sc_ref, sh_ref, g2_ref, gn_ref, gf_ref, wr_ref, win_ref, wout_ref, o_ref,
                h_scr, comb_scr, acc_scr):
    e = pl.program_id(1)

    @pl.when(e == 0)
    def _():
        h = _norm_mod(x_ref[...], gn_ref[...], sc_ref[0], sh_ref[0])
        hb = h.astype(BF16)
        h_scr[...] = hb
        h_lo = (h - hb.astype(F32)).astype(BF16)
        logits = _dot(hb, wr_ref[0]) + _dot(hb, wr_ref[1]) + _dot(h_lo, wr_ref[0])
        comb_scr[...] = _route(logits)
        acc_scr[...] = jnp.zeros_like(acc_scr)

    per = win_ref.shape[0]
    hb = h_scr[...]
    comb = comb_scr[...]
    lane = _iota(comb.shape, 1)
    acts = []
    for j in range(per):
        hid = _dot(hb, win_ref[j])
        a = hid[:, :D_EXPERT]
        b = hid[:, D_EXPERT:]
        w = jnp.sum(jnp.where(lane == e * per + j, comb, 0.0), axis=1, keepdims=True)
        acts.append((a * _sigmoid(a) * b * w).astype(BF16))
    act = jnp.concatenate(acts, axis=1)
    acc_scr[...] += _dot(act, wout_ref[...].reshape(per * D_EXPERT, D_MODEL))

    @pl.when(e == pl.num_programs(1) - 1)
    def _():
        y = x_ref[...] + g2_ref[0] * acc_scr[...]
        ms = jnp.mean(y * y, axis=-1, keepdims=True)
        o_ref[...] = y * lax.rsqrt(ms + EPS) * gf_ref[...]


def _moe(x, sc, sh, g2, gn, gf, wr, w_ein, w_eout, tm):
    n = x.shape[0]
    nb, r, _ = sc.shape
    tiles_per_b = (n // nb) // tm
    row = lambda i, e: (i, 0)
    mod = lambda i, e: (i // tiles_per_b, 0, 0)
    full = lambda i, e: (0, 0)
    per = MOE_EXPERTS_PER_STEP
    return pl.pallas_call(
        _moe_kernel,
        grid=(n // tm, N_EXPERTS // per),
        in_specs=[pl.BlockSpec((tm, D_MODEL), row),
                  pl.BlockSpec((1, r, D_MODEL), mod), pl.BlockSpec((1, r, D_MODEL), mod),
                  pl.BlockSpec((1, r, D_MODEL), mod),
                  pl.BlockSpec((1, D_MODEL), full), pl.BlockSpec((1, D_MODEL), full),
                  pl.BlockSpec((2, D_MODEL, LANES), lambda i, e: (0, 0, 0)),
                  pl.BlockSpec((per, D_MODEL, 2 * D_EXPERT), lambda i, e: (e, 0, 0)),
                  pl.BlockSpec((per, D_EXPERT, D_MODEL), lambda i, e: (e, 0, 0))],
        out_specs=pl.BlockSpec((tm, D_MODEL), row),
        out_shape=jax.ShapeDtypeStruct((n, D_MODEL), F32),
        scratch_shapes=[pltpu.VMEM((tm, D_MODEL), BF16), pltpu.VMEM((tm, LANES), F32),
                        pltpu.VMEM((tm, D_MODEL), F32)],
        compiler_params=_params("arbitrary", "arbitrary"),
        name="moe",
    )(x, sc, sh, g2, gn, gf, wr, w_ein, w_eout)


PAGE = 128
MOBA_PAGES_PER_STEP = 16
NSA_PAGES_PER_STEP = 32
COMPRESS_PAGES_PER_STEP = 32


def _page_view(cache):
    n_phys, page, heads, dh = cache.shape
    return jnp.transpose(cache, (0, 2, 3, 1)).reshape(n_phys, heads * dh, page)


PAGE_RING_DEPTH = 3
SLC_RING_DEPTH = 4
SMALL_PAGE_ROWS = NSA_KV_WIDTH


def _page_ring_step(pt_ref, streams, per):
    depth = streams[0][1].shape[0]
    ns = pl.num_programs(1)
    step = pl.program_id(0) * ns + pl.program_id(1)
    last = pl.num_programs(0) * ns - 1

    def copies(step_idx):
        slot_idx = step_idx % depth
        return [pltpu.make_async_copy(hbm.at[pt_ref[step_idx * per + u]], buf.at[slot_idx, u], sem.at[slot_idx])
                for hbm, buf, sem in streams for u in range(per)]

    split = streams[0][1].shape[2] <= SMALL_PAGE_ROWS

    def start_all(step_idx):
        for i, c in enumerate(copies(step_idx)):
            c.start(priority=i % 2 if split else 0)

    @pl.when(step == 0)
    def _():
        for j in range(depth - 1):
            @pl.when(j <= last)
            def _():
                start_all(j)

    for c in copies(step):
        c.wait()

    @pl.when(step + (depth - 1) <= last)
    def _():
        start_all(step + (depth - 1))

    return step % depth


def _head_diag(full, heads):
    rows = full.shape[0]
    head = _iota((rows, D_HEAD), 0) // (rows // heads)
    out = jnp.zeros((rows, D_HEAD), F32)
    for h in range(heads):
        out = out + jnp.where(head == h, full[:, h * D_HEAD:(h + 1) * D_HEAD], 0.0)
    return out


def _moba_s_kernel(pt_ref, qbd_ref, k_hbm, v_hbm, m_ref, l_ref, ks_ref, o_ref, kbuf, vbuf, ksem, vsem):
    per = MOBA_PAGES_PER_STEP
    s = pl.program_id(1)
    slot = _page_ring_step(pt_ref, [(k_hbm, kbuf, ksem), (v_hbm, vbuf, vsem)], per)

    @pl.when(s == 0)
    def _():
        m_ref[...] = jnp.zeros_like(m_ref)
        l_ref[...] = jnp.zeros_like(l_ref)
        ks_ref[...] = jnp.zeros_like(ks_ref)

    qbd = qbd_ref[0]
    lane_q = _iota(m_ref.shape[1:], 1)
    lane_k = _iota(ks_ref.shape[1:], 1)
    ppb = MOBA_BLOCK // PAGE
    m_all, l_all, ks_all = m_ref[0], l_ref[0], ks_ref[0]
    kts = [kbuf[slot, u] for u in range(per)]
    sc_all = _dot(qbd, jnp.concatenate([kt.astype(BF16) for kt in kts], axis=1))
    for j in range(per // ppb):
        blk = s * (per // ppb) + j
        sc = sc_all[:, j * MOBA_BLOCK:(j + 1) * MOBA_BLOCK]
        m = jnp.max(sc, axis=-1, keepdims=True)
        p = jnp.exp(sc - m)
        l = jnp.sum(p, axis=-1, keepdims=True)
        vt = jnp.concatenate([vbuf[slot, j * ppb + t].astype(BF16) for t in range(ppb)], axis=1)
        o_ref[0, j] = _head_diag(_dot_nt(p.astype(BF16), vt), MOBA_HEADS)
        kb = kts[j * ppb]
        for t in range(1, ppb):
            kb = kb + kts[j * ppb + t]
        ksum = jnp.sum(kb, axis=-1, keepdims=True)
        m_all = jnp.where(lane_q == blk, m, m_all)
        l_all = jnp.where(lane_q == blk, l, l_all)
        ks_all = jnp.where(lane_k == blk, ksum, ks_all)
    m_ref[0] = m_all
    l_ref[0] = l_all
    ks_ref[0] = ks_all


def _moba_s_pass(pt_flat, qbd, kt_pages, vt_pages, batch, n_pages):
    per = MOBA_PAGES_PER_STEP
    rows = qbd.shape[1]
    ppb = MOBA_BLOCK // PAGE
    stat = lambda b, s, pt: (b, 0, 0)
    return pl.pallas_call(
        _moba_s_kernel,
        grid_spec=pltpu.PrefetchScalarGridSpec(
            num_scalar_prefetch=1,
            grid=(batch, n_pages // per),
            in_specs=[pl.BlockSpec((1, rows, MOBA_WIDTH), stat),
                      pl.BlockSpec(memory_space=pl.ANY), pl.BlockSpec(memory_space=pl.ANY)],
            out_specs=[pl.BlockSpec((1, rows, LANES), stat), pl.BlockSpec((1, rows, LANES), stat),
                       pl.BlockSpec((1, MOBA_WIDTH, LANES), stat),
                       pl.BlockSpec((1, per // ppb, rows, D_HEAD), lambda b, s, pt: (b, s, 0, 0))],
            scratch_shapes=[pltpu.VMEM((PAGE_RING_DEPTH, per, MOBA_WIDTH, PAGE), F32),
                            pltpu.VMEM((PAGE_RING_DEPTH, per, MOBA_WIDTH, PAGE), F32),
                            pltpu.SemaphoreType.DMA((PAGE_RING_DEPTH,)), pltpu.SemaphoreType.DMA((PAGE_RING_DEPTH,))]),
        out_shape=[jax.ShapeDtypeStruct((batch, rows, LANES), F32), jax.ShapeDtypeStruct((batch, rows, LANES), F32),
                   jax.ShapeDtypeStruct((batch, MOBA_WIDTH, LANES), F32),
                   jax.ShapeDtypeStruct((batch, n_pages // ppb, rows, D_HEAD), F32)],
        compiler_params=_params("arbitrary", "arbitrary"),
        name="moba_decode_pages",
    )(pt_flat, qbd, kt_pages, vt_pages)


def _moba_s_combine_kernel(m_ref, l_ref, ks_ref, o_ref, qf_ref, qbd_ref, kn_ref, vn_ref, out_ref, *, n_pages, ts):
    rows = m_ref.shape[1]
    nblk = n_pages // (MOBA_BLOCK // PAGE)
    lane = _iota((rows, LANES), 1)
    kmean = ks_ref[0] * (1.0 / MOBA_BLOCK)
    gate = jnp.dot(qf_ref[0], kmean, precision=HIGHEST, preferred_element_type=F32)
    gate = jnp.where(lane < nblk, gate, NEG_INF)
    selp = _rank_select(gate, nblk, MOBA_TOPK) & (lane < nblk)
    qbd = qbd_ref[0]
    s_own = _dot_nt(qbd, _new_rows(kn_ref).astype(BF16))
    valid_own = lane <= (_iota((rows, LANES), 0) % ts)
    s_own = jnp.where(valid_own, s_own, NEG_INF)
    m_all = jnp.where(selp, m_ref[0], NEG_INF)
    big_m = jnp.maximum(jnp.max(m_all, axis=-1, keepdims=True), jnp.max(s_own, axis=-1, keepdims=True))
    wgt = jnp.where(selp, jnp.exp(m_ref[0] - big_m), 0.0)
    p_own = jnp.where(valid_own, jnp.exp(s_own - big_m), 0.0)
    denom = jnp.sum(wgt * l_ref[0], axis=-1, keepdims=True) + jnp.sum(p_own, axis=-1, keepdims=True)
    num = _head_diag(_dot(p_own.astype(BF16), _new_rows(vn_ref).astype(BF16)), MOBA_HEADS)
    for j in range(nblk):
        num = num + wgt[:, j:j + 1] * o_ref[0, j]
    out_ref[0] = num / jnp.maximum(denom, TINY)


def _moba_s_combine(m, l, ks, o, qf, qbd, kn, vn, n_pages, ts):
    batch, rows, _ = m.shape
    b3 = lambda b: (b, 0, 0)
    return pl.pallas_call(
        functools.partial(_moba_s_combine_kernel, n_pages=n_pages, ts=ts),
        grid=(batch,),
        in_specs=[pl.BlockSpec((1, rows, LANES), b3), pl.BlockSpec((1, rows, LANES), b3),
                  pl.BlockSpec((1, MOBA_WIDTH, LANES), b3),
                  pl.BlockSpec((1, o.shape[1], rows, D_HEAD), lambda b: (b, 0, 0, 0)),
                  pl.BlockSpec((1, rows, MOBA_WIDTH), b3), pl.BlockSpec((1, rows, MOBA_WIDTH), b3),
                  pl.BlockSpec((1, ts, MOBA_WIDTH), b3), pl.BlockSpec((1, ts, MOBA_WIDTH), b3)],
        out_specs=pl.BlockSpec((1, rows, D_HEAD), b3),
        out_shape=jax.ShapeDtypeStruct((batch, rows, D_HEAD), F32),
        compiler_params=_params("arbitrary"),
        name="moba_decode_combine",
    )(m, l, ks, o, qf, qbd, kn, vn)


def _block_diag_q(q, batch, ts, heads):
    q4 = q.reshape(batch, ts, heads, D_HEAD)
    eye = jnp.eye(heads, dtype=q.dtype)
    return jnp.einsum('bchd,hk->bhckd', q4, eye).reshape(batch, heads * ts, heads * D_HEAD)


def _new_rows(ref, lane0=0, width=None):
    rows = ref[0] if width is None else ref[0, :, lane0:lane0 + width]
    return jnp.concatenate([rows, jnp.zeros((LANES - rows.shape[0], rows.shape[1]), rows.dtype)], axis=0)


def _compress_s_kernel(pt_ref, k_hbm, v_hbm, pea_ref, peb_ref, wa_ref, wb_ref, w2_ref, ok_ref, ov_ref,
                       kbuf, vbuf, ksem, vsem, x_scr, a_scr, b_scr):
    per = COMPRESS_PAGES_PER_STEP
    s = pl.program_id(1)
    nseg = per * PAGE // CMP_STRIDE
    slot = _page_ring_step(pt_ref, [(k_hbm, kbuf, ksem), (v_hbm, vbuf, vsem)], per)
    for i, (buf, o_ref) in enumerate(((kbuf, ok_ref), (vbuf, ov_ref))):
        x_i, a_i, b_i, pea_i, peb_i = x_scr.at[i], a_scr.at[i], b_scr.at[i], pea_ref.at[i], peb_ref.at[i]
        for u in range(per):
            x_i[u * PAGE:(u + 1) * PAGE, :] = buf[slot, u].T
        xs = [x_i[pl.ds(l, nseg, stride=CMP_STRIDE), :] for l in range(CMP_STRIDE)]
        xa = jnp.concatenate([(xs[l] + pea_i[l:l + 1, :]).astype(BF16) for l in range(CMP_STRIDE)], axis=1)
        xb = jnp.concatenate([(xs[l] + peb_i[l:l + 1, :]).astype(BF16) for l in range(CMP_STRIDE)], axis=1)
        a_i[pl.ds(pl.multiple_of(s * nseg, nseg), nseg), :] = _dot(xa, wa_ref[i])
        b_i[pl.ds(pl.multiple_of(s * nseg, nseg), nseg), :] = _dot(xb, wb_ref[i])

        @pl.when(s == pl.num_programs(1) - 1)
        def _(i=i, o_ref=o_ref, a_i=a_i, b_i=b_i):
            total = a_i.shape[0]
            hid = a_i[...] + pltpu.roll(b_i[...], total - 1, 0)
            o_ref[0] = _dot(_gelu(hid).astype(BF16), w2_ref[i])


def _compress_s(pt_flat, k_pages, v_pages, cw_k, cw_v, batch, n_pages):
    per = COMPRESS_PAGES_PER_STEP
    total = n_pages * PAGE // CMP_STRIDE
    hidden = NSA_KV_HEADS * CMP_HIDDEN
    width = CMP_STRIDE * NSA_KV_WIDTH
    wa, wb, pea, peb, w2bd = [jnp.stack([a, b]) for a, b in zip(cw_k, cw_v)]
    full3 = lambda b, s, pt: (0, 0, 0)
    out_spec = pl.BlockSpec((1, total, NSA_KV_WIDTH), lambda b, s, pt: (b, 0, 0))
    out = jax.ShapeDtypeStruct((batch, total, NSA_KV_WIDTH), F32)
    page_buf = pltpu.VMEM((PAGE_RING_DEPTH, per, NSA_KV_WIDTH, PAGE), F32)
    return pl.pallas_call(
        _compress_s_kernel,
        grid_spec=pltpu.PrefetchScalarGridSpec(
            num_scalar_prefetch=1,
            grid=(batch, n_pages // per),
            in_specs=[pl.BlockSpec(memory_space=pl.ANY), pl.BlockSpec(memory_space=pl.ANY),
                      pl.BlockSpec((2, CMP_STRIDE, NSA_KV_WIDTH), full3),
                      pl.BlockSpec((2, CMP_STRIDE, NSA_KV_WIDTH), full3),
                      pl.BlockSpec((2, width, hidden), full3), pl.BlockSpec((2, width, hidden), full3),
                      pl.BlockSpec((2, hidden, NSA_KV_WIDTH), full3)],
            out_specs=[out_spec, out_spec],
            scratch_shapes=[page_buf, page_buf,
                            pltpu.SemaphoreType.DMA((PAGE_RING_DEPTH,)), pltpu.SemaphoreType.DMA((PAGE_RING_DEPTH,)),
                            pltpu.VMEM((2, per * PAGE, NSA_KV_WIDTH), F32), pltpu.VMEM((2, total, hidden), F32),
                            pltpu.VMEM((2, total, hidden), F32)]),
        out_shape=[out, out],
        compiler_params=_params("arbitrary", "arbitrary"),
        name="compress_decode",
    )(pt_flat, k_pages, v_pages, pea, peb, wa.reshape(2, width, hidden), wb.reshape(2, width, hidden), w2bd)


def _stack_group_q(q_ref, g):
    return jnp.concatenate(
        [q_ref[0, :, (g * NSA_GROUP + r) * D_HEAD:(g * NSA_GROUP + r + 1) * D_HEAD] * SCALE
         for r in range(NSA_GROUP)], axis=0).astype(BF16)


def _nsa_s_kernel(q_ref, kc_ref, vc_ref, wk_ref, wv_ref, kn_ref, vn_ref, gate_ref, ov_ref, part_ref, sel_ref, *, ts):
    rows = NSA_GROUP * ts
    nc_pad = kc_ref.shape[1]
    wlen = wk_ref.shape[3]
    gates = _sigmoid(gate_ref[0])
    n_i = _iota((rows, nc_pad), 1)
    valid_c = n_i < nc_pad - 1
    c_of_row = _iota((rows, 1), 0) % ts
    valid_w = _iota((rows, wlen), 1) >= c_of_row
    valid_n = _iota((rows, LANES), 1) <= c_of_row
    for g in range(NSA_KV_HEADS):
        lane0 = g * D_HEAD
        qs = _stack_group_q(q_ref, g)
        s = jnp.where(valid_c, _dot_nt(qs, kc_ref[0, :, lane0:lane0 + D_HEAD].astype(BF16)), NEG_INF)
        m = jnp.max(s, axis=-1, keepdims=True)
        p = jnp.where(valid_c, jnp.exp(s - m), 0.0)
        p = p / jnp.maximum(jnp.sum(p, axis=-1, keepdims=True), TINY)
        o_c = _dot(p.astype(BF16), vc_ref[0, :, lane0:lane0 + D_HEAD].astype(BF16))
        psum = jnp.sum(p.reshape(NSA_GROUP, ts, nc_pad), axis=0)
        imp = jnp.dot(psum, ov_ref[...], precision=HIGHEST, preferred_element_type=F32)
        keep = _rank_select(imp, LANES, SLC_TOPN - 1)
        sel_ref[0, g] = jnp.where(keep, 0.0, NEG_INF)
        s_w = jnp.where(valid_w, _dot(qs, wk_ref[0, g].astype(BF16)), NEG_INF)
        s_n = jnp.where(valid_n, _dot_nt(qs, _new_rows(kn_ref, lane0, D_HEAD).astype(BF16)), NEG_INF)
        m = jnp.maximum(jnp.max(s_w, axis=-1, keepdims=True), jnp.max(s_n, axis=-1, keepdims=True))
        p_w = jnp.where(valid_w, jnp.exp(s_w - m), 0.0)
        p_n = jnp.where(valid_n, jnp.exp(s_n - m), 0.0)
        den = jnp.sum(p_w, axis=-1, keepdims=True) + jnp.sum(p_n, axis=-1, keepdims=True)
        o_w = (_dot_nt(p_w.astype(BF16), wv_ref[0, g].astype(BF16))
               + _dot(p_n.astype(BF16), _new_rows(vn_ref, lane0, D_HEAD).astype(BF16))) / jnp.maximum(den, TINY)
        for r in range(NSA_GROUP):
            hd = g * NSA_GROUP + r
            rs = slice(r * ts, (r + 1) * ts)
            part_ref[0, :, hd * D_HEAD:(hd + 1) * D_HEAD] = (
                gates[:, 3 * hd:3 * hd + 1] * o_c[rs] + gates[:, 3 * hd + 2:3 * hd + 3] * o_w[rs])


def _nsa_s(q3, kcmp, vcmp, wk_t, wv_t, kn_w, vn_w, gate3, ts):
    batch = q3.shape[0]
    nc_pad = kcmp.shape[1]
    wlen = wk_t.shape[3]
    ov = _overlap_matrix(nc_pad, LANES)
    b3 = lambda b: (b, 0, 0)
    b4 = lambda b: (b, 0, 0, 0)
    return pl.pallas_call(
        functools.partial(_nsa_s_kernel, ts=ts),
        grid=(batch,),
        in_specs=[pl.BlockSpec((1, ts, NSA_WIDTH), b3),
                  pl.BlockSpec((1, nc_pad, NSA_KV_WIDTH), b3), pl.BlockSpec((1, nc_pad, NSA_KV_WIDTH), b3),
                  pl.BlockSpec((1, NSA_KV_HEADS, D_HEAD, wlen), b4), pl.BlockSpec((1, NSA_KV_HEADS, D_HEAD, wlen), b4),
                  pl.BlockSpec((1, ts, NSA_KV_WIDTH), lambda b: (b, 0, 2)),
                  pl.BlockSpec((1, ts, NSA_KV_WIDTH), lambda b: (b, 0, 2)),
                  pl.BlockSpec((1, ts, LANES), b3),
                  pl.BlockSpec((nc_pad, LANES), lambda b: (0, 0))],
        out_specs=[pl.BlockSpec((1, ts, NSA_WIDTH), b3), pl.BlockSpec((1, NSA_KV_HEADS, ts, LANES), b4)],
        out_shape=[jax.ShapeDtypeStruct((batch, ts, NSA_WIDTH), F32),
                   jax.ShapeDtypeStruct((batch, NSA_KV_HEADS, ts, LANES), F32)],
        compiler_params=_params("arbitrary"),
        name="nsa_decode_cmp_win",
    )(q3, kcmp, vcmp, wk_t, wv_t, kn_w, vn_w, gate3, ov)


def _slc_s_kernel(pt_ref, q_ref, bias_ref, hot_ref, k_hbm, v_hbm, kn_ref, vn_ref, gate_ref, part_ref, o_ref,
                  kbuf, vbuf, ksem, vsem, m_scr, l_scr, acc_scr, *, ts):
    per = NSA_PAGES_PER_STEP
    s = pl.program_id(1)
    slot = _page_ring_step(pt_ref, [(k_hbm, kbuf, ksem), (v_hbm, vbuf, vsem)], per)
    rows = NSA_GROUP * ts
    zero = jnp.zeros((rows, D_HEAD), BF16)
    q2 = jnp.concatenate([jnp.concatenate([_stack_group_q(q_ref, 0), zero], axis=1),
                          jnp.concatenate([zero, _stack_group_q(q_ref, 1)], axis=1)], axis=0)
    in_g0 = _iota((NSA_KV_HEADS * rows, D_HEAD), 0) < rows

    def own_group(full):
        return jnp.where(in_g0, full[:, :D_HEAD], full[:, D_HEAD:])

    @pl.when(s == 0)
    def _():
        valid_n = _iota((NSA_KV_HEADS * rows, LANES), 1) <= (_iota((NSA_KV_HEADS * rows, 1), 0) % ts)
        sc = jnp.where(valid_n, _dot_nt(q2, _new_rows(kn_ref).astype(BF16)), NEG_INF)
        m = jnp.max(sc, axis=-1, keepdims=True)
        p = jnp.exp(sc - m)
        m_scr[...] = m
        l_scr[...] = jnp.sum(p, axis=-1, keepdims=True)
        acc_scr[...] = own_group(_dot(p.astype(BF16), _new_rows(vn_ref).astype(BF16)))

    kt = jnp.concatenate([kbuf[slot, u].astype(BF16) for u in range(per)], axis=1)
    vt = jnp.concatenate([vbuf[slot, u].astype(BF16) for u in range(per)], axis=1)
    nb = per * PAGE // SLC_BLOCK
    bias = jnp.concatenate([bias_ref[0, g] for g in range(NSA_KV_HEADS) for _ in range(NSA_GROUP)], axis=0)
    cols = bias[:, :nb]
    for k in range(1, bias.shape[1] // nb):
        cols = jnp.where(s == k, bias[:, k * nb:(k + 1) * nb], cols)
    sc = _dot(jnp.concatenate([q2, cols.astype(BF16)], axis=1),
              jnp.concatenate([kt, hot_ref[...]], axis=0))
    m_old = m_scr[...]
    m_new = jnp.maximum(m_old, jnp.max(sc, axis=-1, keepdims=True))
    pf = jnp.exp(sc - m_new)
    alpha = jnp.exp(m_old - m_new)
    l_scr[...] = alpha * l_scr[...] + jnp.sum(pf, axis=-1, keepdims=True)
    acc_scr[...] = alpha * acc_scr[...] + own_group(_dot_nt(pf.astype(BF16), vt))
    m_scr[...] = m_new

    @pl.when(s == pl.num_programs(1) - 1)
    def _():
        gates = _sigmoid(gate_ref[0])
        o_s = acc_scr[...] / jnp.maximum(l_scr[...], TINY)
        for hd in range(NSA_HEADS):
            cols = slice(hd * D_HEAD, (hd + 1) * D_HEAD)
            o_ref[0, :, cols] = part_ref[0, :, cols] + gates[:, 3 * hd + 1:3 * hd + 2] * o_s[hd * ts:(hd + 1) * ts]


def _slc_s(pt_flat, q3, sel_bias, k_pages, v_pages, kn_s, vn_s, gate3, part, n_pages, ts):
    batch = q3.shape[0]
    per = NSA_PAGES_PER_STEP
    rows = NSA_GROUP * ts
    nb = per * PAGE // SLC_BLOCK
    hot = (jnp.arange(nb)[:, None] == jnp.arange(per * PAGE)[None, :] // SLC_BLOCK).astype(BF16)
    b3 = lambda b, s, pt: (b, 0, 0)
    return pl.pallas_call(
        functools.partial(_slc_s_kernel, ts=ts),
        grid_spec=pltpu.PrefetchScalarGridSpec(
            num_scalar_prefetch=1,
            grid=(batch, n_pages // per),
            in_specs=[pl.BlockSpec((1, ts, NSA_WIDTH), b3),
                      pl.BlockSpec((1, NSA_KV_HEADS, ts, LANES), lambda b, s, pt: (b, 0, 0, 0)),
                      pl.BlockSpec((nb, per * PAGE), lambda b, s, pt: (0, 0)),
                      pl.BlockSpec(memory_space=pl.ANY), pl.BlockSpec(memory_space=pl.ANY),
                      pl.BlockSpec((1, ts, NSA_KV_WIDTH), lambda b, s, pt: (b, 0, 1)),
                      pl.BlockSpec((1, ts, NSA_KV_WIDTH), lambda b, s, pt: (b, 0, 1)),
                      pl.BlockSpec((1, ts, LANES), b3), pl.BlockSpec((1, ts, NSA_WIDTH), b3)],
            out_specs=pl.BlockSpec((1, ts, NSA_WIDTH), b3),
            scratch_shapes=[pltpu.VMEM((SLC_RING_DEPTH, per, NSA_KV_WIDTH, PAGE), F32),
                            pltpu.VMEM((SLC_RING_DEPTH, per, NSA_KV_WIDTH, PAGE), F32),
                            pltpu.SemaphoreType.DMA((SLC_RING_DEPTH,)), pltpu.SemaphoreType.DMA((SLC_RING_DEPTH,)),
                            pltpu.VMEM((NSA_KV_HEADS * rows, 1), F32), pltpu.VMEM((NSA_KV_HEADS * rows, 1), F32),
                            pltpu.VMEM((NSA_KV_HEADS * rows, D_HEAD), F32)]),
        out_shape=jax.ShapeDtypeStruct((batch, ts, NSA_WIDTH), F32),
        compiler_params=_params("arbitrary", "arbitrary"),
        name="nsa_decode_slc",
    )(pt_flat, q3, sel_bias, hot, k_pages, v_pages, kn_s, vn_s, gate3, part)


def _prep_weights(w_ada, b_ada, norm_mix_g, w_in, pe_cmp_k, w_cmp_k1, w_cmp_k2, pe_cmp_v, w_cmp_v1, w_cmp_v2,
                  w_br_a, w_br_b, w_out, norm_ffn_g, w_router_grp, w_router_exp, w_expert_in, w_expert_out,
                  norm_final_g):
    wr = jnp.concatenate([w_router_grp, w_router_exp], axis=1)
    wr = jnp.pad(wr, ((0, 0), (0, LANES - wr.shape[1])))
    wr_hi = wr.astype(BF16)
    wr = jnp.stack([wr_hi, (wr - wr_hi.astype(F32)).astype(BF16)])
    w_in_main, w_in_merge = _reorder_w_in(w_in)
    return dict(
        w_ada=w_ada, b_ada=b_ada.reshape(1, -1), g_mix=norm_mix_g.reshape(1, -1),
        w_in=w_in_main, w_merge=w_in_merge,
        cmp_k=_compress_weights(pe_cmp_k, w_cmp_k1, w_cmp_k2),
        cmp_v=_compress_weights(pe_cmp_v, w_cmp_v1, w_cmp_v2),
        w_br_a=w_br_a.astype(BF16), w_br_b=w_br_b.astype(BF16), w_out=w_out.astype(BF16),
        g_ffn=norm_ffn_g.reshape(1, -1), wr=wr,
        w_ein=w_expert_in.astype(BF16), w_eout=w_expert_out.astype(BF16),
        g_final=norm_final_g.reshape(1, -1))


def _prompt_layer(x, mod, w, batch, seq):
    sh1, sc1, g1, sh2, sc2, g2 = mod
    cos, sin = _rope_tables(jnp.arange(seq, dtype=jnp.int32))
    qk_a, _, q_b, k_n, v_n, gate, kt_a, vt_a, kt_n, vt_n = _inproj(
        x, sc1, sh1, w['g_mix'], cos, sin, w['w_in'], 256, transposed=True)
    o_a = _moba_p(qk_a, vt_a, batch, seq)
    kcmp = _compress_p(k_n[:, :NSA_KV_WIDTH], w['cmp_k'], batch, seq)
    vcmp = _compress_p(v_n[:, :NSA_KV_WIDTH], w['cmp_v'], batch, seq)
    o_b = _nsa_p(q_b, k_n, vt_n, kcmp, vcmp, gate, batch, seq)
    x1 = _merge(o_a, o_b, x, sc1, sh1, g1, w['g_mix'], w['w_merge'], w['w_br_a'], w['w_br_b'], w['w_out'], 256)
    y = _moe(x1, sc2, sh2, g2, w['g_ffn'], w['g_final'], w['wr'], w['w_ein'], w['w_eout'], 512)
    return y, (kt_a, vt_a, kt_n, vt_n)


def _sample_layer(x, mod, w, caches, win_state, page_table, batch, ts):
    sh1, sc1, g1, sh2, sc2, g2 = mod
    moba_k, moba_v, cmp_k, cmp_v, slc_k, slc_v = caches
    win_k, win_v = win_state
    n_pages = page_table.shape[1]
    assert moba_k.shape[1] == PAGE and win_k.shape[1] == WINDOW and ts <= LANES
    assert n_pages * PAGE == LANES * SLC_BLOCK and n_pages * PAGE // MOBA_BLOCK <= LANES
    n = batch * ts
    pos = n_pages * PAGE + (jnp.arange(n, dtype=jnp.int32) % ts)
    cos, sin = _rope_tables(pos)
    qk_a, v_a, q_b, k_n, v_n, gate = _inproj(x, sc1, sh1, w['g_mix'], cos, sin, w['w_in'], n)
    pt_flat = page_table.reshape(-1)
    kv = NSA_KV_WIDTH
    qf = _block_diag_q(qk_a[:, :MOBA_WIDTH], batch, ts, MOBA_HEADS)
    qbd = (qf * SCALE).astype(BF16)
    m, l, ks, o = _moba_s_pass(pt_flat, qbd, _page_view(moba_k), _page_view(moba_v), batch, n_pages)
    o_a = _moba_s_combine(m, l, ks, o, qf, qbd, qk_a[:, MOBA_WIDTH:].reshape(batch, ts, MOBA_WIDTH),
                          v_a.reshape(batch, ts, MOBA_WIDTH), n_pages, ts)
    o_a = o_a.reshape(batch, MOBA_HEADS, ts, D_HEAD).transpose(0, 2, 1, 3).reshape(n, MOBA_WIDTH)
    kcmp, vcmp = _compress_s(pt_flat, _page_view(cmp_k), _page_view(cmp_v), w['cmp_k'], w['cmp_v'], batch, n_pages)
    q3 = q_b.reshape(batch, ts, NSA_WIDTH)
    gate3 = gate.reshape(batch, ts, LANES)
    kn3 = k_n.reshape(batch, ts, 3 * kv)
    vn3 = v_n.reshape(batch, ts, 3 * kv)
    part, sel = _nsa_s(q3, kcmp, vcmp, jnp.transpose(win_k, (0, 2, 3, 1)), jnp.transpose(win_v, (0, 2, 3, 1)),
                       kn3, vn3, gate3, ts)
    o_b = _slc_s(pt_flat, q3, sel, _page_view(slc_k), _page_view(slc_v), kn3, vn3, gate3, part,
                 n_pages, ts).reshape(n, NSA_WIDTH)
    x1 = _merge(o_a, o_b, x, sc1, sh1, g1, w['g_mix'], w['w_merge'], w['w_br_a'], w['w_br_b'], w['w_out'], n)
    y = _moe(x1, sc2, sh2, g2, w['g_ffn'], w['g_final'], w['wr'], w['w_ein'], w['w_eout'], n)
    return y, (qk_a, v_a, k_n, v_n)


def kernel(x_prompt, x_sample, c_prompt, c_sample, cache_moba_k, cache_moba_v, cache_nsa_cmp_k, cache_nsa_cmp_v,
           cache_nsa_slc_k, cache_nsa_slc_v, state_nsa_win_k, state_nsa_win_v, page_table, w_ada, b_ada, norm_mix_g,
           w_in, pe_cmp_k, w_cmp_k1, w_cmp_k2, pe_cmp_v, w_cmp_v1, w_cmp_v2, w_br_a, w_br_b, w_out, norm_ffn_g,
           w_router_grp, w_router_exp, w_expert_in, w_expert_out, norm_final_g):
    bp, tp, _ = x_prompt.shape
    bs, ts, _ = x_sample.shape
    w = _prep_weights(w_ada[0], b_ada[0], norm_mix_g[0], w_in[0], pe_cmp_k[0], w_cmp_k1[0], w_cmp_k2[0], pe_cmp_v[0],
                      w_cmp_v1[0], w_cmp_v2[0], w_br_a[0], w_br_b[0], w_out[0], norm_ffn_g[0], w_router_grp[0],
                      w_router_exp[0], w_expert_in[0], w_expert_out[0], norm_final_g)
    mod = _ada(jnp.concatenate([c_prompt, c_sample], axis=0), w['w_ada'], w['b_ada'])
    mod_p = [m.reshape(bp, 1, D_MODEL) for m in jnp.split(mod[:bp], 6, axis=-1)]
    y_p, new_p = _prompt_layer(x_prompt.reshape(bp * tp, D_MODEL), mod_p, w, bp, tp)

    mod_s = [jnp.repeat(m, ts, axis=0).reshape(1, bs * ts, D_MODEL) for m in jnp.split(mod[bp:], 6, axis=-1)]
    caches = (cache_moba_k[0], cache_moba_v[0], cache_nsa_cmp_k[0], cache_nsa_cmp_v[0], cache_nsa_slc_k[0],
              cache_nsa_slc_v[0])
    y_s, new_s = _sample_layer(x_sample.reshape(bs * ts, D_MODEL), mod_s, w, caches,
                               (state_nsa_win_k[0], state_nsa_win_v[0]), page_table, bs, ts)

    kv = NSA_KV_WIDTH

    def new_rows(new, b, t):
        qk, v, k_n, v_n = new
        rows = lambda a, heads: a.reshape(1, b, t, heads, D_HEAD)
        return (rows(qk[:, MOBA_WIDTH:], MOBA_HEADS), rows(v, MOBA_HEADS),
                rows(k_n[:, :kv], NSA_KV_HEADS), rows(v_n[:, :kv], NSA_KV_HEADS),
                rows(k_n[:, kv:2 * kv], NSA_KV_HEADS), rows(v_n[:, kv:2 * kv], NSA_KV_HEADS),
                rows(k_n[:, 2 * kv:], NSA_KV_HEADS), rows(v_n[:, 2 * kv:], NSA_KV_HEADS))

    def new_rows_t(new, b, t):
        kt_a, vt_a, kt_n, vt_n = new
        rows = lambda a: jnp.transpose(a.reshape(1, b, a.shape[1] // D_HEAD, D_HEAD, a.shape[2]), (0, 1, 4, 2, 3))
        return (rows(kt_a), rows(vt_a), rows(kt_n[:, :kv]), rows(vt_n[:, :kv]),
                rows(kt_n[:, kv:2 * kv]), rows(vt_n[:, kv:2 * kv]),
                rows(kt_n[:, 2 * kv:, t - wb:]), rows(vt_n[:, 2 * kv:, t - wb:]))

    wb = state_nsa_win_k.shape[2]
    outs_p = new_rows_t(new_p, bp, tp)
    outs_s = new_rows(new_s, bs, ts)
    win_k = jnp.concatenate([state_nsa_win_k, outs_s[6]], axis=2)[:, :, ts:]
    win_v = jnp.concatenate([state_nsa_win_v, outs_s[7]], axis=2)[:, :, ts:]
    outs_s = outs_s[:6] + (win_k, win_v)
    return (y_p.reshape(bp, tp, D_MODEL), y_s.reshape(bs, ts, D_MODEL)) + outs_p + outs_s
```
